```python
import math
import jax
import jax.numpy as jnp
from jax import lax
import numpy as np

D_MODEL = 1024
BATCH = 2
SEQ = 8192
DEPTH = 2

GRID_W = 64
CTX_LEN = 256
F32 = jnp.float32

N_EVEN = (DEPTH + 1) // 2
N_ODD = DEPTH // 2

RET_WIDTH = D_MODEL // 2
RET_HEADS = 4
RET_HEAD_DIM = RET_WIDTH // RET_HEADS
RET_CHUNK = 128
ROPE_BASE = 10000.0

HY_WIDTH = D_MODEL // 2
HY_ORDER = 2
HY_SHORT = 3
HY_EMB = 33
HY_FFN = 64
HY_SHORT_DECAY_PCT = 0.3
HY_LONG_DECAY_PCT = 1.5
HY_TARGET = 1e-2

AB_IN = 4 * RET_WIDTH + (HY_ORDER + 1) * HY_WIDTH
AB_OUT = RET_WIDTH + HY_WIDTH

GLA_HEADS = 4
GLA_KEY = D_MODEL // 2
GLA_VAL = D_MODEL
GLA_DK = GLA_KEY // GLA_HEADS
GLA_DV = GLA_VAL // GLA_HEADS
GLA_RANK = 16
GLA_TAU = 16.0
GLA_CHUNK = 64
GLA_SPLITS = [GLA_KEY, 2 * GLA_KEY, 2 * GLA_KEY + GLA_VAL, 2 * GLA_KEY + 2 * GLA_VAL,
              2 * GLA_KEY + 2 * GLA_VAL + GLA_RANK]
GLA_IN = 2 * GLA_KEY + 2 * GLA_VAL + 2 * GLA_RANK

N_EXPERTS = 16
N_GROUPS = 4
EXPERTS_PER_GROUP = N_EXPERTS // N_GROUPS
TOP_K = 2
EXPERT_HIDDEN = D_MODEL // 2

DN_ALPHA = (2.0 * DEPTH) ** 0.25
DN_BETA = (8.0 * DEPTH) ** -0.25
LN_EPS = 1e-5

kernel_name = 'hybrid_retention_hyena_gla_moe_dit'


def layer_norm(x, g, b):
    xf = x.astype(F32)
    mu = jnp.mean(xf, axis=-1, keepdims=True)
    var = jnp.mean(jnp.square(xf - mu), axis=-1, keepdims=True)
    return ((xf - mu) * lax.rsqrt(var + LN_EPS) * g.astype(F32) + b.astype(F32)).astype(x.dtype)


def head_layernorm(o):
    of = o.astype(F32)
    mu = jnp.mean(of, axis=-1, keepdims=True)
    var = jnp.mean(jnp.square(of - mu), axis=-1, keepdims=True)
    return ((of - mu) * lax.rsqrt(var + LN_EPS)).astype(o.dtype)


def head_rmsnorm(o):
    of = o.astype(F32)
    return (of * lax.rsqrt(jnp.mean(jnp.square(of), axis=-1, keepdims=True) + LN_EPS)).astype(o.dtype)


def ada(cond, w, b):
    return jnp.split(jax.nn.silu(cond) @ w + b, 6, axis=-1)


def post_norm(x, gate, y, g, b):
    return layer_norm(DN_ALPHA * x + gate * y, g, b)


def to_heads(t, n):
    bsz, length, width = t.shape
    return t.reshape(bsz, length, n, width // n).transpose(0, 2, 1, 3)


def from_heads(t):
    bsz, n, length, d = t.shape
    return t.transpose(0, 2, 1, 3).reshape(bsz, length, n * d)


def rev(t):
    return jnp.flip(t, axis=2)


def to_chunks(t, size):
    bsz, n, length, d = t.shape
    return jnp.moveaxis(t.reshape(bsz, n, length // size, size, d), 2, 0)


def from_chunks(t):
    nc, bsz, n, size, d = t.shape
    return jnp.moveaxis(t, 0, 2).reshape(bsz, n, nc * size, d)


def axial_rope(t, rows):
    hd = t.shape[-1]
    quarter = hd // 4
    inv = ROPE_BASE ** (-jnp.arange(quarter, dtype=F32) / quarter)
    r = jnp.repeat(jnp.arange(rows, dtype=F32), GRID_W)
    col = jnp.tile(jnp.arange(GRID_W, dtype=F32), rows)
    ang = jnp.concatenate([r[:, None] * inv, col[:, None] * inv], axis=-1)
    cos, sin = jnp.cos(ang), jnp.sin(ang)
    t1, t2 = jnp.split(t.astype(F32), 2, axis=-1)
    return jnp.concatenate([t1 * cos - t2 * sin, t1 * sin + t2 * cos], axis=-1).astype(t.dtype)


def retention_scan(q, k, v, log_gamma, s0):
    C = RET_CHUNK
    lg = log_gamma.astype(F32)[:, None]
    idx = jnp.arange(C, dtype=F32)
    diff = idx[:, None] - idx[None, :]
    intra = jnp.where(diff >= 0, jnp.exp(jnp.maximum(diff, 0.0)[None] * lg[:, :, None]), 0.0)
    q_decay = jnp.exp((idx + 1.0) * lg)[..., None]
    k_decay = jnp.exp((C - 1.0 - idx) * lg)[..., None]
    c_decay = jnp.exp(C * lg)[..., None]

    def step(s, inp):
        qc, kc, vc = inp
        scores = jnp.einsum('bhid,bhjd->bhij', qc, kc) * intra
        o = jnp.einsum('bhij,bhje->bhie', scores, vc) + jnp.einsum('bhid,bhde->bhie', qc * q_decay, s)
        s = c_decay * s + jnp.einsum('bhjd,bhje->bhde', kc * k_decay, vc)
        return s, o

    s_fin, o = lax.scan(step, s0, (to_chunks(q, C), to_chunks(k, C), to_chunks(v, C)))
    return from_chunks(o).astype(v.dtype), s_fin


def gla_scan(q, k, v, log_a, s0):
    C = GLA_CHUNK
    mask = jnp.tril(jnp.ones((C, C), dtype=bool))

    def step(s, inp):
        qc, kc, vc, ac = inp
        b = jnp.cumsum(ac, axis=2)
        qd = qc * jnp.exp(b)
        kd = kc * jnp.exp(-b)
        scores = jnp.where(mask, jnp.einsum('bhid,bhjd->bhij', qd, kd), 0.0)
        o = jnp.einsum('bhij,bhje->bhie', scores, vc) + jnp.einsum('bhid,bhde->bhie', qd, s)
        b_end = b[:, :, -1:, :]
        s = jnp.exp(b_end)[:, :, 0, :, None] * s + jnp.einsum('bhjd,bhje->bhde', kc * jnp.exp(b_end - b), vc)
        return s, o

    s_fin, o = lax.scan(step, s0, tuple(to_chunks(t, C) for t in (q, k, v, log_a)))
    return from_chunks(o).astype(v.dtype), s_fin


def final_state(k, v, log_a):
    la = jnp.broadcast_to(log_a.astype(F32), k.shape)
    cum = jnp.cumsum(la, axis=2)
    return jnp.einsum('bhld,bhle->bhde', k * jnp.exp(cum[:, :, -1:, :] - cum), v)


def retention_bidir(q, k, v, lg_f, lg_b, s0_f, s0_b):
    o_f, s_f = retention_scan(q, k, v, lg_f, s0_f)
    o_b, s_b = retention_scan(rev(q), rev(k), rev(v), lg_b, s0_b)
    return o_f + rev(o_b), s_f, s_b


def gla_bidir(q, k, v, la_f, la_b, s0_f, s0_b):
    o_f, s_f = gla_scan(q, k, v, la_f, s0_f)
    o_b, s_b = gla_scan(rev(q), rev(k), rev(v), rev(la_b), s0_b)
    return o_f + rev(o_b), s_f, s_b


def hyena_filters(length, w1, b1, w2, b2, w3, b3, w4, freq):
    t = jnp.linspace(0.0, 1.0, length, dtype=F32)[:, None]
    bands = (HY_EMB - 1) // 2
    w = 2.0 * math.pi * jnp.arange(length, dtype=F32)[:, None] / length
    f = jnp.linspace(1e-4, bands - 1, bands, dtype=F32)[None, :]
    feats = jnp.concatenate([t, jnp.cos(f * w), -jnp.sin(f * w)], axis=-1)
    a = jnp.sin(freq[0] * (feats @ w1 + b1))
    a = jnp.sin(freq[1] * (a @ w2 + b2))
    a = jnp.sin(freq[2] * (a @ w3 + b3))
    filt = (a @ w4).astype(F32).reshape(length, HY_ORDER, 2, HY_WIDTH)
    max_decay = math.log(HY_TARGET) / HY_SHORT_DECAY_PCT
    min_decay = math.log(HY_TARGET) / HY_LONG_DECAY_PCT
    deltas = jnp.abs(jnp.linspace(min_decay, max_decay, HY_WIDTH, dtype=F32))
    filt = filt * jnp.exp(-t * deltas)[:, None, None, :]
    return filt * lax.rsqrt(jnp.sum(filt * filt, axis=0, keepdims=True))


def bidir_long_conv(u, h_f, h_b):
    L = u.shape[1]
    taps = jnp.concatenate([h_f[:1] + h_b[:1], h_f[1:], jnp.zeros_like(h_f[:1]), jnp.flip(h_b[1:], axis=0)], axis=0)
    U = jnp.fft.rfft(u.astype(F32), n=2 * L, axis=1)
    H = jnp.fft.rfft(taps, n=2 * L, axis=0)
    return jnp.fft.irfft(U * H[None], n=2 * L, axis=1)[:, :L].astype(u.dtype)


def centred_depthwise_conv(u, w, b):
    pad = HY_SHORT // 2
    y = lax.conv_general_dilated(u, w[:, None, :].astype(u.dtype), window_strides=(1,), padding=((pad, pad),),
                                 dimension_numbers=('NWC', 'WIO', 'NWC'), feature_group_count=u.shape[-1])
    return y + b


def hyena(u, conv_w, conv_b, filt, skip):
    u = centred_depthwise_conv(u, conv_w, conv_b)
    x1, x2, v = jnp.split(u, 3, axis=-1)
    z = v
    for n, gate in enumerate((x1, x2)):
        z = gate * (bidir_long_conv(z, filt[:, n, 0], filt[:, n, 1]) + skip[n] * z)
    return z


def ret_project(z):
    q, k, v, g = jnp.split(z[..., :4 * RET_WIDTH], 4, axis=-1)
    return (to_heads(q, RET_HEADS), to_heads(k, RET_HEADS) * RET_HEAD_DIM ** -0.5,
            to_heads(v, RET_HEADS), g)


def ret_hyena_mixer(h, hc, rows, w_in, lg_f, lg_b, conv_w, conv_b, filt_w, skip, w_out, need_ctx_out):
    z = h @ w_in
    zc = hc @ w_in
    q, k, v, g = ret_project(z)
    qc, kc, vc, gc = ret_project(zc)
    q, k = axial_rope(q, rows), axial_rope(k, rows)
    if need_ctx_out:
        zero = jnp.zeros(kc.shape[:2] + (RET_HEAD_DIM, RET_HEAD_DIM), F32)
        oc, s_f, s_b = retention_bidir(qc, kc, vc, lg_f, lg_b, zero, zero)
    else:
        s_f = final_state(kc, vc, lg_f[:, None, None])
        s_b = final_state(rev(kc), rev(vc), lg_b[:, None, None])
    o, _, _ = retention_bidir(q, k, v, lg_f, lg_b, s_f, s_b)
    y_ret = from_heads(head_layernorm(o)) * jax.nn.silu(g)
    y_hy = hyena(z[..., 4 * RET_WIDTH:], conv_w, conv_b, hyena_filters(h.shape[1], *filt_w), skip)
    y = jnp.concatenate([y_ret, y_hy], axis=-1) @ w_out
    if not need_ctx_out:
        return y, None
    yc_ret = from_heads(head_layernorm(oc)) * jax.nn.silu(gc)
    yc_hy = hyena(zc[..., 4 * RET_WIDTH:], conv_w, conv_b, hyena_filters(hc.shape[1], *filt_w), skip)
    yc = jnp.concatenate([yc_ret, yc_hy], axis=-1) @ w_out
    return y, yc


def gla_project(hh, w_in, gw_f, gb_f, gw_b, gb_b):
    q, k, v, g, lr_f, lr_b = jnp.split(hh @ w_in, GLA_SPLITS, axis=-1)
    la_f = jax.nn.log_sigmoid((lr_f @ gw_f + gb_f).astype(F32)) / GLA_TAU
    la_b = jax.nn.log_sigmoid((lr_b @ gw_b + gb_b).astype(F32)) / GLA_TAU
    return (to_heads(q, GLA_HEADS) * GLA_DK ** -0.5, to_heads(k, GLA_HEADS), to_heads(v, GLA_HEADS), g,
            to_heads(la_f, GLA_HEADS), to_heads(la_b, GLA_HEADS))


def gla_mixer(h, hc, w_in, gw_f, gb_f, gw_b, gb_b, w_out, need_ctx_out):
    q, k, v, g, la_f, la_b = gla_project(h, w_in, gw_f, gb_f, gw_b, gb_b)
    qc, kc, vc, gc, lac_f, lac_b = gla_project(hc, w_in, gw_f, gb_f, gw_b, gb_b)
    if need_ctx_out:
        zero = jnp.zeros(kc.shape[:2] + (GLA_DK, GLA_DV), F32)
        oc, s_f, s_b = gla_bidir(qc, kc, vc, lac_f, lac_b, zero, zero)
    else:
        s_f = final_state(kc, vc, lac_f)
        s_b = final_state(rev(kc), rev(vc), rev(lac_b))
    o, _, _ = gla_bidir(q, k, v, la_f, la_b, s_f, s_b)
    y = (from_heads(head_rmsnorm(o)) * jax.nn.silu(g)) @ w_out
    if not need_ctx_out:
        return y, None
    yc = (from_heads(head_rmsnorm(oc)) * jax.nn.silu(gc)) @ w_out
    return y, yc


def moe(h, router_w, router_bias, w_gate, w_up, w_down):
    T = h.shape[0]
    s = jax.nn.sigmoid((h @ router_w).astype(F32))
    sel = s + router_bias.astype(F32)
    grp_score = jnp.sum(lax.top_k(sel.reshape(T, N_GROUPS, EXPERTS_PER_GROUP), TOP_K)[0], axis=-1)
    best = jnp.argmax(grp_score, axis=-1)
    in_group = (jnp.arange(N_EXPERTS) // EXPERTS_PER_GROUP)[None, :] == best[:, None]
    _, idx = lax.top_k(jnp.where(in_group, sel, -jnp.inf), TOP_K)
    w = jnp.take_along_axis(s, idx, axis=-1)
    w = w / jnp.sum(w, axis=-1, keepdims=True)
    combine = jnp.sum(jax.nn.one_hot(idx, N_EXPERTS, dtype=F32) * w[..., None], axis=1)
    y = jnp.zeros((T, h.shape[1]), F32)
    for e in range(N_EXPERTS):
        he = jax.nn.silu(h @ w_gate[e]) * (h @ w_up[e])
        y = y + combine[:, e:e + 1] * (he @ w_down[e])
    return y.astype(h.dtype)


def setup_inputs(seed: int = 0) -> dict:
    key = jax.random.key(seed)
    ks = iter(jax.random.split(key, 40))

    def nrm(shape, scale):
        return jax.random.normal(next(ks), shape, F32) * scale

    base_decay = jnp.log(1.0 - 2.0 ** (-5.0 - jnp.arange(RET_HEADS, dtype=F32)))
    return {
        'x': nrm((BATCH, SEQ, D_MODEL), 1.0),
        'c': nrm((BATCH, D_MODEL), 1.0),
        'ctx': nrm((BATCH, CTX_LEN, D_MODEL), 1.0),
        'c_ctx': nrm((D_MODEL,), 1.0),
        'mod_w': nrm((DEPTH, D_MODEL, 6 * D_MODEL), 0.5 * D_MODEL ** -0.5),
        'mod_b': nrm((DEPTH, 6 * D_MODEL), 0.02),
        'ln_g': 1.0 + nrm((DEPTH, 2, D_MODEL), 0.02),
        'ln_b': nrm((DEPTH, 2, D_MODEL), 0.02),
        'ab_w_in': nrm((N_EVEN, D_MODEL, AB_IN), D_MODEL ** -0.5),
        'ret_log_decay_f': base_decay * (1.0 + nrm((N_EVEN, RET_HEADS), 0.05)),
        'ret_log_decay_b': base_decay * (1.0 + nrm((N_EVEN, RET_HEADS), 0.05)),
        'hy_conv_w': nrm((N_EVEN, HY_SHORT, (HY_ORDER + 1) * HY_WIDTH), HY_SHORT ** -0.5),
        'hy_conv_b': nrm((N_EVEN, (HY_ORDER + 1) * HY_WIDTH), 0.02),
        'hy_w1': nrm((N_EVEN, HY_EMB, HY_FFN), HY_EMB ** -0.5),
        'hy_b1': nrm((N_EVEN, HY_FFN), 0.1),
        'hy_w2': nrm((N_EVEN, HY_FFN, HY_FFN), HY_FFN ** -0.5),
        'hy_b2': nrm((N_EVEN, HY_FFN), 0.1),
        'hy_w3': nrm((N_EVEN, HY_FFN, HY_FFN), HY_FFN ** -0.5),
        'hy_b3': nrm((N_EVEN, HY_FFN), 0.1),
        'hy_w4': nrm((N_EVEN, HY_FFN, HY_ORDER * 2 * HY_WIDTH), HY_FFN ** -0.5),
        'hy_freq': 1.0 + nrm((N_EVEN, 3, HY_FFN), 0.1),
        'hy_skip': nrm((N_EVEN, HY_ORDER, HY_WIDTH), 0.5),
        'ab_w_out': nrm((N_EVEN, AB_OUT, D_MODEL), DN_BETA * AB_OUT ** -0.5),
        'gla_w_in': nrm((N_ODD, D_MODEL, GLA_IN), D_MODEL ** -0.5),
        'gla_gate_w_f': nrm((N_ODD, GLA_RANK, GLA_KEY), GLA_RANK ** -0.5),
        'gla_gate_b_f': nrm((N_ODD, GLA_KEY), 0.1),
        'gla_gate_w_b': nrm((N_ODD, GLA_RANK, GLA_KEY), GLA_RANK ** -0.5),
        'gla_gate_b_b': nrm((N_ODD, GLA_KEY), 0.1),
        'gla_w_out': nrm((N_ODD, GLA_VAL, D_MODEL), DN_BETA * GLA_VAL ** -0.5),
        'router_w': nrm((D_MODEL, N_EXPERTS), D_MODEL ** -0.5),
        'router_bias': nrm((N_EXPERTS,), 0.01),
        'exp_w_gate': nrm((DEPTH, N_EXPERTS, D_MODEL, EXPERT_HIDDEN), D_MODEL ** -0.5),
        'exp_w_up': nrm((DEPTH, N_EXPERTS, D_MODEL, EXPERT_HIDDEN), D_MODEL ** -0.5),
        'exp_w_down': nrm((DEPTH, N_EXPERTS, EXPERT_HIDDEN, D_MODEL), DN_BETA * EXPERT_HIDDEN ** -0.5),
    }


def reference(x, c, ctx, c_ctx, mod_w, mod_b, ln_g, ln_b, ab_w_in, ret_log_decay_f, ret_log_decay_b,
              hy_conv_w, hy_conv_b, hy_w1, hy_b1, hy_w2, hy_b2, hy_w3, hy_b3, hy_w4, hy_freq, hy_skip,
              ab_w_out, gla_w_in, gla_gate_w_f, gla_gate_b_f, gla_gate_w_b, gla_gate_b_b, gla_w_out,
              router_w, router_bias, exp_w_gate, exp_w_up, exp_w_down):
    B, L, D = x.shape
    rows = L // GRID_W
    for layer in range(DEPTH):
        last = layer == DEPTH - 1
        i = layer // 2
        sh1, sc1, g1, sh2, sc2, g2 = [t[:, None, :] for t in ada(c, mod_w[layer], mod_b[layer])]
        csh1, csc1, cg1, csh2, csc2, cg2 = ada(c_ctx, mod_w[layer], mod_b[layer])
        h = x * (1.0 + sc1) + sh1
        hc = ctx * (1.0 + csc1) + csh1
        if layer % 2 == 0:
            filt_w = (hy_w1[i], hy_b1[i], hy_w2[i], hy_b2[i], hy_w3[i], hy_b3[i], hy_w4[i], hy_freq[i])
            y, yc = ret_hyena_mixer(h, hc, rows, ab_w_in[i], ret_log_decay_f[i], ret_log_decay_b[i],
                                    hy_conv_w[i], hy_conv_b[i], filt_w, hy_skip[i], ab_w_out[i], not last)
        else:
            y, yc = gla_mixer(h, hc, gla_w_in[i], gla_gate_w_f[i], gla_gate_b_f[i], gla_gate_w_b[i],
                              gla_gate_b_b[i], gla_w_out[i], not last)
        x = post_norm(x, g1, y, ln_g[layer, 0], ln_b[layer, 0])
        h = (x * (1.0 + sc2) + sh2).reshape(B * L, D)
        if last:
            out = moe(h, router_w, router_bias, exp_w_gate[layer], exp_w_up[layer], exp_w_down[layer])
            x = post_norm(x, g2, out.reshape(B, L, D), ln_g[layer, 1], ln_b[layer, 1])
        else:
            ctx = post_norm(ctx, cg1, yc, ln_g[layer, 0], ln_b[layer, 0])
            n_ctx = ctx.shape[1]
            hc = (ctx * (1.0 + csc2) + csh2).reshape(B * n_ctx, D)
            out = moe(jnp.concatenate([h, hc], axis=0), router_w, router_bias,
                      exp_w_gate[layer], exp_w_up[layer], exp_w_down[layer])
            x = post_norm(x, g2, out[:B * L].reshape(B, L, D), ln_g[layer, 1], ln_b[layer, 1])
            ctx = post_norm(ctx, cg2, out[B * L:].reshape(B, n_ctx, D), ln_g[layer, 1], ln_b[layer, 1])
    return x
```

```python
import functools
import math

import numpy as np
import jax
import jax.numpy as jnp
from jax import lax
from jax.experimental import pallas as pl
from jax.experimental.pallas import tpu as pltpu

F32 = jnp.float32
BF16 = jnp.bfloat16

GRID_W = 64
RET_HEADS = 4
RET_HD = 128
RET_CHUNK = 128
ROPE_BASE = 10000.0
HY_WIDTH = 512
HY_EMB = 33
HY_FFN = 64
HY_SHORT_DECAY_PCT = 0.3
HY_LONG_DECAY_PCT = 1.5
HY_TARGET = 1e-2
GLA_HEADS = 4
GLA_DK = 128
GLA_DV = 256
GLA_RANK = 16
GLA_TAU = 16.0
GLA_CHUNK = 64
N_EXPERTS = 16
N_GROUPS = 4
EXPERTS_PER_GROUP = 4
LN_EPS = 1e-5
DEPTH = 2
DN_ALPHA = (2.0 * DEPTH) ** 0.25

LANES = 128
TM = 256
TE = 256
FFT_N2 = 128
FFT_CB = 8
VMEM_LIMIT = 48 * 1024 * 1024

HIGHEST = lax.Precision.HIGHEST


def _cparams(*sem):
    return pltpu.CompilerParams(dimension_semantics=sem, vmem_limit_bytes=VMEM_LIMIT)


def _silu(v):
    return v * (1.0 / (1.0 + jnp.exp(-v)))


def _dot(a, b):
    return jnp.dot(a, b, preferred_element_type=F32)


def _dot_nt(a, b):
    return lax.dot_general(a, b, (((1,), (1,)), ((), ())), preferred_element_type=F32)


def _dot_tn(a, b):
    return lax.dot_general(a, b, (((0,), (0,)), ((), ())), preferred_element_type=F32)


def _ada_kernel(c_ref, w_ref, b_ref, o_ref):
    a = _silu(c_ref[...])
    o_ref[0] = jnp.dot(a, w_ref[0], preferred_element_type=F32, precision=HIGHEST) + b_ref[0]


def _ada(cond, mod_w, mod_b):
    depth, d, n = mod_w.shape
    nt = n // 4
    return pl.pallas_call(
        _ada_kernel,
        out_shape=jax.ShapeDtypeStruct((depth, 8, n), F32),
        grid=(depth, n // nt),
        in_specs=[pl.BlockSpec((8, d), lambda l, j: (0, 0)),
                  pl.BlockSpec((1, d, nt), lambda l, j: (l, 0, j)),
                  pl.BlockSpec((1, 1, nt), lambda l, j: (l, 0, j))],
        out_specs=pl.BlockSpec((1, 8, nt), lambda l, j: (l, 0, j)),
        compiler_params=_cparams("arbitrary", "arbitrary"),
        name="ada_mod",
    )(cond, mod_w, mod_b.reshape(depth, 1, n))


def _inproj_kernel(x_ref, m_ref, w_ref, o_ref, *, chunks):
    sh = m_ref[0, 0:1, :]
    sc = m_ref[0, 1:2, :]
    h = (x_ref[...] * (1.0 + sc) + sh).astype(BF16)
    for lo, hi in chunks:
        o_ref[:, lo:hi] = _dot(h, w_ref[:, lo:hi]).astype(BF16)


def _col_chunks(n, width=512):
    out, lo = [], 0
    while lo < n:
        hi = min(lo + width, n)
        out.append((lo, hi))
        lo = hi
    return tuple(out)


def _inproj(xu, mods, w, tiles_per_batch):
    r, d = xu.shape
    n = w.shape[1]
    tpb = tiles_per_batch

    def cond_of(i):
        return 2 * (i // tpb) + jnp.where(i % tpb == tpb - 1, 1, 0)

    return pl.pallas_call(
        functools.partial(_inproj_kernel, chunks=_col_chunks(n)),
        out_shape=jax.ShapeDtypeStruct((r, n), BF16),
        grid=(r // TM,),
        in_specs=[pl.BlockSpec((TM, d), lambda i: (i, 0)),
                  pl.BlockSpec((1, 6, d), lambda i: (cond_of(i), 0, 0)),
                  pl.BlockSpec((d, n), lambda i: (0, 0))],
        out_specs=pl.BlockSpec((TM, n), lambda i: (i, 0)),
        compiler_params=_cparams("arbitrary"),
        name="inproj",
    )(xu, mods, w)


def _ret_kernel(lg_ref, q_ref, k_ref, v_ref, cos_ref, sin_ref, o_ref, s_ref, *, reverse, nb):
    C = RET_CHUNK

    @pl.when(pl.program_id(0) == 0)
    def _():
        s_ref[...] = jnp.zeros_like(s_ref)

    ii = lax.broadcasted_iota(jnp.int32, (C, C), 0).astype(F32)
    jj = lax.broadcasted_iota(jnp.int32, (C, C), 1).astype(F32)
    ci = lax.broadcasted_iota(jnp.int32, (C, 1), 0).astype(F32)
    diff = (jj - ii) if reverse else (ii - jj)
    order = (1, 0) if reverse else (0, 1)
    for h in range(RET_HEADS):
        lg = lg_ref[h]
        intra = jnp.where(diff >= 0, jnp.exp(jnp.maximum(diff, 0.0) * lg), 0.0)
        if reverse:
            q_dec = jnp.exp((C - ci) * lg)
            k_dec = jnp.exp(ci * lg)
        else:
            q_dec = jnp.exp((ci + 1.0) * lg)
            k_dec = jnp.exp((C - 1.0 - ci) * lg)
        c_dec = jnp.exp(jnp.zeros((1, RET_HD), F32) + C * lg)
        cols = slice(h * RET_HD, (h + 1) * RET_HD)
        for b in range(nb):
            for c in order:
                rows = slice(c * C, (c + 1) * C)
                cosf = cos_ref[rows, :]
                sinf = sin_ref[rows, :]
                q = q_ref[b, rows, cols].astype(F32)
                k = k_ref[b, rows, cols].astype(F32)
                v = v_ref[b, rows, cols]
                q = q * cosf + pltpu.roll(q, RET_HD // 2, 1) * sinf
                k = (k * cosf + pltpu.roll(k, RET_HD // 2, 1) * sinf) * (RET_HD ** -0.5)
                st = s_ref[b, h]
                scores = _dot_nt(q.astype(BF16), k.astype(BF16)) * intra
                o = _dot(scores.astype(BF16), v) + _dot_nt((q * q_dec).astype(BF16), st.astype(BF16))
                s_ref[b, h] = c_dec * st + _dot_tn(v, (k * k_dec).astype(BF16))
                o_ref[b, rows, cols] = o


def _retention(z3, lg, cosf, sinf, reverse):
    nb, s, _ = z3.shape
    nt = s // TM
    w = RET_HEADS * RET_HD

    if reverse:
        def jmap(j):
            return jnp.where(j == 0, nt - 1, nt - 1 - j)
    else:
        def jmap(j):
            return (j + nt - 1) % nt

    grid_spec = pltpu.PrefetchScalarGridSpec(
        num_scalar_prefetch=0,
        grid=(nt,),
        in_specs=[pl.BlockSpec(memory_space=pltpu.SMEM),
                  pl.BlockSpec((nb, TM, w), lambda j: (0, jmap(j), 0)),
                  pl.BlockSpec((nb, TM, w), lambda j: (0, jmap(j), 1)),
                  pl.BlockSpec((nb, TM, w), lambda j: (0, jmap(j), 2)),
                  pl.BlockSpec((TM, RET_HD), lambda j: (jmap(j), 0)),
                  pl.BlockSpec((TM, RET_HD), lambda j: (jmap(j), 0))],
        out_specs=pl.BlockSpec((nb, TM, w), lambda j: (0, jmap(j), 0)),
        scratch_shapes=[pltpu.VMEM((nb, RET_HEADS, RET_HD, RET_HD), F32)],
    )
    return pl.pallas_call(
        functools.partial(_ret_kernel, reverse=reverse, nb=nb),
        out_shape=jax.ShapeDtypeStruct((nb, s, w), F32),
        grid_spec=grid_spec,
        compiler_params=_cparams("arbitrary"),
        name="retention_bwd" if reverse else "retention_fwd",
    )(lg, z3, z3, z3, cosf, sinf)


def _rope_tables(length, ctx_len):
    rows = length // GRID_W
    quarter = RET_HD // 4
    inv = ROPE_BASE ** (-jnp.arange(quarter, dtype=F32) / quarter)
    r = jnp.repeat(jnp.arange(rows, dtype=F32), GRID_W)
    col = jnp.tile(jnp.arange(GRID_W, dtype=F32), rows)
    ang = jnp.concatenate([r[:, None] * inv, col[:, None] * inv], axis=-1)
    cos, sin = jnp.cos(ang), jnp.sin(ang)
    cosf = jnp.concatenate([cos, cos], axis=-1)
    sinf = jnp.concatenate([-sin, sin], axis=-1)
    cosf = jnp.concatenate([cosf, jnp.ones((ctx_len, RET_HD), F32)], axis=0)
    sinf = jnp.concatenate([sinf, jnp.zeros((ctx_len, RET_HD), F32)], axis=0)
    return cosf, sinf


def _shortconv_kernel(u_ref, up_ref, un_ref, w_ref, b_ref, o_ref, *, tpb):
    i = pl.program_id(0)
    r = i % tpb
    first = jnp.logical_or(r == 0, r == tpb - 1)
    last = r >= tpb - 2
    u = u_ref[...].astype(F32)
    prev_row = jnp.where(first, 0.0, up_ref[15:16, :].astype(F32))
    next_row = jnp.where(last, 0.0, un_ref[0:1, :].astype(F32))
    ridx = lax.broadcasted_iota(jnp.int32, u.shape, 0)
    um = jnp.where(ridx == 0, prev_row, pltpu.roll(u, 1, 0))
    up = jnp.where(ridx == TM - 1, next_row, pltpu.roll(u, TM - 1, 0))
    y = w_ref[0:1, :] * um + w_ref[1:2, :] * u + w_ref[2:3, :] * up + b_ref[...]
    o_ref[...] = y.astype(BF16)


def _shortconv(z, col0, width, conv_w, conv_b, tpb):
    r = z.shape[0]
    cw = 512
    cb0 = col0 // cw
    hb = TM // 16
    nrb = r // 16
    return pl.pallas_call(
        functools.partial(_shortconv_kernel, tpb=tpb),
        out_shape=jax.ShapeDtypeStruct((r, width), BF16),
        grid=(r // TM, width // cw),
        in_specs=[pl.BlockSpec((TM, cw), lambda i, j: (i, cb0 + j)),
                  pl.BlockSpec((16, cw), lambda i, j: (jnp.maximum(i * hb - 1, 0), cb0 + j)),
                  pl.BlockSpec((16, cw), lambda i, j: (jnp.minimum((i + 1) * hb, nrb - 1), cb0 + j)),
                  pl.BlockSpec((3, cw), lambda i, j: (0, j)),
                  pl.BlockSpec((1, cw), lambda i, j: (0, j))],
        out_specs=pl.BlockSpec((TM, cw), lambda i, j: (i, j)),
        compiler_params=_cparams("arbitrary", "arbitrary"),
        name="hyena_shortconv",
    )(z, z, z, conv_w, conv_b.reshape(1, width))


def _filter_kernel(w1_ref, b1_ref, w2_ref, b2_ref, w3_ref, b3_ref, w4_ref, fr_ref, dl_ref,
                   f_ref, ss_ref, *, length, tl):
    i = pl.program_id(0)
    pos = (lax.broadcasted_iota(jnp.int32, (tl, 1), 0) + i * tl).astype(F32)
    t = pos * (1.0 / (length - 1))
    bands = (HY_EMB - 1) // 2
    w = (2.0 * math.pi) * pos / length
    lane = lax.broadcasted_iota(jnp.int32, (1, LANES), 1)
    band = jnp.where(lane <= bands, lane - 1, lane - 1 - bands).astype(F32)
    f = 1e-4 + band * ((bands - 1 - 1e-4) / (bands - 1))
    fw = f * w
    feats = jnp.where(lane == 0, t, jnp.where(lane <= bands, jnp.cos(fw),
                                              jnp.where(lane <= 2 * bands, -jnp.sin(fw), 0.0)))
    hdot = lambda a, b: jnp.dot(a, b, preferred_element_type=F32, precision=HIGHEST)
    a = jnp.sin(fr_ref[0:1, :] * (hdot(feats, w1_ref[...]) + b1_ref[...]))
    a = jnp.sin(fr_ref[1:2, :] * (hdot(a, w2_ref[...]) + b2_ref[...]))
    a = jnp.sin(fr_ref[2:3, :] * (hdot(a, w3_ref[...]) + b3_ref[...]))
    filt = hdot(a, w4_ref[...]) * jnp.exp(-t * dl_ref[...])
    f_ref[...] = filt

    @pl.when(i == 0)
    def _():
        ss_ref[...] = jnp.zeros_like(ss_ref)

    ss_ref[...] += jnp.sum(filt * filt, axis=0, keepdims=True)


def _hyena_filters_raw(length, w1, b1, w2, b2, w3, b3, w4, freq):
    nout = w4.shape[1]
    tl = min(length, 512)
    max_decay = math.log(HY_TARGET) / HY_SHORT_DECAY_PCT
    min_decay = math.log(HY_TARGET) / HY_LONG_DECAY_PCT
    deltas = jnp.abs(jnp.linspace(min_decay, max_decay, HY_WIDTH, dtype=F32))
    dl = jnp.tile(deltas, nout // HY_WIDTH).reshape(1, nout)
    pad2 = lambda a, rows, cols: jnp.pad(a, ((0, rows - a.shape[0]), (0, cols - a.shape[1])))
    w1p = pad2(w1, LANES, LANES)
    w2p = pad2(w2, LANES, LANES)
    w3p = pad2(w3, LANES, LANES)
    w4p = pad2(w4, LANES, nout)
    b1p, b2p, b3p = (pad2(b.reshape(1, -1), 1, LANES) for b in (b1, b2, b3))
    frp = pad2(freq, 3, LANES)
    full = lambda a: pl.BlockSpec(a.shape, lambda i: tuple(0 for _ in a.shape))
    args = (w1p, b1p, w2p, b2p, w3p, b3p, w4p, frp, dl)
    return pl.pallas_call(
        functools.partial(_filter_kernel, length=length, tl=tl),
        out_shape=(jax.ShapeDtypeStruct((length, nout), F32), jax.ShapeDtypeStruct((1, nout), F32)),
        grid=(length // tl,),
        in_specs=[full(a) for a in args],
        out_specs=(pl.BlockSpec((tl, nout), lambda i: (i, 0)), pl.BlockSpec((1, nout), lambda i: (0, 0))),
        compiler_params=_cparams("arbitrary"),
        name="hyena_filter_mlp",
    )(*args)


def _dft_tables(na):
    k1n = 2 * na
    n = k1n * FFT_N2
    n1 = np.arange(na)[:, None]
    k1 = np.arange(k1n)[None, :]
    ang = 2.0 * np.pi * ((n1 * k1) % k1n) / k1n
    c, s = np.cos(ang), np.sin(ang)
    ma = np.block([[c, -s], [s, c]])
    n1f = np.arange(k1n)[:, None]
    angf = 2.0 * np.pi * ((n1f * k1) % k1n) / k1n
    ma_real = np.concatenate([np.cos(angf), -np.sin(angf)], axis=1)
    n2 = np.arange(FFT_N2)[:, None]
    angt = 2.0 * np.pi * ((n2 * k1) % n) / n
    twr, twi = np.cos(angt), -np.sin(angt)
    k2 = np.arange(FFT_N2)[None, :]
    angb = 2.0 * np.pi * ((n2 * k2) % FFT_N2) / FFT_N2
    cb, sb = np.cos(angb), np.sin(angb)
    mb = np.block([[cb, -sb], [sb, cb]])
    mc = np.block([[cb, sb], [-sb, cb]])
    angd = 2.0 * np.pi * ((np.arange(k1n)[:, None] * np.arange(na)[None, :]) % k1n) / k1n
    cd, sd = np.cos(angd) / n, np.sin(angd) / n
    md = np.block([[cd, sd], [-sd, cd]])
    as_bf = lambda a: jnp.asarray(a, dtype=F32).astype(BF16)
    as_f = lambda a: jnp.asarray(a, dtype=F32)
    return dict(ma=as_bf(ma), ma_real=as_bf(ma_real), twr=as_f(twr), twi=as_f(twi),
                twr_t=as_f(twr.T), twi_t=as_f(twi.T), mb=as_bf(mb), mc=as_bf(mc), md=as_bf(md))


def _fwd_stages(x2, ma, twr, twi, mb, cb, k1n):
    a = _dot(x2, ma).reshape(cb, FFT_N2, 2 * k1n)
    ar, ai = a[..., :k1n], a[..., k1n:]
    ar2 = ar * twr - ai * twi
    ai2 = ar * twi + ai * twr
    xt = jnp.concatenate([jnp.swapaxes(ar2, 1, 2), jnp.swapaxes(ai2, 1, 2)], axis=-1)
    return _dot(xt.astype(BF16).reshape(cb * k1n, 2 * FFT_N2), mb).reshape(cb, k1n, 2 * FFT_N2)


def _spectrum_kernel(t_ref, ssf_ref, ssb_ref, ma_ref, twr_ref, twi_ref, mb_ref, h_ref, *, na):
    k1n = 2 * na
    cb = t_ref.shape[0]
    x = t_ref[...]
    lane = lax.broadcasted_iota(jnp.int32, x.shape, 2)
    row = lax.broadcasted_iota(jnp.int32, x.shape, 1)
    scale = jnp.where(lane < na, lax.rsqrt(ssf_ref[...]), lax.rsqrt(ssb_ref[...]))
    x = x * scale
    sw = jnp.concatenate([x[..., na:], x[..., :na]], axis=-1)
    x = jnp.where((row == 0) & (lane == 0), x + sw, jnp.where((row == 0) & (lane == na), 0.0, x))
    h = _fwd_stages(x.astype(BF16).reshape(cb * FFT_N2, k1n), ma_ref[...], twr_ref[...], twi_ref[...],
                    mb_ref[...], cb, k1n)
    h_ref[...] = h.astype(BF16)


def _spectrum(taps, ssf, ssb, tb, na):
    c = taps.shape[0]
    k1n = 2 * na
    cb = FFT_CB
    full = lambda a: pl.BlockSpec(a.shape, lambda i: tuple(0 for _ in a.shape))
    return pl.pallas_call(
        functools.partial(_spectrum_kernel, na=na),
        out_shape=jax.ShapeDtypeStruct((c, k1n, 2 * FFT_N2), BF16),
        grid=(c // cb,),
        in_specs=[pl.BlockSpec((cb, FFT_N2, k1n), lambda i: (i, 0, 0)),
                  pl.BlockSpec((cb, 1, 1), lambda i: (i, 0, 0)),
                  pl.BlockSpec((cb, 1, 1), lambda i: (i, 0, 0)),
                  full(tb["ma_real"]), full(tb["twr"]), full(tb["twi"]), full(tb["mb"])],
        out_specs=pl.BlockSpec((cb, k1n, 2 * FFT_N2), lambda i: (i, 0, 0)),
        compiler_params=_cparams("arbitrary"),
        name="hyena_filter_spectrum",
    )(taps, ssf, ssb, tb["ma_real"], tb["twr"], tb["twi"], tb["mb"])


def _fftconv_kernel(u_ref, g_ref, h_ref, sk_ref, ma_ref, twr_ref, twi_ref, mb_ref, mc_ref,
                    twrt_ref, twit_ref, md_ref, o_ref, *, na):
    k1n = 2 * na
    cb = u_ref.shape[0]
    u = u_ref[...]
    x = _fwd_stages(u.reshape(cb * FFT_N2, k1n), ma_ref[...], twr_ref[...], twi_ref[...], mb_ref[...], cb, k1n)
    xr, xi = x[..., :FFT_N2], x[..., FFT_N2:]
    h = h_ref[...].astype(F32)
    hr, hi = h[..., :FFT_N2], h[..., FFT_N2:]
    y = jnp.concatenate([xr * hr - xi * hi, xr * hi + xi * hr], axis=-1)
    c = _dot(y.astype(BF16).reshape(cb * k1n, 2 * FFT_N2), mc_ref[...]).reshape(cb, k1n, 2 * FFT_N2)
    cr, ci = c[..., :FFT_N2], c[..., FFT_N2:]
    twrt, twit = twrt_ref[...], twit_ref[...]
    cr2 = cr * twrt + ci * twit
    ci2 = ci * twrt - cr * twit
    ct = jnp.concatenate([jnp.swapaxes(cr2, 1, 2), jnp.swapaxes(ci2, 1, 2)], axis=-1)
    d = _dot(ct.astype(BF16).reshape(cb * FFT_N2, 2 * k1n), md_ref[...]).reshape(cb, FFT_N2, k1n)
    uf = u.astype(F32)
    o_ref[...] = (g_ref[...].astype(F32) * (d + sk_ref[...] * uf)).astype(BF16)


def _fftconv(u, gate, h, skip, tb, na):
    c = u.shape[0]
    k1n = 2 * na
    cb = FFT_CB
    full = lambda a: pl.BlockSpec(a.shape, lambda i: tuple(0 for _ in a.shape))
    blk = pl.BlockSpec((cb, FFT_N2, k1n), lambda i: (i, 0, 0))
    return pl.pallas_call(
        functools.partial(_fftconv_kernel, na=na),
        out_shape=jax.ShapeDtypeStruct(u.shape, BF16),
        grid=(c // cb,),
        in_specs=[blk, blk,
                  pl.BlockSpec((cb, k1n, 2 * FFT_N2), lambda i: (i, 0, 0)),
                  pl.BlockSpec((cb, 1, 1), lambda i: (i, 0, 0)),
                  full(tb["ma"]), full(tb["twr"]), full(tb["twi"]), full(tb["mb"]), full(tb["mc"]),
                  full(tb["twr_t"]), full(tb["twi_t"]), full(tb["md"])],
        out_specs=blk,
        compiler_params=_cparams("arbitrary"),
        name="hyena_fftconv",
    )(u, gate, h, skip, tb["ma"], tb["twr"], tb["twi"], tb["mb"], tb["mc"], tb["twr_t"], tb["twi_t"], tb["md"])


def _to_freq_layout(a, na):
    nb, t, c = a.shape
    full = FFT_N2 * na
    if t < full:
        a = jnp.pad(a, ((0, 0), (0, full - t), (0, 0)))
    a = a.reshape(nb, na, FFT_N2, c)
    return jnp.transpose(a, (3, 2, 0, 1)).reshape(c, FFT_N2, nb * na)


def _from_freq_layout(a, t):
    c, _, lanes = a.shape
    na = lanes // 2
    a = a.reshape(c, FFT_N2, 2, na)
    return jnp.transpose(a, (2, 3, 1, 0)).reshape(2, na * FFT_N2, c)[:, :t]


def _taps_layout(hf, hb, na):
    lf, c = hf.shape
    n = 2 * FFT_N2 * na
    half = n // 2
    fwd = jnp.pad(hf, ((0, half - lf), (0, 0)))
    bwd = jnp.concatenate([hb[:1], jnp.zeros((half - lf, c), F32), jnp.flip(hb[1:], axis=0)], axis=0)
    taps = jnp.concatenate([fwd, bwd], axis=0).reshape(2 * na, FFT_N2, c)
    return jnp.transpose(taps, (2, 1, 0))


ROW_SUB = 8


def _load_row_tiles(ref):
    return jnp.concatenate([ref[:, j, :] for j in range(ROW_SUB)], axis=-1)


def _store_row_tiles(ref, val):
    for j in range(ROW_SUB):
        ref[:, j, :] = val[:, j * LANES:(j + 1) * LANES]


def _post_norm_mod(x, y, m_ref, lng, lnb, gate_row, sh_row, sc_row):
    u = DN_ALPHA * x + m_ref[0, gate_row:gate_row + 1, :] * y
    mu = jnp.mean(u, axis=-1, keepdims=True)
    var = jnp.mean(jnp.square(u - mu), axis=-1, keepdims=True)
    xn = (u - mu) * lax.rsqrt(var + LN_EPS) * lng + lnb
    if sh_row is None:
        return xn, None
    return xn, xn * (1.0 + m_ref[0, sc_row:sc_row + 1, :]) + m_ref[0, sh_row:sh_row + 1, :]


def _outproj0_kernel(of_ref, ob_ref, g_ref, hy_ref, x_ref, m_ref, w_ref, lng_ref, lnb_ref, xo_ref, h2_ref):
    o = of_ref[...] + ob_ref[...]
    g = g_ref[...].astype(F32)
    parts = []
    for h in range(RET_HEADS):
        oh = o[:, h * RET_HD:(h + 1) * RET_HD]
        mu = jnp.mean(oh, axis=-1, keepdims=True)
        var = jnp.mean(jnp.square(oh - mu), axis=-1, keepdims=True)
        parts.append((oh - mu) * lax.rsqrt(var + LN_EPS))
    yret = (jnp.concatenate(parts, axis=-1) * _silu(g)).astype(BF16)
    wr = yret.shape[1]
    y = _dot(yret, w_ref[:wr, :]) + _dot(hy_ref[...], w_ref[wr:, :])
    xn, h2 = _post_norm_mod(x_ref[...], y, m_ref, lng_ref[...], lnb_ref[...], 2, 3, 4)
    xo_ref[...] = xn
    _store_row_tiles(h2_ref, h2)


def _outproj1_kernel(of_ref, ob_ref, g_ref, x_ref, m_ref, w_ref, lng_ref, lnb_ref, xo_ref, h2_ref):
    o = of_ref[...] + ob_ref[...]
    g = g_ref[...].astype(F32)
    parts = []
    for h in range(GLA_HEADS):
        oh = o[:, h * GLA_DV:(h + 1) * GLA_DV]
        parts.append(oh * lax.rsqrt(jnp.mean(jnp.square(oh), axis=-1, keepdims=True) + LN_EPS))
    a = (jnp.concatenate(parts, axis=-1) * _silu(g)).astype(BF16)
    y = _dot(a, w_ref[...])
    xn, h2 = _post_norm_mod(x_ref[...], y, m_ref, lng_ref[...], lnb_ref[...], 2, 3, 4)
    xo_ref[...] = xn
    _store_row_tiles(h2_ref, h2)


def _top2_rows(vals):
    n = len(vals)
    best_v, best_i = vals[0], jnp.zeros_like(vals[0], dtype=jnp.int32)
    for e in range(1, n):
        take = vals[e] > best_v
        best_v = jnp.where(take, vals[e], best_v)
        best_i = jnp.where(take, e, best_i)
    sec_v = jnp.full_like(vals[0], -jnp.inf)
    sec_i = jnp.zeros_like(best_i)
    for e in range(n):
        take = (best_i != e) & (vals[e] > sec_v)
        sec_v = jnp.where(take, vals[e], sec_v)
        sec_i = jnp.where(take, e, sec_i)
    return best_v, best_i, sec_v, sec_i


def _router_kernel(h_ref, rw_ref, rb_ref, mi_ref, mf_ref, cnt_ref, carry_ref):
    i = pl.program_id(0)

    @pl.when(i == 0)
    def _():
        carry_ref[...] = jnp.zeros_like(carry_ref)

    logits = lax.dot_general(rw_ref[...], _load_row_tiles(h_ref), (((1,), (1,)), ((), ())),
                             preferred_element_type=F32, precision=HIGHEST)
    s = 1.0 / (1.0 + jnp.exp(-logits))
    sel = s + rb_ref[...]
    srow = [s[e:e + 1, :] for e in range(N_EXPERTS)]
    selrow = [sel[e:e + 1, :] for e in range(N_EXPERTS)]
    gscore = []
    for g in range(N_GROUPS):
        grp = selrow[g * EXPERTS_PER_GROUP:(g + 1) * EXPERTS_PER_GROUP]
        bv, _, sv, _ = _top2_rows(grp)
        gscore.append(bv + sv)
    best_g = jnp.zeros_like(gscore[0], dtype=jnp.int32)
    best_s = gscore[0]
    for g in range(1, N_GROUPS):
        take = gscore[g] > best_s
        best_s = jnp.where(take, gscore[g], best_s)
        best_g = jnp.where(take, g, best_g)
    cand_sel, cand_s = [], []
    for j in range(EXPERTS_PER_GROUP):
        cs, ca = selrow[j], srow[j]
        for g in range(1, N_GROUPS):
            cs = jnp.where(best_g == g, selrow[g * EXPERTS_PER_GROUP + j], cs)
            ca = jnp.where(best_g == g, srow[g * EXPERTS_PER_GROUP + j], ca)
        cand_sel.append(cs)
        cand_s.append(ca)
    _, i1, _, i2 = _top2_rows(cand_sel)
    a1, a2 = cand_s[0], cand_s[0]
    for j in range(1, EXPERTS_PER_GROUP):
        a1 = jnp.where(i1 == j, cand_s[j], a1)
        a2 = jnp.where(i2 == j, cand_s[j], a2)
    den = a1 + a2
    w1, w2 = a1 / den, a2 / den
    e1 = best_g * EXPERTS_PER_GROUP + i1
    e2 = best_g * EXPERTS_PER_GROUP + i2
    eidx = lax.broadcasted_iota(jnp.int32, (N_EXPERTS, TM), 0)
    oh1 = (eidx == e1)
    oh2 = (eidx == e2)
    onehot = jnp.where(oh1 | oh2, 1.0, 0.0)
    tri = jnp.where(lax.broadcasted_iota(jnp.int32, (TM, TM), 0) < lax.broadcasted_iota(jnp.int32, (TM, TM), 1),
                    1.0, 0.0).astype(BF16)
    rank = _dot(onehot.astype(BF16), tri) + carry_ref[:, 0:1]
    r1 = jnp.sum(jnp.where(oh1, rank, 0.0), axis=0, keepdims=True)
    r2 = jnp.sum(jnp.where(oh2, rank, 0.0), axis=0, keepdims=True)
    carry_ref[...] = carry_ref[...] + jnp.sum(onehot, axis=1, keepdims=True)
    zi = jnp.zeros((4, TM), jnp.int32)
    mi_ref[...] = jnp.concatenate([e1, e2, r1.astype(jnp.int32), r2.astype(jnp.int32), zi], axis=0)
    mf_ref[...] = jnp.concatenate([w1, w2, jnp.zeros((6, TM), F32)], axis=0)
    cnt_ref[...] = carry_ref[...]


def _router(h2, router_w, router_bias):
    r = h2.shape[0]
    d = router_w.shape[0]
    return pl.pallas_call(
        _router_kernel,
        out_shape=(jax.ShapeDtypeStruct((8, r), jnp.int32), jax.ShapeDtypeStruct((8, r), F32),
                   jax.ShapeDtypeStruct((N_EXPERTS, LANES), F32)),
        grid=(r // TM,),
        in_specs=[pl.BlockSpec((TM, ROW_SUB, LANES), lambda i: (i, 0, 0)),
                  pl.BlockSpec((N_EXPERTS, d), lambda i: (0, 0)),
                  pl.BlockSpec((N_EXPERTS, 1), lambda i: (0, 0))],
        out_specs=(pl.BlockSpec((8, TM), lambda i: (0, i)), pl.BlockSpec((8, TM), lambda i: (0, i)),
                   pl.BlockSpec((N_EXPERTS, LANES), lambda i: (0, 0))),
        scratch_shapes=[pltpu.VMEM((N_EXPERTS, LANES), F32)],
        compiler_params=_cparams("arbitrary"),
        name="moe_router",
    )(h2, router_w.T, router_bias.reshape(N_EXPERTS, 1))


def _dispatch_kernel(slots_ref, zoff_ref, h_ref, xs_ref, zero_ref, sem):
    i = pl.program_id(0)

    @pl.when(i == 0)
    def _():
        zero_ref[...] = jnp.zeros_like(zero_ref)
        for e in range(N_EXPERTS):
            pltpu.make_async_copy(zero_ref, xs_ref.at[pl.ds(zoff_ref[e], TE)], sem).start()
        for e in range(N_EXPERTS):
            pltpu.make_async_copy(zero_ref, xs_ref.at[pl.ds(zoff_ref[e], TE)], sem).wait()
        first_free = zoff_ref[N_EXPERTS]
        n_all = xs_ref.shape[0] // TE

        def fill(t, carry):
            pltpu.make_async_copy(zero_ref, xs_ref.at[pl.ds(t * TE, TE)], sem).start()
            return carry

        def fill_wait(t, carry):
            pltpu.make_async_copy(zero_ref, xs_ref.at[pl.ds(t * TE, TE)], sem).wait()
            return carry

        lax.fori_loop(first_free, n_all, fill, 0)
        lax.fori_loop(first_free, n_all, fill_wait, 0)

    base = i * TM

    def row_copy(r, k):
        return pltpu.make_async_copy(h_ref.at[pl.ds(base + r, 1)], xs_ref.at[pl.ds(slots_ref[0, k, r], 1)], sem)

    def issue(r, carry):
        for k in range(2):
            row_copy(r, k).start()
        return carry

    lax.fori_loop(0, TM, issue, 0)

    def drain(r, carry):
        for k in range(2):
            row_copy(r, k).wait()
        return carry

    lax.fori_loop(0, TM, drain, 0)


def _dispatch(h2, slots3, zoff, p_rows):
    r = h2.shape[0]
    grid_spec = pltpu.PrefetchScalarGridSpec(
        num_scalar_prefetch=0,
        grid=(r // TM,),
        in_specs=[pl.BlockSpec((1, 2, TM), lambda i: (i, 0, 0), memory_space=pltpu.SMEM),
                  pl.BlockSpec(memory_space=pltpu.SMEM),
                  pl.BlockSpec(memory_space=pl.ANY)],
        out_specs=pl.BlockSpec(memory_space=pl.ANY),
        scratch_shapes=[pltpu.VMEM((TE, ROW_SUB, LANES), F32), pltpu.SemaphoreType.DMA(())],
    )
    return pl.pallas_call(
        _dispatch_kernel,
        out_shape=jax.ShapeDtypeStruct((p_rows, ROW_SUB, LANES), F32),
        grid_spec=grid_spec,
        compiler_params=_cparams("arbitrary"),
        name="moe_dispatch",
    )(slots3, zoff, h2)


def _experts_kernel(te_ref, tv_ref, x_ref, wg_ref, wu_ref, wd_ref, y_ref):
    j = pl.program_id(0)

    @pl.when(tv_ref[j] > 0)
    def _():
        x = _load_row_tiles(x_ref).astype(BF16)
        g = _dot(x, wg_ref[0])
        u = _dot(x, wu_ref[0])
        _store_row_tiles(y_ref, _dot((_silu(g) * u).astype(BF16), wd_ref[0]))

    @pl.when(tv_ref[j] == 0)
    def _():
        y_ref[...] = jnp.zeros_like(y_ref)


def _experts(xs, tile_expert, tile_valid, wg, wu, wd):
    p = xs.shape[0]
    d, hdim = wg.shape[1], wg.shape[2]
    grid_spec = pltpu.PrefetchScalarGridSpec(
        num_scalar_prefetch=2,
        grid=(p // TE,),
        in_specs=[pl.BlockSpec((TE, ROW_SUB, LANES), lambda j, te, tv: (j, 0, 0)),
                  pl.BlockSpec((1, d, hdim), lambda j, te, tv: (te[j], 0, 0)),
                  pl.BlockSpec((1, d, hdim), lambda j, te, tv: (te[j], 0, 0)),
                  pl.BlockSpec((1, hdim, d), lambda j, te, tv: (te[j], 0, 0))],
        out_specs=pl.BlockSpec((TE, ROW_SUB, LANES), lambda j, te, tv: (j, 0, 0)),
    )
    return pl.pallas_call(
        _experts_kernel,
        out_shape=jax.ShapeDtypeStruct((p, ROW_SUB, LANES), F32),
        grid_spec=grid_spec,
        compiler_params=_cparams("arbitrary"),
        name="moe_experts",
    )(tile_expert, tile_valid, xs, wg, wu, wd)


def _combine_kernel(slots_ref, ys_ref, wts_ref, x_ref, m_ref, lng_ref, lnb_ref, o_ref, b1_ref, b2_ref, sem):
    bufs = (b1_ref, b2_ref)

    def row_copy(r, k):
        return pltpu.make_async_copy(ys_ref.at[pl.ds(slots_ref[0, k, r], 1)], bufs[k].at[pl.ds(r, 1)], sem)

    def issue(r, carry):
        for k in range(2):
            row_copy(r, k).start()
        return carry

    lax.fori_loop(0, TM, issue, 0)

    def drain(r, carry):
        for k in range(2):
            row_copy(r, k).wait()
        return carry

    lax.fori_loop(0, TM, drain, 0)
    y = wts_ref[:, 0:1] * _load_row_tiles(b1_ref) + wts_ref[:, 1:2] * _load_row_tiles(b2_ref)
    xn, _ = _post_norm_mod(x_ref[...], y, m_ref, lng_ref[...], lnb_ref[...], 5, None, None)
    o_ref[...] = xn


def _combine(ys, slots3, wts, x, mods, lng, lnb, row_tile_of, out_tile_of, cond_of, n_tiles, out_rows):
    d = x.shape[1]
    grid_spec = pltpu.PrefetchScalarGridSpec(
        num_scalar_prefetch=0,
        grid=(n_tiles,),
        in_specs=[pl.BlockSpec((1, 2, TM), lambda i: (i, 0, 0), memory_space=pltpu.SMEM),
                  pl.BlockSpec(memory_space=pl.ANY),
                  pl.BlockSpec((TM, 2), lambda i: (i, 0)),
                  pl.BlockSpec((TM, d), lambda i: (row_tile_of(i), 0)),
                  pl.BlockSpec((1, 6, d), lambda i: (cond_of(i), 0, 0)),
                  pl.BlockSpec((1, d), lambda i: (0, 0)),
                  pl.BlockSpec((1, d), lambda i: (0, 0))],
        out_specs=pl.BlockSpec((TM, d), lambda i: (out_tile_of(i), 0)),
        scratch_shapes=[pltpu.VMEM((TM, ROW_SUB, LANES), F32), pltpu.VMEM((TM, ROW_SUB, LANES), F32),
                        pltpu.SemaphoreType.DMA(())],
    )
    return pl.pallas_call(
        _combine_kernel,
        out_shape=jax.ShapeDtypeStruct((out_rows, d), F32),
        grid_spec=grid_spec,
        compiler_params=_cparams("arbitrary"),
        name="moe_combine",
    )(slots3, ys, wts, x, mods, lng, lnb)


def _moe(h2, x, x_tile_of, out_tile_of, cond_of, out_rows, mods, lng, lnb, router_w, router_bias, wg, wu, wd):
    t = h2.shape[0]
    n_tiles = t // TM
    mi, mf, cnt = _router(h2, router_w, router_bias)
    counts = cnt[:, 0].astype(jnp.int32)
    padded = ((counts + TE - 1) // TE) * TE
    ends = jnp.cumsum(padded)
    offs = ends - padded
    e12 = mi[0:2]
    slots = jnp.take(offs, e12) + mi[2:4]
    slots3 = jnp.transpose(slots.reshape(2, n_tiles, TM), (1, 0, 2))
    n_exp_tiles = (2 * t) // TE + N_EXPERTS
    p_rows = n_exp_tiles * TE
    total_tiles = ends[-1] // TE
    tile_valid = (jnp.arange(n_exp_tiles) < total_tiles).astype(jnp.int32)
    tile_block = jnp.minimum(jnp.arange(n_exp_tiles), total_tiles - 1).astype(jnp.int32)
    tile_expert = jnp.sum((tile_block[:, None] * TE >= ends[None, :]).astype(jnp.int32), axis=1).astype(jnp.int32)
    zoff = jnp.concatenate([jnp.maximum(ends - TE, 0), total_tiles[None]]).astype(jnp.int32)
    xs = _dispatch(h2, slots3, zoff, p_rows)
    ys = _experts(xs, tile_expert, tile_valid, wg, wu, wd)
    wts = jnp.transpose(mf[0:2])
    return _combine(ys, slots3, wts, x, mods, lng, lnb, x_tile_of, out_tile_of, cond_of, n_tiles, out_rows)


def _gla_kernel(q_ref, k_ref, v_ref, lr_ref, gw_ref, gb_ref, o_ref, s_ref, *, reverse, nb):
    C = GLA_CHUNK
    nchunk = TM // C

    @pl.when(pl.program_id(0) == 0)
    def _():
        s_ref[...] = jnp.zeros_like(s_ref)

    ii = lax.broadcasted_iota(jnp.int32, (C, C), 0)
    jj = lax.broadcasted_iota(jnp.int32, (C, C), 1)
    keep = (jj >= ii) if reverse else (jj <= ii)
    tri = jnp.where(keep, 1.0, 0.0)
    order = tuple(reversed(range(nchunk))) if reverse else tuple(range(nchunk))
    end_row = 0 if reverse else C - 1
    gw = gw_ref[...]
    gb = gb_ref[...]
    for b in range(nb):
        pre = _dot(lr_ref[b], gw) + gb
        la_all = (jnp.minimum(pre, 0.0) - jnp.log(1.0 + jnp.exp(-jnp.abs(pre)))) * (1.0 / GLA_TAU)
        for c in order:
            rows = slice(c * C, (c + 1) * C)
            la = la_all[rows, :]
            bc = jnp.dot(tri, la, preferred_element_type=F32, precision=HIGHEST)
            bend = bc[end_row:end_row + 1, :]
            eb = jnp.exp(bc)
            enb = jnp.exp(-bc)
            ekb = jnp.exp(bend - bc)
            ebend = jnp.exp(bend)
            for h in range(GLA_HEADS):
                kc = slice(h * GLA_DK, (h + 1) * GLA_DK)
                vc = slice(h * GLA_DV, (h + 1) * GLA_DV)
                q = q_ref[b, rows, kc].astype(F32) * (GLA_DK ** -0.5)
                k = k_ref[b, rows, kc].astype(F32)
                v = v_ref[b, rows, vc]
                qd = (q * eb[:, kc]).astype(BF16)
                kd = (k * enb[:, kc]).astype(BF16)
                st = s_ref[b, h]
                scores = jnp.where(keep, _dot_nt(qd, kd), 0.0)
                o = _dot(scores.astype(BF16), v) + _dot_nt(qd, st.astype(BF16))
                s_ref[b, h] = st * ebend[:, kc] + _dot_tn(v, (k * ekb[:, kc]).astype(BF16))
                o_ref[b, rows, vc] = o


def _gla(z3, gw, gb, reverse):
    nb, s, _ = z3.shape
    nt = s // TM
    kw = GLA_HEADS * GLA_DK
    vw = GLA_HEADS * GLA_DV

    if reverse:
        def jmap(j):
            return jnp.where(j == 0, nt - 1, nt - 1 - j)
    else:
        def jmap(j):
            return (j + nt - 1) % nt

    lr_blk = (2 * kw + 2 * vw) // LANES
    return pl.pallas_call(
        functools.partial(_gla_kernel, reverse=reverse, nb=nb),
        out_shape=jax.ShapeDtypeStruct((nb, s, vw), F32),
        grid=(nt,),
        in_specs=[pl.BlockSpec((nb, TM, kw), lambda j: (0, jmap(j), 0)),
                  pl.BlockSpec((nb, TM, kw), lambda j: (0, jmap(j), 1)),
                  pl.BlockSpec((nb, TM, vw), lambda j: (0, jmap(j), 1)),
                  pl.BlockSpec((nb, TM, LANES), lambda j: (0, jmap(j), lr_blk)),
                  pl.BlockSpec((LANES, kw), lambda j: (0, 0)),
                  pl.BlockSpec((1, kw), lambda j: (0, 0))],
        out_specs=pl.BlockSpec((nb, TM, vw), lambda j: (0, jmap(j), 0)),
        scratch_shapes=[pltpu.VMEM((nb, GLA_HEADS, GLA_DV, GLA_DK), F32)],
        compiler_params=_cparams("arbitrary"),
        name="gla_bwd" if reverse else "gla_fwd",
    )(z3, z3, z3, z3, gw, gb)


def kernel(x, c, ctx, c_ctx, mod_w, mod_b, ln_g, ln_b, ab_w_in, ret_log_decay_f, ret_log_decay_b, hy_conv_w, hy_conv_b, hy_w1, hy_b1, hy_w2, hy_b2, hy_w3, hy_b3, hy_w4, hy_freq, hy_skip, ab_w_out, gla_w_in, gla_gate_w_f, gla_gate_b_f, gla_gate_w_b, gla_gate_b_b, gla_w_out, router_w, router_bias, exp_w_gate, exp_w_up, exp_w_down):
    nb, length, d = x.shape
    ctx_len = ctx.shape[1]
    assert nb == 2 and ctx_len == TM and length % TM == 0 and length % (FFT_N2 * 2) == 0
    s = length + ctx_len
    tpb = s // TM
    r = nb * s
    lat_tiles = tpb - 1

    cond = jnp.zeros((8, d), F32).at[0].set(c[0]).at[1].set(c_ctx).at[2].set(c[1]).at[3].set(c_ctx)
    mods = _ada(cond, mod_w, mod_b).reshape(DEPTH, 8, 6, d)

    def cond_of(i):
        return 2 * (i // tpb) + jnp.where(i % tpb == tpb - 1, 1, 0)

    xu = jnp.concatenate([x, ctx], axis=1).reshape(r, d)

    z0 = _inproj(xu, mods[0], ab_w_in[0].astype(BF16), tpb)
    z0_3 = z0.reshape(nb, s, -1)
    cosf, sinf = _rope_tables(length, ctx_len)
    o_f = _retention(z0_3, ret_log_decay_f[0], cosf, sinf, False)
    o_b = _retention(z0_3, ret_log_decay_b[0], cosf, sinf, True)

    ret_w = RET_HEADS * RET_HD
    hyu = _shortconv(z0, 4 * ret_w, 3 * HY_WIDTH, hy_conv_w[0], hy_conv_b[0], tpb)
    hyu3 = hyu.reshape(nb, s, 3 * HY_WIDTH)
    na = length // FFT_N2
    tb = _dft_tables(na)
    filt_w = (hy_w1[0], hy_b1[0], hy_w2[0], hy_b2[0], hy_w3[0], hy_b3[0], hy_w4[0], hy_freq[0])
    skip = hy_skip[0]

    def hyena_branch(seq, seq_len):
        filt, ss = _hyena_filters_raw(seq_len, *filt_w)
        filt = filt.reshape(seq_len, 2, 2, HY_WIDTH)
        ss = ss.reshape(2, 2, HY_WIDTH)
        zcur = _to_freq_layout(seq[..., 2 * HY_WIDTH:], na)
        for n in range(2):
            taps = _taps_layout(filt[:, n, 0], filt[:, n, 1], na)
            hspec = _spectrum(taps, ss[n, 0].reshape(-1, 1, 1), ss[n, 1].reshape(-1, 1, 1), tb, na)
            gate = _to_freq_layout(seq[..., n * HY_WIDTH:(n + 1) * HY_WIDTH], na)
            zcur = _fftconv(zcur, gate, hspec, skip[n].reshape(-1, 1, 1), tb, na)
        return _from_freq_layout(zcur, seq_len)

    y_hy_lat = hyena_branch(hyu3[:, :length], length)
    y_hy_ctx = hyena_branch(hyu3[:, length:], ctx_len)
    y_hy = jnp.concatenate([y_hy_lat, y_hy_ctx], axis=1).reshape(r, HY_WIDTH)

    row = lambda i: (i, 0)
    row3 = lambda i: (i, 0, 0)
    full2 = lambda shp: pl.BlockSpec(shp, lambda i: (0, 0))
    x_mid, h2 = pl.pallas_call(
        _outproj0_kernel,
        out_shape=(jax.ShapeDtypeStruct((r, d), F32), jax.ShapeDtypeStruct((r, ROW_SUB, LANES), F32)),
        grid=(r // TM,),
        in_specs=[pl.BlockSpec((TM, ret_w), row), pl.BlockSpec((TM, ret_w), row),
                  pl.BlockSpec((TM, ret_w), lambda i: (i, 3)),
                  pl.BlockSpec((TM, HY_WIDTH), row), pl.BlockSpec((TM, d), row),
                  pl.BlockSpec((1, 6, d), lambda i: (cond_of(i), 0, 0)),
                  full2((ret_w + HY_WIDTH, d)), full2((1, d)), full2((1, d))],
        out_specs=(pl.BlockSpec((TM, d), row), pl.BlockSpec((TM, ROW_SUB, LANES), row3)),
        compiler_params=_cparams("arbitrary"),
        name="outproj_norm0",
    )(o_f.reshape(r, ret_w), o_b.reshape(r, ret_w), z0, y_hy, xu, mods[0], ab_w_out[0].astype(BF16),
      ln_g[0, 0].reshape(1, d), ln_b[0, 0].reshape(1, d))

    x1 = _moe(h2, x_mid, lambda i: i, lambda i: i, cond_of, r, mods[0], ln_g[0, 1].reshape(1, d),
              ln_b[0, 1].reshape(1, d), router_w, router_bias,
              exp_w_gate[0].astype(BF16), exp_w_up[0].astype(BF16), exp_w_down[0].astype(BF16))

    kw = GLA_HEADS * GLA_DK
    vw = GLA_HEADS * GLA_DV
    n_in = gla_w_in.shape[2]
    n_pad = 2 * kw + 2 * vw + LANES
    w_in1 = jnp.pad(gla_w_in[0], ((0, 0), (0, n_pad - n_in))).astype(BF16)
    z1 = _inproj(x1, mods[1], w_in1, tpb)
    z1_3 = z1.reshape(nb, s, n_pad)
    gw_f = jnp.zeros((LANES, kw), F32).at[:GLA_RANK].set(gla_gate_w_f[0]).astype(BF16)
    gw_b = jnp.zeros((LANES, kw), F32).at[GLA_RANK:2 * GLA_RANK].set(gla_gate_w_b[0]).astype(BF16)
    g_f = _gla(z1_3, gw_f, gla_gate_b_f[0].reshape(1, kw), False)
    g_b = _gla(z1_3, gw_b, gla_gate_b_b[0].reshape(1, kw), True)

    n_lat = nb * lat_tiles

    def lat_tile(i):
        return (i // lat_tiles) * tpb + i % lat_tiles

    def lat_cond(i):
        return 2 * (i // lat_tiles)

    lrow = lambda i: (lat_tile(i), 0)
    x_mid1, h2_1 = pl.pallas_call(
        _outproj1_kernel,
        out_shape=(jax.ShapeDtypeStruct((n_lat * TM, d), F32),
                   jax.ShapeDtypeStruct((n_lat * TM, ROW_SUB, LANES), F32)),
        grid=(n_lat,),
        in_specs=[pl.BlockSpec((TM, vw), lrow), pl.BlockSpec((TM, vw), lrow),
                  pl.BlockSpec((TM, vw), lambda i: (lat_tile(i), 2)),
                  pl.BlockSpec((TM, d), lrow),
                  pl.BlockSpec((1, 6, d), lambda i: (lat_cond(i), 0, 0)),
                  full2((vw, d)), full2((1, d)), full2((1, d))],
        out_specs=(pl.BlockSpec((TM, d), row), pl.BlockSpec((TM, ROW_SUB, LANES), row3)),
        compiler_params=_cparams("arbitrary"),
        name="outproj_norm1",
    )(g_f.reshape(r, vw), g_b.reshape(r, vw), z1, x1, mods[1], gla_w_out[0].astype(BF16),
      ln_g[1, 0].reshape(1, d), ln_b[1, 0].reshape(1, d))

    out = _moe(h2_1, x_mid1, lambda i: i, lambda i: i, lat_cond, n_lat * TM, mods[1], ln_g[1, 1].reshape(1, d),
               ln_b[1, 1].reshape(1, d), router_w, router_bias,
               exp_w_gate[1].astype(BF16), exp_w_up[1].astype(BF16), exp_w_down[1].astype(BF16))
    return out.reshape(nb, length, d)
```

```python
import functools
import math

import numpy as np
import jax
import jax.numpy as jnp
from jax import lax
from jax.experimental import pallas as pl
from jax.experimental.pallas import tpu as pltpu

F32 = jnp.float32
BF16 = jnp.bfloat16

GRID_W = 64
RET_HEADS = 4
RET_HD = 128
RET_CHUNK = 128
ROPE_BASE = 10000.0
HY_WIDTH = 512
HY_EMB = 33
HY_FFN = 64
HY_SHORT_DECAY_PCT = 0.3
HY_LONG_DECAY_PCT = 1.5
HY_TARGET = 1e-2
GLA_HEADS = 4
GLA_DK = 128
GLA_DV = 256
GLA_RANK = 16
GLA_TAU = 16.0
GLA_CHUNK = 64
N_EXPERTS = 16
N_GROUPS = 4
EXPERTS_PER_GROUP = 4
LN_EPS = 1e-5
DEPTH = 2
DN_ALPHA = (2.0 * DEPTH) ** 0.25

LANES = 128
TM = 256
TE = 256
FFT_N2 = 128
FFT_CB = 8
VMEM_LIMIT = 48 * 1024 * 1024

HIGHEST = lax.Precision.HIGHEST


def _cparams(*sem):
    return pltpu.CompilerParams(dimension_semantics=sem, vmem_limit_bytes=VMEM_LIMIT)


def _silu(v):
    return v * (1.0 / (1.0 + jnp.exp(-v)))


def _dot(a, b):
    return jnp.dot(a, b, preferred_element_type=F32)


def _dot_nt(a, b):
    return lax.dot_general(a, b, (((1,), (1,)), ((), ())), preferred_element_type=F32)


def _dot_tn(a, b):
    return lax.dot_general(a, b, (((0,), (0,)), ((), ())), preferred_element_type=F32)


def _ada_kernel(c_ref, w_ref, b_ref, o_ref):
    a = _silu(c_ref[...])
    o_ref[0] = jnp.dot(a, w_ref[0], preferred_element_type=F32, precision=HIGHEST) + b_ref[0]


def _ada(cond, mod_w, mod_b):
    depth, d, n = mod_w.shape
    nt = n // 4
    return pl.pallas_call(
        _ada_kernel,
        out_shape=jax.ShapeDtypeStruct((depth, 8, n), F32),
        grid=(depth, n // nt),
        in_specs=[pl.BlockSpec((8, d), lambda l, j: (0, 0)),
                  pl.BlockSpec((1, d, nt), lambda l, j: (l, 0, j)),
                  pl.BlockSpec((1, 1, nt), lambda l, j: (l, 0, j))],
        out_specs=pl.BlockSpec((1, 8, nt), lambda l, j: (l, 0, j)),
        compiler_params=_cparams("arbitrary", "arbitrary"),
        name="ada_mod",
    )(cond, mod_w, mod_b.reshape(depth, 1, n))


def _inproj_kernel(x_ref, m_ref, w_ref, o_ref, *, chunks):
    sh = m_ref[0, 0:1, :]
    sc = m_ref[0, 1:2, :]
    h = (x_ref[...] * (1.0 + sc) + sh).astype(BF16)
    for lo, hi in chunks:
        o_ref[:, lo:hi] = _dot(h, w_ref[:, lo:hi]).astype(BF16)


def _col_chunks(n, width=512):
    out, lo = [], 0
    while lo < n:
        hi = min(lo + width, n)
        out.append((lo, hi))
        lo = hi
    return tuple(out)


def _inproj(xu, mods, w, tiles_per_batch):
    r, d = xu.shape
    n = w.shape[1]
    tpb = tiles_per_batch

    def cond_of(i):
        return 2 * (i // tpb) + jnp.where(i % tpb == tpb - 1, 1, 0)

    return pl.pallas_call(
        functools.partial(_inproj_kernel, chunks=_col_chunks(n)),
        out_shape=jax.ShapeDtypeStruct((r, n), BF16),
        grid=(r // TM,),
        in_specs=[pl.BlockSpec((TM, d), lambda i: (i, 0)),
                  pl.BlockSpec((1, 6, d), lambda i: (cond_of(i), 0, 0)),
                  pl.BlockSpec((d, n), lambda i: (0, 0))],
        out_specs=pl.BlockSpec((TM, n), lambda i: (i, 0)),
        compiler_params=_cparams("arbitrary"),
        name="inproj",
    )(xu, mods, w)


def _ret_kernel(lg_ref, q_ref, k_ref, v_ref, cos_ref, sin_ref, o_ref, s_ref, *, reverse, nb):
    C = RET_CHUNK

    @pl.when(pl.program_id(0) == 0)
    def _():
        s_ref[...] = jnp.zeros_like(s_ref)

    ii = lax.broadcasted_iota(jnp.int32, (C, C), 0).astype(F32)
    jj = lax.broadcasted_iota(jnp.int32, (C, C), 1).astype(F32)
    ci = lax.broadcasted_iota(jnp.int32, (C, 1), 0).astype(F32)
    diff = (jj - ii) if reverse else (ii - jj)
    order = (1, 0) if reverse else (0, 1)
    for h in range(RET_HEADS):
        lg = lg_ref[h]
        intra = jnp.where(diff >= 0, jnp.exp(jnp.maximum(diff, 0.0) * lg), 0.0)
        if reverse:
            q_dec = jnp.exp((C - ci) * lg)
            k_dec = jnp.exp(ci * lg)
        else:
            q_dec = jnp.exp((ci + 1.0) * lg)
            k_dec = jnp.exp((C - 1.0 - ci) * lg)
        c_dec = jnp.exp(jnp.zeros((1, RET_HD), F32) + C * lg)
        cols = slice(h * RET_HD, (h + 1) * RET_HD)
        for b in range(nb):
            for c in order:
                rows = slice(c * C, (c + 1) * C)
                cosf = cos_ref[rows, :]
                sinf = sin_ref[rows, :]
                q = q_ref[b, rows, cols].astype(F32)
                k = k_ref[b, rows, cols].astype(F32)
                v = v_ref[b, rows, cols]
                q = q * cosf + pltpu.roll(q, RET_HD // 2, 1) * sinf
                k = (k * cosf + pltpu.roll(k, RET_HD // 2, 1) * sinf) * (RET_HD ** -0.5)
                st = s_ref[b, h]
                scores = _dot_nt(q.astype(BF16), k.astype(BF16)) * intra
                o = _dot(scores.astype(BF16), v) + _dot_nt((q * q_dec).astype(BF16), st.astype(BF16))
                s_ref[b, h] = c_dec * st + _dot_tn(v, (k * k_dec).astype(BF16))
                o_ref[b, rows, cols] = o


def _retention(z3, lg, cosf, sinf, reverse):
    nb, s, _ = z3.shape
    nt = s // TM
    w = RET_HEADS * RET_HD

    if reverse:
        def jmap(j):
            return jnp.where(j == 0, nt - 1, nt - 1 - j)
    else:
        def jmap(j):
            return (j + nt - 1) % nt

    grid_spec = pltpu.PrefetchScalarGridSpec(
        num_scalar_prefetch=0,
        grid=(nt,),
        in_specs=[pl.BlockSpec(memory_space=pltpu.SMEM),
                  pl.BlockSpec((nb, TM, w), lambda j: (0, jmap(j), 0)),
                  pl.BlockSpec((nb, TM, w), lambda j: (0, jmap(j), 1)),
                  pl.BlockSpec((nb, TM, w), lambda j: (0, jmap(j), 2)),
                  pl.BlockSpec((TM, RET_HD), lambda j: (jmap(j), 0)),
                  pl.BlockSpec((TM, RET_HD), lambda j: (jmap(j), 0))],
        out_specs=pl.BlockSpec((nb, TM, w), lambda j: (0, jmap(j), 0)),
        scratch_shapes=[pltpu.VMEM((nb, RET_HEADS, RET_HD, RET_HD), F32)],
    )
    return pl.pallas_call(
        functools.partial(_ret_kernel, reverse=reverse, nb=nb),
        out_shape=jax.ShapeDtypeStruct((nb, s, w), F32),
        grid_spec=grid_spec,
        compiler_params=_cparams("arbitrary"),
        name="retention_bwd" if reverse else "retention_fwd",
    )(lg, z3, z3, z3, cosf, sinf)


def _rope_tables(length, ctx_len):
    rows = length // GRID_W
    quarter = RET_HD // 4
    inv = ROPE_BASE ** (-jnp.arange(quarter, dtype=F32) / quarter)
    r = jnp.repeat(jnp.arange(rows, dtype=F32), GRID_W)
    col = jnp.tile(jnp.arange(GRID_W, dtype=F32), rows)
    ang = jnp.concatenate([r[:, None] * inv, col[:, None] * inv], axis=-1)
    cos, sin = jnp.cos(ang), jnp.sin(ang)
    cosf = jnp.concatenate([cos, cos], axis=-1)
    sinf = jnp.concatenate([-sin, sin], axis=-1)
    cosf = jnp.concatenate([cosf, jnp.ones((ctx_len, RET_HD), F32)], axis=0)
    sinf = jnp.concatenate([sinf, jnp.zeros((ctx_len, RET_HD), F32)], axis=0)
    return cosf, sinf


def _shortconv_kernel(u_ref, up_ref, un_ref, w_ref, b_ref, o_ref, *, tpb):
    i = pl.program_id(0)
    r = i % tpb
    first = jnp.logical_or(r == 0, r == tpb - 1)
    last = r >= tpb - 2
    u = u_ref[...].astype(F32)
    prev_row = jnp.where(first, 0.0, up_ref[15:16, :].astype(F32))
    next_row = jnp.where(last, 0.0, un_ref[0:1, :].astype(F32))
    ridx = lax.broadcasted_iota(jnp.int32, u.shape, 0)
    um = jnp.where(ridx == 0, prev_row, pltpu.roll(u, 1, 0))
    up = jnp.where(ridx == TM - 1, next_row, pltpu.roll(u, TM - 1, 0))
    y = w_ref[0:1, :] * um + w_ref[1:2, :] * u + w_ref[2:3, :] * up + b_ref[...]
    o_ref[...] = y.astype(BF16)


def _shortconv(z, col0, width, conv_w, conv_b, tpb):
    r = z.shape[0]
    cw = 512
    cb0 = col0 // cw
    hb = TM // 16
    nrb = r // 16
    return pl.pallas_call(
        functools.partial(_shortconv_kernel, tpb=tpb),
        out_shape=jax.ShapeDtypeStruct((r, width), BF16),
        grid=(r // TM, width // cw),
        in_specs=[pl.BlockSpec((TM, cw), lambda i, j: (i, cb0 + j)),
                  pl.BlockSpec((16, cw), lambda i, j: (jnp.maximum(i * hb - 1, 0), cb0 + j)),
                  pl.BlockSpec((16, cw), lambda i, j: (jnp.minimum((i + 1) * hb, nrb - 1), cb0 + j)),
                  pl.BlockSpec((3, cw), lambda i, j: (0, j)),
                  pl.BlockSpec((1, cw), lambda i, j: (0, j))],
        out_specs=pl.BlockSpec((TM, cw), lambda i, j: (i, j)),
        compiler_params=_cparams("arbitrary", "arbitrary"),
        name="hyena_shortconv",
    )(z, z, z, conv_w, conv_b.reshape(1, width))


def _filter_kernel(w1_ref, b1_ref, w2_ref, b2_ref, w3_ref, b3_ref, w4_ref, fr_ref, dl_ref,
                   f_ref, ss_ref, *, length, pb):
    i = pl.program_id(0)
    pos = (lax.broadcasted_iota(jnp.int32, (1, pb), 1) + i * pb).astype(F32)
    t = pos * (1.0 / (length - 1))
    bands = (HY_EMB - 1) // 2
    w = (2.0 * math.pi) * pos / length
    sub = lax.broadcasted_iota(jnp.int32, (LANES, 1), 0)
    band = jnp.where(sub <= bands, sub - 1, sub - 1 - bands).astype(F32)
    f = 1e-4 + band * ((bands - 1 - 1e-4) / (bands - 1))
    fw = f * w
    feats = jnp.where(sub == 0, t, jnp.where(sub <= bands, jnp.cos(fw),
                                             jnp.where(sub <= 2 * bands, -jnp.sin(fw), 0.0)))
    hdot = lambda a, b: jnp.dot(a, b, preferred_element_type=F32, precision=HIGHEST)
    a = jnp.sin(fr_ref[:, 0:1] * (hdot(w1_ref[...], feats) + b1_ref[...]))
    a = jnp.sin(fr_ref[:, 1:2] * (hdot(w2_ref[...], a) + b2_ref[...]))
    a = jnp.sin(fr_ref[:, 2:3] * (hdot(w3_ref[...], a) + b3_ref[...]))

    @pl.when(i == 0)
    def _():
        ss_ref[...] = jnp.zeros_like(ss_ref)

    nout = w4_ref.shape[0]
    for cb in range(nout // HY_WIDTH):
        rows = slice(cb * HY_WIDTH, (cb + 1) * HY_WIDTH)
        filt = hdot(w4_ref[rows, :], a) * jnp.exp(-dl_ref[rows, :] * t)
        for j in range(pb // LANES):
            f_ref[rows, j, :] = filt[:, j * LANES:(j + 1) * LANES]
        ss_ref[rows, :] += jnp.sum(filt * filt, axis=1, keepdims=True)


def _hyena_filters_raw(length, w1, b1, w2, b2, w3, b3, w4, freq):
    nout = w4.shape[1]
    pb = min(length, 1024)
    max_decay = math.log(HY_TARGET) / HY_SHORT_DECAY_PCT
    min_decay = math.log(HY_TARGET) / HY_LONG_DECAY_PCT
    deltas = jnp.abs(jnp.linspace(min_decay, max_decay, HY_WIDTH, dtype=F32))
    dl = jnp.tile(deltas, nout // HY_WIDTH).reshape(nout, 1)
    pad2 = lambda a, rows, cols: jnp.pad(a, ((0, rows - a.shape[0]), (0, cols - a.shape[1])))
    w1p = pad2(w1.T, LANES, LANES)
    w2p = pad2(w2.T, LANES, LANES)
    w3p = pad2(w3.T, LANES, LANES)
    w4p = pad2(w4.T, nout, LANES)
    b1p, b2p, b3p = (pad2(b.reshape(-1, 1), LANES, 1) for b in (b1, b2, b3))
    frp = pad2(freq.T, LANES, 3)
    full = lambda a: pl.BlockSpec(a.shape, lambda i: tuple(0 for _ in a.shape))
    args = (w1p, b1p, w2p, b2p, w3p, b3p, w4p, frp, dl)
    return pl.pallas_call(
        functools.partial(_filter_kernel, length=length, pb=pb),
        out_shape=(jax.ShapeDtypeStruct((nout, length // LANES, LANES), F32),
                   jax.ShapeDtypeStruct((nout, 1), F32)),
        grid=(length // pb,),
        in_specs=[full(a) for a in args],
        out_specs=(pl.BlockSpec((nout, pb // LANES, LANES), lambda i: (0, i, 0)),
                   pl.BlockSpec((nout, 1), lambda i: (0, 0))),
        compiler_params=_cparams("arbitrary"),
        name="hyena_filter_mlp",
    )(*args)


def _dft_tables(na):
    k1n = 2 * na
    n = k1n * FFT_N2
    n1 = np.arange(na)[:, None]
    k1 = np.arange(k1n)[None, :]
    ang = 2.0 * np.pi * ((n1 * k1) % k1n) / k1n
    c, s = np.cos(ang), np.sin(ang)
    ma = np.block([[c, -s], [s, c]])
    n1f = np.arange(FFT_N2)[:, None]
    angf = 2.0 * np.pi * ((n1f * k1) % k1n) / k1n
    live = (n1f < na).astype(np.float64)
    ma_f = np.concatenate([np.cos(angf), -np.sin(angf)], axis=1) * live
    ma_b = np.concatenate([np.cos(angf), np.sin(angf)], axis=1) * live
    n2 = np.arange(FFT_N2)[:, None]
    angt = 2.0 * np.pi * ((n2 * k1) % n) / n
    twr, twi = np.cos(angt), -np.sin(angt)
    k2 = np.arange(FFT_N2)[None, :]
    angb = 2.0 * np.pi * ((n2 * k2) % FFT_N2) / FFT_N2
    cb, sb = np.cos(angb), np.sin(angb)
    mb = np.block([[cb, -sb], [sb, cb]])
    mc = np.block([[cb, sb], [-sb, cb]])
    angd = 2.0 * np.pi * ((np.arange(k1n)[:, None] * np.arange(na)[None, :]) % k1n) / k1n
    cd, sd = np.cos(angd) / n, np.sin(angd) / n
    md = np.block([[cd, sd], [-sd, cd]])
    as_bf = lambda a: jnp.asarray(a, dtype=F32).astype(BF16)
    as_f = lambda a: jnp.asarray(a, dtype=F32)
    mb2 = np.concatenate([mb, mc], axis=0)
    return dict(ma=as_bf(ma), ma_f=as_bf(ma_f), ma_b=as_bf(ma_b), twr=as_f(twr), twi=as_f(twi),
                twr_t=as_f(twr.T), twi_t=as_f(twi.T), mb=as_bf(mb), mb2=as_bf(mb2), mc=as_bf(mc), md=as_bf(md))


def _fwd_stages(x2, ma, twr, twi, mb, cb, k1n):
    a = _dot(x2, ma).reshape(cb, FFT_N2, 2 * k1n)
    ar, ai = a[..., :k1n], a[..., k1n:]
    ar2 = ar * twr - ai * twi
    ai2 = ar * twi + ai * twr
    xt = jnp.concatenate([jnp.swapaxes(ar2, 1, 2), jnp.swapaxes(ai2, 1, 2)], axis=-1)
    return _dot(xt.astype(BF16).reshape(cb * k1n, 2 * FFT_N2), mb).reshape(cb, k1n, 2 * FFT_N2)


def _spectrum_kernel(hf_ref, hb_ref, ssf_ref, ssb_ref, maf_ref, mab_ref, twr_ref, twi_ref, mb2_ref, h_ref, *, na):
    k1n = 2 * na
    cb = h_ref.shape[0]

    def stage_a(ref, ss_ref, m_ref):
        x = ref[0, 0] * lax.rsqrt(ss_ref[0, 0])
        x = jnp.concatenate([x, jnp.zeros((cb, FFT_N2 - na, FFT_N2), F32)], axis=1)
        xt = jnp.swapaxes(x, 1, 2).astype(BF16).reshape(cb * FFT_N2, FFT_N2)
        a = _dot(xt, m_ref[...]).reshape(cb, FFT_N2, 2 * k1n)
        return a[..., :k1n], a[..., k1n:]

    fr, fi = stage_a(hf_ref, ssf_ref, maf_ref)
    br, bi = stage_a(hb_ref, ssb_ref, mab_ref)
    twr, twi = twr_ref[...], twi_ref[...]
    parts = (fr * twr - fi * twi, fr * twi + fi * twr, br * twr + bi * twi, bi * twr - br * twi)
    xt = jnp.concatenate([jnp.swapaxes(p, 1, 2) for p in parts], axis=-1)
    h = _dot(xt.astype(BF16).reshape(cb * k1n, 4 * FFT_N2), mb2_ref[...])
    h_ref[...] = h.reshape(cb, k1n, 2 * FFT_N2).astype(BF16)


def _spectrum(filt5, ss5, order, tb, na):
    c = filt5.shape[2]
    k1n = 2 * na
    cb = FFT_CB
    full = lambda a: pl.BlockSpec(a.shape, lambda i: tuple(0 for _ in a.shape))
    fblk = lambda d: pl.BlockSpec((1, 1, cb, na, FFT_N2), lambda i: (order, d, i, 0, 0))
    sblk = lambda d: pl.BlockSpec((1, 1, cb, 1, 1), lambda i: (order, d, i, 0, 0))
    return pl.pallas_call(
        functools.partial(_spectrum_kernel, na=na),
        out_shape=jax.ShapeDtypeStruct((c, k1n, 2 * FFT_N2), BF16),
        grid=(c // cb,),
        in_specs=[fblk(0), fblk(1), sblk(0), sblk(1),
                  full(tb["ma_f"]), full(tb["ma_b"]), full(tb["twr"]), full(tb["twi"]), full(tb["mb2"])],
        out_specs=pl.BlockSpec((cb, k1n, 2 * FFT_N2), lambda i: (i, 0, 0)),
        compiler_params=_cparams("arbitrary"),
        name="hyena_filter_spectrum",
    )(filt5, filt5, ss5, ss5, tb["ma_f"], tb["ma_b"], tb["twr"], tb["twi"], tb["mb2"])


def _fftconv_kernel(u_ref, g_ref, h_ref, sk_ref, ma_ref, twr_ref, twi_ref, mb_ref, mc_ref,
                    twrt_ref, twit_ref, md_ref, o_ref, *, na):
    k1n = 2 * na
    cb = u_ref.shape[0]
    u = u_ref[...]
    x = _fwd_stages(u.reshape(cb * FFT_N2, k1n), ma_ref[...], twr_ref[...], twi_ref[...], mb_ref[...], cb, k1n)
    xr, xi = x[..., :FFT_N2], x[..., FFT_N2:]
    h = h_ref[...].astype(F32)
    hr, hi = h[..., :FFT_N2], h[..., FFT_N2:]
    y = jnp.concatenate([xr * hr - xi * hi, xr * hi + xi * hr], axis=-1)
    c = _dot(y.astype(BF16).reshape(cb * k1n, 2 * FFT_N2), mc_ref[...]).reshape(cb, k1n, 2 * FFT_N2)
    cr, ci = c[..., :FFT_N2], c[..., FFT_N2:]
    twrt, twit = twrt_ref[...], twit_ref[...]
    cr2 = cr * twrt + ci * twit
    ci2 = ci * twrt - cr * twit
    ct = jnp.concatenate([jnp.swapaxes(cr2, 1, 2), jnp.swapaxes(ci2, 1, 2)], axis=-1)
    d = _dot(ct.astype(BF16).reshape(cb * FFT_N2, 2 * k1n), md_ref[...]).reshape(cb, FFT_N2, k1n)
    uf = u.astype(F32)
    o_ref[...] = (g_ref[...].astype(F32) * (d + sk_ref[...] * uf)).astype(BF16)


def _fftconv(u, gate, h, skip, tb, na):
    c = u.shape[0]
    k1n = 2 * na
    cb = FFT_CB
    full = lambda a: pl.BlockSpec(a.shape, lambda i: tuple(0 for _ in a.shape))
    blk = pl.BlockSpec((cb, FFT_N2, k1n), lambda i: (i, 0, 0))
    return pl.pallas_call(
        functools.partial(_fftconv_kernel, na=na),
        out_shape=jax.ShapeDtypeStruct(u.shape, BF16),
        grid=(c // cb,),
        in_specs=[blk, blk,
                  pl.BlockSpec((cb, k1n, 2 * FFT_N2), lambda i: (i, 0, 0)),
                  pl.BlockSpec((cb, 1, 1), lambda i: (i, 0, 0)),
                  full(tb["ma"]), full(tb["twr"]), full(tb["twi"]), full(tb["mb"]), full(tb["mc"]),
                  full(tb["twr_t"]), full(tb["twi_t"]), full(tb["md"])],
        out_specs=blk,
        compiler_params=_cparams("arbitrary"),
        name="hyena_fftconv",
    )(u, gate, h, skip, tb["ma"], tb["twr"], tb["twi"], tb["mb"], tb["mc"], tb["twr_t"], tb["twi_t"], tb["md"])


def _to_freq_layout(a, na):
    nb, _, c = a.shape
    a = a.reshape(nb, na, FFT_N2, c)
    return jnp.transpose(a, (3, 2, 0, 1)).reshape(c, FFT_N2, nb * na)


def _from_freq_layout(a):
    c, _, lanes = a.shape
    na = lanes // 2
    a = a.reshape(c, FFT_N2, 2, na)
    return jnp.transpose(a, (2, 3, 1, 0)).reshape(2, na * FFT_N2, c)


def _ctx_dft_tables(n):
    big = 2 * n
    j = np.arange(n)[:, None]
    k = np.arange(big)[None, :]
    ang = 2.0 * np.pi * ((j * k) % big) / big
    c, s = np.cos(ang), np.sin(ang)
    fh = np.block([[c, -s], [c, s]])
    ff = np.block([[c, -s], [s, c]])
    fi = np.block([[c.T, s.T], [-s.T, c.T]]) / big
    as_bf = lambda a: jnp.asarray(a, dtype=F32).astype(BF16)
    return as_bf(fh), as_bf(ff), as_bf(fi)


def _ctxconv_kernel(u_ref, hf_ref, hb_ref, ssf_ref, ssb_ref, sk_ref, fh_ref, ff_ref, fi_ref, o_ref):
    z = u_ref[2].astype(F32)
    half = fh_ref.shape[1] // 2
    for n in range(2):
        taps = jnp.concatenate([hf_ref[n] * lax.rsqrt(ssf_ref[n]), hb_ref[n] * lax.rsqrt(ssb_ref[n])], axis=-1)
        h = _dot(taps.astype(BF16), fh_ref[...])
        x = _dot(z.astype(BF16), ff_ref[...])
        hr, hi = h[:, :half], h[:, half:]
        xr, xi = x[:, :half], x[:, half:]
        y = jnp.concatenate([xr * hr - xi * hi, xr * hi + xi * hr], axis=-1)
        conv = _dot(y.astype(BF16), fi_ref[...])
        z = u_ref[n].astype(F32) * (conv + sk_ref[n] * z)
    o_ref[...] = z.astype(BF16)


def _ctxconv(u3, hf, hb, ssf, ssb, skip, tables):
    _, c, lanes = u3.shape
    args = (u3, hf, hb, ssf, ssb, skip) + tuple(tables)
    full = lambda a: pl.BlockSpec(a.shape, lambda i: tuple(0 for _ in a.shape))
    return pl.pallas_call(
        _ctxconv_kernel,
        out_shape=jax.ShapeDtypeStruct((c, lanes), BF16),
        grid=(1,),
        in_specs=[full(a) for a in args],
        out_specs=pl.BlockSpec((c, lanes), lambda i: (0, 0)),
        compiler_params=_cparams("arbitrary"),
        name="hyena_ctxconv",
    )(*args)


ROW_SUB = 8


def _load_row_tiles(ref):
    return jnp.concatenate([ref[:, j, :] for j in range(ROW_SUB)], axis=-1)


def _store_row_tiles(ref, val):
    for j in range(ROW_SUB):
        ref[:, j, :] = val[:, j * LANES:(j + 1) * LANES]


def _post_norm_mod(x, y, m_ref, lng, lnb, gate_row, sh_row, sc_row):
    u = DN_ALPHA * x + m_ref[0, gate_row:gate_row + 1, :] * y
    mu = jnp.mean(u, axis=-1, keepdims=True)
    var = jnp.mean(jnp.square(u - mu), axis=-1, keepdims=True)
    xn = (u - mu) * lax.rsqrt(var + LN_EPS) * lng + lnb
    if sh_row is None:
        return xn, None
    return xn, xn * (1.0 + m_ref[0, sc_row:sc_row + 1, :]) + m_ref[0, sh_row:sh_row + 1, :]


def _outproj0_kernel(of_ref, ob_ref, g_ref, hy_ref, x_ref, m_ref, w_ref, lng_ref, lnb_ref, xo_ref, h2_ref):
    o = of_ref[...] + ob_ref[...]
    g = g_ref[...].astype(F32)
    parts = []
    for h in range(RET_HEADS):
        oh = o[:, h * RET_HD:(h + 1) * RET_HD]
        mu = jnp.mean(oh, axis=-1, keepdims=True)
        var = jnp.mean(jnp.square(oh - mu), axis=-1, keepdims=True)
        parts.append((oh - mu) * lax.rsqrt(var + LN_EPS))
    yret = (jnp.concatenate(parts, axis=-1) * _silu(g)).astype(BF16)
    wr = yret.shape[1]
    y = _dot(yret, w_ref[:wr, :]) + _dot(hy_ref[...], w_ref[wr:, :])
    xn, h2 = _post_norm_mod(x_ref[...], y, m_ref, lng_ref[...], lnb_ref[...], 2, 3, 4)
    xo_ref[...] = xn
    _store_row_tiles(h2_ref, h2)


def _outproj1_kernel(of_ref, ob_ref, g_ref, x_ref, m_ref, w_ref, lng_ref, lnb_ref, xo_ref, h2_ref):
    o = of_ref[...] + ob_ref[...]
    g = g_ref[...].astype(F32)
    parts = []
    for h in range(GLA_HEADS):
        oh = o[:, h * GLA_DV:(h + 1) * GLA_DV]
        parts.append(oh * lax.rsqrt(jnp.mean(jnp.square(oh), axis=-1, keepdims=True) + LN_EPS))
    a = (jnp.concatenate(parts, axis=-1) * _silu(g)).astype(BF16)
    y = _dot(a, w_ref[...])
    xn, h2 = _post_norm_mod(x_ref[...], y, m_ref, lng_ref[...], lnb_ref[...], 2, 3, 4)
    xo_ref[...] = xn
    _store_row_tiles(h2_ref, h2)


def _top2_rows(vals):
    n = len(vals)
    best_v, best_i = vals[0], jnp.zeros_like(vals[0], dtype=jnp.int32)
    for e in range(1, n):
        take = vals[e] > best_v
        best_v = jnp.where(take, vals[e], best_v)
        best_i = jnp.where(take, e, best_i)
    sec_v = jnp.full_like(vals[0], -jnp.inf)
    sec_i = jnp.zeros_like(best_i)
    for e in range(n):
        take = (best_i != e) & (vals[e] > sec_v)
        sec_v = jnp.where(take, vals[e], sec_v)
        sec_i = jnp.where(take, e, sec_i)
    return best_v, best_i, sec_v, sec_i


def _router_kernel(h_ref, rw_ref, rb_ref, mi_ref, mf_ref, cnt_ref, carry_ref):
    i = pl.program_id(0)

    @pl.when(i == 0)
    def _():
        carry_ref[...] = jnp.zeros_like(carry_ref)

    logits = lax.dot_general(rw_ref[...], _load_row_tiles(h_ref), (((1,), (1,)), ((), ())),
                             preferred_element_type=F32, precision=HIGHEST)
    s = 1.0 / (1.0 + jnp.exp(-logits))
    sel = s + rb_ref[...]
    srow = [s[e:e + 1, :] for e in range(N_EXPERTS)]
    selrow = [sel[e:e + 1, :] for e in range(N_EXPERTS)]
    gscore = []
    for g in range(N_GROUPS):
        grp = selrow[g * EXPERTS_PER_GROUP:(g + 1) * EXPERTS_PER_GROUP]
        bv, _, sv, _ = _top2_rows(grp)
        gscore.append(bv + sv)
    best_g = jnp.zeros_like(gscore[0], dtype=jnp.int32)
    best_s = gscore[0]
    for g in range(1, N_GROUPS):
        take = gscore[g] > best_s
        best_s = jnp.where(take, gscore[g], best_s)
        best_g = jnp.where(take, g, best_g)
    cand_sel, cand_s = [], []
    for j in range(EXPERTS_PER_GROUP):
        cs, ca = selrow[j], srow[j]
        for g in range(1, N_GROUPS):
            cs = jnp.where(best_g == g, selrow[g * EXPERTS_PER_GROUP + j], cs)
            ca = jnp.where(best_g == g, srow[g * EXPERTS_PER_GROUP + j], ca)
        cand_sel.append(cs)
        cand_s.append(ca)
    _, i1, _, i2 = _top2_rows(cand_sel)
    a1, a2 = cand_s[0], cand_s[0]
    for j in range(1, EXPERTS_PER_GROUP):
        a1 = jnp.where(i1 == j, cand_s[j], a1)
        a2 = jnp.where(i2 == j, cand_s[j], a2)
    den = a1 + a2
    w1, w2 = a1 / den, a2 / den
    e1 = best_g * EXPERTS_PER_GROUP + i1
    e2 = best_g * EXPERTS_PER_GROUP + i2
    eidx = lax.broadcasted_iota(jnp.int32, (N_EXPERTS, TM), 0)
    oh1 = (eidx == e1)
    oh2 = (eidx == e2)
    onehot = jnp.where(oh1 | oh2, 1.0, 0.0)
    tri = jnp.where(lax.broadcasted_iota(jnp.int32, (TM, TM), 0) < lax.broadcasted_iota(jnp.int32, (TM, TM), 1),
                    1.0, 0.0).astype(BF16)
    rank = _dot(onehot.astype(BF16), tri) + carry_ref[:, 0:1]
    r1 = jnp.sum(jnp.where(oh1, rank, 0.0), axis=0, keepdims=True)
    r2 = jnp.sum(jnp.where(oh2, rank, 0.0), axis=0, keepdims=True)
    carry_ref[...] = carry_ref[...] + jnp.sum(onehot, axis=1, keepdims=True)
    zi = jnp.zeros((4, TM), jnp.int32)
    mi_ref[...] = jnp.concatenate([e1, e2, r1.astype(jnp.int32), r2.astype(jnp.int32), zi], axis=0)
    mf_ref[...] = jnp.concatenate([w1, w2, jnp.zeros((6, TM), F32)], axis=0)
    cnt_ref[...] = carry_ref[...]


def _router(h2, router_w, router_bias):
    r = h2.shape[0]
    d = router_w.shape[0]
    return pl.pallas_call(
        _router_kernel,
        out_shape=(jax.ShapeDtypeStruct((8, r), jnp.int32), jax.ShapeDtypeStruct((8, r), F32),
                   jax.ShapeDtypeStruct((N_EXPERTS, LANES), F32)),
        grid=(r // TM,),
        in_specs=[pl.BlockSpec((TM, ROW_SUB, LANES), lambda i: (i, 0, 0)),
                  pl.BlockSpec((N_EXPERTS, d), lambda i: (0, 0)),
                  pl.BlockSpec((N_EXPERTS, 1), lambda i: (0, 0))],
        out_specs=(pl.BlockSpec((8, TM), lambda i: (0, i)), pl.BlockSpec((8, TM), lambda i: (0, i)),
                   pl.BlockSpec((N_EXPERTS, LANES), lambda i: (0, 0))),
        scratch_shapes=[pltpu.VMEM((N_EXPERTS, LANES), F32)],
        compiler_params=_cparams("arbitrary"),
        name="moe_router",
    )(h2, router_w.T, router_bias.reshape(N_EXPERTS, 1))


def _dispatch_kernel(slots_ref, zoff_ref, h_ref, xs_ref, zero_ref, sem):
    i = pl.program_id(0)

    @pl.when(i == 0)
    def _():
        zero_ref[...] = jnp.zeros_like(zero_ref)
        for e in range(N_EXPERTS):
            pltpu.make_async_copy(zero_ref, xs_ref.at[pl.ds(zoff_ref[e], TE)], sem).start()
        for e in range(N_EXPERTS):
            pltpu.make_async_copy(zero_ref, xs_ref.at[pl.ds(zoff_ref[e], TE)], sem).wait()
        first_free = zoff_ref[N_EXPERTS]
        n_all = xs_ref.shape[0] // TE

        def fill(t, carry):
            pltpu.make_async_copy(zero_ref, xs_ref.at[pl.ds(t * TE, TE)], sem).start()
            return carry

        def fill_wait(t, carry):
            pltpu.make_async_copy(zero_ref, xs_ref.at[pl.ds(t * TE, TE)], sem).wait()
            return carry

        lax.fori_loop(first_free, n_all, fill, 0)
        lax.fori_loop(first_free, n_all, fill_wait, 0)

    def row_copy(r, k):
        return pltpu.make_async_copy(h_ref.at[pl.ds(r, 1)], xs_ref.at[pl.ds(slots_ref[0, k, r], 1)], sem)

    def issue(r, carry):
        for k in range(2):
            row_copy(r, k).start()
        return carry

    lax.fori_loop(0, TM, issue, 0, unroll=8)

    def drain(r, carry):
        for k in range(2):
            row_copy(r, k).wait()
        return carry

    lax.fori_loop(0, TM, drain, 0, unroll=8)


def _dispatch(h2, slots3, zoff, p_rows):
    r = h2.shape[0]
    grid_spec = pltpu.PrefetchScalarGridSpec(
        num_scalar_prefetch=0,
        grid=(r // TM,),
        in_specs=[pl.BlockSpec((1, 2, TM), lambda i: (i, 0, 0), memory_space=pltpu.SMEM),
                  pl.BlockSpec(memory_space=pltpu.SMEM),
                  pl.BlockSpec((TM, ROW_SUB, LANES), lambda i: (i, 0, 0))],
        out_specs=pl.BlockSpec(memory_space=pl.ANY),
        scratch_shapes=[pltpu.VMEM((TE, ROW_SUB, LANES), F32), pltpu.SemaphoreType.DMA(())],
    )
    return pl.pallas_call(
        _dispatch_kernel,
        out_shape=jax.ShapeDtypeStruct((p_rows, ROW_SUB, LANES), F32),
        grid_spec=grid_spec,
        compiler_params=_cparams("arbitrary"),
        name="moe_dispatch",
    )(slots3, zoff, h2)


def _experts_kernel(te_ref, tv_ref, x_ref, wg_ref, wu_ref, wd_ref, y_ref):
    j = pl.program_id(0)

    @pl.when(tv_ref[j] > 0)
    def _():
        x = _load_row_tiles(x_ref).astype(BF16)
        g = _dot(x, wg_ref[0])
        u = _dot(x, wu_ref[0])
        _store_row_tiles(y_ref, _dot((_silu(g) * u).astype(BF16), wd_ref[0]))

    @pl.when(tv_ref[j] == 0)
    def _():
        y_ref[...] = jnp.zeros_like(y_ref)


def _experts(xs, tile_expert, tile_valid, wg, wu, wd):
    p = xs.shape[0]
    d, hdim = wg.shape[1], wg.shape[2]
    grid_spec = pltpu.PrefetchScalarGridSpec(
        num_scalar_prefetch=2,
        grid=(p // TE,),
        in_specs=[pl.BlockSpec((TE, ROW_SUB, LANES), lambda j, te, tv: (j, 0, 0)),
                  pl.BlockSpec((1, d, hdim), lambda j, te, tv: (te[j], 0, 0)),
                  pl.BlockSpec((1, d, hdim), lambda j, te, tv: (te[j], 0, 0)),
                  pl.BlockSpec((1, hdim, d), lambda j, te, tv: (te[j], 0, 0))],
        out_specs=pl.BlockSpec((TE, ROW_SUB, LANES), lambda j, te, tv: (j, 0, 0)),
    )
    return pl.pallas_call(
        _experts_kernel,
        out_shape=jax.ShapeDtypeStruct((p, ROW_SUB, LANES), F32),
        grid_spec=grid_spec,
        compiler_params=_cparams("arbitrary"),
        name="moe_experts",
    )(tile_expert, tile_valid, xs, wg, wu, wd)


def _combine_kernel(slots_ref, ys_ref, wts_ref, x_ref, m_ref, lng_ref, lnb_ref, o_ref, b1_ref, b2_ref, sem):
    bufs = (b1_ref, b2_ref)

    def row_copy(r, k):
        return pltpu.make_async_copy(ys_ref.at[pl.ds(slots_ref[0, k, r], 1)], bufs[k].at[pl.ds(r, 1)], sem)

    def issue(r, carry):
        for k in range(2):
            row_copy(r, k).start()
        return carry

    lax.fori_loop(0, TM, issue, 0, unroll=8)

    def drain(r, carry):
        for k in range(2):
            row_copy(r, k).wait()
        return carry

    lax.fori_loop(0, TM, drain, 0, unroll=8)
    y = wts_ref[:, 0:1] * _load_row_tiles(b1_ref) + wts_ref[:, 1:2] * _load_row_tiles(b2_ref)
    xn, _ = _post_norm_mod(x_ref[...], y, m_ref, lng_ref[...], lnb_ref[...], 5, None, None)
    o_ref[...] = xn


def _combine(ys, slots3, wts, x, mods, lng, lnb, row_tile_of, out_tile_of, cond_of, n_tiles, out_rows):
    d = x.shape[1]
    grid_spec = pltpu.PrefetchScalarGridSpec(
        num_scalar_prefetch=0,
        grid=(n_tiles,),
        in_specs=[pl.BlockSpec((1, 2, TM), lambda i: (i, 0, 0), memory_space=pltpu.SMEM),
                  pl.BlockSpec(memory_space=pl.ANY),
                  pl.BlockSpec((TM, 2), lambda i: (i, 0)),
                  pl.BlockSpec((TM, d), lambda i: (row_tile_of(i), 0)),
                  pl.BlockSpec((1, 6, d), lambda i: (cond_of(i), 0, 0)),
                  pl.BlockSpec((1, d), lambda i: (0, 0)),
                  pl.BlockSpec((1, d), lambda i: (0, 0))],
        out_specs=pl.BlockSpec((TM, d), lambda i: (out_tile_of(i), 0)),
        scratch_shapes=[pltpu.VMEM((TM, ROW_SUB, LANES), F32), pltpu.VMEM((TM, ROW_SUB, LANES), F32),
                        pltpu.SemaphoreType.DMA(())],
    )
    return pl.pallas_call(
        _combine_kernel,
        out_shape=jax.ShapeDtypeStruct((out_rows, d), F32),
        grid_spec=grid_spec,
        compiler_params=_cparams("arbitrary"),
        name="moe_combine",
    )(slots3, ys, wts, x, mods, lng, lnb)


def _moe(h2, x, x_tile_of, out_tile_of, cond_of, out_rows, mods, lng, lnb, router_w, router_bias, wg, wu, wd):
    t = h2.shape[0]
    n_tiles = t // TM
    mi, mf, cnt = _router(h2, router_w, router_bias)
    counts = cnt[:, 0].astype(jnp.int32)
    padded = ((counts + TE - 1) // TE) * TE
    ends = jnp.cumsum(padded)
    offs = ends - padded
    e12 = mi[0:2]
    eids = jnp.arange(N_EXPERTS, dtype=jnp.int32)[:, None, None]
    slots = jnp.sum(jnp.where(e12[None] == eids, offs[:, None, None], 0), axis=0) + mi[2:4]
    slots3 = jnp.transpose(slots.reshape(2, n_tiles, TM), (1, 0, 2))
    n_exp_tiles = (2 * t) // TE + N_EXPERTS
    p_rows = n_exp_tiles * TE
    total_tiles = ends[-1] // TE
    tile_valid = (jnp.arange(n_exp_tiles) < total_tiles).astype(jnp.int32)
    tile_block = jnp.minimum(jnp.arange(n_exp_tiles), total_tiles - 1).astype(jnp.int32)
    tile_expert = jnp.sum((tile_block[:, None] * TE >= ends[None, :]).astype(jnp.int32), axis=1).astype(jnp.int32)
    zoff = jnp.concatenate([jnp.maximum(ends - TE, 0), total_tiles[None]]).astype(jnp.int32)
    xs = _dispatch(h2, slots3, zoff, p_rows)
    ys = _experts(xs, tile_expert, tile_valid, wg, wu, wd)
    wts = jnp.transpose(mf[0:2])
    return _combine(ys, slots3, wts, x, mods, lng, lnb, x_tile_of, out_tile_of, cond_of, n_tiles, out_rows)


def _gla_kernel(q_ref, k_ref, v_ref, lr_ref, gw_ref, gb_ref, o_ref, s_ref, *, reverse, nb):
    C = GLA_CHUNK
    nchunk = TM // C

    @pl.when(pl.program_id(0) == 0)
    def _():
        s_ref[...] = jnp.zeros_like(s_ref)

    ii = lax.broadcasted_iota(jnp.int32, (C, C), 0)
    jj = lax.broadcasted_iota(jnp.int32, (C, C), 1)
    keep = (jj >= ii) if reverse else (jj <= ii)
    tri = jnp.where(keep, 1.0, 0.0)
    order = tuple(reversed(range(nchunk))) if reverse else tuple(range(nchunk))
    end_row = 0 if reverse else C - 1
    gw = gw_ref[...]
    gb = gb_ref[...]
    for b in range(nb):
        pre = _dot(lr_ref[b], gw) + gb
        la_all = (jnp.minimum(pre, 0.0) - jnp.log(1.0 + jnp.exp(-jnp.abs(pre)))) * (1.0 / GLA_TAU)
        for c in order:
            rows = slice(c * C, (c + 1) * C)
            la = la_all[rows, :]
            bc = jnp.dot(tri, la, preferred_element_type=F32, precision=HIGHEST)
            bend = bc[end_row:end_row + 1, :]
            eb = jnp.exp(bc)
            enb = jnp.exp(-bc)
            ekb = jnp.exp(bend - bc)
            ebend = jnp.exp(bend)
            for h in range(GLA_HEADS):
                kc = slice(h * GLA_DK, (h + 1) * GLA_DK)
                vc = slice(h * GLA_DV, (h + 1) * GLA_DV)
                q = q_ref[b, rows, kc].astype(F32) * (GLA_DK ** -0.5)
                k = k_ref[b, rows, kc].astype(F32)
                v = v_ref[b, rows, vc]
                qd = (q * eb[:, kc]).astype(BF16)
                kd = (k * enb[:, kc]).astype(BF16)
                st = s_ref[b, h]
                scores = jnp.where(keep, _dot_nt(qd, kd), 0.0)
                o = _dot(scores.astype(BF16), v) + _dot_nt(qd, st.astype(BF16))
                s_ref[b, h] = st * ebend[:, kc] + _dot_tn(v, (k * ekb[:, kc]).astype(BF16))
                o_ref[b, rows, vc] = o


def _gla(z3, gw, gb, reverse):
    nb, s, _ = z3.shape
    nt = s // TM
    kw = GLA_HEADS * GLA_DK
    vw = GLA_HEADS * GLA_DV

    if reverse:
        def jmap(j):
            return jnp.where(j == 0, nt - 1, nt - 1 - j)
    else:
        def jmap(j):
            return (j + nt - 1) % nt

    lr_blk = (2 * kw + 2 * vw) // LANES
    return pl.pallas_call(
        functools.partial(_gla_kernel, reverse=reverse, nb=nb),
        out_shape=jax.ShapeDtypeStruct((nb, s, vw), F32),
        grid=(nt,),
        in_specs=[pl.BlockSpec((nb, TM, kw), lambda j: (0, jmap(j), 0)),
                  pl.BlockSpec((nb, TM, kw), lambda j: (0, jmap(j), 1)),
                  pl.BlockSpec((nb, TM, vw), lambda j: (0, jmap(j), 1)),
                  pl.BlockSpec((nb, TM, LANES), lambda j: (0, jmap(j), lr_blk)),
                  pl.BlockSpec((LANES, kw), lambda j: (0, 0)),
                  pl.BlockSpec((1, kw), lambda j: (0, 0))],
        out_specs=pl.BlockSpec((nb, TM, vw), lambda j: (0, jmap(j), 0)),
        scratch_shapes=[pltpu.VMEM((nb, GLA_HEADS, GLA_DV, GLA_DK), F32)],
        compiler_params=_cparams("arbitrary"),
        name="gla_bwd" if reverse else "gla_fwd",
    )(z3, z3, z3, z3, gw, gb)


def kernel(x, c, ctx, c_ctx, mod_w, mod_b, ln_g, ln_b, ab_w_in, ret_log_decay_f, ret_log_decay_b, hy_conv_w, hy_conv_b, hy_w1, hy_b1, hy_w2, hy_b2, hy_w3, hy_b3, hy_w4, hy_freq, hy_skip, ab_w_out, gla_w_in, gla_gate_w_f, gla_gate_b_f, gla_gate_w_b, gla_gate_b_b, gla_w_out, router_w, router_bias, exp_w_gate, exp_w_up, exp_w_down):
    nb, length, d = x.shape
    ctx_len = ctx.shape[1]
    assert nb == 2 and ctx_len == TM and length % TM == 0 and length % (FFT_N2 * 2) == 0
    s = length + ctx_len
    tpb = s // TM
    r = nb * s
    lat_tiles = tpb - 1

    cond = jnp.zeros((8, d), F32).at[0].set(c[0]).at[1].set(c_ctx).at[2].set(c[1]).at[3].set(c_ctx)
    mods = _ada(cond, mod_w, mod_b).reshape(DEPTH, 8, 6, d)

    def cond_of(i):
        return 2 * (i // tpb) + jnp.where(i % tpb == tpb - 1, 1, 0)

    xu = jnp.concatenate([x, ctx], axis=1).reshape(r, d)

    z0 = _inproj(xu, mods[0], ab_w_in[0].astype(BF16), tpb)
    z0_3 = z0.reshape(nb, s, -1)
    cosf, sinf = _rope_tables(length, ctx_len)
    o_f = _retention(z0_3, ret_log_decay_f[0], cosf, sinf, False)
    o_b = _retention(z0_3, ret_log_decay_b[0], cosf, sinf, True)

    ret_w = RET_HEADS * RET_HD
    hyu = _shortconv(z0, 4 * ret_w, 3 * HY_WIDTH, hy_conv_w[0], hy_conv_b[0], tpb)
    hyu3 = hyu.reshape(nb, s, 3 * HY_WIDTH)
    na = length // FFT_N2
    tb = _dft_tables(na)
    filt_w = (hy_w1[0], hy_b1[0], hy_w2[0], hy_b2[0], hy_w3[0], hy_b3[0], hy_w4[0], hy_freq[0])
    skip = hy_skip[0]

    filt, ss = _hyena_filters_raw(length, *filt_w)
    filt5 = filt.reshape(2, 2, HY_WIDTH, na, FFT_N2)
    ss5 = ss.reshape(2, 2, HY_WIDTH, 1, 1)
    seq = hyu3[:, :length]
    zcur = _to_freq_layout(seq[..., 2 * HY_WIDTH:], na)
    for n in range(2):
        hspec = _spectrum(filt5, ss5, n, tb, na)
        gate = _to_freq_layout(seq[..., n * HY_WIDTH:(n + 1) * HY_WIDTH], na)
        zcur = _fftconv(zcur, gate, hspec, skip[n].reshape(-1, 1, 1), tb, na)
    y_hy_lat = _from_freq_layout(zcur)

    filt_c, ss_c = _hyena_filters_raw(ctx_len, *filt_w)
    filt_c = filt_c.reshape(2, 2, HY_WIDTH, ctx_len)
    ss_c = ss_c.reshape(2, 2, HY_WIDTH, 1)
    u_ctx = jnp.transpose(hyu3[:, length:].reshape(nb, ctx_len, 3, HY_WIDTH), (2, 3, 0, 1))
    y_ctx_t = _ctxconv(u_ctx.reshape(3, HY_WIDTH, nb * ctx_len), filt_c[:, 0], filt_c[:, 1], ss_c[:, 0], ss_c[:, 1],
                       skip.reshape(2, HY_WIDTH, 1), _ctx_dft_tables(ctx_len))
    y_hy_ctx = jnp.transpose(y_ctx_t.reshape(HY_WIDTH, nb, ctx_len), (1, 2, 0))
    y_hy = jnp.concatenate([y_hy_lat, y_hy_ctx], axis=1).reshape(r, HY_WIDTH)

    row = lambda i: (i, 0)
    row3 = lambda i: (i, 0, 0)
    full2 = lambda shp: pl.BlockSpec(shp, lambda i: (0, 0))
    x_mid, h2 = pl.pallas_call(
        _outproj0_kernel,
        out_shape=(jax.ShapeDtypeStruct((r, d), F32), jax.ShapeDtypeStruct((r, ROW_SUB, LANES), F32)),
        grid=(r // TM,),
        in_specs=[pl.BlockSpec((TM, ret_w), row), pl.BlockSpec((TM, ret_w), row),
                  pl.BlockSpec((TM, ret_w), lambda i: (i, 3)),
                  pl.BlockSpec((TM, HY_WIDTH), row), pl.BlockSpec((TM, d), row),
                  pl.BlockSpec((1, 6, d), lambda i: (cond_of(i), 0, 0)),
                  full2((ret_w + HY_WIDTH, d)), full2((1, d)), full2((1, d))],
        out_specs=(pl.BlockSpec((TM, d), row), pl.BlockSpec((TM, ROW_SUB, LANES), row3)),
        compiler_params=_cparams("arbitrary"),
        name="outproj_norm0",
    )(o_f.reshape(r, ret_w), o_b.reshape(r, ret_w), z0, y_hy, xu, mods[0], ab_w_out[0].astype(BF16),
      ln_g[0, 0].reshape(1, d), ln_b[0, 0].reshape(1, d))

    x1 = _moe(h2, x_mid, lambda i: i, lambda i: i, cond_of, r, mods[0], ln_g[0, 1].reshape(1, d),
              ln_b[0, 1].reshape(1, d), router_w, router_bias,
              exp_w_gate[0].astype(BF16), exp_w_up[0].astype(BF16), exp_w_down[0].astype(BF16))

    kw = GLA_HEADS * GLA_DK
    vw = GLA_HEADS * GLA_DV
    n_in = gla_w_in.shape[2]
    n_pad = 2 * kw + 2 * vw + LANES
    w_in1 = jnp.pad(gla_w_in[0], ((0, 0), (0, n_pad - n_in))).astype(BF16)
    z1 = _inproj(x1, mods[1], w_in1, tpb)
    z1_3 = z1.reshape(nb, s, n_pad)
    gw_f = jnp.zeros((LANES, kw), F32).at[:GLA_RANK].set(gla_gate_w_f[0]).astype(BF16)
    gw_b = jnp.zeros((LANES, kw), F32).at[GLA_RANK:2 * GLA_RANK].set(gla_gate_w_b[0]).astype(BF16)
    g_f = _gla(z1_3, gw_f, gla_gate_b_f[0].reshape(1, kw), False)
    g_b = _gla(z1_3, gw_b, gla_gate_b_b[0].reshape(1, kw), True)

    n_lat = nb * lat_tiles

    def lat_tile(i):
        return (i // lat_tiles) * tpb + i % lat_tiles

    def lat_cond(i):
        return 2 * (i // lat_tiles)

    lrow = lambda i: (lat_tile(i), 0)
    x_mid1, h2_1 = pl.pallas_call(
        _outproj1_kernel,
        out_shape=(jax.ShapeDtypeStruct((n_lat * TM, d), F32),
                   jax.ShapeDtypeStruct((n_lat * TM, ROW_SUB, LANES), F32)),
        grid=(n_lat,),
        in_specs=[pl.BlockSpec((TM, vw), lrow), pl.BlockSpec((TM, vw), lrow),
                  pl.BlockSpec((TM, vw), lambda i: (lat_tile(i), 2)),
                  pl.BlockSpec((TM, d), lrow),
                  pl.BlockSpec((1, 6, d), lambda i: (lat_cond(i), 0, 0)),
                  full2((vw, d)), full2((1, d)), full2((1, d))],
        out_specs=(pl.BlockSpec((TM, d), row), pl.BlockSpec((TM, ROW_SUB, LANES), row3)),
        compiler_params=_cparams("arbitrary"),
        name="outproj_norm1",
    )(g_f.reshape(r, vw), g_b.reshape(r, vw), z1, x1, mods[1], gla_w_out[0].astype(BF16),
      ln_g[1, 0].reshape(1, d), ln_b[1, 0].reshape(1, d))

    out = _moe(h2_1, x_mid1, lambda i: i, lambda i: i, lat_cond, n_lat * TM, mods[1], ln_g[1, 1].reshape(1, d),
               ln_b[1, 1].reshape(1, d), router_w, router_bias,
               exp_w_gate[1].astype(BF16), exp_w_up[1].astype(BF16), exp_w_down[1].astype(BF16))
    return out.reshape(nb, length, d)
```

```python
import functools
import math

import numpy as np
import jax
import jax.numpy as jnp
from jax import lax
from jax.experimental import pallas as pl
from jax.experimental.pallas import tpu as pltpu

F32 = jnp.float32
BF16 = jnp.bfloat16

GRID_W = 64
RET_HEADS = 4
RET_HD = 128
RET_CHUNK = 128
ROPE_BASE = 10000.0
HY_WIDTH = 512
HY_EMB = 33
HY_FFN = 64
HY_SHORT_DECAY_PCT = 0.3
HY_LONG_DECAY_PCT = 1.5
HY_TARGET = 1e-2
GLA_HEADS = 4
GLA_DK = 128
GLA_DV = 256
GLA_RANK = 16
GLA_TAU = 16.0
GLA_CHUNK = 64
N_EXPERTS = 16
N_GROUPS = 4
EXPERTS_PER_GROUP = 4
LN_EPS = 1e-5
DEPTH = 2
DN_ALPHA = (2.0 * DEPTH) ** 0.25

PAIRS = ((0, 1), (0, 2), (0, 3), (1, 2), (1, 3), (2, 3))
N_CLASSES = N_GROUPS * len(PAIRS)
CLASS_ROWS = 32

LANES = 128
SUBLANES = 8
TM = 256
TE = 256
FFT_N2 = 128
FFT_CB = 8
VMEM_LIMIT = 48 * 1024 * 1024

HIGHEST = lax.Precision.HIGHEST


def _cparams(*sem):
    return pltpu.CompilerParams(dimension_semantics=sem, vmem_limit_bytes=VMEM_LIMIT)


def _silu(v):
    return v * (1.0 / (1.0 + jnp.exp(-v)))


def _dot(a, b):
    return jnp.dot(a, b, preferred_element_type=F32)


def _dot_nt(a, b):
    return lax.dot_general(a, b, (((1,), (1,)), ((), ())), preferred_element_type=F32)


def _dot_tn(a, b):
    return lax.dot_general(a, b, (((0,), (0,)), ((), ())), preferred_element_type=F32)


def _ada_kernel(c_ref, w_ref, b_ref, o_ref):
    a = _silu(c_ref[...])
    o_ref[0] = jnp.dot(a, w_ref[0], preferred_element_type=F32, precision=HIGHEST) + b_ref[0]


def _ada(cond, mod_w, mod_b):
    depth, d, n = mod_w.shape
    nt = n // 4
    return pl.pallas_call(
        _ada_kernel,
        out_shape=jax.ShapeDtypeStruct((depth, 8, n), F32),
        grid=(depth, n // nt),
        in_specs=[pl.BlockSpec((8, d), lambda l, j: (0, 0)),
                  pl.BlockSpec((1, d, nt), lambda l, j: (l, 0, j)),
                  pl.BlockSpec((1, 1, nt), lambda l, j: (l, 0, j))],
        out_specs=pl.BlockSpec((1, 8, nt), lambda l, j: (l, 0, j)),
        compiler_params=_cparams("arbitrary", "arbitrary"),
        name="ada_mod",
    )(cond, mod_w, mod_b.reshape(depth, 1, n))


def _inproj_kernel(x_ref, m_ref, w_ref, o_ref, *, chunks):
    sh = m_ref[0, 0:1, :]
    sc = m_ref[0, 1:2, :]
    h = (x_ref[...] * (1.0 + sc) + sh).astype(BF16)
    for lo, hi in chunks:
        o_ref[:, lo:hi] = _dot(h, w_ref[:, lo:hi]).astype(BF16)


def _col_chunks(n, width=512):
    out, lo = [], 0
    while lo < n:
        hi = min(lo + width, n)
        out.append((lo, hi))
        lo = hi
    return tuple(out)


def _inproj(xu, mods, w, tiles_per_batch):
    r, d = xu.shape
    n = w.shape[1]
    tpb = tiles_per_batch

    def cond_of(i):
        return 2 * (i // tpb) + jnp.where(i % tpb == tpb - 1, 1, 0)

    return pl.pallas_call(
        functools.partial(_inproj_kernel, chunks=_col_chunks(n)),
        out_shape=jax.ShapeDtypeStruct((r, n), BF16),
        grid=(r // TM,),
        in_specs=[pl.BlockSpec((TM, d), lambda i: (i, 0)),
                  pl.BlockSpec((1, 6, d), lambda i: (cond_of(i), 0, 0)),
                  pl.BlockSpec((d, n), lambda i: (0, 0))],
        out_specs=pl.BlockSpec((TM, n), lambda i: (i, 0)),
        compiler_params=_cparams("arbitrary"),
        name="inproj",
    )(xu, mods, w)


def _ret_kernel(lg_ref, q_ref, k_ref, v_ref, cos_ref, sin_ref, o_ref, s_ref, *, reverse, nb):
    C = RET_CHUNK

    @pl.when(pl.program_id(0) == 0)
    def _():
        s_ref[...] = jnp.zeros_like(s_ref)

    ii = lax.broadcasted_iota(jnp.int32, (C, C), 0).astype(F32)
    jj = lax.broadcasted_iota(jnp.int32, (C, C), 1).astype(F32)
    ci = lax.broadcasted_iota(jnp.int32, (C, 1), 0).astype(F32)
    diff = (jj - ii) if reverse else (ii - jj)
    order = (1, 0) if reverse else (0, 1)
    for h in range(RET_HEADS):
        lg = lg_ref[h]
        intra = jnp.where(diff >= 0, jnp.exp(jnp.maximum(diff, 0.0) * lg), 0.0)
        if reverse:
            q_dec = jnp.exp((C - ci) * lg)
            k_dec = jnp.exp(ci * lg)
        else:
            q_dec = jnp.exp((ci + 1.0) * lg)
            k_dec = jnp.exp((C - 1.0 - ci) * lg)
        c_dec = jnp.exp(jnp.zeros((1, RET_HD), F32) + C * lg)
        cols = slice(h * RET_HD, (h + 1) * RET_HD)
        for b in range(nb):
            for c in order:
                rows = slice(c * C, (c + 1) * C)
                cosf = cos_ref[rows, :]
                sinf = sin_ref[rows, :]
                q = q_ref[b, rows, cols].astype(F32)
                k = k_ref[b, rows, cols].astype(F32)
                v = v_ref[b, rows, cols]
                q = q * cosf + pltpu.roll(q, RET_HD // 2, 1) * sinf
                k = (k * cosf + pltpu.roll(k, RET_HD // 2, 1) * sinf) * (RET_HD ** -0.5)
                st = s_ref[b, h]
                scores = _dot_nt(q.astype(BF16), k.astype(BF16)) * intra
                o = _dot(scores.astype(BF16), v) + _dot_nt((q * q_dec).astype(BF16), st.astype(BF16))
                s_ref[b, h] = c_dec * st + _dot_tn(v, (k * k_dec).astype(BF16))
                o_ref[b, rows, cols] = o


def _retention(z3, col0, lg, cosf, sinf, reverse):
    nb, s, _ = z3.shape
    nt = s // TM
    w = RET_HEADS * RET_HD
    cb0 = col0 // w

    if reverse:
        def jmap(j):
            return jnp.where(j == 0, nt - 1, nt - 1 - j)
    else:
        def jmap(j):
            return (j + nt - 1) % nt

    grid_spec = pltpu.PrefetchScalarGridSpec(
        num_scalar_prefetch=0,
        grid=(nt,),
        in_specs=[pl.BlockSpec(memory_space=pltpu.SMEM),
                  pl.BlockSpec((nb, TM, w), lambda j: (0, jmap(j), cb0)),
                  pl.BlockSpec((nb, TM, w), lambda j: (0, jmap(j), cb0 + 1)),
                  pl.BlockSpec((nb, TM, w), lambda j: (0, jmap(j), cb0 + 2)),
                  pl.BlockSpec((TM, RET_HD), lambda j: (jmap(j), 0)),
                  pl.BlockSpec((TM, RET_HD), lambda j: (jmap(j), 0))],
        out_specs=pl.BlockSpec((nb, TM, w), lambda j: (0, jmap(j), 0)),
        scratch_shapes=[pltpu.VMEM((nb, RET_HEADS, RET_HD, RET_HD), F32)],
    )
    return pl.pallas_call(
        functools.partial(_ret_kernel, reverse=reverse, nb=nb),
        out_shape=jax.ShapeDtypeStruct((nb, s, w), F32),
        grid_spec=grid_spec,
        compiler_params=_cparams("arbitrary"),
        name="retention_bwd" if reverse else "retention_fwd",
    )(lg, z3, z3, z3, cosf, sinf)


def _rope_tables(length, ctx_len):
    rows = length // GRID_W
    quarter = RET_HD // 4
    inv = ROPE_BASE ** (-jnp.arange(quarter, dtype=F32) / quarter)
    r = jnp.repeat(jnp.arange(rows, dtype=F32), GRID_W)
    col = jnp.tile(jnp.arange(GRID_W, dtype=F32), rows)
    ang = jnp.concatenate([r[:, None] * inv, col[:, None] * inv], axis=-1)
    cos, sin = jnp.cos(ang), jnp.sin(ang)
    cosf = jnp.concatenate([cos, cos], axis=-1)
    sinf = jnp.concatenate([-sin, sin], axis=-1)
    cosf = jnp.concatenate([cosf, jnp.ones((ctx_len, RET_HD), F32)], axis=0)
    sinf = jnp.concatenate([sinf, jnp.zeros((ctx_len, RET_HD), F32)], axis=0)
    return cosf, sinf


def _shortconv_kernel(u_ref, up_ref, un_ref, w_ref, b_ref, o_ref, *, tpb):
    i = pl.program_id(0)
    r = i % tpb
    first = jnp.logical_or(r == 0, r == tpb - 1)
    last = r >= tpb - 2
    u = u_ref[...].astype(F32)
    prev_row = jnp.where(first, 0.0, up_ref[15:16, :].astype(F32))
    next_row = jnp.where(last, 0.0, un_ref[0:1, :].astype(F32))
    ridx = lax.broadcasted_iota(jnp.int32, u.shape, 0)
    um = jnp.where(ridx == 0, prev_row, pltpu.roll(u, 1, 0))
    up = jnp.where(ridx == TM - 1, next_row, pltpu.roll(u, TM - 1, 0))
    y = w_ref[0:1, :] * um + w_ref[1:2, :] * u + w_ref[2:3, :] * up + b_ref[...]
    o_ref[...] = y.astype(BF16)


def _shortconv(z, width, conv_w, conv_b, tpb):
    r = z.shape[0]
    hb = TM // 16
    nrb = r // 16
    return pl.pallas_call(
        functools.partial(_shortconv_kernel, tpb=tpb),
        out_shape=jax.ShapeDtypeStruct((r, width), BF16),
        grid=(r // TM,),
        in_specs=[pl.BlockSpec((TM, width), lambda i: (i, 0)),
                  pl.BlockSpec((16, width), lambda i: (jnp.maximum(i * hb - 1, 0), 0)),
                  pl.BlockSpec((16, width), lambda i: (jnp.minimum((i + 1) * hb, nrb - 1), 0)),
                  pl.BlockSpec((3, width), lambda i: (0, 0)),
                  pl.BlockSpec((1, width), lambda i: (0, 0))],
        out_specs=pl.BlockSpec((TM, width), lambda i: (i, 0)),
        compiler_params=_cparams("arbitrary"),
        name="hyena_shortconv",
    )(z, z, z, conv_w, conv_b.reshape(1, width))


FEAT_ROWS = 40


def _filter_kernel(w1_ref, b1_ref, w2_ref, b2_ref, w3_ref, b3_ref, w4_ref, fr_ref, dl_ref,
                   f_ref, ss_ref, *, length, pb):
    i = pl.program_id(0)
    pos = (lax.broadcasted_iota(jnp.int32, (1, pb), 1) + i * pb).astype(F32)
    t = pos * (1.0 / (length - 1))
    bands = (HY_EMB - 1) // 2
    w = (2.0 * math.pi) * pos / length
    sub = lax.broadcasted_iota(jnp.int32, (FEAT_ROWS, 1), 0)
    band = jnp.where(sub <= bands, sub - 1, sub - 1 - bands).astype(F32)
    f = 1e-4 + band * ((bands - 1 - 1e-4) / (bands - 1))
    fw = f * w
    feats = jnp.where(sub == 0, t, jnp.where(sub <= bands, jnp.cos(fw),
                                             jnp.where(sub <= 2 * bands, -jnp.sin(fw), 0.0)))
    feats = jnp.concatenate([feats, jnp.zeros((LANES - FEAT_ROWS, pb), F32)], axis=0)
    hdot = lambda a, b: jnp.dot(a, b, preferred_element_type=F32, precision=HIGHEST)
    a = jnp.sin(fr_ref[:, 0:1] * (hdot(w1_ref[...], feats) + b1_ref[...]))
    a = jnp.sin(fr_ref[:, 1:2] * (hdot(w2_ref[...], a) + b2_ref[...]))
    a = jnp.sin(fr_ref[:, 2:3] * (hdot(w3_ref[...], a) + b3_ref[...]))

    @pl.when(i == 0)
    def _():
        ss_ref[...] = jnp.zeros_like(ss_ref)

    nout = w4_ref.shape[0]
    for cb in range(nout // HY_WIDTH):
        rows = slice(cb * HY_WIDTH, (cb + 1) * HY_WIDTH)
        filt = hdot(w4_ref[rows, :], a) * jnp.exp(-dl_ref[rows, :] * t)
        for j in range(pb // LANES):
            f_ref[rows, j, :] = filt[:, j * LANES:(j + 1) * LANES]
        ss_ref[rows, :] += jnp.sum(filt * filt, axis=1, keepdims=True)


def _hyena_filters_raw(length, w1, b1, w2, b2, w3, b3, w4, freq):
    nout = w4.shape[1]
    pb = min(length, 1024)
    max_decay = math.log(HY_TARGET) / HY_SHORT_DECAY_PCT
    min_decay = math.log(HY_TARGET) / HY_LONG_DECAY_PCT
    deltas = jnp.abs(jnp.linspace(min_decay, max_decay, HY_WIDTH, dtype=F32))
    dl = jnp.tile(deltas, nout // HY_WIDTH).reshape(nout, 1)
    w1p = jnp.pad(w1.T, ((0, 0), (0, LANES - w1.shape[0])))
    col = lambda b: b.reshape(-1, 1)
    full = lambda a: pl.BlockSpec(a.shape, lambda i: tuple(0 for _ in a.shape))
    args = (w1p, col(b1), w2.T, col(b2), w3.T, col(b3), w4.T, freq.T, dl)
    return pl.pallas_call(
        functools.partial(_filter_kernel, length=length, pb=pb),
        out_shape=(jax.ShapeDtypeStruct((nout, length // LANES, LANES), F32),
                   jax.ShapeDtypeStruct((nout, 1), F32)),
        grid=(length // pb,),
        in_specs=[full(a) for a in args],
        out_specs=(pl.BlockSpec((nout, pb // LANES, LANES), lambda i: (0, i, 0)),
                   pl.BlockSpec((nout, 1), lambda i: (0, 0))),
        compiler_params=_cparams("arbitrary"),
        name="hyena_filter_mlp",
    )(*args)


def _dft_tables(na):
    k1n = 2 * na
    n = k1n * FFT_N2
    n1 = np.arange(na)[:, None]
    k1 = np.arange(k1n)[None, :]
    ang = 2.0 * np.pi * ((n1 * k1) % k1n) / k1n
    c, s = np.cos(ang), np.sin(ang)
    ma = np.block([[c, -s], [s, c]])
    n1f = np.arange(FFT_N2)[:, None]
    angf = 2.0 * np.pi * ((n1f * k1) % k1n) / k1n
    live = (n1f < na).astype(np.float64)
    ma_f = np.concatenate([np.cos(angf), -np.sin(angf)], axis=1) * live
    ma_b = np.concatenate([np.cos(angf), np.sin(angf)], axis=1) * live
    n2 = np.arange(FFT_N2)[:, None]
    angt = 2.0 * np.pi * ((n2 * k1) % n) / n
    twr, twi = np.cos(angt), -np.sin(angt)
    k2 = np.arange(FFT_N2)[None, :]
    angb = 2.0 * np.pi * ((n2 * k2) % FFT_N2) / FFT_N2
    cb, sb = np.cos(angb), np.sin(angb)
    mb = np.block([[cb, -sb], [sb, cb]])
    mc = np.block([[cb, sb], [-sb, cb]])
    angd = 2.0 * np.pi * ((np.arange(k1n)[:, None] * np.arange(na)[None, :]) % k1n) / k1n
    cd, sd = np.cos(angd) / n, np.sin(angd) / n
    md = np.block([[cd, sd], [-sd, cd]])
    as_bf = lambda a: jnp.asarray(a, dtype=F32).astype(BF16)
    as_f = lambda a: jnp.asarray(a, dtype=F32)
    mb2 = np.concatenate([mb, mc], axis=0)
    return dict(ma=as_bf(ma), ma_f=as_bf(ma_f), ma_b=as_bf(ma_b), twr=as_f(twr), twi=as_f(twi),
                twr_t=as_f(twr.T), twi_t=as_f(twi.T), mb=as_bf(mb), mb2=as_bf(mb2), mc=as_bf(mc), md=as_bf(md))


def _fwd_stages(x2, ma, twr, twi, mb, cb, k1n):
    a = _dot(x2, ma).reshape(cb, FFT_N2, 2 * k1n)
    ar, ai = a[..., :k1n], a[..., k1n:]
    ar2 = ar * twr - ai * twi
    ai2 = ar * twi + ai * twr
    xt = jnp.concatenate([jnp.swapaxes(ar2, 1, 2), jnp.swapaxes(ai2, 1, 2)], axis=-1)
    return _dot(xt.astype(BF16).reshape(cb * k1n, 2 * FFT_N2), mb).reshape(cb, k1n, 2 * FFT_N2)


def _spectrum_kernel(hf_ref, hb_ref, ssf_ref, ssb_ref, maf_ref, mab_ref, twr_ref, twi_ref, mb2_ref, h_ref, *, na):
    k1n = 2 * na
    cb = h_ref.shape[0]

    def stage_a(ref, ss_ref, m_ref):
        x = ref[0, 0] * lax.rsqrt(ss_ref[0, 0])
        x = jnp.concatenate([x, jnp.zeros((cb, FFT_N2 - na, FFT_N2), F32)], axis=1)
        xt = jnp.swapaxes(x, 1, 2).astype(BF16).reshape(cb * FFT_N2, FFT_N2)
        a = _dot(xt, m_ref[...]).reshape(cb, FFT_N2, 2 * k1n)
        return a[..., :k1n], a[..., k1n:]

    fr, fi = stage_a(hf_ref, ssf_ref, maf_ref)
    br, bi = stage_a(hb_ref, ssb_ref, mab_ref)
    twr, twi = twr_ref[...], twi_ref[...]
    parts = (fr * twr - fi * twi, fr * twi + fi * twr, br * twr + bi * twi, bi * twr - br * twi)
    xt = jnp.concatenate([jnp.swapaxes(p, 1, 2) for p in parts], axis=-1)
    h = _dot(xt.astype(BF16).reshape(cb * k1n, 4 * FFT_N2), mb2_ref[...])
    h_ref[...] = h.reshape(cb, k1n, 2 * FFT_N2).astype(BF16)


def _spectrum(filt5, ss5, order, tb, na):
    c = filt5.shape[2]
    k1n = 2 * na
    cb = FFT_CB
    full = lambda a: pl.BlockSpec(a.shape, lambda i: tuple(0 for _ in a.shape))
    fblk = lambda d: pl.BlockSpec((1, 1, cb, na, FFT_N2), lambda i: (order, d, i, 0, 0))
    sblk = lambda d: pl.BlockSpec((1, 1, cb, 1, 1), lambda i: (order, d, i, 0, 0))
    return pl.pallas_call(
        functools.partial(_spectrum_kernel, na=na),
        out_shape=jax.ShapeDtypeStruct((c, k1n, 2 * FFT_N2), BF16),
        grid=(c // cb,),
        in_specs=[fblk(0), fblk(1), sblk(0), sblk(1),
                  full(tb["ma_f"]), full(tb["ma_b"]), full(tb["twr"]), full(tb["twi"]), full(tb["mb2"])],
        out_specs=pl.BlockSpec((cb, k1n, 2 * FFT_N2), lambda i: (i, 0, 0)),
        compiler_params=_cparams("arbitrary"),
        name="hyena_filter_spectrum",
    )(filt5, filt5, ss5, ss5, tb["ma_f"], tb["ma_b"], tb["twr"], tb["twi"], tb["mb2"])


def _fftconv_kernel(u_ref, g_ref, h_ref, sk_ref, ma_ref, twr_ref, twi_ref, mb_ref, mc_ref,
                    twrt_ref, twit_ref, md_ref, o_ref, *, na):
    k1n = 2 * na
    cb = u_ref.shape[0]
    u = u_ref[...]
    x = _fwd_stages(u.reshape(cb * FFT_N2, k1n), ma_ref[...], twr_ref[...], twi_ref[...], mb_ref[...], cb, k1n)
    xr, xi = x[..., :FFT_N2], x[..., FFT_N2:]
    h = h_ref[...].astype(F32)
    hr, hi = h[..., :FFT_N2], h[..., FFT_N2:]
    y = jnp.concatenate([xr * hr - xi * hi, xr * hi + xi * hr], axis=-1)
    c = _dot(y.astype(BF16).reshape(cb * k1n, 2 * FFT_N2), mc_ref[...]).reshape(cb, k1n, 2 * FFT_N2)
    cr, ci = c[..., :FFT_N2], c[..., FFT_N2:]
    twrt, twit = twrt_ref[...], twit_ref[...]
    cr2 = cr * twrt + ci * twit
    ci2 = ci * twrt - cr * twit
    ct = jnp.concatenate([jnp.swapaxes(cr2, 1, 2), jnp.swapaxes(ci2, 1, 2)], axis=-1)
    d = _dot(ct.astype(BF16).reshape(cb * FFT_N2, 2 * k1n), md_ref[...]).reshape(cb, FFT_N2, k1n)
    uf = u.astype(F32)
    o_ref[...] = (g_ref[...].astype(F32) * (d + sk_ref[...] * uf)).astype(BF16)


def _fftconv(u, gate, h, skip, tb, na):
    c = u.shape[0]
    k1n = 2 * na
    cb = FFT_CB
    full = lambda a: pl.BlockSpec(a.shape, lambda i: tuple(0 for _ in a.shape))
    blk = pl.BlockSpec((cb, FFT_N2, k1n), lambda i: (i, 0, 0))
    return pl.pallas_call(
        functools.partial(_fftconv_kernel, na=na),
        out_shape=jax.ShapeDtypeStruct(u.shape, BF16),
        grid=(c // cb,),
        in_specs=[blk, blk,
                  pl.BlockSpec((cb, k1n, 2 * FFT_N2), lambda i: (i, 0, 0)),
                  pl.BlockSpec((cb, 1, 1), lambda i: (i, 0, 0)),
                  full(tb["ma"]), full(tb["twr"]), full(tb["twi"]), full(tb["mb"]), full(tb["mc"]),
                  full(tb["twr_t"]), full(tb["twi_t"]), full(tb["md"])],
        out_specs=blk,
        compiler_params=_cparams("arbitrary"),
        name="hyena_fftconv",
    )(u, gate, h, skip, tb["ma"], tb["twr"], tb["twi"], tb["mb"], tb["mc"], tb["twr_t"], tb["twi_t"], tb["md"])


def _to_freq_layout(a, na):
    nb, _, c = a.shape
    a = a.reshape(nb, na, FFT_N2, c)
    return jnp.transpose(a, (3, 2, 0, 1)).reshape(c, FFT_N2, nb * na)


def _from_freq_layout(a):
    c, _, lanes = a.shape
    na = lanes // 2
    a = a.reshape(c, FFT_N2, 2, na)
    return jnp.transpose(a, (2, 3, 1, 0)).reshape(2, na * FFT_N2, c)


def _ctx_dft_tables(n):
    big = 2 * n
    j = np.arange(n)[:, None]
    k = np.arange(big)[None, :]
    ang = 2.0 * np.pi * ((j * k) % big) / big
    c, s = np.cos(ang), np.sin(ang)
    fh = np.block([[c, -s], [c, s]])
    ff = np.block([[c, -s], [s, c]])
    fi = np.block([[c.T, s.T], [-s.T, c.T]]) / big
    as_bf = lambda a: jnp.asarray(a, dtype=F32).astype(BF16)
    return as_bf(fh), as_bf(ff), as_bf(fi)


def _ctxconv_kernel(u_ref, hf_ref, hb_ref, ssf_ref, ssb_ref, sk_ref, fh_ref, ff_ref, fi_ref, o_ref):
    z = u_ref[2].astype(F32)
    half = fh_ref.shape[1] // 2
    for n in range(2):
        taps = jnp.concatenate([hf_ref[n] * lax.rsqrt(ssf_ref[n]), hb_ref[n] * lax.rsqrt(ssb_ref[n])], axis=-1)
        h = _dot(taps.astype(BF16), fh_ref[...])
        x = _dot(z.astype(BF16), ff_ref[...])
        hr, hi = h[:, :half], h[:, half:]
        xr, xi = x[:, :half], x[:, half:]
        y = jnp.concatenate([xr * hr - xi * hi, xr * hi + xi * hr], axis=-1)
        conv = _dot(y.astype(BF16), fi_ref[...])
        z = u_ref[n].astype(F32) * (conv + sk_ref[n] * z)
    o_ref[...] = z.astype(BF16)


def _ctxconv(u3, hf, hb, ssf, ssb, skip, tables):
    _, c, lanes = u3.shape
    args = (u3, hf, hb, ssf, ssb, skip) + tuple(tables)
    full = lambda a: pl.BlockSpec(a.shape, lambda i: tuple(0 for _ in a.shape))
    return pl.pallas_call(
        _ctxconv_kernel,
        out_shape=jax.ShapeDtypeStruct((c, lanes), BF16),
        grid=(1,),
        in_specs=[full(a) for a in args],
        out_specs=pl.BlockSpec((c, lanes), lambda i: (0, 0)),
        compiler_params=_cparams("arbitrary"),
        name="hyena_ctxconv",
    )(*args)


def _load_row_tiles(ref, n, stride=SUBLANES, first=0):
    return jnp.concatenate([ref[pl.ds(first + j, n, stride=stride), :] for j in range(SUBLANES)], axis=-1)


def _store_row_tiles(ref, val, n, stride=SUBLANES, first=0):
    for j in range(SUBLANES):
        ref[pl.ds(first + j, n, stride=stride), :] = val[:, j * LANES:(j + 1) * LANES]


def _row_tile(ref, idx, tiles=1):
    size = tiles * SUBLANES
    return ref.at[pl.ds(pl.multiple_of(idx * size, size), size)]


def _top2_rows(vals):
    n = len(vals)
    best_v, best_i = vals[0], jnp.zeros_like(vals[0], dtype=jnp.int32)
    for e in range(1, n):
        take = vals[e] > best_v
        best_v = jnp.where(take, vals[e], best_v)
        best_i = jnp.where(take, e, best_i)
    sec_v = jnp.full_like(vals[0], -jnp.inf)
    sec_i = jnp.zeros_like(best_i)
    for e in range(n):
        take = (best_i != e) & (vals[e] > sec_v)
        sec_v = jnp.where(take, vals[e], sec_v)
        sec_i = jnp.where(take, e, sec_i)
    return best_v, best_i, sec_v, sec_i


def _route(h2, rw_ref, rb_ref, carry_ref, mi_ref, mf_ref, cnt_ref):
    logits = lax.dot_general(rw_ref[...], h2, (((1,), (1,)), ((), ())),
                             preferred_element_type=F32, precision=HIGHEST)
    s = 1.0 / (1.0 + jnp.exp(-logits))
    sel = s + rb_ref[...]
    srow = [s[e:e + 1, :] for e in range(N_EXPERTS)]
    selrow = [sel[e:e + 1, :] for e in range(N_EXPERTS)]
    gscore = []
    for g in range(N_GROUPS):
        grp = selrow[g * EXPERTS_PER_GROUP:(g + 1) * EXPERTS_PER_GROUP]
        bv, _, sv, _ = _top2_rows(grp)
        gscore.append(bv + sv)
    best_g = jnp.zeros_like(gscore[0], dtype=jnp.int32)
    best_s = gscore[0]
    for g in range(1, N_GROUPS):
        take = gscore[g] > best_s
        best_s = jnp.where(take, gscore[g], best_s)
        best_g = jnp.where(take, g, best_g)
    cand_sel, cand_s = [], []
    for j in range(EXPERTS_PER_GROUP):
        cs, ca = selrow[j], srow[j]
        for g in range(1, N_GROUPS):
            cs = jnp.where(best_g == g, selrow[g * EXPERTS_PER_GROUP + j], cs)
            ca = jnp.where(best_g == g, srow[g * EXPERTS_PER_GROUP + j], ca)
        cand_sel.append(cs)
        cand_s.append(ca)
    _, i1, _, i2 = _top2_rows(cand_sel)
    a1, a2 = cand_s[0], cand_s[0]
    for j in range(1, EXPERTS_PER_GROUP):
        a1 = jnp.where(i1 == j, cand_s[j], a1)
        a2 = jnp.where(i2 == j, cand_s[j], a2)
    den = a1 + a2
    w1, w2 = a1 / den, a2 / den
    swap = i2 < i1
    lo = jnp.where(swap, i2, i1)
    hi = jnp.where(swap, i1, i2)
    w_lo = jnp.where(swap, w2, w1)
    w_hi = jnp.where(swap, w1, w2)
    pair = jnp.where(lo == 0, 0, jnp.where(lo == 1, 3, 5)) + (hi - lo - 1)
    cls = best_g * len(PAIRS) + pair
    cidx = lax.broadcasted_iota(jnp.int32, (CLASS_ROWS, TM), 0)
    hit = cidx == cls
    onehot = jnp.where(hit, 1.0, 0.0)
    tri = jnp.where(lax.broadcasted_iota(jnp.int32, (TM, TM), 0) < lax.broadcasted_iota(jnp.int32, (TM, TM), 1),
                    1.0, 0.0).astype(BF16)
    rank = _dot(onehot.astype(BF16), tri) + carry_ref[:, 0:1]
    rk = jnp.sum(jnp.where(hit, rank, 0.0), axis=0, keepdims=True)
    carry_ref[...] = carry_ref[...] + jnp.sum(onehot, axis=1, keepdims=True)
    mi_ref[...] = jnp.concatenate([cls, rk.astype(jnp.int32), jnp.zeros((6, TM), jnp.int32)], axis=0)
    mf_ref[...] = jnp.concatenate([w_lo, w_hi, jnp.zeros((6, TM), F32)], axis=0)
    cnt_ref[...] = carry_ref[...]


def _post_norm_mod(x, y, m_ref, lng, lnb, gate_row, sh_row, sc_row):
    u = DN_ALPHA * x + m_ref[0, gate_row:gate_row + 1, :] * y
    mu = jnp.mean(u, axis=-1, keepdims=True)
    var = jnp.mean(jnp.square(u - mu), axis=-1, keepdims=True)
    xn = (u - mu) * lax.rsqrt(var + LN_EPS) * lng + lnb
    if sh_row is None:
        return xn, None
    return xn, xn * (1.0 + m_ref[0, sc_row:sc_row + 1, :]) + m_ref[0, sh_row:sh_row + 1, :]


def _finish_outproj(y, x_ref, m_ref, lng_ref, lnb_ref, rw_ref, rb_ref, xo_ref, h2_ref, mi_ref, mf_ref, cnt_ref,
                    carry_ref):
    @pl.when(pl.program_id(0) == 0)
    def _():
        carry_ref[...] = jnp.zeros_like(carry_ref)

    xn, h2 = _post_norm_mod(x_ref[...], y, m_ref, lng_ref[...], lnb_ref[...], 2, 3, 4)
    xo_ref[...] = xn
    _store_row_tiles(h2_ref, h2, TM)
    _route(h2, rw_ref, rb_ref, carry_ref, mi_ref, mf_ref, cnt_ref)


def _outproj0_kernel(of_ref, ob_ref, g_ref, hy_ref, x_ref, m_ref, w_ref, lng_ref, lnb_ref, rw_ref, rb_ref,
                     xo_ref, h2_ref, mi_ref, mf_ref, cnt_ref, carry_ref):
    o = of_ref[...] + ob_ref[...]
    g = g_ref[...].astype(F32)
    parts = []
    for h in range(RET_HEADS):
        oh = o[:, h * RET_HD:(h + 1) * RET_HD]
        mu = jnp.mean(oh, axis=-1, keepdims=True)
        var = jnp.mean(jnp.square(oh - mu), axis=-1, keepdims=True)
        parts.append((oh - mu) * lax.rsqrt(var + LN_EPS))
    yret = (jnp.concatenate(parts, axis=-1) * _silu(g)).astype(BF16)
    wr = yret.shape[1]
    y = _dot(yret, w_ref[:wr, :]) + _dot(hy_ref[...], w_ref[wr:, :])
    _finish_outproj(y, x_ref, m_ref, lng_ref, lnb_ref, rw_ref, rb_ref, xo_ref, h2_ref, mi_ref, mf_ref, cnt_ref,
                    carry_ref)


def _outproj1_kernel(of_ref, ob_ref, g_ref, x_ref, m_ref, w_ref, lng_ref, lnb_ref, rw_ref, rb_ref,
                     xo_ref, h2_ref, mi_ref, mf_ref, cnt_ref, carry_ref):
    o = of_ref[...] + ob_ref[...]
    g = g_ref[...].astype(F32)
    parts = []
    for h in range(GLA_HEADS):
        oh = o[:, h * GLA_DV:(h + 1) * GLA_DV]
        parts.append(oh * lax.rsqrt(jnp.mean(jnp.square(oh), axis=-1, keepdims=True) + LN_EPS))
    a = (jnp.concatenate(parts, axis=-1) * _silu(g)).astype(BF16)
    y = _dot(a, w_ref[...])
    _finish_outproj(y, x_ref, m_ref, lng_ref, lnb_ref, rw_ref, rb_ref, xo_ref, h2_ref, mi_ref, mf_ref, cnt_ref,
                    carry_ref)


def _outproj_call(body, name, n_tiles, d, data_specs, data_args, mods_spec, mods, w, lng, lnb, router_w, router_bias):
    rows = n_tiles * TM
    full2 = lambda a: pl.BlockSpec(a.shape, lambda i: (0, 0))
    rw_t = router_w.T
    rb = router_bias.reshape(N_EXPERTS, 1)
    return pl.pallas_call(
        body,
        out_shape=(jax.ShapeDtypeStruct((rows, d), F32),
                   jax.ShapeDtypeStruct((rows * SUBLANES, LANES), F32),
                   jax.ShapeDtypeStruct((8, rows), jnp.int32),
                   jax.ShapeDtypeStruct((8, rows), F32),
                   jax.ShapeDtypeStruct((CLASS_ROWS, LANES), F32)),
        grid=(n_tiles,),
        in_specs=list(data_specs) + [mods_spec, full2(w), full2(lng), full2(lnb), full2(rw_t), full2(rb)],
        out_specs=(pl.BlockSpec((TM, d), lambda i: (i, 0)),
                   pl.BlockSpec((TM * SUBLANES, LANES), lambda i: (i, 0)),
                   pl.BlockSpec((8, TM), lambda i: (0, i)),
                   pl.BlockSpec((8, TM), lambda i: (0, i)),
                   pl.BlockSpec((CLASS_ROWS, LANES), lambda i: (0, 0))),
        scratch_shapes=[pltpu.VMEM((CLASS_ROWS, LANES), F32)],
        compiler_params=_cparams("arbitrary"),
        name=name,
    )(*data_args, mods, w, lng, lnb, rw_t, rb)


def _dispatch_kernel(slots_ref, zoff_ref, h_ref, xs_ref, zero_ref, sem):
    i = pl.program_id(0)

    @pl.when(i == 0)
    def _():
        zero_ref[...] = jnp.zeros_like(zero_ref)
        for e in range(N_CLASSES):
            pltpu.make_async_copy(zero_ref, _row_tile(xs_ref, zoff_ref[e], TE), sem).start()
        for e in range(N_CLASSES):
            pltpu.make_async_copy(zero_ref, _row_tile(xs_ref, zoff_ref[e], TE), sem).wait()
        first_free = zoff_ref[N_CLASSES]
        n_all = xs_ref.shape[0] // (TE * SUBLANES)

        def fill(t, carry):
            pltpu.make_async_copy(zero_ref, _row_tile(xs_ref, t, TE), sem).start()
            return carry

        def fill_wait(t, carry):
            pltpu.make_async_copy(zero_ref, _row_tile(xs_ref, t, TE), sem).wait()
            return carry

        lax.fori_loop(first_free, n_all, fill, 0)
        lax.fori_loop(first_free, n_all, fill_wait, 0)

    def row_copy(r):
        return pltpu.make_async_copy(_row_tile(h_ref, r), _row_tile(xs_ref, slots_ref[0, 0, r]), sem)

    def issue(r2, carry):
        row_copy(2 * r2).start(priority=0)
        row_copy(2 * r2 + 1).start(priority=1)
        return carry

    lax.fori_loop(0, TM // 2, issue, 0, unroll=4)

    def drain(r, carry):
        row_copy(r).wait()
        return carry

    lax.fori_loop(0, TM, drain, 0, unroll=8)


def _dispatch(h2, slots3, zoff, p_rows):
    n_tiles = h2.shape[0] // (TM * SUBLANES)
    grid_spec = pltpu.PrefetchScalarGridSpec(
        num_scalar_prefetch=0,
        grid=(n_tiles,),
        in_specs=[pl.BlockSpec((1, 1, TM), lambda i: (i, 0, 0), memory_space=pltpu.SMEM),
                  pl.BlockSpec(memory_space=pltpu.SMEM),
                  pl.BlockSpec((TM * SUBLANES, LANES), lambda i: (i, 0))],
        out_specs=pl.BlockSpec(memory_space=pl.ANY),
        scratch_shapes=[pltpu.VMEM((TE * SUBLANES, LANES), F32), pltpu.SemaphoreType.DMA(())],
    )
    return pl.pallas_call(
        _dispatch_kernel,
        out_shape=jax.ShapeDtypeStruct((p_rows * SUBLANES, LANES), F32),
        grid_spec=grid_spec,
        compiler_params=_cparams("arbitrary"),
        name="moe_dispatch",
    )(slots3, zoff, h2)


def _experts_kernel(ea_ref, eb_ref, tv_ref, x_ref, wga_ref, wua_ref, wda_ref, wgb_ref, wub_ref, wdb_ref, y_ref):
    j = pl.program_id(0)

    @pl.when(tv_ref[j] > 0)
    def _():
        x = _load_row_tiles(x_ref, TE).astype(BF16)
        for which, (wg, wu, wd) in enumerate(((wga_ref, wua_ref, wda_ref), (wgb_ref, wub_ref, wdb_ref))):
            g = _dot(x, wg[0])
            u = _dot(x, wu[0])
            y = _dot((_silu(g) * u).astype(BF16), wd[0])
            _store_row_tiles(y_ref, y, TE, stride=2 * SUBLANES, first=which * SUBLANES)

    @pl.when(tv_ref[j] == 0)
    def _():
        y_ref[...] = jnp.zeros_like(y_ref)


def _experts(xs, tile_ea, tile_eb, tile_valid, wg, wu, wd):
    n_tiles = xs.shape[0] // (TE * SUBLANES)
    d, hdim = wg.shape[1], wg.shape[2]
    wa = lambda shp: pl.BlockSpec(shp, lambda j, ea, eb, tv: (ea[j], 0, 0))
    wb = lambda shp: pl.BlockSpec(shp, lambda j, ea, eb, tv: (eb[j], 0, 0))
    grid_spec = pltpu.PrefetchScalarGridSpec(
        num_scalar_prefetch=3,
        grid=(n_tiles,),
        in_specs=[pl.BlockSpec((TE * SUBLANES, LANES), lambda j, ea, eb, tv: (j, 0)),
                  wa((1, d, hdim)), wa((1, d, hdim)), wa((1, hdim, d)),
                  wb((1, d, hdim)), wb((1, d, hdim)), wb((1, hdim, d))],
        out_specs=pl.BlockSpec((2 * TE * SUBLANES, LANES), lambda j, ea, eb, tv: (j, 0)),
    )
    return pl.pallas_call(
        _experts_kernel,
        out_shape=jax.ShapeDtypeStruct((2 * xs.shape[0], LANES), F32),
        grid_spec=grid_spec,
        compiler_params=_cparams("arbitrary"),
        name="moe_experts",
    )(tile_ea, tile_eb, tile_valid, xs, wg, wu, wd, wg, wu, wd)


def _combine_kernel(slots_ref, ys_ref, wts_ref, x_ref, m_ref, lng_ref, lnb_ref, o_ref, buf_ref, sem):
    def row_copy(r):
        return pltpu.make_async_copy(_row_tile(ys_ref, slots_ref[0, 0, r], 2), _row_tile(buf_ref, r, 2), sem)

    def issue(r2, carry):
        row_copy(2 * r2).start(priority=0)
        row_copy(2 * r2 + 1).start(priority=1)
        return carry

    lax.fori_loop(0, TM // 2, issue, 0, unroll=4)

    def drain(r, carry):
        row_copy(r).wait()
        return carry

    lax.fori_loop(0, TM, drain, 0, unroll=8)
    y_lo = _load_row_tiles(buf_ref, TM, stride=2 * SUBLANES)
    y_hi = _load_row_tiles(buf_ref, TM, stride=2 * SUBLANES, first=SUBLANES)
    y = wts_ref[:, 0:1] * y_lo + wts_ref[:, 1:2] * y_hi
    xn, _ = _post_norm_mod(x_ref[...], y, m_ref, lng_ref[...], lnb_ref[...], 5, None, None)
    o_ref[...] = xn


def _combine(ys, slots3, wts, x, mods, lng, lnb, cond_of, n_tiles):
    d = x.shape[1]
    grid_spec = pltpu.PrefetchScalarGridSpec(
        num_scalar_prefetch=0,
        grid=(n_tiles,),
        in_specs=[pl.BlockSpec((1, 1, TM), lambda i: (i, 0, 0), memory_space=pltpu.SMEM),
                  pl.BlockSpec(memory_space=pl.ANY),
                  pl.BlockSpec((TM, 2), lambda i: (i, 0)),
                  pl.BlockSpec((TM, d), lambda i: (i, 0)),
                  pl.BlockSpec((1, 6, d), lambda i: (cond_of(i), 0, 0)),
                  pl.BlockSpec((1, d), lambda i: (0, 0)),
                  pl.BlockSpec((1, d), lambda i: (0, 0))],
        out_specs=pl.BlockSpec((TM, d), lambda i: (i, 0)),
        scratch_shapes=[pltpu.VMEM((2 * TM * SUBLANES, LANES), F32), pltpu.SemaphoreType.DMA(())],
    )
    return pl.pallas_call(
        _combine_kernel,
        out_shape=jax.ShapeDtypeStruct((n_tiles * TM, d), F32),
        grid_spec=grid_spec,
        compiler_params=_cparams("arbitrary"),
        name="moe_combine",
    )(slots3, ys, wts, x, mods, lng, lnb)


def _moe(h2, mi, mf, cnt, x, cond_of, mods, lng, lnb, wg, wu, wd):
    t = x.shape[0]
    n_tiles = t // TM
    counts = cnt[:N_CLASSES, 0].astype(jnp.int32)
    padded = ((counts + TE - 1) // TE) * TE
    ends = jnp.cumsum(padded)
    offs = ends - padded
    cls = mi[0]
    cids = jnp.arange(N_CLASSES, dtype=jnp.int32)[:, None]
    slots = jnp.sum(jnp.where(cls[None] == cids, offs[:, None], 0), axis=0) + mi[1]
    slots3 = slots.reshape(n_tiles, 1, TM)
    n_cls_tiles = t // TE + N_CLASSES
    total_tiles = ends[-1] // TE
    tile_ids = jnp.arange(n_cls_tiles, dtype=jnp.int32)
    tile_valid = (tile_ids < total_tiles).astype(jnp.int32)
    tile_last = jnp.minimum(tile_ids, total_tiles - 1)
    tile_class = jnp.sum((tile_last[:, None] * TE >= ends[None, :]).astype(jnp.int32), axis=1)
    tile_class = jnp.minimum(tile_class, N_CLASSES - 1)
    pair_lo = jnp.asarray([p[0] for p in PAIRS], jnp.int32)
    pair_hi = jnp.asarray([p[1] for p in PAIRS], jnp.int32)
    pair_sel = (tile_class % len(PAIRS))[:, None] == jnp.arange(len(PAIRS), dtype=jnp.int32)[None, :]
    group0 = (tile_class // len(PAIRS)) * EXPERTS_PER_GROUP
    tile_ea = (group0 + jnp.sum(jnp.where(pair_sel, pair_lo[None, :], 0), axis=1)).astype(jnp.int32)
    tile_eb = (group0 + jnp.sum(jnp.where(pair_sel, pair_hi[None, :], 0), axis=1)).astype(jnp.int32)
    zoff = jnp.concatenate([jnp.maximum(ends // TE - 1, 0), total_tiles[None]]).astype(jnp.int32)
    xs = _dispatch(h2, slots3, zoff, n_cls_tiles * TE)
    ys = _experts(xs, tile_ea, tile_eb, tile_valid, wg, wu, wd)
    wts = jnp.transpose(mf[0:2])
    return _combine(ys, slots3, wts, x, mods, lng, lnb, cond_of, n_tiles)


def _chunk_cumsum(x, tri):
    hi = x.astype(BF16)
    r1 = x - hi.astype(F32)
    mid = r1.astype(BF16)
    lo = (r1 - mid.astype(F32)).astype(BF16)
    return _dot(tri, hi) + _dot(tri, mid) + _dot(tri, lo)


def _gla_kernel(q_ref, k_ref, v_ref, lr_ref, gw_ref, gb_ref, o_ref, s_ref, *, reverse, nb):
    C = GLA_CHUNK
    nchunk = TM // C

    @pl.when(pl.program_id(0) == 0)
    def _():
        s_ref[...] = jnp.zeros_like(s_ref)

    ii = lax.broadcasted_iota(jnp.int32, (C, C), 0)
    jj = lax.broadcasted_iota(jnp.int32, (C, C), 1)
    keep = (jj >= ii) if reverse else (jj <= ii)
    tri = jnp.where(keep, 1.0, 0.0).astype(BF16)
    order = tuple(reversed(range(nchunk))) if reverse else tuple(range(nchunk))
    end_row = 0 if reverse else C - 1
    gw = gw_ref[...]
    gb = gb_ref[...]
    for b in range(nb):
        pre = _dot(lr_ref[b], gw) + gb
        la_all = (jnp.minimum(pre, 0.0) - jnp.log(1.0 + jnp.exp(-jnp.abs(pre)))) * (1.0 / GLA_TAU)
        for c in order:
            rows = slice(c * C, (c + 1) * C)
            bc = _chunk_cumsum(la_all[rows, :], tri)
            bend = bc[end_row:end_row + 1, :]
            eb = jnp.exp(bc)
            enb = jnp.exp(-bc)
            ekb = jnp.exp(bend - bc)
            ebend = jnp.exp(bend)
            for h in range(GLA_HEADS):
                kc = slice(h * GLA_DK, (h + 1) * GLA_DK)
                vc = slice(h * GLA_DV, (h + 1) * GLA_DV)
                q = q_ref[b, rows, kc].astype(F32) * (GLA_DK ** -0.5)
                k = k_ref[b, rows, kc].astype(F32)
                v = v_ref[b, rows, vc]
                qd = (q * eb[:, kc]).astype(BF16)
                kd = (k * enb[:, kc]).astype(BF16)
                st = s_ref[b, h]
                scores = jnp.where(keep, _dot_nt(qd, kd), 0.0)
                o = _dot(scores.astype(BF16), v) + _dot_nt(qd, st.astype(BF16))
                s_ref[b, h] = st * ebend[:, kc] + _dot_tn(v, (k * ekb[:, kc]).astype(BF16))
                o_ref[b, rows, vc] = o


def _gla(z3, gw, gb, reverse):
    nb, s, _ = z3.shape
    nt = s // TM
    kw = GLA_HEADS * GLA_DK
    vw = GLA_HEADS * GLA_DV

    if reverse:
        def jmap(j):
            return jnp.where(j == 0, nt - 1, nt - 1 - j)
    else:
        def jmap(j):
            return (j + nt - 1) % nt

    lr_blk = (2 * kw + 2 * vw) // LANES
    return pl.pallas_call(
        functools.partial(_gla_kernel, reverse=reverse, nb=nb),
        out_shape=jax.ShapeDtypeStruct((nb, s, vw), F32),
        grid=(nt,),
        in_specs=[pl.BlockSpec((nb, TM, kw), lambda j: (0, jmap(j), 0)),
                  pl.BlockSpec((nb, TM, kw), lambda j: (0, jmap(j), 1)),
                  pl.BlockSpec((nb, TM, vw), lambda j: (0, jmap(j), 1)),
                  pl.BlockSpec((nb, TM, LANES), lambda j: (0, jmap(j), lr_blk)),
                  pl.BlockSpec((LANES, kw), lambda j: (0, 0)),
                  pl.BlockSpec((1, kw), lambda j: (0, 0))],
        out_specs=pl.BlockSpec((nb, TM, vw), lambda j: (0, jmap(j), 0)),
        scratch_shapes=[pltpu.VMEM((nb, GLA_HEADS, GLA_DV, GLA_DK), F32)],
        compiler_params=_cparams("arbitrary"),
        name="gla_bwd" if reverse else "gla_fwd",
    )(z3, z3, z3, z3, gw, gb)


def kernel(x, c, ctx, c_ctx, mod_w, mod_b, ln_g, ln_b, ab_w_in, ret_log_decay_f, ret_log_decay_b, hy_conv_w, hy_conv_b, hy_w1, hy_b1, hy_w2, hy_b2, hy_w3, hy_b3, hy_w4, hy_freq, hy_skip, ab_w_out, gla_w_in, gla_gate_w_f, gla_gate_b_f, gla_gate_w_b, gla_gate_b_b, gla_w_out, router_w, router_bias, exp_w_gate, exp_w_up, exp_w_down):
    nb, length, d = x.shape
    ctx_len = ctx.shape[1]
    assert nb == 2 and ctx_len == TM and length % TM == 0 and length % (FFT_N2 * 2) == 0
    s = length + ctx_len
    tpb = s // TM
    r = nb * s
    lat_tiles = tpb - 1

    cond = jnp.zeros((8, d), F32).at[0].set(c[0]).at[1].set(c_ctx).at[2].set(c[1]).at[3].set(c_ctx)
    mods = _ada(cond, mod_w, mod_b).reshape(DEPTH, 8, 6, d)

    def cond_of(i):
        return 2 * (i // tpb) + jnp.where(i % tpb == tpb - 1, 1, 0)

    xu = jnp.concatenate([x, ctx], axis=1).reshape(r, d)
    row = lambda i: (i, 0)

    ret_w = RET_HEADS * RET_HD
    hy_w = 3 * HY_WIDTH
    w_in0 = jnp.concatenate([ab_w_in[0][:, 4 * ret_w:], ab_w_in[0][:, :4 * ret_w]], axis=1).astype(BF16)
    z0 = _inproj(xu, mods[0], w_in0, tpb)
    z0_3 = z0.reshape(nb, s, -1)
    cosf, sinf = _rope_tables(length, ctx_len)
    o_f = _retention(z0_3, hy_w, ret_log_decay_f[0], cosf, sinf, False)
    o_b = _retention(z0_3, hy_w, ret_log_decay_b[0], cosf, sinf, True)

    hyu = _shortconv(z0, hy_w, hy_conv_w[0], hy_conv_b[0], tpb)
    hyu3 = hyu.reshape(nb, s, hy_w)
    na = length // FFT_N2
    tb = _dft_tables(na)
    filt_w = (hy_w1[0], hy_b1[0], hy_w2[0], hy_b2[0], hy_w3[0], hy_b3[0], hy_w4[0], hy_freq[0])
    skip = hy_skip[0]

    filt, ss = _hyena_filters_raw(length, *filt_w)
    filt5 = filt.reshape(2, 2, HY_WIDTH, na, FFT_N2)
    ss5 = ss.reshape(2, 2, HY_WIDTH, 1, 1)
    seq = hyu3[:, :length]
    zcur = _to_freq_layout(seq[..., 2 * HY_WIDTH:], na)
    for n in range(2):
        hspec = _spectrum(filt5, ss5, n, tb, na)
        gate = _to_freq_layout(seq[..., n * HY_WIDTH:(n + 1) * HY_WIDTH], na)
        zcur = _fftconv(zcur, gate, hspec, skip[n].reshape(-1, 1, 1), tb, na)
    y_hy_lat = _from_freq_layout(zcur)

    filt_c, ss_c = _hyena_filters_raw(ctx_len, *filt_w)
    filt_c = filt_c.reshape(2, 2, HY_WIDTH, ctx_len)
    ss_c = ss_c.reshape(2, 2, HY_WIDTH, 1)
    u_ctx = jnp.transpose(hyu3[:, length:].reshape(nb, ctx_len, 3, HY_WIDTH), (2, 3, 0, 1))
    y_ctx_t = _ctxconv(u_ctx.reshape(3, HY_WIDTH, nb * ctx_len), filt_c[:, 0], filt_c[:, 1], ss_c[:, 0], ss_c[:, 1],
                       skip.reshape(2, HY_WIDTH, 1), _ctx_dft_tables(ctx_len))
    y_hy_ctx = jnp.transpose(y_ctx_t.reshape(HY_WIDTH, nb, ctx_len), (1, 2, 0))
    y_hy = jnp.concatenate([y_hy_lat, y_hy_ctx], axis=1).reshape(r, HY_WIDTH)

    gate_blk = (hy_w + 3 * ret_w) // ret_w
    x_mid, h2, mi, mf, cnt = _outproj_call(
        _outproj0_kernel, "outproj_norm0", r // TM, d,
        [pl.BlockSpec((TM, ret_w), row), pl.BlockSpec((TM, ret_w), row),
         pl.BlockSpec((TM, ret_w), lambda i: (i, gate_blk)),
         pl.BlockSpec((TM, HY_WIDTH), row), pl.BlockSpec((TM, d), row)],
        (o_f.reshape(r, ret_w), o_b.reshape(r, ret_w), z0, y_hy, xu),
        pl.BlockSpec((1, 6, d), lambda i: (cond_of(i), 0, 0)), mods[0], ab_w_out[0].astype(BF16),
        ln_g[0, 0].reshape(1, d), ln_b[0, 0].reshape(1, d), router_w, router_bias)

    x1 = _moe(h2, mi, mf, cnt, x_mid, cond_of, mods[0], ln_g[0, 1].reshape(1, d), ln_b[0, 1].reshape(1, d),
              exp_w_gate[0].astype(BF16), exp_w_up[0].astype(BF16), exp_w_down[0].astype(BF16))

    kw = GLA_HEADS * GLA_DK
    vw = GLA_HEADS * GLA_DV
    n_in = gla_w_in.shape[2]
    n_pad = 2 * kw + 2 * vw + LANES
    w_in1 = jnp.pad(gla_w_in[0], ((0, 0), (0, n_pad - n_in))).astype(BF16)
    z1 = _inproj(x1, mods[1], w_in1, tpb)
    z1_3 = z1.reshape(nb, s, n_pad)
    gw_f = jnp.zeros((LANES, kw), F32).at[:GLA_RANK].set(gla_gate_w_f[0]).astype(BF16)
    gw_b = jnp.zeros((LANES, kw), F32).at[GLA_RANK:2 * GLA_RANK].set(gla_gate_w_b[0]).astype(BF16)
    g_f = _gla(z1_3, gw_f, gla_gate_b_f[0].reshape(1, kw), False)
    g_b = _gla(z1_3, gw_b, gla_gate_b_b[0].reshape(1, kw), True)

    n_lat = nb * lat_tiles

    def lat_tile(i):
        return (i // lat_tiles) * tpb + i % lat_tiles

    def lat_cond(i):
        return 2 * (i // lat_tiles)

    lrow = lambda i: (lat_tile(i), 0)
    x_mid1, h2_1, mi1, mf1, cnt1 = _outproj_call(
        _outproj1_kernel, "outproj_norm1", n_lat, d,
        [pl.BlockSpec((TM, vw), lrow), pl.BlockSpec((TM, vw), lrow),
         pl.BlockSpec((TM, vw), lambda i: (lat_tile(i), 2)),
         pl.BlockSpec((TM, d), lrow)],
        (g_f.reshape(r, vw), g_b.reshape(r, vw), z1, x1),
        pl.BlockSpec((1, 6, d), lambda i: (lat_cond(i), 0, 0)), mods[1], gla_w_out[0].astype(BF16),
        ln_g[1, 0].reshape(1, d), ln_b[1, 0].reshape(1, d), router_w, router_bias)

    out = _moe(h2_1, mi1, mf1, cnt1, x_mid1, lat_cond, mods[1], ln_g[1, 1].reshape(1, d), ln_b[1, 1].reshape(1, d),
               exp_w_gate[1].astype(BF16), exp_w_up[1].astype(BF16), exp_w_down[1].astype(BF16))
    return out.reshape(nb, length, d)
```

```python
import functools
import math

import numpy as np
import jax
import jax.numpy as jnp
from jax import lax
from jax.experimental import pallas as pl
from jax.experimental.pallas import tpu as pltpu

F32 = jnp.float32
BF16 = jnp.bfloat16

GRID_W = 64
RET_HEADS = 4
RET_HD = 128
RET_CHUNK = 128
ROPE_BASE = 10000.0
HY_WIDTH = 512
HY_EMB = 33
HY_FFN = 64
HY_SHORT_DECAY_PCT = 0.3
HY_LONG_DECAY_PCT = 1.5
HY_TARGET = 1e-2
GLA_HEADS = 4
GLA_DK = 128
GLA_DV = 256
GLA_RANK = 16
GLA_TAU = 16.0
GLA_CHUNK = 64
N_EXPERTS = 16
N_GROUPS = 4
EXPERTS_PER_GROUP = 4
LN_EPS = 1e-5
DEPTH = 2
DN_ALPHA = (2.0 * DEPTH) ** 0.25

PAIRS = ((0, 1), (0, 2), (0, 3), (1, 2), (1, 3), (2, 3))
N_CLASSES = N_GROUPS * len(PAIRS)
CLASS_ROWS = 32

LANES = 128
SUBLANES = 8
TM = 256
TE = 256
FFT_N2 = 128
FFT_CB = 8
VMEM_LIMIT = 48 * 1024 * 1024

HIGHEST = lax.Precision.HIGHEST


def _cparams(*sem):
    return pltpu.CompilerParams(dimension_semantics=sem, vmem_limit_bytes=VMEM_LIMIT)


def _silu(v):
    return v * (1.0 / (1.0 + jnp.exp(-v)))


def _dot(a, b):
    return jnp.dot(a, b, preferred_element_type=F32)


def _dot_nt(a, b):
    return lax.dot_general(a, b, (((1,), (1,)), ((), ())), preferred_element_type=F32)


def _dot_tn(a, b):
    return lax.dot_general(a, b, (((0,), (0,)), ((), ())), preferred_element_type=F32)


def _split2(a):
    hi = a.astype(BF16)
    return hi, (a - hi.astype(F32)).astype(BF16)


def _dot3(a, b, dot=_dot):
    ah, al = _split2(a)
    bh, bl = _split2(b)
    return dot(ah, bh) + dot(ah, bl) + dot(al, bh)


def _ada_kernel(c_ref, w_ref, b_ref, o_ref):
    a = _silu(c_ref[...])
    o_ref[0] = jnp.dot(a, w_ref[0], preferred_element_type=F32, precision=HIGHEST) + b_ref[0]


def _ada(cond, mod_w, mod_b):
    depth, d, n = mod_w.shape
    nt = n // 4
    return pl.pallas_call(
        _ada_kernel,
        out_shape=jax.ShapeDtypeStruct((depth, 8, n), F32),
        grid=(depth, n // nt),
        in_specs=[pl.BlockSpec((8, d), lambda l, j: (0, 0)),
                  pl.BlockSpec((1, d, nt), lambda l, j: (l, 0, j)),
                  pl.BlockSpec((1, 1, nt), lambda l, j: (l, 0, j))],
        out_specs=pl.BlockSpec((1, 8, nt), lambda l, j: (l, 0, j)),
        compiler_params=_cparams("arbitrary", "arbitrary"),
        name="ada_mod",
    )(cond, mod_w, mod_b.reshape(depth, 1, n))


def _inproj_kernel(*refs, chunks, tpb, split):
    if split:
        x_ref, c_ref, m_ref, w_ref, o_ref = refs
        x = jnp.where(pl.program_id(0) % tpb == tpb - 1, c_ref[...], x_ref[...])
    else:
        x_ref, m_ref, w_ref, o_ref = refs
        x = x_ref[...]
    sh = m_ref[0, 0:1, :]
    sc = m_ref[0, 1:2, :]
    h = (x * (1.0 + sc) + sh).astype(BF16)
    for lo, hi in chunks:
        o_ref[:, lo:hi] = _dot(h, w_ref[:, lo:hi]).astype(BF16)


def _lat_or_ctx_specs(width, tpb, tile_of=lambda i: i):
    lat_tiles = tpb - 1
    lat = lambda i: ((tile_of(i) // tpb) * lat_tiles + jnp.minimum(tile_of(i) % tpb, lat_tiles - 1), 0)
    ctx = lambda i: (tile_of(i) // tpb, 0)
    return pl.BlockSpec((TM, width), lat), pl.BlockSpec((TM, width), ctx)


def _col_chunks(n, width=512):
    out, lo = [], 0
    while lo < n:
        hi = min(lo + width, n)
        out.append((lo, hi))
        lo = hi
    return tuple(out)


def _inproj(src, mods, w, tiles_per_batch, r):
    d, n = w.shape
    tpb = tiles_per_batch
    split = isinstance(src, tuple)

    def cond_of(i):
        return 2 * (i // tpb) + jnp.where(i % tpb == tpb - 1, 1, 0)

    src_specs = list(_lat_or_ctx_specs(d, tpb)) if split else [pl.BlockSpec((TM, d), lambda i: (i, 0))]
    src_args = tuple(src) if split else (src,)
    return pl.pallas_call(
        functools.partial(_inproj_kernel, chunks=_col_chunks(n), tpb=tpb, split=split),
        out_shape=jax.ShapeDtypeStruct((r, n), BF16),
        grid=(r // TM,),
        in_specs=src_specs + [pl.BlockSpec((1, 6, d), lambda i: (cond_of(i), 0, 0)),
                              pl.BlockSpec((d, n), lambda i: (0, 0))],
        out_specs=pl.BlockSpec((TM, n), lambda i: (i, 0)),
        compiler_params=_cparams("arbitrary"),
        name="inproj",
    )(*src_args, mods, w)


def _ret_kernel(lg_ref, q_ref, k_ref, v_ref, cos_ref, sin_ref, o_ref, s_ref, *, reverse, nb):
    C = RET_CHUNK

    @pl.when(pl.program_id(0) == 0)
    def _():
        s_ref[...] = jnp.zeros_like(s_ref)

    ii = lax.broadcasted_iota(jnp.int32, (C, C), 0).astype(F32)
    jj = lax.broadcasted_iota(jnp.int32, (C, C), 1).astype(F32)
    ci = lax.broadcasted_iota(jnp.int32, (C, 1), 0).astype(F32)
    diff = (jj - ii) if reverse else (ii - jj)
    order = (1, 0) if reverse else (0, 1)
    for h in range(RET_HEADS):
        lg = lg_ref[h]
        intra = jnp.where(diff >= 0, jnp.exp(jnp.maximum(diff, 0.0) * lg), 0.0)
        if reverse:
            q_dec = jnp.exp((C - ci) * lg)
            k_dec = jnp.exp(ci * lg)
        else:
            q_dec = jnp.exp((ci + 1.0) * lg)
            k_dec = jnp.exp((C - 1.0 - ci) * lg)
        c_dec = jnp.exp(jnp.zeros((1, RET_HD), F32) + C * lg)
        cols = slice(h * RET_HD, (h + 1) * RET_HD)
        for b in range(nb):
            for c in order:
                rows = slice(c * C, (c + 1) * C)
                cosf = cos_ref[rows, :]
                sinf = sin_ref[rows, :]
                q = q_ref[b, rows, cols].astype(F32)
                k = k_ref[b, rows, cols].astype(F32)
                v = v_ref[b, rows, cols]
                q = q * cosf + pltpu.roll(q, RET_HD // 2, 1) * sinf
                k = (k * cosf + pltpu.roll(k, RET_HD // 2, 1) * sinf) * (RET_HD ** -0.5)
                st = s_ref[b, h]
                scores = _dot_nt(q.astype(BF16), k.astype(BF16)) * intra
                o = _dot(scores.astype(BF16), v) + _dot_nt((q * q_dec).astype(BF16), st.astype(BF16))
                s_ref[b, h] = c_dec * st + _dot_tn(v, (k * k_dec).astype(BF16))
                o_ref[b, rows, cols] = o.astype(BF16)


def _retention(z3, col0, lg, cosf, sinf, reverse):
    nb, s, _ = z3.shape
    nt = s // TM
    w = RET_HEADS * RET_HD
    cb0 = col0 // w

    if reverse:
        def jmap(j):
            return jnp.where(j == 0, nt - 1, nt - 1 - j)
    else:
        def jmap(j):
            return (j + nt - 1) % nt

    grid_spec = pltpu.PrefetchScalarGridSpec(
        num_scalar_prefetch=0,
        grid=(nt,),
        in_specs=[pl.BlockSpec(memory_space=pltpu.SMEM),
                  pl.BlockSpec((nb, TM, w), lambda j: (0, jmap(j), cb0)),
                  pl.BlockSpec((nb, TM, w), lambda j: (0, jmap(j), cb0 + 1)),
                  pl.BlockSpec((nb, TM, w), lambda j: (0, jmap(j), cb0 + 2)),
                  pl.BlockSpec((TM, RET_HD), lambda j: (jmap(j), 0)),
                  pl.BlockSpec((TM, RET_HD), lambda j: (jmap(j), 0))],
        out_specs=pl.BlockSpec((nb, TM, w), lambda j: (0, jmap(j), 0)),
        scratch_shapes=[pltpu.VMEM((nb, RET_HEADS, RET_HD, RET_HD), F32)],
    )
    return pl.pallas_call(
        functools.partial(_ret_kernel, reverse=reverse, nb=nb),
        out_shape=jax.ShapeDtypeStruct((nb, s, w), BF16),
        grid_spec=grid_spec,
        compiler_params=_cparams("arbitrary"),
        name="retention_bwd" if reverse else "retention_fwd",
    )(lg, z3, z3, z3, cosf, sinf)


def _rope_tables(length, ctx_len):
    rows = length // GRID_W
    quarter = RET_HD // 4
    inv = ROPE_BASE ** (-jnp.arange(quarter, dtype=F32) / quarter)
    r = jnp.repeat(jnp.arange(rows, dtype=F32), GRID_W)
    col = jnp.tile(jnp.arange(GRID_W, dtype=F32), rows)
    ang = jnp.concatenate([r[:, None] * inv, col[:, None] * inv], axis=-1)
    cos, sin = jnp.cos(ang), jnp.sin(ang)
    cosf = jnp.concatenate([cos, cos], axis=-1)
    sinf = jnp.concatenate([-sin, sin], axis=-1)
    cosf = jnp.concatenate([cosf, jnp.ones((ctx_len, RET_HD), F32)], axis=0)
    sinf = jnp.concatenate([sinf, jnp.zeros((ctx_len, RET_HD), F32)], axis=0)
    return cosf, sinf


def _shortconv_kernel(u_ref, up_ref, un_ref, w_ref, b_ref, o_ref, *, tpb):
    i = pl.program_id(0)
    r = i % tpb
    first = jnp.logical_or(r == 0, r == tpb - 1)
    last = r >= tpb - 2
    u = u_ref[...].astype(F32)
    prev_row = jnp.where(first, 0.0, up_ref[15:16, :].astype(F32))
    next_row = jnp.where(last, 0.0, un_ref[0:1, :].astype(F32))
    ridx = lax.broadcasted_iota(jnp.int32, u.shape, 0)
    um = jnp.where(ridx == 0, prev_row, pltpu.roll(u, 1, 0))
    up = jnp.where(ridx == TM - 1, next_row, pltpu.roll(u, TM - 1, 0))
    y = w_ref[0:1, :] * um + w_ref[1:2, :] * u + w_ref[2:3, :] * up + b_ref[...]
    o_ref[...] = y.astype(BF16)


def _shortconv(z, width, conv_w, conv_b, tpb):
    r = z.shape[0]
    hb = TM // 16
    nrb = r // 16
    return pl.pallas_call(
        functools.partial(_shortconv_kernel, tpb=tpb),
        out_shape=jax.ShapeDtypeStruct((r, width), BF16),
        grid=(r // TM,),
        in_specs=[pl.BlockSpec((TM, width), lambda i: (i, 0)),
                  pl.BlockSpec((16, width), lambda i: (jnp.maximum(i * hb - 1, 0), 0)),
                  pl.BlockSpec((16, width), lambda i: (jnp.minimum((i + 1) * hb, nrb - 1), 0)),
                  pl.BlockSpec((3, width), lambda i: (0, 0)),
                  pl.BlockSpec((1, width), lambda i: (0, 0))],
        out_specs=pl.BlockSpec((TM, width), lambda i: (i, 0)),
        compiler_params=_cparams("arbitrary"),
        name="hyena_shortconv",
    )(z, z, z, conv_w, conv_b.reshape(1, width))


FEAT_ROWS = 40


def _filter_kernel(w1_ref, b1_ref, w2_ref, b2_ref, w3_ref, b3_ref, w4_ref, fr_ref, dl_ref,
                   f_ref, ss_ref, *, length, pb):
    i = pl.program_id(0)
    pos = (lax.broadcasted_iota(jnp.int32, (1, pb), 1) + i * pb).astype(F32)
    t = pos * (1.0 / (length - 1))
    bands = (HY_EMB - 1) // 2
    w = (2.0 * math.pi) * pos / length
    sub = lax.broadcasted_iota(jnp.int32, (FEAT_ROWS, 1), 0)
    band = jnp.where(sub <= bands, sub - 1, sub - 1 - bands).astype(F32)
    f = 1e-4 + band * ((bands - 1 - 1e-4) / (bands - 1))
    fw = f * w
    feats = jnp.where(sub == 0, t, jnp.where(sub <= bands, jnp.cos(fw),
                                             jnp.where(sub <= 2 * bands, -jnp.sin(fw), 0.0)))
    feats = jnp.concatenate([feats, jnp.zeros((LANES - FEAT_ROWS, pb), F32)], axis=0)
    hdot = _dot3
    a = jnp.sin(fr_ref[:, 0:1] * (hdot(w1_ref[...], feats) + b1_ref[...]))
    a = jnp.sin(fr_ref[:, 1:2] * (hdot(w2_ref[...], a) + b2_ref[...]))
    a = jnp.sin(fr_ref[:, 2:3] * (hdot(w3_ref[...], a) + b3_ref[...]))

    @pl.when(i == 0)
    def _():
        ss_ref[...] = jnp.zeros_like(ss_ref)

    nout = w4_ref.shape[0]
    for cb in range(nout // HY_WIDTH):
        rows = slice(cb * HY_WIDTH, (cb + 1) * HY_WIDTH)
        filt = hdot(w4_ref[rows, :], a) * jnp.exp(-dl_ref[rows, :] * t)
        for j in range(pb // LANES):
            f_ref[rows, j, :] = filt[:, j * LANES:(j + 1) * LANES]
        ss_ref[rows, :] += jnp.sum(filt * filt, axis=1, keepdims=True)


def _hyena_filters_raw(length, w1, b1, w2, b2, w3, b3, w4, freq):
    nout = w4.shape[1]
    pb = min(length, 1024)
    max_decay = math.log(HY_TARGET) / HY_SHORT_DECAY_PCT
    min_decay = math.log(HY_TARGET) / HY_LONG_DECAY_PCT
    deltas = jnp.abs(jnp.linspace(min_decay, max_decay, HY_WIDTH, dtype=F32))
    dl = jnp.tile(deltas, nout // HY_WIDTH).reshape(nout, 1)
    w1p = jnp.pad(w1.T, ((0, 0), (0, LANES - w1.shape[0])))
    col = lambda b: b.reshape(-1, 1)
    full = lambda a: pl.BlockSpec(a.shape, lambda i: tuple(0 for _ in a.shape))
    args = (w1p, col(b1), w2.T, col(b2), w3.T, col(b3), w4.T, freq.T, dl)
    return pl.pallas_call(
        functools.partial(_filter_kernel, length=length, pb=pb),
        out_shape=(jax.ShapeDtypeStruct((nout, length // LANES, LANES), F32),
                   jax.ShapeDtypeStruct((nout, 1), F32)),
        grid=(length // pb,),
        in_specs=[full(a) for a in args],
        out_specs=(pl.BlockSpec((nout, pb // LANES, LANES), lambda i: (0, i, 0)),
                   pl.BlockSpec((nout, 1), lambda i: (0, 0))),
        compiler_params=_cparams("arbitrary"),
        name="hyena_filter_mlp",
    )(*args)


def _dft_tables(na):
    k1n = 2 * na
    n = k1n * FFT_N2
    n1 = np.arange(na)[:, None]
    k1 = np.arange(k1n)[None, :]
    ang = 2.0 * np.pi * ((n1 * k1) % k1n) / k1n
    c, s = np.cos(ang), np.sin(ang)
    ma = np.block([[c, -s], [s, c]])
    n1f = np.arange(FFT_N2)[:, None]
    angf = 2.0 * np.pi * ((n1f * k1) % k1n) / k1n
    live = (n1f < na).astype(np.float64)
    ma_f = np.concatenate([np.cos(angf), -np.sin(angf)], axis=1) * live
    ma_b = np.concatenate([np.cos(angf), np.sin(angf)], axis=1) * live
    n2 = np.arange(FFT_N2)[:, None]
    angt = 2.0 * np.pi * ((n2 * k1) % n) / n
    twr, twi = np.cos(angt), -np.sin(angt)
    k2 = np.arange(FFT_N2)[None, :]
    angb = 2.0 * np.pi * ((n2 * k2) % FFT_N2) / FFT_N2
    cb, sb = np.cos(angb), np.sin(angb)
    mb = np.block([[cb, -sb], [sb, cb]])
    mc = np.block([[cb, sb], [-sb, cb]])
    angd = 2.0 * np.pi * ((np.arange(k1n)[:, None] * np.arange(na)[None, :]) % k1n) / k1n
    cd, sd = np.cos(angd) / n, np.sin(angd) / n
    md = np.block([[cd, sd], [-sd, cd]])
    as_bf = lambda a: jnp.asarray(a, dtype=F32).astype(BF16)
    as_f = lambda a: jnp.asarray(a, dtype=F32)
    mb2 = np.concatenate([mb, mc], axis=0)
    return dict(ma=as_bf(ma), ma_f=as_bf(ma_f), ma_b=as_bf(ma_b), twr=as_f(twr), twi=as_f(twi),
                twr_t=as_f(twr.T), twi_t=as_f(twi.T), mb=as_bf(mb), mb2=as_bf(mb2), mc=as_bf(mc), md=as_bf(md))


def _fwd_stages(x2, ma, twr, twi, mb, cb, k1n):
    a = _dot(x2, ma).reshape(cb, FFT_N2, 2 * k1n)
    ar, ai = a[..., :k1n], a[..., k1n:]
    ar2 = ar * twr - ai * twi
    ai2 = ar * twi + ai * twr
    xt = jnp.concatenate([jnp.swapaxes(ar2, 1, 2), jnp.swapaxes(ai2, 1, 2)], axis=-1)
    return _dot(xt.astype(BF16).reshape(cb * k1n, 2 * FFT_N2), mb).reshape(cb, k1n, 2 * FFT_N2)


def _spectrum_kernel(hf_ref, hb_ref, ssf_ref, ssb_ref, maf_ref, mab_ref, twr_ref, twi_ref, mb2_ref, h_ref, *, na):
    k1n = 2 * na
    cb = h_ref.shape[0]

    def stage_a(ref, ss_ref, m_ref):
        x = ref[0, 0] * lax.rsqrt(ss_ref[0, 0])
        x = jnp.concatenate([x, jnp.zeros((cb, FFT_N2 - na, FFT_N2), F32)], axis=1)
        xt = jnp.swapaxes(x, 1, 2).astype(BF16).reshape(cb * FFT_N2, FFT_N2)
        a = _dot(xt, m_ref[...]).reshape(cb, FFT_N2, 2 * k1n)
        return a[..., :k1n], a[..., k1n:]

    fr, fi = stage_a(hf_ref, ssf_ref, maf_ref)
    br, bi = stage_a(hb_ref, ssb_ref, mab_ref)
    twr, twi = twr_ref[...], twi_ref[...]
    parts = (fr * twr - fi * twi, fr * twi + fi * twr, br * twr + bi * twi, bi * twr - br * twi)
    xt = jnp.concatenate([jnp.swapaxes(p, 1, 2) for p in parts], axis=-1)
    h = _dot(xt.astype(BF16).reshape(cb * k1n, 4 * FFT_N2), mb2_ref[...])
    h_ref[...] = h.reshape(cb, k1n, 2 * FFT_N2).astype(BF16)


def _spectrum(filt5, ss5, order, tb, na):
    c = filt5.shape[2]
    k1n = 2 * na
    cb = FFT_CB
    full = lambda a: pl.BlockSpec(a.shape, lambda i: tuple(0 for _ in a.shape))
    fblk = lambda d: pl.BlockSpec((1, 1, cb, na, FFT_N2), lambda i: (order, d, i, 0, 0))
    sblk = lambda d: pl.BlockSpec((1, 1, cb, 1, 1), lambda i: (order, d, i, 0, 0))
    return pl.pallas_call(
        functools.partial(_spectrum_kernel, na=na),
        out_shape=jax.ShapeDtypeStruct((c, k1n, 2 * FFT_N2), BF16),
        grid=(c // cb,),
        in_specs=[fblk(0), fblk(1), sblk(0), sblk(1),
                  full(tb["ma_f"]), full(tb["ma_b"]), full(tb["twr"]), full(tb["twi"]), full(tb["mb2"])],
        out_specs=pl.BlockSpec((cb, k1n, 2 * FFT_N2), lambda i: (i, 0, 0)),
        compiler_params=_cparams("arbitrary"),
        name="hyena_filter_spectrum",
    )(filt5, filt5, ss5, ss5, tb["ma_f"], tb["ma_b"], tb["twr"], tb["twi"], tb["mb2"])


def _fftconv_kernel(u_ref, g_ref, h_ref, sk_ref, ma_ref, twr_ref, twi_ref, mb_ref, mc_ref,
                    twrt_ref, twit_ref, md_ref, o_ref, *, na):
    k1n = 2 * na
    cb = u_ref.shape[0]
    u = u_ref[...]
    x = _fwd_stages(u.reshape(cb * FFT_N2, k1n), ma_ref[...], twr_ref[...], twi_ref[...], mb_ref[...], cb, k1n)
    xr, xi = x[..., :FFT_N2], x[..., FFT_N2:]
    h = h_ref[...].astype(F32)
    hr, hi = h[..., :FFT_N2], h[..., FFT_N2:]
    y = jnp.concatenate([xr * hr - xi * hi, xr * hi + xi * hr], axis=-1)
    c = _dot(y.astype(BF16).reshape(cb * k1n, 2 * FFT_N2), mc_ref[...]).reshape(cb, k1n, 2 * FFT_N2)
    cr, ci = c[..., :FFT_N2], c[..., FFT_N2:]
    twrt, twit = twrt_ref[...], twit_ref[...]
    cr2 = cr * twrt + ci * twit
    ci2 = ci * twrt - cr * twit
    ct = jnp.concatenate([jnp.swapaxes(cr2, 1, 2), jnp.swapaxes(ci2, 1, 2)], axis=-1)
    d = _dot(ct.astype(BF16).reshape(cb * FFT_N2, 2 * k1n), md_ref[...]).reshape(cb, FFT_N2, k1n)
    uf = u.astype(F32)
    o_ref[...] = (g_ref[...].astype(F32) * (d + sk_ref[...] * uf)).astype(BF16)


def _fftconv(u, gate, h, skip, tb, na):
    c = u.shape[0]
    k1n = 2 * na
    cb = FFT_CB
    full = lambda a: pl.BlockSpec(a.shape, lambda i: tuple(0 for _ in a.shape))
    blk = pl.BlockSpec((cb, FFT_N2, k1n), lambda i: (i, 0, 0))
    return pl.pallas_call(
        functools.partial(_fftconv_kernel, na=na),
        out_shape=jax.ShapeDtypeStruct(u.shape, BF16),
        grid=(c // cb,),
        in_specs=[blk, blk,
                  pl.BlockSpec((cb, k1n, 2 * FFT_N2), lambda i: (i, 0, 0)),
                  pl.BlockSpec((cb, 1, 1), lambda i: (i, 0, 0)),
                  full(tb["ma"]), full(tb["twr"]), full(tb["twi"]), full(tb["mb"]), full(tb["mc"]),
                  full(tb["twr_t"]), full(tb["twi_t"]), full(tb["md"])],
        out_specs=blk,
        compiler_params=_cparams("arbitrary"),
        name="hyena_fftconv",
    )(u, gate, h, skip, tb["ma"], tb["twr"], tb["twi"], tb["mb"], tb["mc"], tb["twr_t"], tb["twi_t"], tb["md"])


def _to_freq_layout(a, na):
    nb, _, c = a.shape
    a = a.reshape(nb, na, FFT_N2, c)
    return jnp.transpose(a, (3, 2, 0, 1)).reshape(c, FFT_N2, nb * na)


def _from_freq_layout(a):
    c, _, lanes = a.shape
    na = lanes // 2
    a = a.reshape(c, FFT_N2, 2, na)
    return jnp.transpose(a, (2, 3, 1, 0)).reshape(2, na * FFT_N2, c)


def _ctx_dft_tables(n):
    big = 2 * n
    j = np.arange(n)[:, None]
    k = np.arange(big)[None, :]
    ang = 2.0 * np.pi * ((j * k) % big) / big
    c, s = np.cos(ang), np.sin(ang)
    fh = np.block([[c, -s], [c, s]])
    ff = np.block([[c, -s], [s, c]])
    fi = np.block([[c.T, s.T], [-s.T, c.T]]) / big
    as_bf = lambda a: jnp.asarray(a, dtype=F32).astype(BF16)
    return as_bf(fh), as_bf(ff), as_bf(fi)


def _ctxconv_kernel(u_ref, hf_ref, hb_ref, ssf_ref, ssb_ref, sk_ref, fh_ref, ff_ref, fi_ref, o_ref):
    z = u_ref[2].astype(F32)
    half = fh_ref.shape[1] // 2
    for n in range(2):
        taps = jnp.concatenate([hf_ref[n] * lax.rsqrt(ssf_ref[n]), hb_ref[n] * lax.rsqrt(ssb_ref[n])], axis=-1)
        h = _dot(taps.astype(BF16), fh_ref[...])
        x = _dot(z.astype(BF16), ff_ref[...])
        hr, hi = h[:, :half], h[:, half:]
        xr, xi = x[:, :half], x[:, half:]
        y = jnp.concatenate([xr * hr - xi * hi, xr * hi + xi * hr], axis=-1)
        conv = _dot(y.astype(BF16), fi_ref[...])
        z = u_ref[n].astype(F32) * (conv + sk_ref[n] * z)
    o_ref[...] = z.astype(BF16)


def _ctxconv(u3, hf, hb, ssf, ssb, skip, tables):
    _, c, lanes = u3.shape
    args = (u3, hf, hb, ssf, ssb, skip) + tuple(tables)
    full = lambda a: pl.BlockSpec(a.shape, lambda i: tuple(0 for _ in a.shape))
    return pl.pallas_call(
        _ctxconv_kernel,
        out_shape=jax.ShapeDtypeStruct((c, lanes), BF16),
        grid=(1,),
        in_specs=[full(a) for a in args],
        out_specs=pl.BlockSpec((c, lanes), lambda i: (0, 0)),
        compiler_params=_cparams("arbitrary"),
        name="hyena_ctxconv",
    )(*args)


def _load_row_tiles(ref, n, stride=SUBLANES, first=0):
    return jnp.concatenate([ref[pl.ds(first + j, n, stride=stride), :] for j in range(SUBLANES)], axis=-1)


def _store_row_tiles(ref, val, n, stride=SUBLANES, first=0):
    for j in range(SUBLANES):
        ref[pl.ds(first + j, n, stride=stride), :] = val[:, j * LANES:(j + 1) * LANES]


def _row_tile(ref, idx, tiles=1):
    size = tiles * SUBLANES
    return ref.at[pl.ds(pl.multiple_of(idx * size, size), size)]


def _top2_rows(vals):
    n = len(vals)
    best_v, best_i = vals[0], jnp.zeros_like(vals[0], dtype=jnp.int32)
    for e in range(1, n):
        take = vals[e] > best_v
        best_v = jnp.where(take, vals[e], best_v)
        best_i = jnp.where(take, e, best_i)
    sec_v = jnp.full_like(vals[0], -jnp.inf)
    sec_i = jnp.zeros_like(best_i)
    for e in range(n):
        take = (best_i != e) & (vals[e] > sec_v)
        sec_v = jnp.where(take, vals[e], sec_v)
        sec_i = jnp.where(take, e, sec_i)
    return best_v, best_i, sec_v, sec_i


def _route(h2, rw_ref, rb_ref, carry_ref, mi_ref, mf_ref, cnt_ref, cols):
    logits = _dot3(rw_ref[...], h2, dot=_dot_nt)
    s = 1.0 / (1.0 + jnp.exp(-logits))
    sel = s + rb_ref[...]
    srow = [s[e:e + 1, :] for e in range(N_EXPERTS)]
    selrow = [sel[e:e + 1, :] for e in range(N_EXPERTS)]
    gscore = []
    for g in range(N_GROUPS):
        grp = selrow[g * EXPERTS_PER_GROUP:(g + 1) * EXPERTS_PER_GROUP]
        bv, _, sv, _ = _top2_rows(grp)
        gscore.append(bv + sv)
    best_g = jnp.zeros_like(gscore[0], dtype=jnp.int32)
    best_s = gscore[0]
    for g in range(1, N_GROUPS):
        take = gscore[g] > best_s
        best_s = jnp.where(take, gscore[g], best_s)
        best_g = jnp.where(take, g, best_g)
    cand_sel, cand_s = [], []
    for j in range(EXPERTS_PER_GROUP):
        cs, ca = selrow[j], srow[j]
        for g in range(1, N_GROUPS):
            cs = jnp.where(best_g == g, selrow[g * EXPERTS_PER_GROUP + j], cs)
            ca = jnp.where(best_g == g, srow[g * EXPERTS_PER_GROUP + j], ca)
        cand_sel.append(cs)
        cand_s.append(ca)
    _, i1, _, i2 = _top2_rows(cand_sel)
    a1, a2 = cand_s[0], cand_s[0]
    for j in range(1, EXPERTS_PER_GROUP):
        a1 = jnp.where(i1 == j, cand_s[j], a1)
        a2 = jnp.where(i2 == j, cand_s[j], a2)
    den = a1 + a2
    w1, w2 = a1 / den, a2 / den
    swap = i2 < i1
    lo = jnp.where(swap, i2, i1)
    hi = jnp.where(swap, i1, i2)
    w_lo = jnp.where(swap, w2, w1)
    w_hi = jnp.where(swap, w1, w2)
    pair = jnp.where(lo == 0, 0, jnp.where(lo == 1, 3, 5)) + (hi - lo - 1)
    cls = best_g * len(PAIRS) + pair
    cidx = lax.broadcasted_iota(jnp.int32, (CLASS_ROWS, TM), 0)
    hit = cidx == cls
    onehot = jnp.where(hit, 1.0, 0.0)
    tri = jnp.where(lax.broadcasted_iota(jnp.int32, (TM, TM), 0) < lax.broadcasted_iota(jnp.int32, (TM, TM), 1),
                    1.0, 0.0).astype(BF16)
    rank = _dot(onehot.astype(BF16), tri) + carry_ref[:, 0:1]
    rk = jnp.sum(jnp.where(hit, rank, 0.0), axis=0, keepdims=True)
    carry_ref[...] = carry_ref[...] + jnp.sum(onehot, axis=1, keepdims=True)
    mi_ref[:, cols] = jnp.concatenate([cls, rk.astype(jnp.int32), jnp.zeros((6, TM), jnp.int32)], axis=0)
    mf_ref[:, cols] = jnp.concatenate([w_lo, w_hi, jnp.zeros((6, TM), F32)], axis=0)
    cnt_ref[...] = carry_ref[...]


def _post_norm_mod(x, y, m_ref, lng, lnb, gate_row, sh_row, sc_row):
    u = DN_ALPHA * x + m_ref[0, gate_row:gate_row + 1, :] * y
    mu = jnp.mean(u, axis=-1, keepdims=True)
    var = jnp.mean(jnp.square(u - mu), axis=-1, keepdims=True)
    xn = (u - mu) * lax.rsqrt(var + LN_EPS) * lng + lnb
    if sh_row is None:
        return xn, None
    return xn, xn * (1.0 + m_ref[0, sc_row:sc_row + 1, :]) + m_ref[0, sh_row:sh_row + 1, :]


def _outproj_kernel(*refs, layer0, tpb):
    nd = 8 if layer0 else 5
    halves = (refs[:nd], refs[nd:2 * nd])
    w_ref, lng_ref, lnb_ref, rw_ref, rb_ref, xo_ref, h2_ref, mi_ref, mf_ref, cnt_ref, carry_ref = refs[2 * nd:]
    i = pl.program_id(0)

    @pl.when(i == 0)
    def _():
        carry_ref[...] = jnp.zeros_like(carry_ref)

    for half, data in enumerate(halves):
        if layer0:
            of_ref, ob_ref, g_ref, hyl_ref, hyc_ref, xl_ref, xc_ref, m_ref = data
            is_ctx = (2 * i + half) % tpb == tpb - 1
            x = jnp.where(is_ctx, xc_ref[...], xl_ref[...])
            hy = jnp.where(is_ctx, hyc_ref[...], hyl_ref[...])
            o = of_ref[...].astype(F32) + ob_ref[...].astype(F32)
            parts = []
            for h in range(RET_HEADS):
                oh = o[:, h * RET_HD:(h + 1) * RET_HD]
                mu = jnp.mean(oh, axis=-1, keepdims=True)
                var = jnp.mean(jnp.square(oh - mu), axis=-1, keepdims=True)
                parts.append((oh - mu) * lax.rsqrt(var + LN_EPS))
            yret = (jnp.concatenate(parts, axis=-1) * _silu(g_ref[...].astype(F32))).astype(BF16)
            wr = yret.shape[1]
            y = _dot(yret, w_ref[:wr, :]) + _dot(hy, w_ref[wr:, :])
        else:
            of_ref, ob_ref, g_ref, x_ref, m_ref = data
            x = x_ref[...]
            o = of_ref[...].astype(F32) + ob_ref[...].astype(F32)
            parts = []
            for h in range(GLA_HEADS):
                oh = o[:, h * GLA_DV:(h + 1) * GLA_DV]
                parts.append(oh * lax.rsqrt(jnp.mean(jnp.square(oh), axis=-1, keepdims=True) + LN_EPS))
            a = (jnp.concatenate(parts, axis=-1) * _silu(g_ref[...].astype(F32))).astype(BF16)
            y = _dot(a, w_ref[...])
        rows = slice(half * TM, (half + 1) * TM)
        xn, h2 = _post_norm_mod(x, y, m_ref, lng_ref[...], lnb_ref[...], 2, 3, 4)
        xo_ref[rows, :] = xn
        _store_row_tiles(h2_ref, h2, TM, first=half * TM * SUBLANES)
        _route(h2, rw_ref, rb_ref, carry_ref, mi_ref, mf_ref, cnt_ref, rows)


def _outproj_call(layer0, tpb, n_tiles, d, half_specs, half_args, w, lng, lnb, router_w, router_bias):
    rows = n_tiles * TM
    full2 = lambda a: pl.BlockSpec(a.shape, lambda i: (0, 0))
    rw_t = router_w.T
    rb = router_bias.reshape(N_EXPERTS, 1)
    return pl.pallas_call(
        functools.partial(_outproj_kernel, layer0=layer0, tpb=tpb),
        out_shape=(jax.ShapeDtypeStruct((rows, d), F32),
                   jax.ShapeDtypeStruct((rows * SUBLANES, LANES), F32),
                   jax.ShapeDtypeStruct((8, rows), jnp.int32),
                   jax.ShapeDtypeStruct((8, rows), F32),
                   jax.ShapeDtypeStruct((CLASS_ROWS, LANES), F32)),
        grid=(n_tiles // 2,),
        in_specs=list(half_specs(0)) + list(half_specs(1)) + [full2(w), full2(lng), full2(lnb), full2(rw_t), full2(rb)],
        out_specs=(pl.BlockSpec((2 * TM, d), lambda i: (i, 0)),
                   pl.BlockSpec((2 * TM * SUBLANES, LANES), lambda i: (i, 0)),
                   pl.BlockSpec((8, 2 * TM), lambda i: (0, i)),
                   pl.BlockSpec((8, 2 * TM), lambda i: (0, i)),
                   pl.BlockSpec((CLASS_ROWS, LANES), lambda i: (0, 0))),
        scratch_shapes=[pltpu.VMEM((CLASS_ROWS, LANES), F32)],
        compiler_params=_cparams("arbitrary"),
        name="outproj_norm0" if layer0 else "outproj_norm1",
    )(*half_args, *half_args, w, lng, lnb, rw_t, rb)


def _dispatch_kernel(slots_ref, zoff_ref, h_ref, xs_ref, zero_ref, sem):
    i = pl.program_id(0)

    @pl.when(i == 0)
    def _():
        zero_ref[...] = jnp.zeros_like(zero_ref)
        for e in range(N_CLASSES):
            pltpu.make_async_copy(zero_ref, _row_tile(xs_ref, zoff_ref[e], TE), sem).start()
        for e in range(N_CLASSES):
            pltpu.make_async_copy(zero_ref, _row_tile(xs_ref, zoff_ref[e], TE), sem).wait()
        first_free = zoff_ref[N_CLASSES]
        n_all = xs_ref.shape[0] // (TE * SUBLANES)

        def fill(t, carry):
            pltpu.make_async_copy(zero_ref, _row_tile(xs_ref, t, TE), sem).start()
            return carry

        def fill_wait(t, carry):
            pltpu.make_async_copy(zero_ref, _row_tile(xs_ref, t, TE), sem).wait()
            return carry

        lax.fori_loop(first_free, n_all, fill, 0)
        lax.fori_loop(first_free, n_all, fill_wait, 0)

    def row_copy(r):
        return pltpu.make_async_copy(_row_tile(h_ref, r), _row_tile(xs_ref, slots_ref[0, 0, r]), sem)

    def issue(r2, carry):
        row_copy(2 * r2).start(priority=0)
        row_copy(2 * r2 + 1).start(priority=1)
        return carry

    lax.fori_loop(0, TM // 2, issue, 0, unroll=4)

    def drain(r, carry):
        row_copy(r).wait()
        return carry

    lax.fori_loop(0, TM, drain, 0, unroll=8)


def _dispatch(h2, slots3, zoff, p_rows):
    n_tiles = h2.shape[0] // (TM * SUBLANES)
    grid_spec = pltpu.PrefetchScalarGridSpec(
        num_scalar_prefetch=0,
        grid=(n_tiles,),
        in_specs=[pl.BlockSpec((1, 1, TM), lambda i: (i, 0, 0), memory_space=pltpu.SMEM),
                  pl.BlockSpec(memory_space=pltpu.SMEM),
                  pl.BlockSpec((TM * SUBLANES, LANES), lambda i: (i, 0))],
        out_specs=pl.BlockSpec(memory_space=pl.ANY),
        scratch_shapes=[pltpu.VMEM((TE * SUBLANES, LANES), F32), pltpu.SemaphoreType.DMA(())],
    )
    return pl.pallas_call(
        _dispatch_kernel,
        out_shape=jax.ShapeDtypeStruct((p_rows * SUBLANES, LANES), F32),
        grid_spec=grid_spec,
        compiler_params=_cparams("arbitrary"),
        name="moe_dispatch",
    )(slots3, zoff, h2)


def _experts_kernel(ea_ref, eb_ref, tv_ref, x_ref, wga_ref, wua_ref, wda_ref, wgb_ref, wub_ref, wdb_ref, y_ref):
    j = pl.program_id(0)

    @pl.when(tv_ref[j] > 0)
    def _():
        x = _load_row_tiles(x_ref, TE).astype(BF16)
        for which, (wg, wu, wd) in enumerate(((wga_ref, wua_ref, wda_ref), (wgb_ref, wub_ref, wdb_ref))):
            g = _dot(x, wg[0])
            u = _dot(x, wu[0])
            y = _dot((_silu(g) * u).astype(BF16), wd[0])
            _store_row_tiles(y_ref, y, TE, stride=2 * SUBLANES, first=which * SUBLANES)

    @pl.when(tv_ref[j] == 0)
    def _():
        y_ref[...] = jnp.zeros_like(y_ref)


def _experts(xs, tile_ea, tile_eb, tile_valid, wg, wu, wd):
    n_tiles = xs.shape[0] // (TE * SUBLANES)
    d, hdim = wg.shape[1], wg.shape[2]
    wa = lambda shp: pl.BlockSpec(shp, lambda j, ea, eb, tv: (ea[j], 0, 0))
    wb = lambda shp: pl.BlockSpec(shp, lambda j, ea, eb, tv: (eb[j], 0, 0))
    grid_spec = pltpu.PrefetchScalarGridSpec(
        num_scalar_prefetch=3,
        grid=(n_tiles,),
        in_specs=[pl.BlockSpec((TE * SUBLANES, LANES), lambda j, ea, eb, tv: (j, 0)),
                  wa((1, d, hdim)), wa((1, d, hdim)), wa((1, hdim, d)),
                  wb((1, d, hdim)), wb((1, d, hdim)), wb((1, hdim, d))],
        out_specs=pl.BlockSpec((2 * TE * SUBLANES, LANES), lambda j, ea, eb, tv: (j, 0)),
    )
    return pl.pallas_call(
        _experts_kernel,
        out_shape=jax.ShapeDtypeStruct((2 * xs.shape[0], LANES), F32),
        grid_spec=grid_spec,
        compiler_params=_cparams("arbitrary"),
        name="moe_experts",
    )(tile_ea, tile_eb, tile_valid, xs, wg, wu, wd, wg, wu, wd)


def _combine_kernel(slots_ref, next_slots_ref, ys_ref, wts_ref, x_ref, m_ref, lng_ref, lnb_ref, o_ref, buf_ref, sem):
    i = pl.program_id(0)
    cur = i % 2

    def row_copy(s_ref, b, r):
        return pltpu.make_async_copy(_row_tile(ys_ref, s_ref[0, 0, r], 2), _row_tile(buf_ref.at[b], r, 2), sem.at[b])

    def issue_all(s_ref, b):
        def issue(r2, carry):
            row_copy(s_ref, b, 2 * r2).start(priority=0)
            row_copy(s_ref, b, 2 * r2 + 1).start(priority=1)
            return carry

        lax.fori_loop(0, TM // 2, issue, 0, unroll=4)

    @pl.when(i == 0)
    def _():
        issue_all(slots_ref, 0)

    @pl.when(i + 1 < pl.num_programs(0))
    def _():
        issue_all(next_slots_ref, 1 - cur)

    def drain(r, carry):
        row_copy(slots_ref, cur, r).wait()
        return carry

    lax.fori_loop(0, TM, drain, 0, unroll=8)
    buf = buf_ref.at[cur]
    y_lo = _load_row_tiles(buf, TM, stride=2 * SUBLANES)
    y_hi = _load_row_tiles(buf, TM, stride=2 * SUBLANES, first=SUBLANES)
    y = wts_ref[:, 0:1] * y_lo + wts_ref[:, 1:2] * y_hi
    xn, _ = _post_norm_mod(x_ref[...], y, m_ref, lng_ref[...], lnb_ref[...], 5, None, None)
    o_ref[...] = xn


def _combine(ys, slots3, wts, x, mods, lng, lnb, cond_of, n_tiles):
    d = x.shape[1]
    grid_spec = pltpu.PrefetchScalarGridSpec(
        num_scalar_prefetch=0,
        grid=(n_tiles,),
        in_specs=[pl.BlockSpec((1, 1, TM), lambda i: (i, 0, 0), memory_space=pltpu.SMEM),
                  pl.BlockSpec((1, 1, TM), lambda i: (jnp.minimum(i + 1, n_tiles - 1), 0, 0),
                               memory_space=pltpu.SMEM),
                  pl.BlockSpec(memory_space=pl.ANY),
                  pl.BlockSpec((TM, 2), lambda i: (i, 0)),
                  pl.BlockSpec((TM, d), lambda i: (i, 0)),
                  pl.BlockSpec((1, 6, d), lambda i: (cond_of(i), 0, 0)),
                  pl.BlockSpec((1, d), lambda i: (0, 0)),
                  pl.BlockSpec((1, d), lambda i: (0, 0))],
        out_specs=pl.BlockSpec((TM, d), lambda i: (i, 0)),
        scratch_shapes=[pltpu.VMEM((2, 2 * TM * SUBLANES, LANES), F32), pltpu.SemaphoreType.DMA((2,))],
    )
    return pl.pallas_call(
        _combine_kernel,
        out_shape=jax.ShapeDtypeStruct((n_tiles * TM, d), F32),
        grid_spec=grid_spec,
        compiler_params=_cparams("arbitrary"),
        name="moe_combine",
    )(slots3, slots3, ys, wts, x, mods, lng, lnb)


def _moe(h2, mi, mf, cnt, x, cond_of, mods, lng, lnb, wg, wu, wd):
    t = x.shape[0]
    n_tiles = t // TM
    counts = cnt[:N_CLASSES, 0].astype(jnp.int32)
    padded = ((counts + TE - 1) // TE) * TE
    ends = jnp.cumsum(padded)
    offs = ends - padded
    cls = mi[0]
    cids = jnp.arange(N_CLASSES, dtype=jnp.int32)[:, None]
    slots = jnp.sum(jnp.where(cls[None] == cids, offs[:, None], 0), axis=0) + mi[1]
    slots3 = slots.reshape(n_tiles, 1, TM)
    n_cls_tiles = t // TE + N_CLASSES
    total_tiles = ends[-1] // TE
    tile_ids = jnp.arange(n_cls_tiles, dtype=jnp.int32)
    tile_valid = (tile_ids < total_tiles).astype(jnp.int32)
    tile_last = jnp.minimum(tile_ids, total_tiles - 1)
    tile_class = jnp.sum((tile_last[:, None] * TE >= ends[None, :]).astype(jnp.int32), axis=1)
    tile_class = jnp.minimum(tile_class, N_CLASSES - 1)
    pair_lo = jnp.asarray([p[0] for p in PAIRS], jnp.int32)
    pair_hi = jnp.asarray([p[1] for p in PAIRS], jnp.int32)
    pair_sel = (tile_class % len(PAIRS))[:, None] == jnp.arange(len(PAIRS), dtype=jnp.int32)[None, :]
    group0 = (tile_class // len(PAIRS)) * EXPERTS_PER_GROUP
    tile_ea = (group0 + jnp.sum(jnp.where(pair_sel, pair_lo[None, :], 0), axis=1)).astype(jnp.int32)
    tile_eb = (group0 + jnp.sum(jnp.where(pair_sel, pair_hi[None, :], 0), axis=1)).astype(jnp.int32)
    zoff = jnp.concatenate([jnp.maximum(ends // TE - 1, 0), total_tiles[None]]).astype(jnp.int32)
    xs = _dispatch(h2, slots3, zoff, n_cls_tiles * TE)
    ys = _experts(xs, tile_ea, tile_eb, tile_valid, wg, wu, wd)
    wts = jnp.transpose(mf[0:2])
    return _combine(ys, slots3, wts, x, mods, lng, lnb, cond_of, n_tiles)


def _chunk_cumsum(x, tri):
    hi = x.astype(BF16)
    r1 = x - hi.astype(F32)
    mid = r1.astype(BF16)
    lo = (r1 - mid.astype(F32)).astype(BF16)
    return _dot(tri, hi) + _dot(tri, mid) + _dot(tri, lo)


def _gla_kernel(q_ref, k_ref, v_ref, lr_ref, gw_ref, gb_ref, o_ref, s_ref, *, reverse, nb):
    C = GLA_CHUNK
    nchunk = TM // C

    @pl.when(pl.program_id(0) == 0)
    def _():
        s_ref[...] = jnp.zeros_like(s_ref)

    ii = lax.broadcasted_iota(jnp.int32, (C, C), 0)
    jj = lax.broadcasted_iota(jnp.int32, (C, C), 1)
    keep = (jj >= ii) if reverse else (jj <= ii)
    tri = jnp.where(keep, 1.0, 0.0).astype(BF16)
    order = tuple(reversed(range(nchunk))) if reverse else tuple(range(nchunk))
    end_row = 0 if reverse else C - 1
    gw = gw_ref[...]
    gb = gb_ref[...]
    for b in range(nb):
        pre = _dot(lr_ref[b], gw) + gb
        la_all = (jnp.minimum(pre, 0.0) - jnp.log(1.0 + jnp.exp(-jnp.abs(pre)))) * (1.0 / GLA_TAU)
        for c in order:
            rows = slice(c * C, (c + 1) * C)
            bc = _chunk_cumsum(la_all[rows, :], tri)
            bend = bc[end_row:end_row + 1, :]
            eb = jnp.exp(bc)
            enb = jnp.exp(-bc)
            ekb = jnp.exp(bend - bc)
            ebend = jnp.exp(bend)
            for h in range(GLA_HEADS):
                kc = slice(h * GLA_DK, (h + 1) * GLA_DK)
                vc = slice(h * GLA_DV, (h + 1) * GLA_DV)
                q = q_ref[b, rows, kc].astype(F32) * (GLA_DK ** -0.5)
                k = k_ref[b, rows, kc].astype(F32)
                v = v_ref[b, rows, vc]
                qd = (q * eb[:, kc]).astype(BF16)
                kd = (k * enb[:, kc]).astype(BF16)
                st = s_ref[b, h]
                scores = jnp.where(keep, _dot_nt(qd, kd), 0.0)
                o = _dot(scores.astype(BF16), v) + _dot_nt(qd, st.astype(BF16))
                s_ref[b, h] = st * ebend[:, kc] + _dot_tn(v, (k * ekb[:, kc]).astype(BF16))
                o_ref[b, rows, vc] = o.astype(BF16)


def _gla(z3, gw, gb, reverse):
    nb, s, _ = z3.shape
    nt = s // TM
    kw = GLA_HEADS * GLA_DK
    vw = GLA_HEADS * GLA_DV

    if reverse:
        def jmap(j):
            return jnp.where(j == 0, nt - 1, nt - 1 - j)
    else:
        def jmap(j):
            return (j + nt - 1) % nt

    lr_blk = (2 * kw + 2 * vw) // LANES
    return pl.pallas_call(
        functools.partial(_gla_kernel, reverse=reverse, nb=nb),
        out_shape=jax.ShapeDtypeStruct((nb, s, vw), BF16),
        grid=(nt,),
        in_specs=[pl.BlockSpec((nb, TM, kw), lambda j: (0, jmap(j), 0)),
                  pl.BlockSpec((nb, TM, kw), lambda j: (0, jmap(j), 1)),
                  pl.BlockSpec((nb, TM, vw), lambda j: (0, jmap(j), 1)),
                  pl.BlockSpec((nb, TM, LANES), lambda j: (0, jmap(j), lr_blk)),
                  pl.BlockSpec((LANES, kw), lambda j: (0, 0)),
                  pl.BlockSpec((1, kw), lambda j: (0, 0))],
        out_specs=pl.BlockSpec((nb, TM, vw), lambda j: (0, jmap(j), 0)),
        scratch_shapes=[pltpu.VMEM((nb, GLA_HEADS, GLA_DV, GLA_DK), F32)],
        compiler_params=_cparams("arbitrary"),
        name="gla_bwd" if reverse else "gla_fwd",
    )(z3, z3, z3, z3, gw, gb)


def kernel(x, c, ctx, c_ctx, mod_w, mod_b, ln_g, ln_b, ab_w_in, ret_log_decay_f, ret_log_decay_b, hy_conv_w, hy_conv_b, hy_w1, hy_b1, hy_w2, hy_b2, hy_w3, hy_b3, hy_w4, hy_freq, hy_skip, ab_w_out, gla_w_in, gla_gate_w_f, gla_gate_b_f, gla_gate_w_b, gla_gate_b_b, gla_w_out, router_w, router_bias, exp_w_gate, exp_w_up, exp_w_down):
    nb, length, d = x.shape
    ctx_len = ctx.shape[1]
    assert nb == 2 and ctx_len == TM and length % TM == 0 and length % (FFT_N2 * 2) == 0
    s = length + ctx_len
    tpb = s // TM
    r = nb * s
    lat_tiles = tpb - 1

    cond = jnp.zeros((8, d), F32).at[0].set(c[0]).at[1].set(c_ctx).at[2].set(c[1]).at[3].set(c_ctx)
    mods = _ada(cond, mod_w, mod_b).reshape(DEPTH, 8, 6, d)

    def cond_of(i):
        return 2 * (i // tpb) + jnp.where(i % tpb == tpb - 1, 1, 0)

    x_lat = x.reshape(nb * length, d)
    x_ctx = ctx.reshape(nb * ctx_len, d)

    ret_w = RET_HEADS * RET_HD
    hy_w = 3 * HY_WIDTH
    w_in0 = jnp.concatenate([ab_w_in[0][:, 4 * ret_w:], ab_w_in[0][:, :4 * ret_w]], axis=1).astype(BF16)
    z0 = _inproj((x_lat, x_ctx), mods[0], w_in0, tpb, r)
    z0_3 = z0.reshape(nb, s, -1)
    cosf, sinf = _rope_tables(length, ctx_len)
    o_f = _retention(z0_3, hy_w, ret_log_decay_f[0], cosf, sinf, False)
    o_b = _retention(z0_3, hy_w, ret_log_decay_b[0], cosf, sinf, True)

    hyu = _shortconv(z0, hy_w, hy_conv_w[0], hy_conv_b[0], tpb)
    hyu3 = hyu.reshape(nb, s, hy_w)
    na = length // FFT_N2
    tb = _dft_tables(na)
    filt_w = (hy_w1[0], hy_b1[0], hy_w2[0], hy_b2[0], hy_w3[0], hy_b3[0], hy_w4[0], hy_freq[0])
    skip = hy_skip[0]

    filt, ss = _hyena_filters_raw(length, *filt_w)
    filt5 = filt.reshape(2, 2, HY_WIDTH, na, FFT_N2)
    ss5 = ss.reshape(2, 2, HY_WIDTH, 1, 1)
    seq = hyu3[:, :length]
    zcur = _to_freq_layout(seq[..., 2 * HY_WIDTH:], na)
    for n in range(2):
        hspec = _spectrum(filt5, ss5, n, tb, na)
        gate = _to_freq_layout(seq[..., n * HY_WIDTH:(n + 1) * HY_WIDTH], na)
        zcur = _fftconv(zcur, gate, hspec, skip[n].reshape(-1, 1, 1), tb, na)
    y_hy_lat = _from_freq_layout(zcur)

    filt_c, ss_c = _hyena_filters_raw(ctx_len, *filt_w)
    filt_c = filt_c.reshape(2, 2, HY_WIDTH, ctx_len)
    ss_c = ss_c.reshape(2, 2, HY_WIDTH, 1)
    u_ctx = jnp.transpose(hyu3[:, length:].reshape(nb, ctx_len, 3, HY_WIDTH), (2, 3, 0, 1))
    y_ctx_t = _ctxconv(u_ctx.reshape(3, HY_WIDTH, nb * ctx_len), filt_c[:, 0], filt_c[:, 1], ss_c[:, 0], ss_c[:, 1],
                       skip.reshape(2, HY_WIDTH, 1), _ctx_dft_tables(ctx_len))
    y_hy_ctx = jnp.transpose(y_ctx_t.reshape(HY_WIDTH, nb, ctx_len), (1, 2, 0))

    gate_blk = (hy_w + 3 * ret_w) // ret_w

    def specs0(h):
        tile = lambda i: 2 * i + h
        trow = lambda i: (tile(i), 0)
        return [pl.BlockSpec((TM, ret_w), trow), pl.BlockSpec((TM, ret_w), trow),
                pl.BlockSpec((TM, ret_w), lambda i: (tile(i), gate_blk)),
                *_lat_or_ctx_specs(HY_WIDTH, tpb, tile), *_lat_or_ctx_specs(d, tpb, tile),
                pl.BlockSpec((1, 6, d), lambda i: (cond_of(tile(i)), 0, 0))]

    x_mid, h2, mi, mf, cnt = _outproj_call(
        True, tpb, r // TM, d, specs0,
        (o_f.reshape(r, ret_w), o_b.reshape(r, ret_w), z0, y_hy_lat.reshape(nb * length, HY_WIDTH),
         y_hy_ctx.reshape(nb * ctx_len, HY_WIDTH), x_lat, x_ctx, mods[0]),
        ab_w_out[0].astype(BF16), ln_g[0, 0].reshape(1, d), ln_b[0, 0].reshape(1, d), router_w, router_bias)

    x1 = _moe(h2, mi, mf, cnt, x_mid, cond_of, mods[0], ln_g[0, 1].reshape(1, d), ln_b[0, 1].reshape(1, d),
              exp_w_gate[0].astype(BF16), exp_w_up[0].astype(BF16), exp_w_down[0].astype(BF16))

    kw = GLA_HEADS * GLA_DK
    vw = GLA_HEADS * GLA_DV
    n_in = gla_w_in.shape[2]
    n_pad = 2 * kw + 2 * vw + LANES
    w_in1 = jnp.pad(gla_w_in[0], ((0, 0), (0, n_pad - n_in))).astype(BF16)
    z1 = _inproj(x1, mods[1], w_in1, tpb, r)
    z1_3 = z1.reshape(nb, s, n_pad)
    gw_f = jnp.zeros((LANES, kw), F32).at[:GLA_RANK].set(gla_gate_w_f[0]).astype(BF16)
    gw_b = jnp.zeros((LANES, kw), F32).at[GLA_RANK:2 * GLA_RANK].set(gla_gate_w_b[0]).astype(BF16)
    g_f = _gla(z1_3, gw_f, gla_gate_b_f[0].reshape(1, kw), False)
    g_b = _gla(z1_3, gw_b, gla_gate_b_b[0].reshape(1, kw), True)

    n_lat = nb * lat_tiles

    def lat_tile(i):
        return (i // lat_tiles) * tpb + i % lat_tiles

    def lat_cond(i):
        return 2 * (i // lat_tiles)

    def specs1(h):
        lrow = lambda i: (lat_tile(2 * i + h), 0)
        return [pl.BlockSpec((TM, vw), lrow), pl.BlockSpec((TM, vw), lrow),
                pl.BlockSpec((TM, vw), lambda i: (lat_tile(2 * i + h), 2)),
                pl.BlockSpec((TM, d), lrow),
                pl.BlockSpec((1, 6, d), lambda i: (lat_cond(2 * i + h), 0, 0))]

    x_mid1, h2_1, mi1, mf1, cnt1 = _outproj_call(
        False, tpb, n_lat, d, specs1, (g_f.reshape(r, vw), g_b.reshape(r, vw), z1, x1, mods[1]),
        gla_w_out[0].astype(BF16), ln_g[1, 0].reshape(1, d), ln_b[1, 0].reshape(1, d), router_w, router_bias)

    out = _moe(h2_1, mi1, mf1, cnt1, x_mid1, lat_cond, mods[1], ln_g[1, 1].reshape(1, d), ln_b[1, 1].reshape(1, d),
               exp_w_gate[1].astype(BF16), exp_w_up[1].astype(BF16), exp_w_down[1].astype(BF16))
    return out.reshape(nb, length, d)
```

```python
import functools
import math

import numpy as np
import jax
import jax.numpy as jnp
from jax import lax
from jax.experimental import pallas as pl
from jax.experimental.pallas import tpu as pltpu

F32 = jnp.float32
BF16 = jnp.bfloat16

GRID_W = 64
RET_HEADS = 4
RET_HD = 128
RET_CHUNK = 128
ROPE_BASE = 10000.0
HY_WIDTH = 512
HY_EMB = 33
HY_FFN = 64
HY_SHORT_DECAY_PCT = 0.3
HY_LONG_DECAY_PCT = 1.5
HY_TARGET = 1e-2
GLA_HEADS = 4
GLA_DK = 128
GLA_DV = 256
GLA_RANK = 16
GLA_TAU = 16.0
GLA_CHUNK = 64
N_EXPERTS = 16
N_GROUPS = 4
EXPERTS_PER_GROUP = 4
LN_EPS = 1e-5
DEPTH = 2
DN_ALPHA = (2.0 * DEPTH) ** 0.25

PAIRS = ((0, 1), (0, 2), (0, 3), (1, 2), (1, 3), (2, 3))
N_CLASSES = N_GROUPS * len(PAIRS)
CLASS_ROWS = 32

LANES = 128
SUBLANES = 8
TM = 256
TE = 256
FFT_N2 = 128
FFT_CB = 16
VMEM_LIMIT = 48 * 1024 * 1024

HIGHEST = lax.Precision.HIGHEST


def _cparams(*sem):
    return pltpu.CompilerParams(dimension_semantics=sem, vmem_limit_bytes=VMEM_LIMIT)


def _silu(v):
    return v * (1.0 / (1.0 + jnp.exp(-v)))


def _dot(a, b):
    return jnp.dot(a, b, preferred_element_type=F32)


def _dot_nt(a, b):
    return lax.dot_general(a, b, (((1,), (1,)), ((), ())), preferred_element_type=F32)


def _dot_tn(a, b):
    return lax.dot_general(a, b, (((0,), (0,)), ((), ())), preferred_element_type=F32)


def _split2(a):
    hi = a.astype(BF16)
    return hi, (a - hi.astype(F32)).astype(BF16)


def _dot3(a, b, dot=_dot):
    ah, al = _split2(a)
    bh, bl = _split2(b)
    return dot(ah, bh) + dot(ah, bl) + dot(al, bh)


def _ada_kernel(c_ref, w_ref, b_ref, o_ref):
    a = _silu(c_ref[...])
    o_ref[0] = jnp.dot(a, w_ref[0], preferred_element_type=F32, precision=HIGHEST) + b_ref[0]


def _ada(cond, mod_w, mod_b):
    depth, d, n = mod_w.shape
    nt = n // 4
    return pl.pallas_call(
        _ada_kernel,
        out_shape=jax.ShapeDtypeStruct((depth, 8, n), F32),
        grid=(depth, n // nt),
        in_specs=[pl.BlockSpec((8, d), lambda l, j: (0, 0)),
                  pl.BlockSpec((1, d, nt), lambda l, j: (l, 0, j)),
                  pl.BlockSpec((1, 1, nt), lambda l, j: (l, 0, j))],
        out_specs=pl.BlockSpec((1, 8, nt), lambda l, j: (l, 0, j)),
        compiler_params=_cparams("arbitrary", "arbitrary"),
        name="ada_mod",
    )(cond, mod_w, mod_b.reshape(depth, 1, n))


def _inproj_kernel(*refs, chunks, tpb, split):
    if split:
        x_ref, c_ref, m_ref, w_ref, o_ref = refs
        x = jnp.where(pl.program_id(0) % tpb == tpb - 1, c_ref[...], x_ref[...])
    else:
        x_ref, m_ref, w_ref, o_ref = refs
        x = x_ref[...]
    sh = m_ref[0, 0:1, :]
    sc = m_ref[0, 1:2, :]
    h = (x * (1.0 + sc) + sh).astype(BF16)
    for lo, hi in chunks:
        o_ref[:, lo:hi] = _dot(h, w_ref[:, lo:hi]).astype(BF16)


def _lat_or_ctx_specs(width, tpb, tile_of=lambda i: i):
    lat_tiles = tpb - 1
    lat = lambda i: ((tile_of(i) // tpb) * lat_tiles + jnp.minimum(tile_of(i) % tpb, lat_tiles - 1), 0)
    ctx = lambda i: (tile_of(i) // tpb, 0)
    return pl.BlockSpec((TM, width), lat), pl.BlockSpec((TM, width), ctx)


def _col_chunks(n, width=512):
    out, lo = [], 0
    while lo < n:
        hi = min(lo + width, n)
        out.append((lo, hi))
        lo = hi
    return tuple(out)


def _inproj(src, mods, w, tiles_per_batch, r):
    d, n = w.shape
    tpb = tiles_per_batch
    split = isinstance(src, tuple)

    def cond_of(i):
        return 2 * (i // tpb) + jnp.where(i % tpb == tpb - 1, 1, 0)

    src_specs = list(_lat_or_ctx_specs(d, tpb)) if split else [pl.BlockSpec((TM, d), lambda i: (i, 0))]
    src_args = tuple(src) if split else (src,)
    return pl.pallas_call(
        functools.partial(_inproj_kernel, chunks=_col_chunks(n), tpb=tpb, split=split),
        out_shape=jax.ShapeDtypeStruct((r, n), BF16),
        grid=(r // TM,),
        in_specs=src_specs + [pl.BlockSpec((1, 6, d), lambda i: (cond_of(i), 0, 0)),
                              pl.BlockSpec((d, n), lambda i: (0, 0))],
        out_specs=pl.BlockSpec((TM, n), lambda i: (i, 0)),
        compiler_params=_cparams("arbitrary"),
        name="inproj",
    )(*src_args, mods, w)


def _ret_kernel(lg_ref, q_ref, k_ref, v_ref, cos_ref, sin_ref, o_ref, s_ref, *, reverse, nb):
    C = RET_CHUNK

    @pl.when(pl.program_id(0) == 0)
    def _():
        s_ref[...] = jnp.zeros_like(s_ref)

    ii = lax.broadcasted_iota(jnp.int32, (C, C), 0).astype(F32)
    jj = lax.broadcasted_iota(jnp.int32, (C, C), 1).astype(F32)
    ci = lax.broadcasted_iota(jnp.int32, (C, 1), 0).astype(F32)
    diff = (jj - ii) if reverse else (ii - jj)
    order = (1, 0) if reverse else (0, 1)
    for h in range(RET_HEADS):
        lg = lg_ref[h]
        intra = jnp.where(diff >= 0, jnp.exp(jnp.maximum(diff, 0.0) * lg), 0.0)
        if reverse:
            q_dec = jnp.exp((C - ci) * lg)
            k_dec = jnp.exp(ci * lg)
        else:
            q_dec = jnp.exp((ci + 1.0) * lg)
            k_dec = jnp.exp((C - 1.0 - ci) * lg)
        c_dec = jnp.exp(jnp.zeros((1, RET_HD), F32) + C * lg)
        cols = slice(h * RET_HD, (h + 1) * RET_HD)
        for b in range(nb):
            for c in order:
                rows = slice(c * C, (c + 1) * C)
                cosf = cos_ref[rows, :]
                sinf = sin_ref[rows, :]
                q = q_ref[b, rows, cols].astype(F32)
                k = k_ref[b, rows, cols].astype(F32)
                v = v_ref[b, rows, cols]
                q = q * cosf + pltpu.roll(q, RET_HD // 2, 1) * sinf
                k = (k * cosf + pltpu.roll(k, RET_HD // 2, 1) * sinf) * (RET_HD ** -0.5)
                st = s_ref[b, h]
                scores = _dot_nt(q.astype(BF16), k.astype(BF16)) * intra
                o = _dot(scores.astype(BF16), v) + _dot_nt((q * q_dec).astype(BF16), st.astype(BF16))
                s_ref[b, h] = c_dec * st + _dot_tn(v, (k * k_dec).astype(BF16))
                o_ref[b, rows, cols] = o.astype(BF16)


def _retention(z3, col0, lg, cosf, sinf, reverse):
    nb, s, _ = z3.shape
    nt = s // TM
    w = RET_HEADS * RET_HD
    cb0 = col0 // w

    if reverse:
        def jmap(j):
            return jnp.where(j == 0, nt - 1, nt - 1 - j)
    else:
        def jmap(j):
            return (j + nt - 1) % nt

    grid_spec = pltpu.PrefetchScalarGridSpec(
        num_scalar_prefetch=0,
        grid=(nt,),
        in_specs=[pl.BlockSpec(memory_space=pltpu.SMEM),
                  pl.BlockSpec((nb, TM, w), lambda j: (0, jmap(j), cb0)),
                  pl.BlockSpec((nb, TM, w), lambda j: (0, jmap(j), cb0 + 1)),
                  pl.BlockSpec((nb, TM, w), lambda j: (0, jmap(j), cb0 + 2)),
                  pl.BlockSpec((TM, RET_HD), lambda j: (jmap(j), 0)),
                  pl.BlockSpec((TM, RET_HD), lambda j: (jmap(j), 0))],
        out_specs=pl.BlockSpec((nb, TM, w), lambda j: (0, jmap(j), 0)),
        scratch_shapes=[pltpu.VMEM((nb, RET_HEADS, RET_HD, RET_HD), F32)],
    )
    return pl.pallas_call(
        functools.partial(_ret_kernel, reverse=reverse, nb=nb),
        out_shape=jax.ShapeDtypeStruct((nb, s, w), BF16),
        grid_spec=grid_spec,
        compiler_params=_cparams("arbitrary"),
        name="retention_bwd" if reverse else "retention_fwd",
    )(lg, z3, z3, z3, cosf, sinf)


def _rope_tables(length, ctx_len):
    rows = length // GRID_W
    quarter = RET_HD // 4
    inv = ROPE_BASE ** (-jnp.arange(quarter, dtype=F32) / quarter)
    r = jnp.repeat(jnp.arange(rows, dtype=F32), GRID_W)
    col = jnp.tile(jnp.arange(GRID_W, dtype=F32), rows)
    ang = jnp.concatenate([r[:, None] * inv, col[:, None] * inv], axis=-1)
    cos, sin = jnp.cos(ang), jnp.sin(ang)
    cosf = jnp.concatenate([cos, cos], axis=-1)
    sinf = jnp.concatenate([-sin, sin], axis=-1)
    cosf = jnp.concatenate([cosf, jnp.ones((ctx_len, RET_HD), F32)], axis=0)
    sinf = jnp.concatenate([sinf, jnp.zeros((ctx_len, RET_HD), F32)], axis=0)
    return cosf, sinf


def _shortconv_freq(x, cw, na):
    row = lax.broadcasted_iota(jnp.int32, (1, FFT_N2, 1), 1)
    lane = lax.broadcasted_iota(jnp.int32, (1, 1, 2 * na), 2)
    wrap_prev = pltpu.roll(x[:, FFT_N2 - 1:FFT_N2, :], 1, 2)
    prev = jnp.where(row == 0, jnp.where((lane == 0) | (lane == na), 0.0, wrap_prev), pltpu.roll(x, 1, 1))
    wrap_next = pltpu.roll(x[:, 0:1, :], 2 * na - 1, 2)
    nxt = jnp.where(row == FFT_N2 - 1, jnp.where((lane == na - 1) | (lane == 2 * na - 1), 0.0, wrap_next),
                    pltpu.roll(x, FFT_N2 - 1, 1))
    return cw[..., 0:1] * prev + cw[..., 1:2] * x + cw[..., 2:3] * nxt + cw[..., 3:4]


def _shortconv_lanes(x, cw, n):
    lane = lax.broadcasted_iota(jnp.int32, (1, 2 * n), 1)
    prev = jnp.where((lane == 0) | (lane == n), 0.0, pltpu.roll(x, 1, 1))
    nxt = jnp.where((lane == n - 1) | (lane == 2 * n - 1), 0.0, pltpu.roll(x, 2 * n - 1, 1))
    return cw[:, 0:1] * prev + cw[:, 1:2] * x + cw[:, 2:3] * nxt + cw[:, 3:4]


FEAT_ROWS = 40


def _filter_kernel(w1_ref, b1_ref, w2_ref, b2_ref, w3_ref, b3_ref, w4_ref, fr_ref, dl_ref,
                   f_ref, ss_ref, *, length, pb):
    i = pl.program_id(0)
    pos = (lax.broadcasted_iota(jnp.int32, (1, pb), 1) + i * pb).astype(F32)
    t = pos * (1.0 / (length - 1))
    bands = (HY_EMB - 1) // 2
    w = (2.0 * math.pi) * pos / length
    sub = lax.broadcasted_iota(jnp.int32, (FEAT_ROWS, 1), 0)
    band = jnp.where(sub <= bands, sub - 1, sub - 1 - bands).astype(F32)
    f = 1e-4 + band * ((bands - 1 - 1e-4) / (bands - 1))
    fw = f * w
    feats = jnp.where(sub == 0, t, jnp.where(sub <= bands, jnp.cos(fw),
                                             jnp.where(sub <= 2 * bands, -jnp.sin(fw), 0.0)))
    feats = jnp.concatenate([feats, jnp.zeros((LANES - FEAT_ROWS, pb), F32)], axis=0)
    hdot = _dot3
    a = jnp.sin(fr_ref[:, 0:1] * (hdot(w1_ref[...], feats) + b1_ref[...]))
    a = jnp.sin(fr_ref[:, 1:2] * (hdot(w2_ref[...], a) + b2_ref[...]))
    a = jnp.sin(fr_ref[:, 2:3] * (hdot(w3_ref[...], a) + b3_ref[...]))

    @pl.when(i == 0)
    def _():
        ss_ref[...] = jnp.zeros_like(ss_ref)

    nout = w4_ref.shape[0]
    for cb in range(nout // HY_WIDTH):
        rows = slice(cb * HY_WIDTH, (cb + 1) * HY_WIDTH)
        filt = hdot(w4_ref[rows, :], a) * jnp.exp(-dl_ref[rows, :] * t)
        for j in range(pb // LANES):
            f_ref[rows, j, :] = filt[:, j * LANES:(j + 1) * LANES]
        ss_ref[rows, :] += jnp.sum(filt * filt, axis=1, keepdims=True)


def _hyena_filters_raw(length, w1, b1, w2, b2, w3, b3, w4, freq):
    nout = w4.shape[1]
    pb = min(length, 1024)
    max_decay = math.log(HY_TARGET) / HY_SHORT_DECAY_PCT
    min_decay = math.log(HY_TARGET) / HY_LONG_DECAY_PCT
    deltas = jnp.abs(jnp.linspace(min_decay, max_decay, HY_WIDTH, dtype=F32))
    dl = jnp.tile(deltas, nout // HY_WIDTH).reshape(nout, 1)
    w1p = jnp.pad(w1.T, ((0, 0), (0, LANES - w1.shape[0])))
    col = lambda b: b.reshape(-1, 1)
    full = lambda a: pl.BlockSpec(a.shape, lambda i: tuple(0 for _ in a.shape))
    args = (w1p, col(b1), w2.T, col(b2), w3.T, col(b3), w4.T, freq.T, dl)
    return pl.pallas_call(
        functools.partial(_filter_kernel, length=length, pb=pb),
        out_shape=(jax.ShapeDtypeStruct((nout, length // LANES, LANES), F32),
                   jax.ShapeDtypeStruct((nout, 1), F32)),
        grid=(length // pb,),
        in_specs=[full(a) for a in args],
        out_specs=(pl.BlockSpec((nout, pb // LANES, LANES), lambda i: (0, i, 0)),
                   pl.BlockSpec((nout, 1), lambda i: (0, 0))),
        compiler_params=_cparams("arbitrary"),
        name="hyena_filter_mlp",
    )(*args)


def _dft_tables(na):
    k1n = 2 * na
    n = k1n * FFT_N2
    n1 = np.arange(na)[:, None]
    k1 = np.arange(k1n)[None, :]
    ang = 2.0 * np.pi * ((n1 * k1) % k1n) / k1n
    c, s = np.cos(ang), np.sin(ang)
    ma = np.block([[c, -s], [s, c]])
    n1f = np.arange(FFT_N2)[:, None]
    angf = 2.0 * np.pi * ((n1f * k1) % k1n) / k1n
    live = (n1f < na).astype(np.float64)
    ma_f = np.concatenate([np.cos(angf), -np.sin(angf)], axis=1) * live
    ma_b = np.concatenate([np.cos(angf), np.sin(angf)], axis=1) * live
    n2 = np.arange(FFT_N2)[:, None]
    angt = 2.0 * np.pi * ((n2 * k1) % n) / n
    twr, twi = np.cos(angt), -np.sin(angt)
    k2 = np.arange(FFT_N2)[None, :]
    angb = 2.0 * np.pi * ((n2 * k2) % FFT_N2) / FFT_N2
    cb, sb = np.cos(angb), np.sin(angb)
    mb = np.block([[cb, -sb], [sb, cb]])
    mc = np.block([[cb, sb], [-sb, cb]])
    angd = 2.0 * np.pi * ((np.arange(k1n)[:, None] * np.arange(na)[None, :]) % k1n) / k1n
    cd, sd = np.cos(angd) / n, np.sin(angd) / n
    md = np.block([[cd, sd], [-sd, cd]])
    as_bf = lambda a: jnp.asarray(a, dtype=F32).astype(BF16)
    as_f = lambda a: jnp.asarray(a, dtype=F32)
    mb2 = np.concatenate([mb, mc], axis=0)
    return dict(ma=as_bf(ma), ma_f=as_bf(ma_f), ma_b=as_bf(ma_b), twr=as_f(twr), twi=as_f(twi),
                twr_t=as_f(twr.T), twi_t=as_f(twi.T), mb=as_bf(mb), mb2=as_bf(mb2), mc=as_bf(mc), md=as_bf(md))


def _fwd_stages(x2, ma, twr, twi, mb, cb, k1n):
    a = _dot(x2, ma).reshape(cb, FFT_N2, 2 * k1n)
    ar, ai = a[..., :k1n], a[..., k1n:]
    ar2 = ar * twr - ai * twi
    ai2 = ar * twi + ai * twr
    xt = jnp.concatenate([jnp.swapaxes(ar2, 1, 2), jnp.swapaxes(ai2, 1, 2)], axis=-1)
    return _dot(xt.astype(BF16).reshape(cb * k1n, 2 * FFT_N2), mb).reshape(cb, k1n, 2 * FFT_N2)


def _spectrum_kernel(hf_ref, hb_ref, ssf_ref, ssb_ref, maf_ref, mab_ref, twr_ref, twi_ref, mb2_ref, h_ref, *, na):
    k1n = 2 * na
    cb = h_ref.shape[0]

    def stage_a(ref, ss_ref, m_ref):
        x = ref[0, 0] * lax.rsqrt(ss_ref[0, 0])
        x = jnp.concatenate([x, jnp.zeros((cb, FFT_N2 - na, FFT_N2), F32)], axis=1)
        xt = jnp.swapaxes(x, 1, 2).astype(BF16).reshape(cb * FFT_N2, FFT_N2)
        a = _dot(xt, m_ref[...]).reshape(cb, FFT_N2, 2 * k1n)
        return a[..., :k1n], a[..., k1n:]

    fr, fi = stage_a(hf_ref, ssf_ref, maf_ref)
    br, bi = stage_a(hb_ref, ssb_ref, mab_ref)
    twr, twi = twr_ref[...], twi_ref[...]
    parts = (fr * twr - fi * twi, fr * twi + fi * twr, br * twr + bi * twi, bi * twr - br * twi)
    xt = jnp.concatenate([jnp.swapaxes(p, 1, 2) for p in parts], axis=-1)
    h = _dot(xt.astype(BF16).reshape(cb * k1n, 4 * FFT_N2), mb2_ref[...])
    h_ref[...] = h.reshape(cb, k1n, 2 * FFT_N2).astype(BF16)


def _spectrum(filt5, ss5, order, tb, na):
    c = filt5.shape[2]
    k1n = 2 * na
    cb = FFT_CB
    full = lambda a: pl.BlockSpec(a.shape, lambda i: tuple(0 for _ in a.shape))
    fblk = lambda d: pl.BlockSpec((1, 1, cb, na, FFT_N2), lambda i: (order, d, i, 0, 0))
    sblk = lambda d: pl.BlockSpec((1, 1, cb, 1, 1), lambda i: (order, d, i, 0, 0))
    return pl.pallas_call(
        functools.partial(_spectrum_kernel, na=na),
        out_shape=jax.ShapeDtypeStruct((c, k1n, 2 * FFT_N2), BF16),
        grid=(c // cb,),
        in_specs=[fblk(0), fblk(1), sblk(0), sblk(1),
                  full(tb["ma_f"]), full(tb["ma_b"]), full(tb["twr"]), full(tb["twi"]), full(tb["mb2"])],
        out_specs=pl.BlockSpec((cb, k1n, 2 * FFT_N2), lambda i: (i, 0, 0)),
        compiler_params=_cparams("arbitrary"),
        name="hyena_filter_spectrum",
    )(filt5, filt5, ss5, ss5, tb["ma_f"], tb["ma_b"], tb["twr"], tb["twi"], tb["mb2"])


def _fftconv_kernel(u_ref, g_ref, h_ref, sk_ref, cwu_ref, cwg_ref, ma_ref, twr_ref, twi_ref, mb_ref, mc_ref,
                    twrt_ref, twit_ref, md_ref, o_ref, *, na, conv_u):
    k1n = 2 * na
    cb = u_ref.shape[0]
    uf = u_ref[...].astype(F32)
    if conv_u:
        uf = _shortconv_freq(uf, cwu_ref[...], na)
    gate = _shortconv_freq(g_ref[...].astype(F32), cwg_ref[...], na)
    x = _fwd_stages(uf.astype(BF16).reshape(cb * FFT_N2, k1n), ma_ref[...], twr_ref[...], twi_ref[...],
                    mb_ref[...], cb, k1n)
    xr, xi = x[..., :FFT_N2], x[..., FFT_N2:]
    h = h_ref[...].astype(F32)
    hr, hi = h[..., :FFT_N2], h[..., FFT_N2:]
    y = jnp.concatenate([xr * hr - xi * hi, xr * hi + xi * hr], axis=-1)
    c = _dot(y.astype(BF16).reshape(cb * k1n, 2 * FFT_N2), mc_ref[...]).reshape(cb, k1n, 2 * FFT_N2)
    cr, ci = c[..., :FFT_N2], c[..., FFT_N2:]
    twrt, twit = twrt_ref[...], twit_ref[...]
    cr2 = cr * twrt + ci * twit
    ci2 = ci * twrt - cr * twit
    ct = jnp.concatenate([jnp.swapaxes(cr2, 1, 2), jnp.swapaxes(ci2, 1, 2)], axis=-1)
    d = _dot(ct.astype(BF16).reshape(cb * FFT_N2, 2 * k1n), md_ref[...]).reshape(cb, FFT_N2, k1n)
    o_ref[...] = (gate * (d + sk_ref[...] * uf)).astype(BF16)


def _fftconv(u, gate, h, skip, cw_u, cw_g, tb, na, conv_u):
    c = u.shape[0]
    k1n = 2 * na
    cb = FFT_CB
    full = lambda a: pl.BlockSpec(a.shape, lambda i: tuple(0 for _ in a.shape))
    blk = pl.BlockSpec((cb, FFT_N2, k1n), lambda i: (i, 0, 0))
    cwblk = pl.BlockSpec((cb, 1, 4), lambda i: (i, 0, 0))
    return pl.pallas_call(
        functools.partial(_fftconv_kernel, na=na, conv_u=conv_u),
        out_shape=jax.ShapeDtypeStruct(u.shape, BF16),
        grid=(c // cb,),
        in_specs=[blk, blk,
                  pl.BlockSpec((cb, k1n, 2 * FFT_N2), lambda i: (i, 0, 0)),
                  pl.BlockSpec((cb, 1, 1), lambda i: (i, 0, 0)), cwblk, cwblk,
                  full(tb["ma"]), full(tb["twr"]), full(tb["twi"]), full(tb["mb"]), full(tb["mc"]),
                  full(tb["twr_t"]), full(tb["twi_t"]), full(tb["md"])],
        out_specs=blk,
        compiler_params=_cparams("arbitrary"),
        name="hyena_fftconv",
    )(u, gate, h, skip, cw_u, cw_g, tb["ma"], tb["twr"], tb["twi"], tb["mb"], tb["mc"], tb["twr_t"], tb["twi_t"],
      tb["md"])


def _to_freq_layout(a, na):
    nb, _, c = a.shape
    a = a.reshape(nb, na, FFT_N2, c)
    return jnp.transpose(a, (3, 2, 0, 1)).reshape(c, FFT_N2, nb * na)


def _from_freq_layout(a):
    c, _, lanes = a.shape
    na = lanes // 2
    a = a.reshape(c, FFT_N2, 2, na)
    return jnp.transpose(a, (2, 3, 1, 0)).reshape(2, na * FFT_N2, c)


def _ctx_dft_tables(n):
    big = 2 * n
    j = np.arange(n)[:, None]
    k = np.arange(big)[None, :]
    ang = 2.0 * np.pi * ((j * k) % big) / big
    c, s = np.cos(ang), np.sin(ang)
    fh = np.block([[c, -s], [c, s]])
    ff = np.block([[c, -s], [s, c]])
    fi = np.block([[c.T, s.T], [-s.T, c.T]]) / big
    as_bf = lambda a: jnp.asarray(a, dtype=F32).astype(BF16)
    return as_bf(fh), as_bf(ff), as_bf(fi)


def _ctxconv_kernel(u_ref, cw_ref, hf_ref, hb_ref, ssf_ref, ssb_ref, sk_ref, fh_ref, ff_ref, fi_ref, o_ref):
    n_tok = u_ref.shape[2] // 2
    short = lambda j: _shortconv_lanes(u_ref[j].astype(F32), cw_ref[j], n_tok)
    gates = (short(0), short(1))
    z = short(2)
    half = fh_ref.shape[1] // 2
    for n in range(2):
        taps = jnp.concatenate([hf_ref[n] * lax.rsqrt(ssf_ref[n]), hb_ref[n] * lax.rsqrt(ssb_ref[n])], axis=-1)
        h = _dot(taps.astype(BF16), fh_ref[...])
        x = _dot(z.astype(BF16), ff_ref[...])
        hr, hi = h[:, :half], h[:, half:]
        xr, xi = x[:, :half], x[:, half:]
        y = jnp.concatenate([xr * hr - xi * hi, xr * hi + xi * hr], axis=-1)
        conv = _dot(y.astype(BF16), fi_ref[...])
        z = gates[n] * (conv + sk_ref[n] * z)
    o_ref[...] = z.astype(BF16)


def _ctxconv(u3, cw3, hf, hb, ssf, ssb, skip, tables):
    _, c, lanes = u3.shape
    args = (u3, cw3, hf, hb, ssf, ssb, skip) + tuple(tables)
    full = lambda a: pl.BlockSpec(a.shape, lambda i: tuple(0 for _ in a.shape))
    return pl.pallas_call(
        _ctxconv_kernel,
        out_shape=jax.ShapeDtypeStruct((c, lanes), BF16),
        grid=(1,),
        in_specs=[full(a) for a in args],
        out_specs=pl.BlockSpec((c, lanes), lambda i: (0, 0)),
        compiler_params=_cparams("arbitrary"),
        name="hyena_ctxconv",
    )(*args)


def _load_row_tiles(ref, n, stride=SUBLANES, first=0):
    return jnp.concatenate([ref[pl.ds(first + j, n, stride=stride), :] for j in range(SUBLANES)], axis=-1)


def _store_row_tiles(ref, val, n, stride=SUBLANES, first=0):
    for j in range(SUBLANES):
        ref[pl.ds(first + j, n, stride=stride), :] = val[:, j * LANES:(j + 1) * LANES]


def _row_tile(ref, idx, tiles=1):
    size = tiles * SUBLANES
    return ref.at[pl.ds(pl.multiple_of(idx * size, size), size)]


def _top2_rows(vals):
    n = len(vals)
    best_v, best_i = vals[0], jnp.zeros_like(vals[0], dtype=jnp.int32)
    for e in range(1, n):
        take = vals[e] > best_v
        best_v = jnp.where(take, vals[e], best_v)
        best_i = jnp.where(take, e, best_i)
    sec_v = jnp.full_like(vals[0], -jnp.inf)
    sec_i = jnp.zeros_like(best_i)
    for e in range(n):
        take = (best_i != e) & (vals[e] > sec_v)
        sec_v = jnp.where(take, vals[e], sec_v)
        sec_i = jnp.where(take, e, sec_i)
    return best_v, best_i, sec_v, sec_i


def _route(h2, rw_ref, rb_ref, carry_ref, mi_ref, mf_ref, cnt_ref, cols):
    logits = _dot3(rw_ref[...], h2, dot=_dot_nt)
    s = 1.0 / (1.0 + jnp.exp(-logits))
    sel = s + rb_ref[...]
    srow = [s[e:e + 1, :] for e in range(N_EXPERTS)]
    selrow = [sel[e:e + 1, :] for e in range(N_EXPERTS)]
    gscore = []
    for g in range(N_GROUPS):
        grp = selrow[g * EXPERTS_PER_GROUP:(g + 1) * EXPERTS_PER_GROUP]
        bv, _, sv, _ = _top2_rows(grp)
        gscore.append(bv + sv)
    best_g = jnp.zeros_like(gscore[0], dtype=jnp.int32)
    best_s = gscore[0]
    for g in range(1, N_GROUPS):
        take = gscore[g] > best_s
        best_s = jnp.where(take, gscore[g], best_s)
        best_g = jnp.where(take, g, best_g)
    cand_sel, cand_s = [], []
    for j in range(EXPERTS_PER_GROUP):
        cs, ca = selrow[j], srow[j]
        for g in range(1, N_GROUPS):
            cs = jnp.where(best_g == g, selrow[g * EXPERTS_PER_GROUP + j], cs)
            ca = jnp.where(best_g == g, srow[g * EXPERTS_PER_GROUP + j], ca)
        cand_sel.append(cs)
        cand_s.append(ca)
    _, i1, _, i2 = _top2_rows(cand_sel)
    a1, a2 = cand_s[0], cand_s[0]
    for j in range(1, EXPERTS_PER_GROUP):
        a1 = jnp.where(i1 == j, cand_s[j], a1)
        a2 = jnp.where(i2 == j, cand_s[j], a2)
    den = a1 + a2
    w1, w2 = a1 / den, a2 / den
    swap = i2 < i1
    lo = jnp.where(swap, i2, i1)
    hi = jnp.where(swap, i1, i2)
    w_lo = jnp.where(swap, w2, w1)
    w_hi = jnp.where(swap, w1, w2)
    pair = jnp.where(lo == 0, 0, jnp.where(lo == 1, 3, 5)) + (hi - lo - 1)
    cls = best_g * len(PAIRS) + pair
    cidx = lax.broadcasted_iota(jnp.int32, (CLASS_ROWS, TM), 0)
    hit = cidx == cls
    onehot = jnp.where(hit, 1.0, 0.0)
    tri = jnp.where(lax.broadcasted_iota(jnp.int32, (TM, TM), 0) < lax.broadcasted_iota(jnp.int32, (TM, TM), 1),
                    1.0, 0.0).astype(BF16)
    rank = _dot(onehot.astype(BF16), tri) + carry_ref[:, 0:1]
    rk = jnp.sum(jnp.where(hit, rank, 0.0), axis=0, keepdims=True)
    carry_ref[...] = carry_ref[...] + jnp.sum(onehot, axis=1, keepdims=True)
    mi_ref[:, cols] = jnp.concatenate([cls, rk.astype(jnp.int32), jnp.zeros((6, TM), jnp.int32)], axis=0)
    mf_ref[:, cols] = jnp.concatenate([w_lo, w_hi, jnp.zeros((6, TM), F32)], axis=0)
    cnt_ref[...] = carry_ref[...]


def _post_norm_mod(x, y, m_ref, lng, lnb, gate_row, sh_row, sc_row):
    u = DN_ALPHA * x + m_ref[0, gate_row:gate_row + 1, :] * y
    mu = jnp.mean(u, axis=-1, keepdims=True)
    var = jnp.mean(jnp.square(u - mu), axis=-1, keepdims=True)
    xn = (u - mu) * lax.rsqrt(var + LN_EPS) * lng + lnb
    if sh_row is None:
        return xn, None
    return xn, xn * (1.0 + m_ref[0, sc_row:sc_row + 1, :]) + m_ref[0, sh_row:sh_row + 1, :]


def _outproj_kernel(*refs, layer0, tpb):
    nd = 8 if layer0 else 5
    halves = (refs[:nd], refs[nd:2 * nd])
    w_ref, lng_ref, lnb_ref, rw_ref, rb_ref, xo_ref, h2_ref, mi_ref, mf_ref, cnt_ref, carry_ref = refs[2 * nd:]
    i = pl.program_id(0)

    @pl.when(i == 0)
    def _():
        carry_ref[...] = jnp.zeros_like(carry_ref)

    for half, data in enumerate(halves):
        if layer0:
            of_ref, ob_ref, g_ref, hyl_ref, hyc_ref, xl_ref, xc_ref, m_ref = data
            is_ctx = (2 * i + half) % tpb == tpb - 1
            x = jnp.where(is_ctx, xc_ref[...], xl_ref[...])
            hy = jnp.where(is_ctx, hyc_ref[...], hyl_ref[...])
            o = of_ref[...].astype(F32) + ob_ref[...].astype(F32)
            parts = []
            for h in range(RET_HEADS):
                oh = o[:, h * RET_HD:(h + 1) * RET_HD]
                mu = jnp.mean(oh, axis=-1, keepdims=True)
                var = jnp.mean(jnp.square(oh - mu), axis=-1, keepdims=True)
                parts.append((oh - mu) * lax.rsqrt(var + LN_EPS))
            yret = (jnp.concatenate(parts, axis=-1) * _silu(g_ref[...].astype(F32))).astype(BF16)
            wr = yret.shape[1]
            y = _dot(yret, w_ref[:wr, :]) + _dot(hy, w_ref[wr:, :])
        else:
            of_ref, ob_ref, g_ref, x_ref, m_ref = data
            x = x_ref[...]
            o = of_ref[...].astype(F32) + ob_ref[...].astype(F32)
            parts = []
            for h in range(GLA_HEADS):
                oh = o[:, h * GLA_DV:(h + 1) * GLA_DV]
                parts.append(oh * lax.rsqrt(jnp.mean(jnp.square(oh), axis=-1, keepdims=True) + LN_EPS))
            a = (jnp.concatenate(parts, axis=-1) * _silu(g_ref[...].astype(F32))).astype(BF16)
            y = _dot(a, w_ref[...])
        rows = slice(half * TM, (half + 1) * TM)
        xn, h2 = _post_norm_mod(x, y, m_ref, lng_ref[...], lnb_ref[...], 2, 3, 4)
        xo_ref[rows, :] = xn
        _store_row_tiles(h2_ref, h2, TM, first=half * TM * SUBLANES)
        _route(h2, rw_ref, rb_ref, carry_ref, mi_ref, mf_ref, cnt_ref, rows)


def _outproj_call(layer0, tpb, n_tiles, d, half_specs, half_args, w, lng, lnb, router_w, router_bias):
    rows = n_tiles * TM
    full2 = lambda a: pl.BlockSpec(a.shape, lambda i: (0, 0))
    rw_t = router_w.T
    rb = router_bias.reshape(N_EXPERTS, 1)
    return pl.pallas_call(
        functools.partial(_outproj_kernel, layer0=layer0, tpb=tpb),
        out_shape=(jax.ShapeDtypeStruct((rows, d), F32),
                   jax.ShapeDtypeStruct((rows * SUBLANES, LANES), F32),
                   jax.ShapeDtypeStruct((8, rows), jnp.int32),
                   jax.ShapeDtypeStruct((8, rows), F32),
                   jax.ShapeDtypeStruct((CLASS_ROWS, LANES), F32)),
        grid=(n_tiles // 2,),
        in_specs=list(half_specs(0)) + list(half_specs(1)) + [full2(w), full2(lng), full2(lnb), full2(rw_t), full2(rb)],
        out_specs=(pl.BlockSpec((2 * TM, d), lambda i: (i, 0)),
                   pl.BlockSpec((2 * TM * SUBLANES, LANES), lambda i: (i, 0)),
                   pl.BlockSpec((8, 2 * TM), lambda i: (0, i)),
                   pl.BlockSpec((8, 2 * TM), lambda i: (0, i)),
                   pl.BlockSpec((CLASS_ROWS, LANES), lambda i: (0, 0))),
        scratch_shapes=[pltpu.VMEM((CLASS_ROWS, LANES), F32)],
        compiler_params=_cparams("arbitrary"),
        name="outproj_norm0" if layer0 else "outproj_norm1",
    )(*half_args, *half_args, w, lng, lnb, rw_t, rb)


def _dispatch_kernel(slots_ref, zoff_ref, h_ref, xs_ref, zero_ref, sem):
    i = pl.program_id(0)

    @pl.when(i == 0)
    def _():
        zero_ref[...] = jnp.zeros_like(zero_ref)
        def fill_class(e, go):
            @pl.when(zoff_ref[N_CLASSES + 1 + e] > 0)
            def _():
                go(pltpu.make_async_copy(zero_ref, _row_tile(xs_ref, zoff_ref[e], TE), sem))

        for e in range(N_CLASSES):
            fill_class(e, lambda cp: cp.start())
        for e in range(N_CLASSES):
            fill_class(e, lambda cp: cp.wait())
        first_free = zoff_ref[N_CLASSES]
        n_all = xs_ref.shape[0] // (TE * SUBLANES)

        def fill(t, carry):
            pltpu.make_async_copy(zero_ref, _row_tile(xs_ref, t, TE), sem).start()
            return carry

        def fill_wait(t, carry):
            pltpu.make_async_copy(zero_ref, _row_tile(xs_ref, t, TE), sem).wait()
            return carry

        lax.fori_loop(first_free, n_all, fill, 0)
        lax.fori_loop(first_free, n_all, fill_wait, 0)

    def row_copy(r):
        return pltpu.make_async_copy(_row_tile(h_ref, r), _row_tile(xs_ref, slots_ref[0, 0, r]), sem)

    def issue(r2, carry):
        row_copy(2 * r2).start(priority=0)
        row_copy(2 * r2 + 1).start(priority=1)
        return carry

    lax.fori_loop(0, TM // 2, issue, 0, unroll=4)

    def drain(r, carry):
        row_copy(r).wait()
        return carry

    lax.fori_loop(0, TM, drain, 0, unroll=8)


def _dispatch(h2, slots3, zoff, p_rows):
    n_tiles = h2.shape[0] // (TM * SUBLANES)
    grid_spec = pltpu.PrefetchScalarGridSpec(
        num_scalar_prefetch=0,
        grid=(n_tiles,),
        in_specs=[pl.BlockSpec((1, 1, TM), lambda i: (i, 0, 0), memory_space=pltpu.SMEM),
                  pl.BlockSpec(memory_space=pltpu.SMEM),
                  pl.BlockSpec((TM * SUBLANES, LANES), lambda i: (i, 0))],
        out_specs=pl.BlockSpec(memory_space=pl.ANY),
        scratch_shapes=[pltpu.VMEM((TE * SUBLANES, LANES), F32), pltpu.SemaphoreType.DMA(())],
    )
    return pl.pallas_call(
        _dispatch_kernel,
        out_shape=jax.ShapeDtypeStruct((p_rows * SUBLANES, LANES), F32),
        grid_spec=grid_spec,
        compiler_params=_cparams("arbitrary"),
        name="moe_dispatch",
    )(slots3, zoff, h2)


def _experts_kernel(ea_ref, eb_ref, tv_ref, x_ref, wga_ref, wua_ref, wda_ref, wgb_ref, wub_ref, wdb_ref, y_ref):
    j = pl.program_id(0)

    @pl.when(tv_ref[j] > 0)
    def _():
        x = _load_row_tiles(x_ref, TE).astype(BF16)
        for which, (wg, wu, wd) in enumerate(((wga_ref, wua_ref, wda_ref), (wgb_ref, wub_ref, wdb_ref))):
            g = _dot(x, wg[0])
            u = _dot(x, wu[0])
            y = _dot((_silu(g) * u).astype(BF16), wd[0])
            _store_row_tiles(y_ref, y, TE, stride=2 * SUBLANES, first=which * SUBLANES)

    @pl.when(tv_ref[j] == 0)
    def _():
        y_ref[...] = jnp.zeros_like(y_ref)


def _experts(xs, tile_ea, tile_eb, tile_valid, wg, wu, wd):
    n_tiles = xs.shape[0] // (TE * SUBLANES)
    d, hdim = wg.shape[1], wg.shape[2]
    wa = lambda shp: pl.BlockSpec(shp, lambda j, ea, eb, tv: (ea[j], 0, 0))
    wb = lambda shp: pl.BlockSpec(shp, lambda j, ea, eb, tv: (eb[j], 0, 0))
    grid_spec = pltpu.PrefetchScalarGridSpec(
        num_scalar_prefetch=3,
        grid=(n_tiles,),
        in_specs=[pl.BlockSpec((TE * SUBLANES, LANES), lambda j, ea, eb, tv: (j, 0)),
                  wa((1, d, hdim)), wa((1, d, hdim)), wa((1, hdim, d)),
                  wb((1, d, hdim)), wb((1, d, hdim)), wb((1, hdim, d))],
        out_specs=pl.BlockSpec((2 * TE * SUBLANES, LANES), lambda j, ea, eb, tv: (j, 0)),
    )
    return pl.pallas_call(
        _experts_kernel,
        out_shape=jax.ShapeDtypeStruct((2 * xs.shape[0], LANES), F32),
        grid_spec=grid_spec,
        compiler_params=_cparams("arbitrary"),
        name="moe_experts",
    )(tile_ea, tile_eb, tile_valid, xs, wg, wu, wd, wg, wu, wd)


def _combine_kernel(slots_ref, next_slots_ref, ys_ref, wts_ref, x_ref, m_ref, lng_ref, lnb_ref, o_ref, buf_ref, sem):
    i = pl.program_id(0)
    cur = i % 2

    def row_copy(s_ref, b, r):
        return pltpu.make_async_copy(_row_tile(ys_ref, s_ref[0, 0, r], 2), _row_tile(buf_ref.at[b], r, 2), sem.at[b])

    def issue_all(s_ref, b):
        def issue(r2, carry):
            row_copy(s_ref, b, 2 * r2).start(priority=0)
            row_copy(s_ref, b, 2 * r2 + 1).start(priority=1)
            return carry

        lax.fori_loop(0, TM // 2, issue, 0, unroll=4)

    @pl.when(i == 0)
    def _():
        issue_all(slots_ref, 0)

    @pl.when(i + 1 < pl.num_programs(0))
    def _():
        issue_all(next_slots_ref, 1 - cur)

    def drain(r, carry):
        row_copy(slots_ref, cur, r).wait()
        return carry

    lax.fori_loop(0, TM, drain, 0, unroll=8)
    buf = buf_ref.at[cur]
    y_lo = _load_row_tiles(buf, TM, stride=2 * SUBLANES)
    y_hi = _load_row_tiles(buf, TM, stride=2 * SUBLANES, first=SUBLANES)
    y = wts_ref[:, 0:1] * y_lo + wts_ref[:, 1:2] * y_hi
    xn, _ = _post_norm_mod(x_ref[...], y, m_ref, lng_ref[...], lnb_ref[...], 5, None, None)
    o_ref[...] = xn


def _combine(ys, slots3, wts, x, mods, lng, lnb, cond_of, n_tiles):
    d = x.shape[1]
    grid_spec = pltpu.PrefetchScalarGridSpec(
        num_scalar_prefetch=0,
        grid=(n_tiles,),
        in_specs=[pl.BlockSpec((1, 1, TM), lambda i: (i, 0, 0), memory_space=pltpu.SMEM),
                  pl.BlockSpec((1, 1, TM), lambda i: (jnp.minimum(i + 1, n_tiles - 1), 0, 0),
                               memory_space=pltpu.SMEM),
                  pl.BlockSpec(memory_space=pl.ANY),
                  pl.BlockSpec((TM, 2), lambda i: (i, 0)),
                  pl.BlockSpec((TM, d), lambda i: (i, 0)),
                  pl.BlockSpec((1, 6, d), lambda i: (cond_of(i), 0, 0)),
                  pl.BlockSpec((1, d), lambda i: (0, 0)),
                  pl.BlockSpec((1, d), lambda i: (0, 0))],
        out_specs=pl.BlockSpec((TM, d), lambda i: (i, 0)),
        scratch_shapes=[pltpu.VMEM((2, 2 * TM * SUBLANES, LANES), F32), pltpu.SemaphoreType.DMA((2,))],
    )
    return pl.pallas_call(
        _combine_kernel,
        out_shape=jax.ShapeDtypeStruct((n_tiles * TM, d), F32),
        grid_spec=grid_spec,
        compiler_params=_cparams("arbitrary"),
        name="moe_combine",
    )(slots3, slots3, ys, wts, x, mods, lng, lnb)


def _moe(h2, mi, mf, cnt, x, cond_of, mods, lng, lnb, wg, wu, wd):
    t = x.shape[0]
    n_tiles = t // TM
    counts = cnt[:N_CLASSES, 0].astype(jnp.int32)
    padded = ((counts + TE - 1) // TE) * TE
    ends = jnp.cumsum(padded)
    offs = ends - padded
    cls = mi[0]
    cids = jnp.arange(N_CLASSES, dtype=jnp.int32)[:, None]
    slots = jnp.sum(jnp.where(cls[None] == cids, offs[:, None], 0), axis=0) + mi[1]
    slots3 = slots.reshape(n_tiles, 1, TM)
    n_cls_tiles = t // TE + N_CLASSES
    total_tiles = ends[-1] // TE
    tile_ids = jnp.arange(n_cls_tiles, dtype=jnp.int32)
    tile_valid = (tile_ids < total_tiles).astype(jnp.int32)
    tile_last = jnp.minimum(tile_ids, total_tiles - 1)
    tile_class = jnp.sum((tile_last[:, None] * TE >= ends[None, :]).astype(jnp.int32), axis=1)
    tile_class = jnp.minimum(tile_class, N_CLASSES - 1)
    pair_lo = jnp.asarray([p[0] for p in PAIRS], jnp.int32)
    pair_hi = jnp.asarray([p[1] for p in PAIRS], jnp.int32)
    pair_sel = (tile_class % len(PAIRS))[:, None] == jnp.arange(len(PAIRS), dtype=jnp.int32)[None, :]
    group0 = (tile_class // len(PAIRS)) * EXPERTS_PER_GROUP
    tile_ea = (group0 + jnp.sum(jnp.where(pair_sel, pair_lo[None, :], 0), axis=1)).astype(jnp.int32)
    tile_eb = (group0 + jnp.sum(jnp.where(pair_sel, pair_hi[None, :], 0), axis=1)).astype(jnp.int32)
    zoff = jnp.concatenate([jnp.maximum(ends // TE - 1, 0), total_tiles[None], padded]).astype(jnp.int32)
    xs = _dispatch(h2, slots3, zoff, n_cls_tiles * TE)
    ys = _experts(xs, tile_ea, tile_eb, tile_valid, wg, wu, wd)
    wts = jnp.transpose(mf[0:2])
    return _combine(ys, slots3, wts, x, mods, lng, lnb, cond_of, n_tiles)


def _chunk_cumsum(x, tri):
    hi = x.astype(BF16)
    r1 = x - hi.astype(F32)
    mid = r1.astype(BF16)
    lo = (r1 - mid.astype(F32)).astype(BF16)
    return _dot(tri, hi) + _dot(tri, mid) + _dot(tri, lo)


def _gla_kernel(q_ref, k_ref, v_ref, lr_ref, gw_ref, gb_ref, o_ref, s_ref, *, reverse, nb):
    C = GLA_CHUNK
    nchunk = TM // C

    @pl.when(pl.program_id(0) == 0)
    def _():
        s_ref[...] = jnp.zeros_like(s_ref)

    ii = lax.broadcasted_iota(jnp.int32, (C, C), 0)
    jj = lax.broadcasted_iota(jnp.int32, (C, C), 1)
    keep = (jj >= ii) if reverse else (jj <= ii)
    tri = jnp.where(keep, 1.0, 0.0).astype(BF16)
    order = tuple(reversed(range(nchunk))) if reverse else tuple(range(nchunk))
    end_row = 0 if reverse else C - 1
    gw = gw_ref[...]
    gb = gb_ref[...]
    for b in range(nb):
        pre = _dot(lr_ref[b], gw) + gb
        la_all = (jnp.minimum(pre, 0.0) - jnp.log(1.0 + jnp.exp(-jnp.abs(pre)))) * (1.0 / GLA_TAU)
        for c in order:
            rows = slice(c * C, (c + 1) * C)
            bc = _chunk_cumsum(la_all[rows, :], tri)
            bend = bc[end_row:end_row + 1, :]
            eb = jnp.exp(bc)
            enb = jnp.exp(-bc)
            ekb = jnp.exp(bend - bc)
            ebend = jnp.exp(bend)
            for h in range(GLA_HEADS):
                kc = slice(h * GLA_DK, (h + 1) * GLA_DK)
                vc = slice(h * GLA_DV, (h + 1) * GLA_DV)
                q = q_ref[b, rows, kc].astype(F32) * (GLA_DK ** -0.5)
                k = k_ref[b, rows, kc].astype(F32)
                v = v_ref[b, rows, vc]
                qd = (q * eb[:, kc]).astype(BF16)
                kd = (k * enb[:, kc]).astype(BF16)
                st = s_ref[b, h]
                scores = jnp.where(keep, _dot_nt(qd, kd), 0.0)
                o = _dot(scores.astype(BF16), v) + _dot_nt(qd, st.astype(BF16))
                s_ref[b, h] = st * ebend[:, kc] + _dot_tn(v, (k * ekb[:, kc]).astype(BF16))
                o_ref[b, rows, vc] = o.astype(BF16)


def _gla(z3, gw, gb, reverse):
    nb, s, _ = z3.shape
    nt = s // TM
    kw = GLA_HEADS * GLA_DK
    vw = GLA_HEADS * GLA_DV

    if reverse:
        def jmap(j):
            return jnp.where(j == 0, nt - 1, nt - 1 - j)
    else:
        def jmap(j):
            return (j + nt - 1) % nt

    lr_blk = (2 * kw + 2 * vw) // LANES
    return pl.pallas_call(
        functools.partial(_gla_kernel, reverse=reverse, nb=nb),
        out_shape=jax.ShapeDtypeStruct((nb, s, vw), BF16),
        grid=(nt,),
        in_specs=[pl.BlockSpec((nb, TM, kw), lambda j: (0, jmap(j), 0)),
                  pl.BlockSpec((nb, TM, kw), lambda j: (0, jmap(j), 1)),
                  pl.BlockSpec((nb, TM, vw), lambda j: (0, jmap(j), 1)),
                  pl.BlockSpec((nb, TM, LANES), lambda j: (0, jmap(j), lr_blk)),
                  pl.BlockSpec((LANES, kw), lambda j: (0, 0)),
                  pl.BlockSpec((1, kw), lambda j: (0, 0))],
        out_specs=pl.BlockSpec((nb, TM, vw), lambda j: (0, jmap(j), 0)),
        scratch_shapes=[pltpu.VMEM((nb, GLA_HEADS, GLA_DV, GLA_DK), F32)],
        compiler_params=_cparams("arbitrary"),
        name="gla_bwd" if reverse else "gla_fwd",
    )(z3, z3, z3, z3, gw, gb)


def kernel(x, c, ctx, c_ctx, mod_w, mod_b, ln_g, ln_b, ab_w_in, ret_log_decay_f, ret_log_decay_b, hy_conv_w, hy_conv_b, hy_w1, hy_b1, hy_w2, hy_b2, hy_w3, hy_b3, hy_w4, hy_freq, hy_skip, ab_w_out, gla_w_in, gla_gate_w_f, gla_gate_b_f, gla_gate_w_b, gla_gate_b_b, gla_w_out, router_w, router_bias, exp_w_gate, exp_w_up, exp_w_down):
    nb, length, d = x.shape
    ctx_len = ctx.shape[1]
    assert nb == 2 and ctx_len == TM and length % TM == 0 and length % (FFT_N2 * 2) == 0
    s = length + ctx_len
    tpb = s // TM
    r = nb * s
    lat_tiles = tpb - 1

    cond = jnp.zeros((8, d), F32).at[0].set(c[0]).at[1].set(c_ctx).at[2].set(c[1]).at[3].set(c_ctx)
    mods = _ada(cond, mod_w, mod_b).reshape(DEPTH, 8, 6, d)

    def cond_of(i):
        return 2 * (i // tpb) + jnp.where(i % tpb == tpb - 1, 1, 0)

    x_lat = x.reshape(nb * length, d)
    x_ctx = ctx.reshape(nb * ctx_len, d)

    ret_w = RET_HEADS * RET_HD
    hy_w = 3 * HY_WIDTH
    w_in0 = jnp.concatenate([ab_w_in[0][:, 4 * ret_w:], ab_w_in[0][:, :4 * ret_w]], axis=1).astype(BF16)
    z0 = _inproj((x_lat, x_ctx), mods[0], w_in0, tpb, r)
    z0_3 = z0.reshape(nb, s, -1)
    cosf, sinf = _rope_tables(length, ctx_len)
    o_f = _retention(z0_3, hy_w, ret_log_decay_f[0], cosf, sinf, False)
    o_b = _retention(z0_3, hy_w, ret_log_decay_b[0], cosf, sinf, True)

    hyu3 = z0_3[..., :hy_w]
    cw3 = jnp.concatenate([hy_conv_w[0].T, hy_conv_b[0][:, None]], axis=1).reshape(3, HY_WIDTH, 4)
    na = length // FFT_N2
    tb = _dft_tables(na)
    filt_w = (hy_w1[0], hy_b1[0], hy_w2[0], hy_b2[0], hy_w3[0], hy_b3[0], hy_w4[0], hy_freq[0])
    skip = hy_skip[0]

    filt, ss = _hyena_filters_raw(length, *filt_w)
    filt5 = filt.reshape(2, 2, HY_WIDTH, na, FFT_N2)
    ss5 = ss.reshape(2, 2, HY_WIDTH, 1, 1)
    seq = hyu3[:, :length]
    zcur = _to_freq_layout(seq[..., 2 * HY_WIDTH:], na)
    for n in range(2):
        hspec = _spectrum(filt5, ss5, n, tb, na)
        gate = _to_freq_layout(seq[..., n * HY_WIDTH:(n + 1) * HY_WIDTH], na)
        zcur = _fftconv(zcur, gate, hspec, skip[n].reshape(-1, 1, 1), cw3[2].reshape(-1, 1, 4),
                        cw3[n].reshape(-1, 1, 4), tb, na, conv_u=(n == 0))
    y_hy_lat = _from_freq_layout(zcur)

    filt_c, ss_c = _hyena_filters_raw(ctx_len, *filt_w)
    filt_c = filt_c.reshape(2, 2, HY_WIDTH, ctx_len)
    ss_c = ss_c.reshape(2, 2, HY_WIDTH, 1)
    u_ctx = jnp.transpose(hyu3[:, length:].reshape(nb, ctx_len, 3, HY_WIDTH), (2, 3, 0, 1))
    y_ctx_t = _ctxconv(u_ctx.reshape(3, HY_WIDTH, nb * ctx_len), cw3, filt_c[:, 0], filt_c[:, 1], ss_c[:, 0],
                       ss_c[:, 1], skip.reshape(2, HY_WIDTH, 1), _ctx_dft_tables(ctx_len))
    y_hy_ctx = jnp.transpose(y_ctx_t.reshape(HY_WIDTH, nb, ctx_len), (1, 2, 0))

    gate_blk = (hy_w + 3 * ret_w) // ret_w

    def specs0(h):
        tile = lambda i: 2 * i + h
        trow = lambda i: (tile(i), 0)
        return [pl.BlockSpec((TM, ret_w), trow), pl.BlockSpec((TM, ret_w), trow),
                pl.BlockSpec((TM, ret_w), lambda i: (tile(i), gate_blk)),
                *_lat_or_ctx_specs(HY_WIDTH, tpb, tile), *_lat_or_ctx_specs(d, tpb, tile),
                pl.BlockSpec((1, 6, d), lambda i: (cond_of(tile(i)), 0, 0))]

    x_mid, h2, mi, mf, cnt = _outproj_call(
        True, tpb, r // TM, d, specs0,
        (o_f.reshape(r, ret_w), o_b.reshape(r, ret_w), z0, y_hy_lat.reshape(nb * length, HY_WIDTH),
         y_hy_ctx.reshape(nb * ctx_len, HY_WIDTH), x_lat, x_ctx, mods[0]),
        ab_w_out[0].astype(BF16), ln_g[0, 0].reshape(1, d), ln_b[0, 0].reshape(1, d), router_w, router_bias)

    x1 = _moe(h2, mi, mf, cnt, x_mid, cond_of, mods[0], ln_g[0, 1].reshape(1, d), ln_b[0, 1].reshape(1, d),
              exp_w_gate[0].astype(BF16), exp_w_up[0].astype(BF16), exp_w_down[0].astype(BF16))

    kw = GLA_HEADS * GLA_DK
    vw = GLA_HEADS * GLA_DV
    n_in = gla_w_in.shape[2]
    n_pad = 2 * kw + 2 * vw + LANES
    w_in1 = jnp.pad(gla_w_in[0], ((0, 0), (0, n_pad - n_in))).astype(BF16)
    z1 = _inproj(x1, mods[1], w_in1, tpb, r)
    z1_3 = z1.reshape(nb, s, n_pad)
    gw_f = jnp.zeros((LANES, kw), F32).at[:GLA_RANK].set(gla_gate_w_f[0]).astype(BF16)
    gw_b = jnp.zeros((LANES, kw), F32).at[GLA_RANK:2 * GLA_RANK].set(gla_gate_w_b[0]).astype(BF16)
    g_f = _gla(z1_3, gw_f, gla_gate_b_f[0].reshape(1, kw), False)
    g_b = _gla(z1_3, gw_b, gla_gate_b_b[0].reshape(1, kw), True)

    n_lat = nb * lat_tiles

    def lat_tile(i):
        return (i // lat_tiles) * tpb + i % lat_tiles

    def lat_cond(i):
        return 2 * (i // lat_tiles)

    def specs1(h):
        lrow = lambda i: (lat_tile(2 * i + h), 0)
        return [pl.BlockSpec((TM, vw), lrow), pl.BlockSpec((TM, vw), lrow),
                pl.BlockSpec((TM, vw), lambda i: (lat_tile(2 * i + h), 2)),
                pl.BlockSpec((TM, d), lrow),
                pl.BlockSpec((1, 6, d), lambda i: (lat_cond(2 * i + h), 0, 0))]

    x_mid1, h2_1, mi1, mf1, cnt1 = _outproj_call(
        False, tpb, n_lat, d, specs1, (g_f.reshape(r, vw), g_b.reshape(r, vw), z1, x1, mods[1]),
        gla_w_out[0].astype(BF16), ln_g[1, 0].reshape(1, d), ln_b[1, 0].reshape(1, d), router_w, router_bias)

    out = _moe(h2_1, mi1, mf1, cnt1, x_mid1, lat_cond, mods[1], ln_g[1, 1].reshape(1, d), ln_b[1, 1].reshape(1, d),
               exp_w_gate[1].astype(BF16), exp_w_up[1].astype(BF16), exp_w_down[1].astype(BF16))
    return out.reshape(nb, length, d)
```

```python
import functools
import math

import numpy as np
import jax
import jax.numpy as jnp
from jax import lax
from jax.experimental import pallas as pl
from jax.experimental.pallas import tpu as pltpu

F32 = jnp.float32
BF16 = jnp.bfloat16

GRID_W = 64
RET_HEADS = 4
RET_HD = 128
RET_CHUNK = 128
ROPE_BASE = 10000.0
HY_WIDTH = 512
HY_EMB = 33
HY_FFN = 64
HY_SHORT_DECAY_PCT = 0.3
HY_LONG_DECAY_PCT = 1.5
HY_TARGET = 1e-2
GLA_HEADS = 4
GLA_DK = 128
GLA_DV = 256
GLA_RANK = 16
GLA_TAU = 16.0
GLA_CHUNK = 64
N_EXPERTS = 16
N_GROUPS = 4
EXPERTS_PER_GROUP = 4
LN_EPS = 1e-5
DEPTH = 2
DN_ALPHA = (2.0 * DEPTH) ** 0.25

PAIRS = ((0, 1), (0, 2), (0, 3), (1, 2), (1, 3), (2, 3))
N_CLASSES = N_GROUPS * len(PAIRS)
CLASS_ROWS = 32

LANES = 128
SUBLANES = 8
TM = 256
TE = 256
FFT_N2 = 128
FFT_CB = 16
VMEM_LIMIT = 48 * 1024 * 1024

HIGHEST = lax.Precision.HIGHEST


def _cparams(*sem):
    return pltpu.CompilerParams(dimension_semantics=sem, vmem_limit_bytes=VMEM_LIMIT)


def _silu(v):
    return v * (1.0 / (1.0 + jnp.exp(-v)))


def _dot(a, b):
    return jnp.dot(a, b, preferred_element_type=F32)


def _dot_nt(a, b):
    return lax.dot_general(a, b, (((1,), (1,)), ((), ())), preferred_element_type=F32)


def _dot_tn(a, b):
    return lax.dot_general(a, b, (((0,), (0,)), ((), ())), preferred_element_type=F32)


def _split2(a):
    hi = a.astype(BF16)
    return hi, (a - hi.astype(F32)).astype(BF16)


def _dot3(a, b, dot=_dot):
    ah, al = _split2(a)
    bh, bl = _split2(b)
    return dot(ah, bh) + dot(ah, bl) + dot(al, bh)


def _ada_kernel(c_ref, w_ref, b_ref, o_ref):
    a = _silu(c_ref[...])
    o_ref[0] = jnp.dot(a, w_ref[0], preferred_element_type=F32, precision=HIGHEST) + b_ref[0]


def _ada(cond, mod_w, mod_b):
    depth, d, n = mod_w.shape
    nt = n // 4
    return pl.pallas_call(
        _ada_kernel,
        out_shape=jax.ShapeDtypeStruct((depth, 8, n), F32),
        grid=(depth, n // nt),
        in_specs=[pl.BlockSpec((8, d), lambda l, j: (0, 0)),
                  pl.BlockSpec((1, d, nt), lambda l, j: (l, 0, j)),
                  pl.BlockSpec((1, 1, nt), lambda l, j: (l, 0, j))],
        out_specs=pl.BlockSpec((1, 8, nt), lambda l, j: (l, 0, j)),
        compiler_params=_cparams("arbitrary", "arbitrary"),
        name="ada_mod",
    )(cond, mod_w, mod_b.reshape(depth, 1, n))


def _inproj_kernel(x_ref, m_ref, w_ref, o_ref, *, chunks):
    sh = m_ref[0, 0:1, :]
    sc = m_ref[0, 1:2, :]
    h = (x_ref[...] * (1.0 + sc) + sh).astype(BF16)
    for lo, hi in chunks:
        o_ref[:, lo:hi] = _dot(h, w_ref[:, lo:hi]).astype(BF16)


def _lat_or_ctx_specs(width, tpb, tile_of=lambda i: i):
    lat_tiles = tpb - 1
    lat = lambda i: ((tile_of(i) // tpb) * lat_tiles + jnp.minimum(tile_of(i) % tpb, lat_tiles - 1), 0)
    ctx = lambda i: (tile_of(i) // tpb, 0)
    return pl.BlockSpec((TM, width), lat), pl.BlockSpec((TM, width), ctx)


def _col_chunks(n, width=512):
    out, lo = [], 0
    while lo < n:
        hi = min(lo + width, n)
        out.append((lo, hi))
        lo = hi
    return tuple(out)


def _inproj(xu, mods, w, tiles_per_batch):
    r, d = xu.shape
    n = w.shape[1]
    tpb = tiles_per_batch

    def cond_of(i):
        return 2 * (i // tpb) + jnp.where(i % tpb == tpb - 1, 1, 0)

    return pl.pallas_call(
        functools.partial(_inproj_kernel, chunks=_col_chunks(n)),
        out_shape=jax.ShapeDtypeStruct((r, n), BF16),
        grid=(r // TM,),
        in_specs=[pl.BlockSpec((TM, d), lambda i: (i, 0)),
                  pl.BlockSpec((1, 6, d), lambda i: (cond_of(i), 0, 0)),
                  pl.BlockSpec((d, n), lambda i: (0, 0))],
        out_specs=pl.BlockSpec((TM, n), lambda i: (i, 0)),
        compiler_params=_cparams("arbitrary"),
        name="inproj1",
    )(xu, mods, w)


def _inproj0_kernel(x_ref, c_ref, m_ref, w_ref, z_ref, x1_ref, x2_ref, v_ref, hc_ref, *, tpb, ret_w):
    is_ctx = pl.program_id(0) % tpb == tpb - 1
    x = jnp.where(is_ctx, c_ref[...], x_ref[...])
    h = (x * (1.0 + m_ref[0, 1:2, :]) + m_ref[0, 0:1, :]).astype(BF16)
    for lo, hi in _col_chunks(ret_w):
        z_ref[:, lo:hi] = _dot(h, w_ref[:, lo:hi]).astype(BF16)
    hy = [_dot(h, w_ref[:, ret_w + j * HY_WIDTH:ret_w + (j + 1) * HY_WIDTH]).astype(BF16) for j in range(3)]

    @pl.when(jnp.logical_not(is_ctx))
    def _():
        for ref, val in zip((x1_ref, x2_ref, v_ref), hy):
            ref[...] = val

    @pl.when(is_ctx)
    def _():
        for j, val in enumerate(hy):
            hc_ref[:, j * HY_WIDTH:(j + 1) * HY_WIDTH] = val


def _inproj0(x_lat, x_ctx, mods, w, tpb, r, ret_w):
    d = w.shape[0]
    lat_tiles = tpb - 1

    def cond_of(i):
        return 2 * (i // tpb) + jnp.where(i % tpb == tpb - 1, 1, 0)

    lat_spec, ctx_spec = _lat_or_ctx_specs(d, tpb)
    lat_out = pl.BlockSpec((TM, HY_WIDTH), lambda i: ((i // tpb) * lat_tiles + jnp.minimum(i % tpb, lat_tiles - 1), 0))
    hy_lat = jax.ShapeDtypeStruct(((r // TM // tpb) * lat_tiles * TM, HY_WIDTH), BF16)
    return pl.pallas_call(
        functools.partial(_inproj0_kernel, tpb=tpb, ret_w=ret_w),
        out_shape=(jax.ShapeDtypeStruct((r, ret_w), BF16), hy_lat, hy_lat, hy_lat,
                   jax.ShapeDtypeStruct(((r // TM // tpb) * TM, 3 * HY_WIDTH), BF16)),
        grid=(r // TM,),
        in_specs=[lat_spec, ctx_spec, pl.BlockSpec((1, 6, d), lambda i: (cond_of(i), 0, 0)),
                  pl.BlockSpec(w.shape, lambda i: (0, 0))],
        out_specs=(pl.BlockSpec((TM, ret_w), lambda i: (i, 0)), lat_out, lat_out, lat_out,
                   pl.BlockSpec((TM, 3 * HY_WIDTH), lambda i: (i // tpb, 0))),
        compiler_params=_cparams("arbitrary"),
        name="inproj0",
    )(x_lat, x_ctx, mods, w)


def _ret_kernel(lgf_ref, lgb_ref, *refs, nb):
    f_in, b_in, (of_ref, ob_ref, sf_ref, sb_ref) = refs[:5], refs[5:10], refs[10:]

    @pl.when(pl.program_id(0) == 0)
    def _():
        sf_ref[...] = jnp.zeros_like(sf_ref)
        sb_ref[...] = jnp.zeros_like(sb_ref)

    _ret_dir(lgf_ref, *f_in, of_ref, sf_ref, False, nb)
    _ret_dir(lgb_ref, *b_in, ob_ref, sb_ref, True, nb)


def _ret_dir(lg_ref, q_ref, k_ref, v_ref, cos_ref, sin_ref, o_ref, s_ref, reverse, nb):
    C = RET_CHUNK
    ii = lax.broadcasted_iota(jnp.int32, (C, C), 0).astype(F32)
    jj = lax.broadcasted_iota(jnp.int32, (C, C), 1).astype(F32)
    ci = lax.broadcasted_iota(jnp.int32, (C, 1), 0).astype(F32)
    diff = (jj - ii) if reverse else (ii - jj)
    order = (1, 0) if reverse else (0, 1)
    for h in range(RET_HEADS):
        lg = lg_ref[h]
        intra = jnp.where(diff >= 0, jnp.exp(jnp.maximum(diff, 0.0) * lg), 0.0)
        if reverse:
            q_dec = jnp.exp((C - ci) * lg)
            k_dec = jnp.exp(ci * lg)
        else:
            q_dec = jnp.exp((ci + 1.0) * lg)
            k_dec = jnp.exp((C - 1.0 - ci) * lg)
        c_dec = jnp.exp(jnp.zeros((1, RET_HD), F32) + C * lg)
        cols = slice(h * RET_HD, (h + 1) * RET_HD)
        for b in range(nb):
            for c in order:
                rows = slice(c * C, (c + 1) * C)
                cosf = cos_ref[rows, :]
                sinf = sin_ref[rows, :]
                q = q_ref[b, rows, cols].astype(F32)
                k = k_ref[b, rows, cols].astype(F32)
                v = v_ref[b, rows, cols]
                q = q * cosf + pltpu.roll(q, RET_HD // 2, 1) * sinf
                k = (k * cosf + pltpu.roll(k, RET_HD // 2, 1) * sinf) * (RET_HD ** -0.5)
                st = s_ref[b, h]
                scores = _dot_nt(q.astype(BF16), k.astype(BF16)) * intra
                o = _dot(scores.astype(BF16), v) + _dot_nt((q * q_dec).astype(BF16), st.astype(BF16))
                s_ref[b, h] = c_dec * st + _dot_tn(v, (k * k_dec).astype(BF16))
                o_ref[b, rows, cols] = o.astype(BF16)


def _scan_tile_orders(nt):
    fwd = lambda j: (j + nt - 1) % nt
    bwd = lambda j: jnp.where(j == 0, nt - 1, nt - 1 - j)
    return fwd, bwd


def _retention(z3, lg_f, lg_b, cosf, sinf):
    nb, s, _ = z3.shape
    nt = s // TM
    w = RET_HEADS * RET_HD
    smem = pl.BlockSpec(memory_space=pltpu.SMEM)

    def dir_specs(jmap):
        return [pl.BlockSpec((nb, TM, w), lambda j: (0, jmap(j), 0)),
                pl.BlockSpec((nb, TM, w), lambda j: (0, jmap(j), 1)),
                pl.BlockSpec((nb, TM, w), lambda j: (0, jmap(j), 2)),
                pl.BlockSpec((TM, RET_HD), lambda j: (jmap(j), 0)),
                pl.BlockSpec((TM, RET_HD), lambda j: (jmap(j), 0))]

    fwd, bwd = _scan_tile_orders(nt)
    state = pltpu.VMEM((nb, RET_HEADS, RET_HD, RET_HD), F32)
    out = jax.ShapeDtypeStruct((nb, s, w), BF16)
    grid_spec = pltpu.PrefetchScalarGridSpec(
        num_scalar_prefetch=0,
        grid=(nt,),
        in_specs=[smem, smem] + dir_specs(fwd) + dir_specs(bwd),
        out_specs=(pl.BlockSpec((nb, TM, w), lambda j: (0, fwd(j), 0)),
                   pl.BlockSpec((nb, TM, w), lambda j: (0, bwd(j), 0))),
        scratch_shapes=[state, state],
    )
    return pl.pallas_call(
        functools.partial(_ret_kernel, nb=nb),
        out_shape=(out, out),
        grid_spec=grid_spec,
        compiler_params=_cparams("arbitrary"),
        name="retention",
    )(lg_f, lg_b, z3, z3, z3, cosf, sinf, z3, z3, z3, cosf, sinf)


def _rope_tables(length, ctx_len):
    rows = length // GRID_W
    quarter = RET_HD // 4
    inv = ROPE_BASE ** (-jnp.arange(quarter, dtype=F32) / quarter)
    r = jnp.repeat(jnp.arange(rows, dtype=F32), GRID_W)
    col = jnp.tile(jnp.arange(GRID_W, dtype=F32), rows)
    ang = jnp.concatenate([r[:, None] * inv, col[:, None] * inv], axis=-1)
    cos, sin = jnp.cos(ang), jnp.sin(ang)
    cosf = jnp.concatenate([cos, cos], axis=-1)
    sinf = jnp.concatenate([-sin, sin], axis=-1)
    cosf = jnp.concatenate([cosf, jnp.ones((ctx_len, RET_HD), F32)], axis=0)
    sinf = jnp.concatenate([sinf, jnp.zeros((ctx_len, RET_HD), F32)], axis=0)
    return cosf, sinf


def _shortconv_freq(x, cw, na):
    row = lax.broadcasted_iota(jnp.int32, (1, FFT_N2, 1), 1)
    lane = lax.broadcasted_iota(jnp.int32, (1, 1, 2 * na), 2)
    wrap_prev = pltpu.roll(x[:, FFT_N2 - 1:FFT_N2, :], 1, 2)
    prev = jnp.where(row == 0, jnp.where((lane == 0) | (lane == na), 0.0, wrap_prev), pltpu.roll(x, 1, 1))
    wrap_next = pltpu.roll(x[:, 0:1, :], 2 * na - 1, 2)
    nxt = jnp.where(row == FFT_N2 - 1, jnp.where((lane == na - 1) | (lane == 2 * na - 1), 0.0, wrap_next),
                    pltpu.roll(x, FFT_N2 - 1, 1))
    return cw[..., 0:1] * prev + cw[..., 1:2] * x + cw[..., 2:3] * nxt + cw[..., 3:4]


def _shortconv_lanes(x, cw, n):
    lane = lax.broadcasted_iota(jnp.int32, (1, 2 * n), 1)
    prev = jnp.where((lane == 0) | (lane == n), 0.0, pltpu.roll(x, 1, 1))
    nxt = jnp.where((lane == n - 1) | (lane == 2 * n - 1), 0.0, pltpu.roll(x, 2 * n - 1, 1))
    return cw[:, 0:1] * prev + cw[:, 1:2] * x + cw[:, 2:3] * nxt + cw[:, 3:4]


FEAT_ROWS = 40


def _filter_kernel(w1_ref, b1_ref, w2_ref, b2_ref, w3_ref, b3_ref, w4_ref, fr_ref, dl_ref,
                   f_ref, ss_ref, *, length, pb):
    i = pl.program_id(0)
    pos = (lax.broadcasted_iota(jnp.int32, (1, pb), 1) + i * pb).astype(F32)
    t = pos * (1.0 / (length - 1))
    bands = (HY_EMB - 1) // 2
    w = (2.0 * math.pi) * pos / length
    sub = lax.broadcasted_iota(jnp.int32, (FEAT_ROWS, 1), 0)
    band = jnp.where(sub <= bands, sub - 1, sub - 1 - bands).astype(F32)
    f = 1e-4 + band * ((bands - 1 - 1e-4) / (bands - 1))
    fw = f * w
    feats = jnp.where(sub == 0, t, jnp.where(sub <= bands, jnp.cos(fw),
                                             jnp.where(sub <= 2 * bands, -jnp.sin(fw), 0.0)))
    feats = jnp.concatenate([feats, jnp.zeros((LANES - FEAT_ROWS, pb), F32)], axis=0)
    hdot = _dot3
    a = jnp.sin(fr_ref[:, 0:1] * (hdot(w1_ref[...], feats) + b1_ref[...]))
    a = jnp.sin(fr_ref[:, 1:2] * (hdot(w2_ref[...], a) + b2_ref[...]))
    a = jnp.sin(fr_ref[:, 2:3] * (hdot(w3_ref[...], a) + b3_ref[...]))

    @pl.when(i == 0)
    def _():
        ss_ref[...] = jnp.zeros_like(ss_ref)

    nout = w4_ref.shape[0]
    window = jnp.exp(-dl_ref[...] * t)
    for cb in range(nout // HY_WIDTH):
        rows = slice(cb * HY_WIDTH, (cb + 1) * HY_WIDTH)
        filt = hdot(w4_ref[rows, :], a) * window
        for j in range(pb // LANES):
            f_ref[rows, j, :] = filt[:, j * LANES:(j + 1) * LANES]
        ss_ref[rows, :] += jnp.sum(filt * filt, axis=1, keepdims=True)


def _hyena_filters_raw(length, w1, b1, w2, b2, w3, b3, w4, freq):
    nout = w4.shape[1]
    pb = min(length, 1024)
    max_decay = math.log(HY_TARGET) / HY_SHORT_DECAY_PCT
    min_decay = math.log(HY_TARGET) / HY_LONG_DECAY_PCT
    deltas = jnp.abs(jnp.linspace(min_decay, max_decay, HY_WIDTH, dtype=F32))
    dl = deltas.reshape(HY_WIDTH, 1)
    w1p = jnp.pad(w1.T, ((0, 0), (0, LANES - w1.shape[0])))
    col = lambda b: b.reshape(-1, 1)
    full = lambda a: pl.BlockSpec(a.shape, lambda i: tuple(0 for _ in a.shape))
    args = (w1p, col(b1), w2.T, col(b2), w3.T, col(b3), w4.T, freq.T, dl)
    return pl.pallas_call(
        functools.partial(_filter_kernel, length=length, pb=pb),
        out_shape=(jax.ShapeDtypeStruct((nout, length // LANES, LANES), F32),
                   jax.ShapeDtypeStruct((nout, 1), F32)),
        grid=(length // pb,),
        in_specs=[full(a) for a in args],
        out_specs=(pl.BlockSpec((nout, pb // LANES, LANES), lambda i: (0, i, 0)),
                   pl.BlockSpec((nout, 1), lambda i: (0, 0))),
        compiler_params=_cparams("arbitrary"),
        name="hyena_filter_mlp",
    )(*args)


def _dft_tables(na):
    k1n = 2 * na
    n = k1n * FFT_N2
    n1 = np.arange(na)[:, None]
    k1 = np.arange(k1n)[None, :]
    ang = 2.0 * np.pi * ((n1 * k1) % k1n) / k1n
    c, s = np.cos(ang), np.sin(ang)
    ma = np.block([[c, -s], [s, c]])
    n1f = np.arange(FFT_N2)[:, None]
    angf = 2.0 * np.pi * ((n1f * k1) % k1n) / k1n
    live = (n1f < na).astype(np.float64)
    ma_f = np.concatenate([np.cos(angf), -np.sin(angf)], axis=1) * live
    ma_b = np.concatenate([np.cos(angf), np.sin(angf)], axis=1) * live
    n2 = np.arange(FFT_N2)[:, None]
    angt = 2.0 * np.pi * ((n2 * k1) % n) / n
    twr, twi = np.cos(angt), -np.sin(angt)
    k2 = np.arange(FFT_N2)[None, :]
    angb = 2.0 * np.pi * ((n2 * k2) % FFT_N2) / FFT_N2
    cb, sb = np.cos(angb), np.sin(angb)
    mb = np.block([[cb, -sb], [sb, cb]])
    mc = np.block([[cb, sb], [-sb, cb]])
    angd = 2.0 * np.pi * ((np.arange(k1n)[:, None] * np.arange(na)[None, :]) % k1n) / k1n
    cd, sd = np.cos(angd) / n, np.sin(angd) / n
    md = np.block([[cd, sd], [-sd, cd]])
    as_bf = lambda a: jnp.asarray(a, dtype=F32).astype(BF16)
    as_f = lambda a: jnp.asarray(a, dtype=F32)
    mb2 = np.concatenate([mb, mc], axis=0)
    return dict(ma=as_bf(ma), ma_f=as_bf(ma_f), ma_b=as_bf(ma_b), twr=as_f(twr), twi=as_f(twi),
                twr_t=as_f(twr.T), twi_t=as_f(twi.T), mb=as_bf(mb), mb2=as_bf(mb2), mc=as_bf(mc), md=as_bf(md))


def _fwd_stages(x2, ma, twr, twi, mb, cb, k1n):
    a = _dot(x2, ma).reshape(cb, FFT_N2, 2 * k1n)
    ar, ai = a[..., :k1n], a[..., k1n:]
    ar2 = ar * twr - ai * twi
    ai2 = ar * twi + ai * twr
    xt = jnp.concatenate([jnp.swapaxes(ar2, 1, 2), jnp.swapaxes(ai2, 1, 2)], axis=-1)
    return _dot(xt.astype(BF16).reshape(cb * k1n, 2 * FFT_N2), mb).reshape(cb, k1n, 2 * FFT_N2)


def _spectrum_kernel(hf_ref, hb_ref, ssf_ref, ssb_ref, maf_ref, mab_ref, twr_ref, twi_ref, mb2_ref, h_ref, *, na):
    k1n = 2 * na
    cb = h_ref.shape[0]

    def stage_a(ref, ss_ref, m_ref):
        x = ref[0, 0] * lax.rsqrt(ss_ref[0, 0])
        x = jnp.concatenate([x, jnp.zeros((cb, FFT_N2 - na, FFT_N2), F32)], axis=1)
        xt = jnp.swapaxes(x, 1, 2).astype(BF16).reshape(cb * FFT_N2, FFT_N2)
        a = _dot(xt, m_ref[...]).reshape(cb, FFT_N2, 2 * k1n)
        return a[..., :k1n], a[..., k1n:]

    fr, fi = stage_a(hf_ref, ssf_ref, maf_ref)
    br, bi = stage_a(hb_ref, ssb_ref, mab_ref)
    twr, twi = twr_ref[...], twi_ref[...]
    parts = (fr * twr - fi * twi, fr * twi + fi * twr, br * twr + bi * twi, bi * twr - br * twi)
    xt = jnp.concatenate([jnp.swapaxes(p, 1, 2) for p in parts], axis=-1)
    h = _dot(xt.astype(BF16).reshape(cb * k1n, 4 * FFT_N2), mb2_ref[...])
    h_ref[...] = h.reshape(cb, k1n, 2 * FFT_N2).astype(BF16)


def _spectrum(filt5, ss5, order, tb, na):
    c = filt5.shape[2]
    k1n = 2 * na
    cb = FFT_CB
    full = lambda a: pl.BlockSpec(a.shape, lambda i: tuple(0 for _ in a.shape))
    fblk = lambda d: pl.BlockSpec((1, 1, cb, na, FFT_N2), lambda i: (order, d, i, 0, 0))
    sblk = lambda d: pl.BlockSpec((1, 1, cb, 1, 1), lambda i: (order, d, i, 0, 0))
    return pl.pallas_call(
        functools.partial(_spectrum_kernel, na=na),
        out_shape=jax.ShapeDtypeStruct((c, k1n, 2 * FFT_N2), BF16),
        grid=(c // cb,),
        in_specs=[fblk(0), fblk(1), sblk(0), sblk(1),
                  full(tb["ma_f"]), full(tb["ma_b"]), full(tb["twr"]), full(tb["twi"]), full(tb["mb2"])],
        out_specs=pl.BlockSpec((cb, k1n, 2 * FFT_N2), lambda i: (i, 0, 0)),
        compiler_params=_cparams("arbitrary"),
        name="hyena_filter_spectrum",
    )(filt5, filt5, ss5, ss5, tb["ma_f"], tb["ma_b"], tb["twr"], tb["twi"], tb["mb2"])


def _fftconv_kernel(u_ref, g_ref, h_ref, sk_ref, cwu_ref, cwg_ref, ma_ref, twr_ref, twi_ref, mb_ref, mc_ref,
                    twrt_ref, twit_ref, md_ref, o_ref, *, na, conv_u):
    k1n = 2 * na
    cb = u_ref.shape[0]
    uf = u_ref[...].astype(F32)
    if conv_u:
        uf = _shortconv_freq(uf, cwu_ref[...], na)
    gate = _shortconv_freq(g_ref[...].astype(F32), cwg_ref[...], na)
    x = _fwd_stages(uf.astype(BF16).reshape(cb * FFT_N2, k1n), ma_ref[...], twr_ref[...], twi_ref[...],
                    mb_ref[...], cb, k1n)
    xr, xi = x[..., :FFT_N2], x[..., FFT_N2:]
    h = h_ref[...].astype(F32)
    hr, hi = h[..., :FFT_N2], h[..., FFT_N2:]
    y = jnp.concatenate([xr * hr - xi * hi, xr * hi + xi * hr], axis=-1)
    c = _dot(y.astype(BF16).reshape(cb * k1n, 2 * FFT_N2), mc_ref[...]).reshape(cb, k1n, 2 * FFT_N2)
    cr, ci = c[..., :FFT_N2], c[..., FFT_N2:]
    twrt, twit = twrt_ref[...], twit_ref[...]
    cr2 = cr * twrt + ci * twit
    ci2 = ci * twrt - cr * twit
    ct = jnp.concatenate([jnp.swapaxes(cr2, 1, 2), jnp.swapaxes(ci2, 1, 2)], axis=-1)
    d = _dot(ct.astype(BF16).reshape(cb * FFT_N2, 2 * k1n), md_ref[...]).reshape(cb, FFT_N2, k1n)
    o_ref[...] = (gate * (d + sk_ref[...] * uf)).astype(BF16)


def _fftconv(u, gate, h, skip, cw_u, cw_g, tb, na, conv_u):
    c = u.shape[0]
    k1n = 2 * na
    cb = FFT_CB
    full = lambda a: pl.BlockSpec(a.shape, lambda i: tuple(0 for _ in a.shape))
    blk = pl.BlockSpec((cb, FFT_N2, k1n), lambda i: (i, 0, 0))
    cwblk = pl.BlockSpec((cb, 1, 4), lambda i: (i, 0, 0))
    return pl.pallas_call(
        functools.partial(_fftconv_kernel, na=na, conv_u=conv_u),
        out_shape=jax.ShapeDtypeStruct(u.shape, BF16),
        grid=(c // cb,),
        in_specs=[blk, blk,
                  pl.BlockSpec((cb, k1n, 2 * FFT_N2), lambda i: (i, 0, 0)),
                  pl.BlockSpec((cb, 1, 1), lambda i: (i, 0, 0)), cwblk, cwblk,
                  full(tb["ma"]), full(tb["twr"]), full(tb["twi"]), full(tb["mb"]), full(tb["mc"]),
                  full(tb["twr_t"]), full(tb["twi_t"]), full(tb["md"])],
        out_specs=blk,
        compiler_params=_cparams("arbitrary"),
        name="hyena_fftconv",
    )(u, gate, h, skip, cw_u, cw_g, tb["ma"], tb["twr"], tb["twi"], tb["mb"], tb["mc"], tb["twr_t"], tb["twi_t"],
      tb["md"])


def _to_freq_layout(a, na):
    nb, _, c = a.shape
    a = a.reshape(nb, na, FFT_N2, c)
    return jnp.transpose(a, (3, 2, 0, 1)).reshape(c, FFT_N2, nb * na)


def _from_freq_layout(a):
    c, _, lanes = a.shape
    na = lanes // 2
    a = a.reshape(c, FFT_N2, 2, na)
    return jnp.transpose(a, (2, 3, 1, 0)).reshape(2, na * FFT_N2, c)


def _ctx_dft_tables(n):
    big = 2 * n
    j = np.arange(n)[:, None]
    k = np.arange(big)[None, :]
    ang = 2.0 * np.pi * ((j * k) % big) / big
    c, s = np.cos(ang), np.sin(ang)
    fh = np.block([[c, -s], [c, s]])
    ff = np.block([[c, -s], [s, c]])
    fi = np.block([[c.T, s.T], [-s.T, c.T]]) / big
    as_bf = lambda a: jnp.asarray(a, dtype=F32).astype(BF16)
    return as_bf(fh), as_bf(ff), as_bf(fi)


def _ctxconv_kernel(u_ref, cw_ref, hf_ref, hb_ref, ssf_ref, ssb_ref, sk_ref, fh_ref, ff_ref, fi_ref, o_ref):
    n_tok = u_ref.shape[2] // 2
    short = lambda j: _shortconv_lanes(u_ref[j].astype(F32), cw_ref[j], n_tok)
    gates = (short(0), short(1))
    z = short(2)
    half = fh_ref.shape[1] // 2
    for n in range(2):
        taps = jnp.concatenate([hf_ref[n] * lax.rsqrt(ssf_ref[n]), hb_ref[n] * lax.rsqrt(ssb_ref[n])], axis=-1)
        h = _dot(taps.astype(BF16), fh_ref[...])
        x = _dot(z.astype(BF16), ff_ref[...])
        hr, hi = h[:, :half], h[:, half:]
        xr, xi = x[:, :half], x[:, half:]
        y = jnp.concatenate([xr * hr - xi * hi, xr * hi + xi * hr], axis=-1)
        conv = _dot(y.astype(BF16), fi_ref[...])
        z = gates[n] * (conv + sk_ref[n] * z)
    o_ref[...] = z.astype(BF16)


def _ctxconv(u3, cw3, hf, hb, ssf, ssb, skip, tables):
    _, c, lanes = u3.shape
    args = (u3, cw3, hf, hb, ssf, ssb, skip) + tuple(tables)
    full = lambda a: pl.BlockSpec(a.shape, lambda i: tuple(0 for _ in a.shape))
    return pl.pallas_call(
        _ctxconv_kernel,
        out_shape=jax.ShapeDtypeStruct((c, lanes), BF16),
        grid=(1,),
        in_specs=[full(a) for a in args],
        out_specs=pl.BlockSpec((c, lanes), lambda i: (0, 0)),
        compiler_params=_cparams("arbitrary"),
        name="hyena_ctxconv",
    )(*args)


def _load_row_tiles(ref, n, stride=SUBLANES, first=0):
    return jnp.concatenate([ref[pl.ds(first + j, n, stride=stride), :] for j in range(SUBLANES)], axis=-1)


def _store_row_tiles(ref, val, n, stride=SUBLANES, first=0):
    for j in range(SUBLANES):
        ref[pl.ds(first + j, n, stride=stride), :] = val[:, j * LANES:(j + 1) * LANES]


def _row_tile(ref, idx, tiles=1):
    size = tiles * SUBLANES
    return ref.at[pl.ds(pl.multiple_of(idx * size, size), size)]


def _top2_rows(vals):
    n = len(vals)
    best_v, best_i = vals[0], jnp.zeros_like(vals[0], dtype=jnp.int32)
    for e in range(1, n):
        take = vals[e] > best_v
        best_v = jnp.where(take, vals[e], best_v)
        best_i = jnp.where(take, e, best_i)
    sec_v = jnp.full_like(vals[0], -jnp.inf)
    sec_i = jnp.zeros_like(best_i)
    for e in range(n):
        take = (best_i != e) & (vals[e] > sec_v)
        sec_v = jnp.where(take, vals[e], sec_v)
        sec_i = jnp.where(take, e, sec_i)
    return best_v, best_i, sec_v, sec_i


def _route(h2, rw_ref, rb_ref, carry_ref, mi_ref, mf_ref, cnt_ref, cols):
    logits = _dot3(rw_ref[...], h2, dot=_dot_nt)
    s = 1.0 / (1.0 + jnp.exp(-logits))
    sel = s + rb_ref[...]
    srow = [s[e:e + 1, :] for e in range(N_EXPERTS)]
    selrow = [sel[e:e + 1, :] for e in range(N_EXPERTS)]
    gscore = []
    for g in range(N_GROUPS):
        grp = selrow[g * EXPERTS_PER_GROUP:(g + 1) * EXPERTS_PER_GROUP]
        bv, _, sv, _ = _top2_rows(grp)
        gscore.append(bv + sv)
    best_g = jnp.zeros_like(gscore[0], dtype=jnp.int32)
    best_s = gscore[0]
    for g in range(1, N_GROUPS):
        take = gscore[g] > best_s
        best_s = jnp.where(take, gscore[g], best_s)
        best_g = jnp.where(take, g, best_g)
    cand_sel, cand_s = [], []
    for j in range(EXPERTS_PER_GROUP):
        cs, ca = selrow[j], srow[j]
        for g in range(1, N_GROUPS):
            cs = jnp.where(best_g == g, selrow[g * EXPERTS_PER_GROUP + j], cs)
            ca = jnp.where(best_g == g, srow[g * EXPERTS_PER_GROUP + j], ca)
        cand_sel.append(cs)
        cand_s.append(ca)
    _, i1, _, i2 = _top2_rows(cand_sel)
    a1, a2 = cand_s[0], cand_s[0]
    for j in range(1, EXPERTS_PER_GROUP):
        a1 = jnp.where(i1 == j, cand_s[j], a1)
        a2 = jnp.where(i2 == j, cand_s[j], a2)
    den = a1 + a2
    w1, w2 = a1 / den, a2 / den
    swap = i2 < i1
    lo = jnp.where(swap, i2, i1)
    hi = jnp.where(swap, i1, i2)
    w_lo = jnp.where(swap, w2, w1)
    w_hi = jnp.where(swap, w1, w2)
    pair = jnp.where(lo == 0, 0, jnp.where(lo == 1, 3, 5)) + (hi - lo - 1)
    cls = best_g * len(PAIRS) + pair
    cidx = lax.broadcasted_iota(jnp.int32, (CLASS_ROWS, TM), 0)
    hit = cidx == cls
    onehot = jnp.where(hit, 1.0, 0.0)
    tri = jnp.where(lax.broadcasted_iota(jnp.int32, (TM, TM), 0) < lax.broadcasted_iota(jnp.int32, (TM, TM), 1),
                    1.0, 0.0).astype(BF16)
    rank = _dot(onehot.astype(BF16), tri) + carry_ref[:, 0:1]
    rk = jnp.sum(jnp.where(hit, rank, 0.0), axis=0, keepdims=True)
    carry_ref[...] = carry_ref[...] + jnp.sum(onehot, axis=1, keepdims=True)
    mi_ref[:, cols] = jnp.concatenate([cls, rk.astype(jnp.int32), jnp.zeros((6, TM), jnp.int32)], axis=0)
    mf_ref[:, cols] = jnp.concatenate([w_lo, w_hi, jnp.zeros((6, TM), F32)], axis=0)
    cnt_ref[...] = carry_ref[...]


def _post_norm_mod(x, y, m_ref, lng, lnb, gate_row, sh_row, sc_row):
    u = DN_ALPHA * x + m_ref[0, gate_row:gate_row + 1, :] * y
    mu = jnp.mean(u, axis=-1, keepdims=True)
    var = jnp.mean(jnp.square(u - mu), axis=-1, keepdims=True)
    xn = (u - mu) * lax.rsqrt(var + LN_EPS) * lng + lnb
    if sh_row is None:
        return xn, None
    return xn, xn * (1.0 + m_ref[0, sc_row:sc_row + 1, :]) + m_ref[0, sh_row:sh_row + 1, :]


def _outproj_kernel(*refs, layer0, tpb):
    nd = 8 if layer0 else 5
    halves = (refs[:nd], refs[nd:2 * nd])
    w_ref, lng_ref, lnb_ref, rw_ref, rb_ref, xo_ref, h2_ref, mi_ref, mf_ref, cnt_ref, carry_ref = refs[2 * nd:]
    i = pl.program_id(0)

    @pl.when(i == 0)
    def _():
        carry_ref[...] = jnp.zeros_like(carry_ref)

    for half, data in enumerate(halves):
        if layer0:
            of_ref, ob_ref, g_ref, hyl_ref, hyc_ref, xl_ref, xc_ref, m_ref = data
            is_ctx = (2 * i + half) % tpb == tpb - 1
            x = jnp.where(is_ctx, xc_ref[...], xl_ref[...])
            hy = jnp.where(is_ctx, hyc_ref[...], hyl_ref[...])
            o = of_ref[...].astype(F32) + ob_ref[...].astype(F32)
            parts = []
            for h in range(RET_HEADS):
                oh = o[:, h * RET_HD:(h + 1) * RET_HD]
                mu = jnp.mean(oh, axis=-1, keepdims=True)
                var = jnp.mean(jnp.square(oh - mu), axis=-1, keepdims=True)
                parts.append((oh - mu) * lax.rsqrt(var + LN_EPS))
            yret = (jnp.concatenate(parts, axis=-1) * _silu(g_ref[...].astype(F32))).astype(BF16)
            wr = yret.shape[1]
            y = _dot(yret, w_ref[:wr, :]) + _dot(hy, w_ref[wr:, :])
        else:
            of_ref, ob_ref, g_ref, x_ref, m_ref = data
            x = x_ref[...]
            o = of_ref[...].astype(F32) + ob_ref[...].astype(F32)
            parts = []
            for h in range(GLA_HEADS):
                oh = o[:, h * GLA_DV:(h + 1) * GLA_DV]
                parts.append(oh * lax.rsqrt(jnp.mean(jnp.square(oh), axis=-1, keepdims=True) + LN_EPS))
            a = (jnp.concatenate(parts, axis=-1) * _silu(g_ref[...].astype(F32))).astype(BF16)
            y = _dot(a, w_ref[...])
        rows = slice(half * TM, (half + 1) * TM)
        xn, h2 = _post_norm_mod(x, y, m_ref, lng_ref[...], lnb_ref[...], 2, 3, 4)
        xo_ref[rows, :] = xn
        _store_row_tiles(h2_ref, h2, TM, first=half * TM * SUBLANES)
        _route(h2, rw_ref, rb_ref, carry_ref, mi_ref, mf_ref, cnt_ref, rows)


def _outproj_call(layer0, tpb, n_tiles, d, half_specs, half_args, w, lng, lnb, router_w, router_bias):
    rows = n_tiles * TM
    full2 = lambda a: pl.BlockSpec(a.shape, lambda i: (0, 0))
    rw_t = router_w.T
    rb = router_bias.reshape(N_EXPERTS, 1)
    return pl.pallas_call(
        functools.partial(_outproj_kernel, layer0=layer0, tpb=tpb),
        out_shape=(jax.ShapeDtypeStruct((rows, d), F32),
                   jax.ShapeDtypeStruct((rows * SUBLANES, LANES), F32),
                   jax.ShapeDtypeStruct((8, rows), jnp.int32),
                   jax.ShapeDtypeStruct((8, rows), F32),
                   jax.ShapeDtypeStruct((CLASS_ROWS, LANES), F32)),
        grid=(n_tiles // 2,),
        in_specs=list(half_specs(0)) + list(half_specs(1)) + [full2(w), full2(lng), full2(lnb), full2(rw_t), full2(rb)],
        out_specs=(pl.BlockSpec((2 * TM, d), lambda i: (i, 0)),
                   pl.BlockSpec((2 * TM * SUBLANES, LANES), lambda i: (i, 0)),
                   pl.BlockSpec((8, 2 * TM), lambda i: (0, i)),
                   pl.BlockSpec((8, 2 * TM), lambda i: (0, i)),
                   pl.BlockSpec((CLASS_ROWS, LANES), lambda i: (0, 0))),
        scratch_shapes=[pltpu.VMEM((CLASS_ROWS, LANES), F32)],
        compiler_params=_cparams("arbitrary"),
        name="outproj_norm0" if layer0 else "outproj_norm1",
    )(*half_args, *half_args, w, lng, lnb, rw_t, rb)


def _dispatch_kernel(slots_ref, zoff_ref, h_ref, xs_ref, zero_ref, sem):
    i = pl.program_id(0)

    @pl.when(i == 0)
    def _():
        zero_ref[...] = jnp.zeros_like(zero_ref)
        def fill_class(e, go):
            @pl.when(zoff_ref[N_CLASSES + 1 + e] > 0)
            def _():
                go(pltpu.make_async_copy(zero_ref, _row_tile(xs_ref, zoff_ref[e], TE), sem))

        for e in range(N_CLASSES):
            fill_class(e, lambda cp: cp.start())
        for e in range(N_CLASSES):
            fill_class(e, lambda cp: cp.wait())
        first_free = zoff_ref[N_CLASSES]
        n_all = xs_ref.shape[0] // (TE * SUBLANES)

        def fill(t, carry):
            pltpu.make_async_copy(zero_ref, _row_tile(xs_ref, t, TE), sem).start()
            return carry

        def fill_wait(t, carry):
            pltpu.make_async_copy(zero_ref, _row_tile(xs_ref, t, TE), sem).wait()
            return carry

        lax.fori_loop(first_free, n_all, fill, 0)
        lax.fori_loop(first_free, n_all, fill_wait, 0)

    def row_copy(r):
        return pltpu.make_async_copy(_row_tile(h_ref, r), _row_tile(xs_ref, slots_ref[0, 0, r]), sem)

    def issue(r2, carry):
        row_copy(2 * r2).start(priority=0)
        row_copy(2 * r2 + 1).start(priority=1)
        return carry

    lax.fori_loop(0, TM // 2, issue, 0, unroll=4)

    def drain(r, carry):
        row_copy(r).wait()
        return carry

    lax.fori_loop(0, TM, drain, 0, unroll=8)


def _dispatch(h2, slots3, zoff, p_rows):
    n_tiles = h2.shape[0] // (TM * SUBLANES)
    grid_spec = pltpu.PrefetchScalarGridSpec(
        num_scalar_prefetch=0,
        grid=(n_tiles,),
        in_specs=[pl.BlockSpec((1, 1, TM), lambda i: (i, 0, 0), memory_space=pltpu.SMEM),
                  pl.BlockSpec(memory_space=pltpu.SMEM),
                  pl.BlockSpec((TM * SUBLANES, LANES), lambda i: (i, 0))],
        out_specs=pl.BlockSpec(memory_space=pl.ANY),
        scratch_shapes=[pltpu.VMEM((TE * SUBLANES, LANES), F32), pltpu.SemaphoreType.DMA(())],
    )
    return pl.pallas_call(
        _dispatch_kernel,
        out_shape=jax.ShapeDtypeStruct((p_rows * SUBLANES, LANES), F32),
        grid_spec=grid_spec,
        compiler_params=_cparams("arbitrary"),
        name="moe_dispatch",
    )(slots3, zoff, h2)


def _experts_kernel(ea_ref, eb_ref, tv_ref, x_ref, wga_ref, wua_ref, wda_ref, wgb_ref, wub_ref, wdb_ref, y_ref):
    j = pl.program_id(0)

    @pl.when(tv_ref[j] > 0)
    def _():
        x = _load_row_tiles(x_ref, TE).astype(BF16)
        for which, (wg, wu, wd) in enumerate(((wga_ref, wua_ref, wda_ref), (wgb_ref, wub_ref, wdb_ref))):
            g = _dot(x, wg[0])
            u = _dot(x, wu[0])
            y = _dot((_silu(g) * u).astype(BF16), wd[0])
            _store_row_tiles(y_ref, y, TE, stride=2 * SUBLANES, first=which * SUBLANES)

    @pl.when(tv_ref[j] == 0)
    def _():
        y_ref[...] = jnp.zeros_like(y_ref)


def _experts(xs, tile_ea, tile_eb, tile_valid, wg, wu, wd):
    n_tiles = xs.shape[0] // (TE * SUBLANES)
    d, hdim = wg.shape[1], wg.shape[2]
    wa = lambda shp: pl.BlockSpec(shp, lambda j, ea, eb, tv: (ea[j], 0, 0))
    wb = lambda shp: pl.BlockSpec(shp, lambda j, ea, eb, tv: (eb[j], 0, 0))
    grid_spec = pltpu.PrefetchScalarGridSpec(
        num_scalar_prefetch=3,
        grid=(n_tiles,),
        in_specs=[pl.BlockSpec((TE * SUBLANES, LANES), lambda j, ea, eb, tv: (j, 0)),
                  wa((1, d, hdim)), wa((1, d, hdim)), wa((1, hdim, d)),
                  wb((1, d, hdim)), wb((1, d, hdim)), wb((1, hdim, d))],
        out_specs=pl.BlockSpec((2 * TE * SUBLANES, LANES), lambda j, ea, eb, tv: (j, 0)),
    )
    return pl.pallas_call(
        _experts_kernel,
        out_shape=jax.ShapeDtypeStruct((2 * xs.shape[0], LANES), F32),
        grid_spec=grid_spec,
        compiler_params=_cparams("arbitrary"),
        name="moe_experts",
    )(tile_ea, tile_eb, tile_valid, xs, wg, wu, wd, wg, wu, wd)


def _combine_kernel(slots_ref, next_slots_ref, ys_ref, wts_ref, x_ref, m_ref, lng_ref, lnb_ref, o_ref, buf_ref, sem):
    i = pl.program_id(0)
    cur = i % 2

    def row_copy(s_ref, b, r):
        return pltpu.make_async_copy(_row_tile(ys_ref, s_ref[0, 0, r], 2), _row_tile(buf_ref.at[b], r, 2), sem.at[b])

    def issue_all(s_ref, b):
        def issue(r2, carry):
            row_copy(s_ref, b, 2 * r2).start(priority=0)
            row_copy(s_ref, b, 2 * r2 + 1).start(priority=1)
            return carry

        lax.fori_loop(0, TM // 2, issue, 0, unroll=4)

    @pl.when(i == 0)
    def _():
        issue_all(slots_ref, 0)

    @pl.when(i + 1 < pl.num_programs(0))
    def _():
        issue_all(next_slots_ref, 1 - cur)

    def drain(r, carry):
        row_copy(slots_ref, cur, r).wait()
        return carry

    lax.fori_loop(0, TM, drain, 0, unroll=8)
    buf = buf_ref.at[cur]
    y_lo = _load_row_tiles(buf, TM, stride=2 * SUBLANES)
    y_hi = _load_row_tiles(buf, TM, stride=2 * SUBLANES, first=SUBLANES)
    y = wts_ref[:, 0:1] * y_lo + wts_ref[:, 1:2] * y_hi
    xn, _ = _post_norm_mod(x_ref[...], y, m_ref, lng_ref[...], lnb_ref[...], 5, None, None)
    o_ref[...] = xn


def _combine(ys, slots3, wts, x, mods, lng, lnb, cond_of, n_tiles):
    d = x.shape[1]
    grid_spec = pltpu.PrefetchScalarGridSpec(
        num_scalar_prefetch=0,
        grid=(n_tiles,),
        in_specs=[pl.BlockSpec((1, 1, TM), lambda i: (i, 0, 0), memory_space=pltpu.SMEM),
                  pl.BlockSpec((1, 1, TM), lambda i: (jnp.minimum(i + 1, n_tiles - 1), 0, 0),
                               memory_space=pltpu.SMEM),
                  pl.BlockSpec(memory_space=pl.ANY),
                  pl.BlockSpec((TM, 2), lambda i: (i, 0)),
                  pl.BlockSpec((TM, d), lambda i: (i, 0)),
                  pl.BlockSpec((1, 6, d), lambda i: (cond_of(i), 0, 0)),
                  pl.BlockSpec((1, d), lambda i: (0, 0)),
                  pl.BlockSpec((1, d), lambda i: (0, 0))],
        out_specs=pl.BlockSpec((TM, d), lambda i: (i, 0)),
        scratch_shapes=[pltpu.VMEM((2, 2 * TM * SUBLANES, LANES), F32), pltpu.SemaphoreType.DMA((2,))],
    )
    return pl.pallas_call(
        _combine_kernel,
        out_shape=jax.ShapeDtypeStruct((n_tiles * TM, d), F32),
        grid_spec=grid_spec,
        compiler_params=_cparams("arbitrary"),
        name="moe_combine",
    )(slots3, slots3, ys, wts, x, mods, lng, lnb)


def _moe(h2, mi, mf, cnt, x, cond_of, mods, lng, lnb, wg, wu, wd):
    t = x.shape[0]
    n_tiles = t // TM
    counts = cnt[:N_CLASSES, 0].astype(jnp.int32)
    padded = ((counts + TE - 1) // TE) * TE
    ends = jnp.cumsum(padded)
    offs = ends - padded
    cls = mi[0]
    cids = jnp.arange(N_CLASSES, dtype=jnp.int32)[:, None]
    slots = jnp.sum(jnp.where(cls[None] == cids, offs[:, None], 0), axis=0) + mi[1]
    slots3 = slots.reshape(n_tiles, 1, TM)
    n_cls_tiles = t // TE + N_CLASSES
    total_tiles = ends[-1] // TE
    tile_ids = jnp.arange(n_cls_tiles, dtype=jnp.int32)
    tile_valid = (tile_ids < total_tiles).astype(jnp.int32)
    tile_last = jnp.minimum(tile_ids, total_tiles - 1)
    tile_class = jnp.sum((tile_last[:, None] * TE >= ends[None, :]).astype(jnp.int32), axis=1)
    tile_class = jnp.minimum(tile_class, N_CLASSES - 1)
    pair_lo = jnp.asarray([p[0] for p in PAIRS], jnp.int32)
    pair_hi = jnp.asarray([p[1] for p in PAIRS], jnp.int32)
    pair_sel = (tile_class % len(PAIRS))[:, None] == jnp.arange(len(PAIRS), dtype=jnp.int32)[None, :]
    group0 = (tile_class // len(PAIRS)) * EXPERTS_PER_GROUP
    tile_ea = (group0 + jnp.sum(jnp.where(pair_sel, pair_lo[None, :], 0), axis=1)).astype(jnp.int32)
    tile_eb = (group0 + jnp.sum(jnp.where(pair_sel, pair_hi[None, :], 0), axis=1)).astype(jnp.int32)
    zoff = jnp.concatenate([jnp.maximum(ends // TE - 1, 0), total_tiles[None], padded]).astype(jnp.int32)
    xs = _dispatch(h2, slots3, zoff, n_cls_tiles * TE)
    ys = _experts(xs, tile_ea, tile_eb, tile_valid, wg, wu, wd)
    wts = jnp.transpose(mf[0:2])
    return _combine(ys, slots3, wts, x, mods, lng, lnb, cond_of, n_tiles)


def _chunk_cumsum(x, tri):
    hi = x.astype(BF16)
    r1 = x - hi.astype(F32)
    mid = r1.astype(BF16)
    lo = (r1 - mid.astype(F32)).astype(BF16)
    return _dot(tri, hi) + _dot(tri, mid) + _dot(tri, lo)


def _gla_kernel(*refs, nb):
    f_in, b_in, (of_ref, ob_ref, sf_ref, sb_ref) = refs[:6], refs[6:12], refs[12:]

    @pl.when(pl.program_id(0) == 0)
    def _():
        sf_ref[...] = jnp.zeros_like(sf_ref)
        sb_ref[...] = jnp.zeros_like(sb_ref)

    _gla_dir(*f_in, of_ref, sf_ref, False, nb)
    _gla_dir(*b_in, ob_ref, sb_ref, True, nb)


def _gla_dir(q_ref, k_ref, v_ref, lr_ref, gw_ref, gb_ref, o_ref, s_ref, reverse, nb):
    C = GLA_CHUNK
    nchunk = TM // C
    ii = lax.broadcasted_iota(jnp.int32, (C, C), 0)
    jj = lax.broadcasted_iota(jnp.int32, (C, C), 1)
    keep = (jj >= ii) if reverse else (jj <= ii)
    tri = jnp.where(keep, 1.0, 0.0).astype(BF16)
    order = tuple(reversed(range(nchunk))) if reverse else tuple(range(nchunk))
    end_row = 0 if reverse else C - 1
    gw = gw_ref[...]
    gb = gb_ref[...]
    for b in range(nb):
        pre = _dot(lr_ref[b], gw) + gb
        la_all = (jnp.minimum(pre, 0.0) - jnp.log(1.0 + jnp.exp(-jnp.abs(pre)))) * (1.0 / GLA_TAU)
        for c in order:
            rows = slice(c * C, (c + 1) * C)
            bc = _chunk_cumsum(la_all[rows, :], tri)
            bend = bc[end_row:end_row + 1, :]
            eb = jnp.exp(bc)
            enb = jnp.exp(-bc)
            ekb = jnp.exp(bend - bc)
            ebend = jnp.exp(bend)
            for h in range(GLA_HEADS):
                kc = slice(h * GLA_DK, (h + 1) * GLA_DK)
                vc = slice(h * GLA_DV, (h + 1) * GLA_DV)
                q = q_ref[b, rows, kc].astype(F32) * (GLA_DK ** -0.5)
                k = k_ref[b, rows, kc].astype(F32)
                v = v_ref[b, rows, vc]
                qd = (q * eb[:, kc]).astype(BF16)
                kd = (k * enb[:, kc]).astype(BF16)
                st = s_ref[b, h]
                scores = jnp.where(keep, _dot_nt(qd, kd), 0.0)
                o = _dot(scores.astype(BF16), v) + _dot_nt(qd, st.astype(BF16))
                s_ref[b, h] = st * ebend[:, kc] + _dot_tn(v, (k * ekb[:, kc]).astype(BF16))
                o_ref[b, rows, vc] = o.astype(BF16)


def _gla(z3, gw_f, gb_f, gw_b, gb_b):
    nb, s, _ = z3.shape
    nt = s // TM
    kw = GLA_HEADS * GLA_DK
    vw = GLA_HEADS * GLA_DV
    lr_blk = (2 * kw + 2 * vw) // LANES

    def dir_specs(jmap):
        return [pl.BlockSpec((nb, TM, kw), lambda j: (0, jmap(j), 0)),
                pl.BlockSpec((nb, TM, kw), lambda j: (0, jmap(j), 1)),
                pl.BlockSpec((nb, TM, vw), lambda j: (0, jmap(j), 1)),
                pl.BlockSpec((nb, TM, LANES), lambda j: (0, jmap(j), lr_blk)),
                pl.BlockSpec((LANES, kw), lambda j: (0, 0)),
                pl.BlockSpec((1, kw), lambda j: (0, 0))]

    fwd, bwd = _scan_tile_orders(nt)
    state = pltpu.VMEM((nb, GLA_HEADS, GLA_DV, GLA_DK), F32)
    out = jax.ShapeDtypeStruct((nb, s, vw), BF16)
    return pl.pallas_call(
        functools.partial(_gla_kernel, nb=nb),
        out_shape=(out, out),
        grid=(nt,),
        in_specs=dir_specs(fwd) + dir_specs(bwd),
        out_specs=(pl.BlockSpec((nb, TM, vw), lambda j: (0, fwd(j), 0)),
                   pl.BlockSpec((nb, TM, vw), lambda j: (0, bwd(j), 0))),
        scratch_shapes=[state, state],
        compiler_params=_cparams("arbitrary"),
        name="gla",
    )(z3, z3, z3, z3, gw_f, gb_f, z3, z3, z3, z3, gw_b, gb_b)


def kernel(x, c, ctx, c_ctx, mod_w, mod_b, ln_g, ln_b, ab_w_in, ret_log_decay_f, ret_log_decay_b, hy_conv_w, hy_conv_b, hy_w1, hy_b1, hy_w2, hy_b2, hy_w3, hy_b3, hy_w4, hy_freq, hy_skip, ab_w_out, gla_w_in, gla_gate_w_f, gla_gate_b_f, gla_gate_w_b, gla_gate_b_b, gla_w_out, router_w, router_bias, exp_w_gate, exp_w_up, exp_w_down):
    nb, length, d = x.shape
    ctx_len = ctx.shape[1]
    assert nb == 2 and ctx_len == TM and length % TM == 0 and length % (FFT_N2 * 2) == 0
    s = length + ctx_len
    tpb = s // TM
    r = nb * s
    lat_tiles = tpb - 1

    cond = jnp.zeros((8, d), F32).at[0].set(c[0]).at[1].set(c_ctx).at[2].set(c[1]).at[3].set(c_ctx)
    mods = _ada(cond, mod_w, mod_b).reshape(DEPTH, 8, 6, d)

    def cond_of(i):
        return 2 * (i // tpb) + jnp.where(i % tpb == tpb - 1, 1, 0)

    x_lat = x.reshape(nb * length, d)
    x_ctx = ctx.reshape(nb * ctx_len, d)

    ret_w = RET_HEADS * RET_HD
    z0, *hy_lat, hy_ctx = _inproj0(x_lat, x_ctx, mods[0], ab_w_in[0].astype(BF16), tpb, r, 4 * ret_w)
    z0_3 = z0.reshape(nb, s, -1)
    cosf, sinf = _rope_tables(length, ctx_len)
    o_f, o_b = _retention(z0_3, ret_log_decay_f[0], ret_log_decay_b[0], cosf, sinf)

    cw3 = jnp.concatenate([hy_conv_w[0].T, hy_conv_b[0][:, None]], axis=1).reshape(3, HY_WIDTH, 4)
    na = length // FFT_N2
    tb = _dft_tables(na)
    filt_w = (hy_w1[0], hy_b1[0], hy_w2[0], hy_b2[0], hy_w3[0], hy_b3[0], hy_w4[0], hy_freq[0])
    skip = hy_skip[0]

    filt, ss = _hyena_filters_raw(length, *filt_w)
    filt5 = filt.reshape(2, 2, HY_WIDTH, na, FFT_N2)
    ss5 = ss.reshape(2, 2, HY_WIDTH, 1, 1)
    hy_lat = [a.reshape(nb, length, HY_WIDTH) for a in hy_lat]
    zcur = _to_freq_layout(hy_lat[2], na)
    for n in range(2):
        hspec = _spectrum(filt5, ss5, n, tb, na)
        gate = _to_freq_layout(hy_lat[n], na)
        zcur = _fftconv(zcur, gate, hspec, skip[n].reshape(-1, 1, 1), cw3[2].reshape(-1, 1, 4),
                        cw3[n].reshape(-1, 1, 4), tb, na, conv_u=(n == 0))
    y_hy_lat = _from_freq_layout(zcur)

    filt_c, ss_c = _hyena_filters_raw(ctx_len, *filt_w)
    filt_c = filt_c.reshape(2, 2, HY_WIDTH, ctx_len)
    ss_c = ss_c.reshape(2, 2, HY_WIDTH, 1)
    u_ctx = jnp.transpose(hy_ctx.reshape(nb, ctx_len, 3, HY_WIDTH), (2, 3, 0, 1))
    y_ctx_t = _ctxconv(u_ctx.reshape(3, HY_WIDTH, nb * ctx_len), cw3, filt_c[:, 0], filt_c[:, 1], ss_c[:, 0],
                       ss_c[:, 1], skip.reshape(2, HY_WIDTH, 1), _ctx_dft_tables(ctx_len))
    y_hy_ctx = jnp.transpose(y_ctx_t.reshape(HY_WIDTH, nb, ctx_len), (1, 2, 0))

    gate_blk = 3

    def specs0(h):
        tile = lambda i: 2 * i + h
        trow = lambda i: (tile(i), 0)
        return [pl.BlockSpec((TM, ret_w), trow), pl.BlockSpec((TM, ret_w), trow),
                pl.BlockSpec((TM, ret_w), lambda i: (tile(i), gate_blk)),
                *_lat_or_ctx_specs(HY_WIDTH, tpb, tile), *_lat_or_ctx_specs(d, tpb, tile),
                pl.BlockSpec((1, 6, d), lambda i: (cond_of(tile(i)), 0, 0))]

    x_mid, h2, mi, mf, cnt = _outproj_call(
        True, tpb, r // TM, d, specs0,
        (o_f.reshape(r, ret_w), o_b.reshape(r, ret_w), z0, y_hy_lat.reshape(nb * length, HY_WIDTH),
         y_hy_ctx.reshape(nb * ctx_len, HY_WIDTH), x_lat, x_ctx, mods[0]),
        ab_w_out[0].astype(BF16), ln_g[0, 0].reshape(1, d), ln_b[0, 0].reshape(1, d), router_w, router_bias)

    x1 = _moe(h2, mi, mf, cnt, x_mid, cond_of, mods[0], ln_g[0, 1].reshape(1, d), ln_b[0, 1].reshape(1, d),
              exp_w_gate[0].astype(BF16), exp_w_up[0].astype(BF16), exp_w_down[0].astype(BF16))

    kw = GLA_HEADS * GLA_DK
    vw = GLA_HEADS * GLA_DV
    n_in = gla_w_in.shape[2]
    n_pad = 2 * kw + 2 * vw + LANES
    w_in1 = jnp.pad(gla_w_in[0], ((0, 0), (0, n_pad - n_in))).astype(BF16)
    z1 = _inproj(x1, mods[1], w_in1, tpb)
    z1_3 = z1.reshape(nb, s, n_pad)
    gw_f = jnp.zeros((LANES, kw), F32).at[:GLA_RANK].set(gla_gate_w_f[0]).astype(BF16)
    gw_b = jnp.zeros((LANES, kw), F32).at[GLA_RANK:2 * GLA_RANK].set(gla_gate_w_b[0]).astype(BF16)
    g_f, g_b = _gla(z1_3, gw_f, gla_gate_b_f[0].reshape(1, kw), gw_b, gla_gate_b_b[0].reshape(1, kw))

    n_lat = nb * lat_tiles

    def lat_tile(i):
        return (i // lat_tiles) * tpb + i % lat_tiles

    def lat_cond(i):
        return 2 * (i // lat_tiles)

    def specs1(h):
        lrow = lambda i: (lat_tile(2 * i + h), 0)
        return [pl.BlockSpec((TM, vw), lrow), pl.BlockSpec((TM, vw), lrow),
                pl.BlockSpec((TM, vw), lambda i: (lat_tile(2 * i + h), 2)),
                pl.BlockSpec((TM, d), lrow),
                pl.BlockSpec((1, 6, d), lambda i: (lat_cond(2 * i + h), 0, 0))]

    x_mid1, h2_1, mi1, mf1, cnt1 = _outproj_call(
        False, tpb, n_lat, d, specs1, (g_f.reshape(r, vw), g_b.reshape(r, vw), z1, x1, mods[1]),
        gla_w_out[0].astype(BF16), ln_g[1, 0].reshape(1, d), ln_b[1, 0].reshape(1, d), router_w, router_bias)

    out = _moe(h2_1, mi1, mf1, cnt1, x_mid1, lat_cond, mods[1], ln_g[1, 1].reshape(1, d), ln_b[1, 1].reshape(1, d),
               exp_w_gate[1].astype(BF16), exp_w_up[1].astype(BF16), exp_w_down[1].astype(BF16))
    return out.reshape(nb, length, d)
```

```python
import functools
import math

import numpy as np
import jax
import jax.numpy as jnp
from jax import lax
from jax.experimental import pallas as pl
from jax.experimental.pallas import tpu as pltpu

F32 = jnp.float32
BF16 = jnp.bfloat16

GRID_W = 64
RET_HEADS = 4
RET_HD = 128
RET_CHUNK = 128
ROPE_BASE = 10000.0
HY_WIDTH = 512
HY_EMB = 33
HY_FFN = 64
HY_SHORT_DECAY_PCT = 0.3
HY_LONG_DECAY_PCT = 1.5
HY_TARGET = 1e-2
GLA_HEADS = 4
GLA_DK = 128
GLA_DV = 256
GLA_RANK = 16
GLA_TAU = 16.0
GLA_CHUNK = 64
N_EXPERTS = 16
N_GROUPS = 4
EXPERTS_PER_GROUP = 4
LN_EPS = 1e-5
DEPTH = 2
DN_ALPHA = (2.0 * DEPTH) ** 0.25

PAIRS = ((0, 1), (0, 2), (0, 3), (1, 2), (1, 3), (2, 3))
N_CLASSES = N_GROUPS * len(PAIRS)
CLASS_ROWS = 32

LANES = 128
SUBLANES = 8
TM = 256
TE = 256
FFT_N2 = 128
FFT_CB = 16
VMEM_LIMIT = 48 * 1024 * 1024

HIGHEST = lax.Precision.HIGHEST


def _cparams(*sem):
    return pltpu.CompilerParams(dimension_semantics=sem, vmem_limit_bytes=VMEM_LIMIT)


def _silu(v):
    return v * (1.0 / (1.0 + jnp.exp(-v)))


def _dot(a, b):
    return jnp.dot(a, b, preferred_element_type=F32)


def _dot_nt(a, b):
    return lax.dot_general(a, b, (((1,), (1,)), ((), ())), preferred_element_type=F32)


def _dot_tn(a, b):
    return lax.dot_general(a, b, (((0,), (0,)), ((), ())), preferred_element_type=F32)


def _split2(a):
    hi = a.astype(BF16)
    return hi, (a - hi.astype(F32)).astype(BF16)


def _dot3(a, b, dot=_dot):
    ah, al = _split2(a)
    bh, bl = _split2(b)
    return dot(ah, bh) + dot(ah, bl) + dot(al, bh)


def _ada_kernel(c_ref, w_ref, b_ref, o_ref):
    a = _silu(c_ref[...])
    o_ref[0] = jnp.dot(a, w_ref[0], preferred_element_type=F32, precision=HIGHEST) + b_ref[0]


def _ada(cond, mod_w, mod_b):
    depth, d, n = mod_w.shape
    nt = n // 4
    return pl.pallas_call(
        _ada_kernel,
        out_shape=jax.ShapeDtypeStruct((depth, 8, n), F32),
        grid=(depth, n // nt),
        in_specs=[pl.BlockSpec((8, d), lambda l, j: (0, 0)),
                  pl.BlockSpec((1, d, nt), lambda l, j: (l, 0, j)),
                  pl.BlockSpec((1, 1, nt), lambda l, j: (l, 0, j))],
        out_specs=pl.BlockSpec((1, 8, nt), lambda l, j: (l, 0, j)),
        compiler_params=_cparams("arbitrary", "arbitrary"),
        name="ada_mod",
    )(cond, mod_w, mod_b.reshape(depth, 1, n))


def _inproj_kernel(x_ref, m_ref, w_ref, o_ref, *, chunks):
    sh = m_ref[0, 0:1, :]
    sc = m_ref[0, 1:2, :]
    h = (x_ref[...] * (1.0 + sc) + sh).astype(BF16)
    for lo, hi in chunks:
        o_ref[:, lo:hi] = _dot(h, w_ref[:, lo:hi]).astype(BF16)


def _lat_or_ctx_specs(width, tpb, tile_of=lambda i: i):
    lat_tiles = tpb - 1
    lat = lambda i: ((tile_of(i) // tpb) * lat_tiles + jnp.minimum(tile_of(i) % tpb, lat_tiles - 1), 0)
    ctx = lambda i: (tile_of(i) // tpb, 0)
    return pl.BlockSpec((TM, width), lat), pl.BlockSpec((TM, width), ctx)


def _col_chunks(n, width=512):
    out, lo = [], 0
    while lo < n:
        hi = min(lo + width, n)
        out.append((lo, hi))
        lo = hi
    return tuple(out)


def _inproj(xu, mods, w, tiles_per_batch):
    r, d = xu.shape
    n = w.shape[1]
    tpb = tiles_per_batch

    def cond_of(i):
        return 2 * (i // tpb) + jnp.where(i % tpb == tpb - 1, 1, 0)

    return pl.pallas_call(
        functools.partial(_inproj_kernel, chunks=_col_chunks(n)),
        out_shape=jax.ShapeDtypeStruct((r, n), BF16),
        grid=(r // TM,),
        in_specs=[pl.BlockSpec((TM, d), lambda i: (i, 0)),
                  pl.BlockSpec((1, 6, d), lambda i: (cond_of(i), 0, 0)),
                  pl.BlockSpec((d, n), lambda i: (0, 0))],
        out_specs=pl.BlockSpec((TM, n), lambda i: (i, 0)),
        compiler_params=_cparams("arbitrary"),
        name="inproj1",
    )(xu, mods, w)


def _inproj0_kernel(x_ref, c_ref, m_ref, w_ref, z_ref, x1_ref, x2_ref, v_ref, hc_ref, *, tpb, ret_w):
    is_ctx = pl.program_id(0) % tpb == tpb - 1
    x = jnp.where(is_ctx, c_ref[...], x_ref[...])
    h = (x * (1.0 + m_ref[0, 1:2, :]) + m_ref[0, 0:1, :]).astype(BF16)
    for lo, hi in _col_chunks(ret_w):
        z_ref[:, lo:hi] = _dot(h, w_ref[:, lo:hi]).astype(BF16)
    hy = [_dot(h, w_ref[:, ret_w + j * HY_WIDTH:ret_w + (j + 1) * HY_WIDTH]).astype(BF16) for j in range(3)]

    @pl.when(jnp.logical_not(is_ctx))
    def _():
        for ref, val in zip((x1_ref, x2_ref, v_ref), hy):
            ref[...] = val

    @pl.when(is_ctx)
    def _():
        for j, val in enumerate(hy):
            hc_ref[:, j * HY_WIDTH:(j + 1) * HY_WIDTH] = val


def _inproj0(x_lat, x_ctx, mods, w, tpb, r, ret_w):
    d = w.shape[0]
    lat_tiles = tpb - 1

    def cond_of(i):
        return 2 * (i // tpb) + jnp.where(i % tpb == tpb - 1, 1, 0)

    lat_spec, ctx_spec = _lat_or_ctx_specs(d, tpb)
    lat_out = pl.BlockSpec((TM, HY_WIDTH), lambda i: ((i // tpb) * lat_tiles + jnp.minimum(i % tpb, lat_tiles - 1), 0))
    hy_lat = jax.ShapeDtypeStruct(((r // TM // tpb) * lat_tiles * TM, HY_WIDTH), BF16)
    return pl.pallas_call(
        functools.partial(_inproj0_kernel, tpb=tpb, ret_w=ret_w),
        out_shape=(jax.ShapeDtypeStruct((r, ret_w), BF16), hy_lat, hy_lat, hy_lat,
                   jax.ShapeDtypeStruct(((r // TM // tpb) * TM, 3 * HY_WIDTH), BF16)),
        grid=(r // TM,),
        in_specs=[lat_spec, ctx_spec, pl.BlockSpec((1, 6, d), lambda i: (cond_of(i), 0, 0)),
                  pl.BlockSpec(w.shape, lambda i: (0, 0))],
        out_specs=(pl.BlockSpec((TM, ret_w), lambda i: (i, 0)), lat_out, lat_out, lat_out,
                   pl.BlockSpec((TM, 3 * HY_WIDTH), lambda i: (i // tpb, 0))),
        compiler_params=_cparams("arbitrary"),
        name="inproj0",
    )(x_lat, x_ctx, mods, w)


def _ret_kernel(lgf_ref, lgb_ref, *refs, nb):
    f_in, b_in, (of_ref, ob_ref, sf_ref, sb_ref) = refs[:5], refs[5:10], refs[10:]

    @pl.when(pl.program_id(0) == 0)
    def _():
        sf_ref[...] = jnp.zeros_like(sf_ref)
        sb_ref[...] = jnp.zeros_like(sb_ref)

    _ret_dir(lgf_ref, *f_in, of_ref, sf_ref, False, nb)
    _ret_dir(lgb_ref, *b_in, ob_ref, sb_ref, True, nb)


def _ret_dir(lg_ref, q_ref, k_ref, v_ref, cos_ref, sin_ref, o_ref, s_ref, reverse, nb):
    C = RET_CHUNK
    ii = lax.broadcasted_iota(jnp.int32, (C, C), 0).astype(F32)
    jj = lax.broadcasted_iota(jnp.int32, (C, C), 1).astype(F32)
    ci = lax.broadcasted_iota(jnp.int32, (C, 1), 0).astype(F32)
    diff = (jj - ii) if reverse else (ii - jj)
    order = (1, 0) if reverse else (0, 1)
    for h in range(RET_HEADS):
        lg = lg_ref[h]
        intra = jnp.where(diff >= 0, jnp.exp(jnp.maximum(diff, 0.0) * lg), 0.0)
        if reverse:
            q_dec = jnp.exp((C - ci) * lg)
            k_dec = jnp.exp(ci * lg)
        else:
            q_dec = jnp.exp((ci + 1.0) * lg)
            k_dec = jnp.exp((C - 1.0 - ci) * lg)
        c_dec = jnp.exp(jnp.zeros((1, RET_HD), F32) + C * lg)
        cols = slice(h * RET_HD, (h + 1) * RET_HD)
        for b in range(nb):
            for c in order:
                rows = slice(c * C, (c + 1) * C)
                cosf = cos_ref[rows, :]
                sinf = sin_ref[rows, :]
                q = q_ref[b, rows, cols].astype(F32)
                k = k_ref[b, rows, cols].astype(F32)
                v = v_ref[b, rows, cols]
                q = q * cosf + pltpu.roll(q, RET_HD // 2, 1) * sinf
                k = (k * cosf + pltpu.roll(k, RET_HD // 2, 1) * sinf) * (RET_HD ** -0.5)
                st = s_ref[b, h]
                scores = _dot_nt(q.astype(BF16), k.astype(BF16)) * intra
                o = _dot(scores.astype(BF16), v) + _dot_nt((q * q_dec).astype(BF16), st.astype(BF16))
                s_ref[b, h] = c_dec * st + _dot_tn(v, (k * k_dec).astype(BF16))
                o_ref[b, rows, cols] = o.astype(BF16)


def _scan_tile_orders(nt):
    fwd = lambda j: (j + nt - 1) % nt
    bwd = lambda j: jnp.where(j == 0, nt - 1, nt - 1 - j)
    return fwd, bwd


def _retention(z3, lg_f, lg_b, cosf, sinf):
    nb, s, _ = z3.shape
    nt = s // TM
    w = RET_HEADS * RET_HD
    smem = pl.BlockSpec(memory_space=pltpu.SMEM)

    def dir_specs(jmap):
        return [pl.BlockSpec((nb, TM, w), lambda j: (0, jmap(j), 0)),
                pl.BlockSpec((nb, TM, w), lambda j: (0, jmap(j), 1)),
                pl.BlockSpec((nb, TM, w), lambda j: (0, jmap(j), 2)),
                pl.BlockSpec((TM, RET_HD), lambda j: (jmap(j), 0)),
                pl.BlockSpec((TM, RET_HD), lambda j: (jmap(j), 0))]

    fwd, bwd = _scan_tile_orders(nt)
    state = pltpu.VMEM((nb, RET_HEADS, RET_HD, RET_HD), F32)
    out = jax.ShapeDtypeStruct((nb, s, w), BF16)
    grid_spec = pltpu.PrefetchScalarGridSpec(
        num_scalar_prefetch=0,
        grid=(nt,),
        in_specs=[smem, smem] + dir_specs(fwd) + dir_specs(bwd),
        out_specs=(pl.BlockSpec((nb, TM, w), lambda j: (0, fwd(j), 0)),
                   pl.BlockSpec((nb, TM, w), lambda j: (0, bwd(j), 0))),
        scratch_shapes=[state, state],
    )
    return pl.pallas_call(
        functools.partial(_ret_kernel, nb=nb),
        out_shape=(out, out),
        grid_spec=grid_spec,
        compiler_params=_cparams("arbitrary"),
        name="retention",
    )(lg_f, lg_b, z3, z3, z3, cosf, sinf, z3, z3, z3, cosf, sinf)


def _rope_tables(length, ctx_len):
    rows = length // GRID_W
    quarter = RET_HD // 4
    inv = ROPE_BASE ** (-jnp.arange(quarter, dtype=F32) / quarter)
    r = jnp.repeat(jnp.arange(rows, dtype=F32), GRID_W)
    col = jnp.tile(jnp.arange(GRID_W, dtype=F32), rows)
    ang = jnp.concatenate([r[:, None] * inv, col[:, None] * inv], axis=-1)
    cos, sin = jnp.cos(ang), jnp.sin(ang)
    cosf = jnp.concatenate([cos, cos], axis=-1)
    sinf = jnp.concatenate([-sin, sin], axis=-1)
    cosf = jnp.concatenate([cosf, jnp.ones((ctx_len, RET_HD), F32)], axis=0)
    sinf = jnp.concatenate([sinf, jnp.zeros((ctx_len, RET_HD), F32)], axis=0)
    return cosf, sinf


def _shortconv_freq(x, cw, na):
    row = lax.broadcasted_iota(jnp.int32, (1, FFT_N2, 1), 1)
    lane = lax.broadcasted_iota(jnp.int32, (1, 1, 2 * na), 2)
    wrap_prev = pltpu.roll(x[:, FFT_N2 - 1:FFT_N2, :], 1, 2)
    prev = jnp.where(row == 0, jnp.where((lane == 0) | (lane == na), 0.0, wrap_prev), pltpu.roll(x, 1, 1))
    wrap_next = pltpu.roll(x[:, 0:1, :], 2 * na - 1, 2)
    nxt = jnp.where(row == FFT_N2 - 1, jnp.where((lane == na - 1) | (lane == 2 * na - 1), 0.0, wrap_next),
                    pltpu.roll(x, FFT_N2 - 1, 1))
    return cw[..., 0:1] * prev + cw[..., 1:2] * x + cw[..., 2:3] * nxt + cw[..., 3:4]


def _shortconv_lanes(x, cw, n):
    lane = lax.broadcasted_iota(jnp.int32, (1, 2 * n), 1)
    prev = jnp.where((lane == 0) | (lane == n), 0.0, pltpu.roll(x, 1, 1))
    nxt = jnp.where((lane == n - 1) | (lane == 2 * n - 1), 0.0, pltpu.roll(x, 2 * n - 1, 1))
    return cw[:, 0:1] * prev + cw[:, 1:2] * x + cw[:, 2:3] * nxt + cw[:, 3:4]


FEAT_ROWS = 40


def _filter_kernel(w1_ref, b1_ref, w2_ref, b2_ref, w3_ref, b3_ref, w4_ref, fr_ref, dl_ref,
                   f_ref, ss_ref, *, length, pb):
    i = pl.program_id(0)
    pos = (lax.broadcasted_iota(jnp.int32, (1, pb), 1) + i * pb).astype(F32)
    t = pos * (1.0 / (length - 1))
    bands = (HY_EMB - 1) // 2
    w = (2.0 * math.pi) * pos / length
    sub = lax.broadcasted_iota(jnp.int32, (FEAT_ROWS, 1), 0)
    band = jnp.where(sub <= bands, sub - 1, sub - 1 - bands).astype(F32)
    f = 1e-4 + band * ((bands - 1 - 1e-4) / (bands - 1))
    fw = f * w
    feats = jnp.where(sub == 0, t, jnp.where(sub <= bands, jnp.cos(fw),
                                             jnp.where(sub <= 2 * bands, -jnp.sin(fw), 0.0)))
    feats = jnp.concatenate([feats, jnp.zeros((LANES - FEAT_ROWS, pb), F32)], axis=0)
    hdot = _dot3
    a = jnp.sin(fr_ref[:, 0:1] * (hdot(w1_ref[...], feats) + b1_ref[...]))
    a = jnp.sin(fr_ref[:, 1:2] * (hdot(w2_ref[...], a) + b2_ref[...]))
    a = jnp.sin(fr_ref[:, 2:3] * (hdot(w3_ref[...], a) + b3_ref[...]))

    @pl.when(i == 0)
    def _():
        ss_ref[...] = jnp.zeros_like(ss_ref)

    nout = w4_ref.shape[0]
    window = jnp.exp(-dl_ref[...] * t)
    for cb in range(nout // HY_WIDTH):
        rows = slice(cb * HY_WIDTH, (cb + 1) * HY_WIDTH)
        filt = hdot(w4_ref[rows, :], a) * window
        for j in range(pb // LANES):
            f_ref[rows, j, :] = filt[:, j * LANES:(j + 1) * LANES]
        ss_ref[rows, :] += jnp.sum(filt * filt, axis=1, keepdims=True)


def _hyena_filters_raw(length, w1, b1, w2, b2, w3, b3, w4, freq):
    nout = w4.shape[1]
    pb = min(length, 1024)
    max_decay = math.log(HY_TARGET) / HY_SHORT_DECAY_PCT
    min_decay = math.log(HY_TARGET) / HY_LONG_DECAY_PCT
    deltas = jnp.abs(jnp.linspace(min_decay, max_decay, HY_WIDTH, dtype=F32))
    dl = deltas.reshape(HY_WIDTH, 1)
    w1p = jnp.pad(w1.T, ((0, 0), (0, LANES - w1.shape[0])))
    col = lambda b: b.reshape(-1, 1)
    full = lambda a: pl.BlockSpec(a.shape, lambda i: tuple(0 for _ in a.shape))
    args = (w1p, col(b1), w2.T, col(b2), w3.T, col(b3), w4.T, freq.T, dl)
    return pl.pallas_call(
        functools.partial(_filter_kernel, length=length, pb=pb),
        out_shape=(jax.ShapeDtypeStruct((nout, length // LANES, LANES), F32),
                   jax.ShapeDtypeStruct((nout, 1), F32)),
        grid=(length // pb,),
        in_specs=[full(a) for a in args],
        out_specs=(pl.BlockSpec((nout, pb // LANES, LANES), lambda i: (0, i, 0)),
                   pl.BlockSpec((nout, 1), lambda i: (0, 0))),
        compiler_params=_cparams("arbitrary"),
        name="hyena_filter_mlp",
    )(*args)


def _dft_tables(na):
    k1n = 2 * na
    n = k1n * FFT_N2
    n1 = np.arange(na)[:, None]
    k1 = np.arange(k1n)[None, :]
    ang = 2.0 * np.pi * ((n1 * k1) % k1n) / k1n
    c, s = np.cos(ang), np.sin(ang)
    ma = np.block([[c, -s], [s, c]])
    n1f = np.arange(FFT_N2)[:, None]
    angf = 2.0 * np.pi * ((n1f * k1) % k1n) / k1n
    live = (n1f < na).astype(np.float64)
    ma_f = np.concatenate([np.cos(angf), -np.sin(angf)], axis=1) * live
    ma_b = np.concatenate([np.cos(angf), np.sin(angf)], axis=1) * live
    n2 = np.arange(FFT_N2)[:, None]
    angt = 2.0 * np.pi * ((n2 * k1) % n) / n
    twr, twi = np.cos(angt), -np.sin(angt)
    k2 = np.arange(FFT_N2)[None, :]
    angb = 2.0 * np.pi * ((n2 * k2) % FFT_N2) / FFT_N2
    cb, sb = np.cos(angb), np.sin(angb)
    mb = np.block([[cb, -sb], [sb, cb]])
    mc = np.block([[cb, sb], [-sb, cb]])
    angd = 2.0 * np.pi * ((np.arange(k1n)[:, None] * np.arange(na)[None, :]) % k1n) / k1n
    cd, sd = np.cos(angd) / n, np.sin(angd) / n
    md = np.block([[cd, sd], [-sd, cd]])
    as_bf = lambda a: jnp.asarray(a, dtype=F32).astype(BF16)
    as_f = lambda a: jnp.asarray(a, dtype=F32)
    mb2 = np.concatenate([mb, mc], axis=0)
    return dict(ma=as_bf(ma), ma_f=as_bf(ma_f), ma_b=as_bf(ma_b), twr=as_f(twr), twi=as_f(twi),
                twr_t=as_f(twr.T), twi_t=as_f(twi.T), mb=as_bf(mb), mb2=as_bf(mb2), mc=as_bf(mc), md=as_bf(md))


def _fwd_stages(x2, ma, twr, twi, mb, cb, k1n):
    a = _dot(x2, ma).reshape(cb, FFT_N2, 2 * k1n)
    ar, ai = a[..., :k1n], a[..., k1n:]
    ar2 = ar * twr - ai * twi
    ai2 = ar * twi + ai * twr
    xt = jnp.concatenate([jnp.swapaxes(ar2, 1, 2), jnp.swapaxes(ai2, 1, 2)], axis=-1)
    return _dot(xt.astype(BF16).reshape(cb * k1n, 2 * FFT_N2), mb).reshape(cb, k1n, 2 * FFT_N2)


def _spectrum_kernel(hf_ref, hb_ref, ssf_ref, ssb_ref, maf_ref, mab_ref, twr_ref, twi_ref, mb2_ref, h_ref, *, na):
    k1n = 2 * na
    cb = h_ref.shape[0]

    def stage_a(ref, ss_ref, m_ref):
        x = ref[0, 0] * lax.rsqrt(ss_ref[0, 0])
        x = jnp.concatenate([x, jnp.zeros((cb, FFT_N2 - na, FFT_N2), F32)], axis=1)
        xt = jnp.swapaxes(x, 1, 2).astype(BF16).reshape(cb * FFT_N2, FFT_N2)
        a = _dot(xt, m_ref[...]).reshape(cb, FFT_N2, 2 * k1n)
        return a[..., :k1n], a[..., k1n:]

    fr, fi = stage_a(hf_ref, ssf_ref, maf_ref)
    br, bi = stage_a(hb_ref, ssb_ref, mab_ref)
    twr, twi = twr_ref[...], twi_ref[...]
    parts = (fr * twr - fi * twi, fr * twi + fi * twr, br * twr + bi * twi, bi * twr - br * twi)
    xt = jnp.concatenate([jnp.swapaxes(p, 1, 2) for p in parts], axis=-1)
    h = _dot(xt.astype(BF16).reshape(cb * k1n, 4 * FFT_N2), mb2_ref[...])
    h_ref[...] = h.reshape(cb, k1n, 2 * FFT_N2).astype(BF16)


def _spectrum(filt5, ss5, order, tb, na):
    c = filt5.shape[2]
    k1n = 2 * na
    cb = FFT_CB
    full = lambda a: pl.BlockSpec(a.shape, lambda i: tuple(0 for _ in a.shape))
    fblk = lambda d: pl.BlockSpec((1, 1, cb, na, FFT_N2), lambda i: (order, d, i, 0, 0))
    sblk = lambda d: pl.BlockSpec((1, 1, cb, 1, 1), lambda i: (order, d, i, 0, 0))
    return pl.pallas_call(
        functools.partial(_spectrum_kernel, na=na),
        out_shape=jax.ShapeDtypeStruct((c, k1n, 2 * FFT_N2), BF16),
        grid=(c // cb,),
        in_specs=[fblk(0), fblk(1), sblk(0), sblk(1),
                  full(tb["ma_f"]), full(tb["ma_b"]), full(tb["twr"]), full(tb["twi"]), full(tb["mb2"])],
        out_specs=pl.BlockSpec((cb, k1n, 2 * FFT_N2), lambda i: (i, 0, 0)),
        compiler_params=_cparams("arbitrary"),
        name="hyena_filter_spectrum",
    )(filt5, filt5, ss5, ss5, tb["ma_f"], tb["ma_b"], tb["twr"], tb["twi"], tb["mb2"])


def _fftconv_kernel(u_ref, g_ref, h_ref, sk_ref, cwu_ref, cwg_ref, ma_ref, twr_ref, twi_ref, mb_ref, mc_ref,
                    twrt_ref, twit_ref, md_ref, o_ref, *, na, conv_u):
    k1n = 2 * na
    cb = u_ref.shape[0]
    uf = u_ref[...].astype(F32)
    if conv_u:
        uf = _shortconv_freq(uf, cwu_ref[...], na)
    gate = _shortconv_freq(g_ref[...].astype(F32), cwg_ref[...], na)
    x = _fwd_stages(uf.astype(BF16).reshape(cb * FFT_N2, k1n), ma_ref[...], twr_ref[...], twi_ref[...],
                    mb_ref[...], cb, k1n)
    xr, xi = x[..., :FFT_N2], x[..., FFT_N2:]
    h = h_ref[...].astype(F32)
    hr, hi = h[..., :FFT_N2], h[..., FFT_N2:]
    y = jnp.concatenate([xr * hr - xi * hi, xr * hi + xi * hr], axis=-1)
    c = _dot(y.astype(BF16).reshape(cb * k1n, 2 * FFT_N2), mc_ref[...]).reshape(cb, k1n, 2 * FFT_N2)
    cr, ci = c[..., :FFT_N2], c[..., FFT_N2:]
    twrt, twit = twrt_ref[...], twit_ref[...]
    cr2 = cr * twrt + ci * twit
    ci2 = ci * twrt - cr * twit
    ct = jnp.concatenate([jnp.swapaxes(cr2, 1, 2), jnp.swapaxes(ci2, 1, 2)], axis=-1)
    d = _dot(ct.astype(BF16).reshape(cb * FFT_N2, 2 * k1n), md_ref[...]).reshape(cb, FFT_N2, k1n)
    o_ref[...] = (gate * (d + sk_ref[...] * uf)).astype(BF16)


def _fftconv(u, gate, h, skip, cw_u, cw_g, tb, na, conv_u):
    c = u.shape[0]
    k1n = 2 * na
    cb = FFT_CB
    full = lambda a: pl.BlockSpec(a.shape, lambda i: tuple(0 for _ in a.shape))
    blk = pl.BlockSpec((cb, FFT_N2, k1n), lambda i: (i, 0, 0))
    cwblk = pl.BlockSpec((cb, 1, 4), lambda i: (i, 0, 0))
    return pl.pallas_call(
        functools.partial(_fftconv_kernel, na=na, conv_u=conv_u),
        out_shape=jax.ShapeDtypeStruct(u.shape, BF16),
        grid=(c // cb,),
        in_specs=[blk, blk,
                  pl.BlockSpec((cb, k1n, 2 * FFT_N2), lambda i: (i, 0, 0)),
                  pl.BlockSpec((cb, 1, 1), lambda i: (i, 0, 0)), cwblk, cwblk,
                  full(tb["ma"]), full(tb["twr"]), full(tb["twi"]), full(tb["mb"]), full(tb["mc"]),
                  full(tb["twr_t"]), full(tb["twi_t"]), full(tb["md"])],
        out_specs=blk,
        compiler_params=_cparams("arbitrary"),
        name="hyena_fftconv",
    )(u, gate, h, skip, cw_u, cw_g, tb["ma"], tb["twr"], tb["twi"], tb["mb"], tb["mc"], tb["twr_t"], tb["twi_t"],
      tb["md"])


def _to_freq_layout(a, na):
    nb, _, c = a.shape
    a = a.reshape(nb, na, FFT_N2, c)
    return jnp.transpose(a, (3, 2, 0, 1)).reshape(c, FFT_N2, nb * na)


def _from_freq_layout(a):
    c, _, lanes = a.shape
    na = lanes // 2
    a = a.reshape(c, FFT_N2, 2, na)
    return jnp.transpose(a, (2, 3, 1, 0)).reshape(2, na * FFT_N2, c)


def _ctx_dft_tables(n):
    big = 2 * n
    j = np.arange(n)[:, None]
    k = np.arange(big)[None, :]
    ang = 2.0 * np.pi * ((j * k) % big) / big
    c, s = np.cos(ang), np.sin(ang)
    fh = np.block([[c, -s], [c, s]])
    ff = np.block([[c, -s], [s, c]])
    fi = np.block([[c.T, s.T], [-s.T, c.T]]) / big
    as_bf = lambda a: jnp.asarray(a, dtype=F32).astype(BF16)
    return as_bf(fh), as_bf(ff), as_bf(fi)


def _ctxconv_kernel(u_ref, cw_ref, hf_ref, hb_ref, ssf_ref, ssb_ref, sk_ref, fh_ref, ff_ref, fi_ref, o_ref):
    n_tok = u_ref.shape[2] // 2
    short = lambda j: _shortconv_lanes(u_ref[j].astype(F32), cw_ref[j], n_tok)
    gates = (short(0), short(1))
    z = short(2)
    half = fh_ref.shape[1] // 2
    for n in range(2):
        taps = jnp.concatenate([hf_ref[n] * lax.rsqrt(ssf_ref[n]), hb_ref[n] * lax.rsqrt(ssb_ref[n])], axis=-1)
        h = _dot(taps.astype(BF16), fh_ref[...])
        x = _dot(z.astype(BF16), ff_ref[...])
        hr, hi = h[:, :half], h[:, half:]
        xr, xi = x[:, :half], x[:, half:]
        y = jnp.concatenate([xr * hr - xi * hi, xr * hi + xi * hr], axis=-1)
        conv = _dot(y.astype(BF16), fi_ref[...])
        z = gates[n] * (conv + sk_ref[n] * z)
    o_ref[...] = z.astype(BF16)


def _ctxconv(u3, cw3, hf, hb, ssf, ssb, skip, tables):
    _, c, lanes = u3.shape
    args = (u3, cw3, hf, hb, ssf, ssb, skip) + tuple(tables)
    full = lambda a: pl.BlockSpec(a.shape, lambda i: tuple(0 for _ in a.shape))
    return pl.pallas_call(
        _ctxconv_kernel,
        out_shape=jax.ShapeDtypeStruct((c, lanes), BF16),
        grid=(1,),
        in_specs=[full(a) for a in args],
        out_specs=pl.BlockSpec((c, lanes), lambda i: (0, 0)),
        compiler_params=_cparams("arbitrary"),
        name="hyena_ctxconv",
    )(*args)


def _load_row_tiles(ref, n, stride=SUBLANES, first=0):
    return jnp.concatenate([ref[pl.ds(first + j, n, stride=stride), :] for j in range(SUBLANES)], axis=-1)


def _store_row_tiles(ref, val, n, stride=SUBLANES, first=0):
    for j in range(SUBLANES):
        ref[pl.ds(first + j, n, stride=stride), :] = val[:, j * LANES:(j + 1) * LANES]


def _row_tile(ref, idx, tiles=1):
    size = tiles * SUBLANES
    return ref.at[pl.ds(pl.multiple_of(idx * size, size), size)]


def _top2_rows(vals):
    n = len(vals)
    best_v, best_i = vals[0], jnp.zeros_like(vals[0], dtype=jnp.int32)
    for e in range(1, n):
        take = vals[e] > best_v
        best_v = jnp.where(take, vals[e], best_v)
        best_i = jnp.where(take, e, best_i)
    sec_v = jnp.full_like(vals[0], -jnp.inf)
    sec_i = jnp.zeros_like(best_i)
    for e in range(n):
        take = (best_i != e) & (vals[e] > sec_v)
        sec_v = jnp.where(take, vals[e], sec_v)
        sec_i = jnp.where(take, e, sec_i)
    return best_v, best_i, sec_v, sec_i


def _route(h2, rw_ref, rb_ref, carry_ref, mi_ref, mf_ref, cnt_ref, cols):
    logits = _dot3(rw_ref[...], h2, dot=_dot_nt)
    s = 1.0 / (1.0 + jnp.exp(-logits))
    sel = s + rb_ref[...]
    srow = [s[e:e + 1, :] for e in range(N_EXPERTS)]
    selrow = [sel[e:e + 1, :] for e in range(N_EXPERTS)]
    gscore = []
    for g in range(N_GROUPS):
        grp = selrow[g * EXPERTS_PER_GROUP:(g + 1) * EXPERTS_PER_GROUP]
        bv, _, sv, _ = _top2_rows(grp)
        gscore.append(bv + sv)
    best_g = jnp.zeros_like(gscore[0], dtype=jnp.int32)
    best_s = gscore[0]
    for g in range(1, N_GROUPS):
        take = gscore[g] > best_s
        best_s = jnp.where(take, gscore[g], best_s)
        best_g = jnp.where(take, g, best_g)
    cand_sel, cand_s = [], []
    for j in range(EXPERTS_PER_GROUP):
        cs, ca = selrow[j], srow[j]
        for g in range(1, N_GROUPS):
            cs = jnp.where(best_g == g, selrow[g * EXPERTS_PER_GROUP + j], cs)
            ca = jnp.where(best_g == g, srow[g * EXPERTS_PER_GROUP + j], ca)
        cand_sel.append(cs)
        cand_s.append(ca)
    _, i1, _, i2 = _top2_rows(cand_sel)
    a1, a2 = cand_s[0], cand_s[0]
    for j in range(1, EXPERTS_PER_GROUP):
        a1 = jnp.where(i1 == j, cand_s[j], a1)
        a2 = jnp.where(i2 == j, cand_s[j], a2)
    den = a1 + a2
    w1, w2 = a1 / den, a2 / den
    swap = i2 < i1
    lo = jnp.where(swap, i2, i1)
    hi = jnp.where(swap, i1, i2)
    w_lo = jnp.where(swap, w2, w1)
    w_hi = jnp.where(swap, w1, w2)
    pair = jnp.where(lo == 0, 0, jnp.where(lo == 1, 3, 5)) + (hi - lo - 1)
    cls = best_g * len(PAIRS) + pair
    cidx = lax.broadcasted_iota(jnp.int32, (CLASS_ROWS, TM), 0)
    hit = cidx == cls
    onehot = jnp.where(hit, 1.0, 0.0)
    tri = jnp.where(lax.broadcasted_iota(jnp.int32, (TM, TM), 0) < lax.broadcasted_iota(jnp.int32, (TM, TM), 1),
                    1.0, 0.0).astype(BF16)
    rank = _dot(onehot.astype(BF16), tri) + carry_ref[:, 0:1]
    rk = jnp.sum(jnp.where(hit, rank, 0.0), axis=0, keepdims=True)
    carry_ref[...] = carry_ref[...] + jnp.sum(onehot, axis=1, keepdims=True)
    mi_ref[:, cols] = jnp.concatenate([cls, rk.astype(jnp.int32), jnp.zeros((6, TM), jnp.int32)], axis=0)
    mf_ref[:, cols] = jnp.concatenate([w_lo, w_hi, jnp.zeros((6, TM), F32)], axis=0)
    cnt_ref[...] = carry_ref[...]


def _post_norm_mod(x, y, m_ref, lng, lnb, gate_row, sh_row, sc_row):
    u = DN_ALPHA * x + m_ref[0, gate_row:gate_row + 1, :] * y
    mu = jnp.mean(u, axis=-1, keepdims=True)
    var = jnp.mean(jnp.square(u - mu), axis=-1, keepdims=True)
    xn = (u - mu) * lax.rsqrt(var + LN_EPS) * lng + lnb
    if sh_row is None:
        return xn, None
    return xn, xn * (1.0 + m_ref[0, sc_row:sc_row + 1, :]) + m_ref[0, sh_row:sh_row + 1, :]


def _outproj_kernel(*refs, layer0, tpb):
    nd = 8 if layer0 else 5
    halves = (refs[:nd], refs[nd:2 * nd])
    w_ref, lng_ref, lnb_ref, rw_ref, rb_ref, xo_ref, h2_ref, mi_ref, mf_ref, cnt_ref, carry_ref = refs[2 * nd:]
    i = pl.program_id(0)

    @pl.when(i == 0)
    def _():
        carry_ref[...] = jnp.zeros_like(carry_ref)

    for half, data in enumerate(halves):
        if layer0:
            of_ref, ob_ref, g_ref, hyl_ref, hyc_ref, xl_ref, xc_ref, m_ref = data
            is_ctx = (2 * i + half) % tpb == tpb - 1
            x = jnp.where(is_ctx, xc_ref[...], xl_ref[...])
            hy = jnp.where(is_ctx, hyc_ref[...], hyl_ref[...])
            o = of_ref[...].astype(F32) + ob_ref[...].astype(F32)
            parts = []
            for h in range(RET_HEADS):
                oh = o[:, h * RET_HD:(h + 1) * RET_HD]
                mu = jnp.mean(oh, axis=-1, keepdims=True)
                var = jnp.mean(jnp.square(oh - mu), axis=-1, keepdims=True)
                parts.append((oh - mu) * lax.rsqrt(var + LN_EPS))
            yret = (jnp.concatenate(parts, axis=-1) * _silu(g_ref[...].astype(F32))).astype(BF16)
            wr = yret.shape[1]
            y = _dot(yret, w_ref[:wr, :]) + _dot(hy, w_ref[wr:, :])
        else:
            of_ref, ob_ref, g_ref, x_ref, m_ref = data
            x = x_ref[...]
            o = of_ref[...].astype(F32) + ob_ref[...].astype(F32)
            parts = []
            for h in range(GLA_HEADS):
                oh = o[:, h * GLA_DV:(h + 1) * GLA_DV]
                parts.append(oh * lax.rsqrt(jnp.mean(jnp.square(oh), axis=-1, keepdims=True) + LN_EPS))
            a = (jnp.concatenate(parts, axis=-1) * _silu(g_ref[...].astype(F32))).astype(BF16)
            y = _dot(a, w_ref[...])
        rows = slice(half * TM, (half + 1) * TM)
        xn, h2 = _post_norm_mod(x, y, m_ref, lng_ref[...], lnb_ref[...], 2, 3, 4)
        xo_ref[rows, :] = xn
        _store_row_tiles(h2_ref, h2, TM, first=half * TM * SUBLANES)
        _route(h2, rw_ref, rb_ref, carry_ref, mi_ref, mf_ref, cnt_ref, rows)


def _outproj_call(layer0, tpb, n_tiles, d, half_specs, half_args, w, lng, lnb, router_w, router_bias):
    rows = n_tiles * TM
    full2 = lambda a: pl.BlockSpec(a.shape, lambda i: (0, 0))
    rw_t = router_w.T
    rb = router_bias.reshape(N_EXPERTS, 1)
    return pl.pallas_call(
        functools.partial(_outproj_kernel, layer0=layer0, tpb=tpb),
        out_shape=(jax.ShapeDtypeStruct((rows, d), F32),
                   jax.ShapeDtypeStruct((rows * SUBLANES, LANES), F32),
                   jax.ShapeDtypeStruct((8, rows), jnp.int32),
                   jax.ShapeDtypeStruct((8, rows), F32),
                   jax.ShapeDtypeStruct((CLASS_ROWS, LANES), F32)),
        grid=(n_tiles // 2,),
        in_specs=list(half_specs(0)) + list(half_specs(1)) + [full2(w), full2(lng), full2(lnb), full2(rw_t), full2(rb)],
        out_specs=(pl.BlockSpec((2 * TM, d), lambda i: (i, 0)),
                   pl.BlockSpec((2 * TM * SUBLANES, LANES), lambda i: (i, 0)),
                   pl.BlockSpec((8, 2 * TM), lambda i: (0, i)),
                   pl.BlockSpec((8, 2 * TM), lambda i: (0, i)),
                   pl.BlockSpec((CLASS_ROWS, LANES), lambda i: (0, 0))),
        scratch_shapes=[pltpu.VMEM((CLASS_ROWS, LANES), F32)],
        compiler_params=_cparams("arbitrary"),
        name="outproj_norm0" if layer0 else "outproj_norm1",
    )(*half_args, *half_args, w, lng, lnb, rw_t, rb)


def _dispatch_kernel(slots_ref, zoff_ref, h_ref, xs_ref, zero_ref, sem):
    i = pl.program_id(0)

    @pl.when(i == 0)
    def _():
        zero_ref[...] = jnp.zeros_like(zero_ref)
        def fill_class(e, go):
            @pl.when(zoff_ref[N_CLASSES + 1 + e] > 0)
            def _():
                go(pltpu.make_async_copy(zero_ref, _row_tile(xs_ref, zoff_ref[e], TE), sem))

        for e in range(N_CLASSES):
            fill_class(e, lambda cp: cp.start())
        for e in range(N_CLASSES):
            fill_class(e, lambda cp: cp.wait())
        first_free = zoff_ref[N_CLASSES]
        n_all = xs_ref.shape[0] // (TE * SUBLANES)

        def fill(t, carry):
            pltpu.make_async_copy(zero_ref, _row_tile(xs_ref, t, TE), sem).start()
            return carry

        def fill_wait(t, carry):
            pltpu.make_async_copy(zero_ref, _row_tile(xs_ref, t, TE), sem).wait()
            return carry

        lax.fori_loop(first_free, n_all, fill, 0)
        lax.fori_loop(first_free, n_all, fill_wait, 0)

    def row_copy(r):
        return pltpu.make_async_copy(_row_tile(h_ref, r), _row_tile(xs_ref, slots_ref[0, 0, r]), sem)

    for r in range(TM):
        row_copy(r).start(priority=r % 2)
    for r in range(TM):
        row_copy(r).wait()


def _dispatch(h2, slots3, zoff, p_rows):
    n_tiles = h2.shape[0] // (TM * SUBLANES)
    grid_spec = pltpu.PrefetchScalarGridSpec(
        num_scalar_prefetch=0,
        grid=(n_tiles,),
        in_specs=[pl.BlockSpec((1, 1, TM), lambda i: (i, 0, 0), memory_space=pltpu.SMEM),
                  pl.BlockSpec(memory_space=pltpu.SMEM),
                  pl.BlockSpec((TM * SUBLANES, LANES), lambda i: (i, 0))],
        out_specs=pl.BlockSpec(memory_space=pl.ANY),
        scratch_shapes=[pltpu.VMEM((TE * SUBLANES, LANES), F32), pltpu.SemaphoreType.DMA(())],
    )
    return pl.pallas_call(
        _dispatch_kernel,
        out_shape=jax.ShapeDtypeStruct((p_rows * SUBLANES, LANES), F32),
        grid_spec=grid_spec,
        compiler_params=_cparams("arbitrary"),
        name="moe_dispatch",
    )(slots3, zoff, h2)


def _experts_kernel(ea_ref, eb_ref, tv_ref, x_ref, wga_ref, wua_ref, wda_ref, wgb_ref, wub_ref, wdb_ref, y_ref):
    j = pl.program_id(0)

    @pl.when(tv_ref[j] > 0)
    def _():
        x = _load_row_tiles(x_ref, TE).astype(BF16)
        for which, (wg, wu, wd) in enumerate(((wga_ref, wua_ref, wda_ref), (wgb_ref, wub_ref, wdb_ref))):
            g = _dot(x, wg[0])
            u = _dot(x, wu[0])
            y = _dot((_silu(g) * u).astype(BF16), wd[0])
            _store_row_tiles(y_ref, y, TE, stride=2 * SUBLANES, first=which * SUBLANES)

    @pl.when(tv_ref[j] == 0)
    def _():
        y_ref[...] = jnp.zeros_like(y_ref)


def _experts(xs, tile_ea, tile_eb, tile_valid, wg, wu, wd):
    n_tiles = xs.shape[0] // (TE * SUBLANES)
    d, hdim = wg.shape[1], wg.shape[2]
    wa = lambda shp: pl.BlockSpec(shp, lambda j, ea, eb, tv: (ea[j], 0, 0))
    wb = lambda shp: pl.BlockSpec(shp, lambda j, ea, eb, tv: (eb[j], 0, 0))
    grid_spec = pltpu.PrefetchScalarGridSpec(
        num_scalar_prefetch=3,
        grid=(n_tiles,),
        in_specs=[pl.BlockSpec((TE * SUBLANES, LANES), lambda j, ea, eb, tv: (j, 0)),
                  wa((1, d, hdim)), wa((1, d, hdim)), wa((1, hdim, d)),
                  wb((1, d, hdim)), wb((1, d, hdim)), wb((1, hdim, d))],
        out_specs=pl.BlockSpec((2 * TE * SUBLANES, LANES), lambda j, ea, eb, tv: (j, 0)),
    )
    return pl.pallas_call(
        _experts_kernel,
        out_shape=jax.ShapeDtypeStruct((2 * xs.shape[0], LANES), F32),
        grid_spec=grid_spec,
        compiler_params=_cparams("arbitrary"),
        name="moe_experts",
    )(tile_ea, tile_eb, tile_valid, xs, wg, wu, wd, wg, wu, wd)


def _combine_kernel(slots_ref, next_slots_ref, ys_ref, wts_ref, x_ref, m_ref, lng_ref, lnb_ref, o_ref, buf_ref, sem):
    i = pl.program_id(0)
    cur = i % 2

    def row_copy(s_ref, b, r):
        return pltpu.make_async_copy(_row_tile(ys_ref, s_ref[0, 0, r], 2), _row_tile(buf_ref.at[b], r, 2), sem.at[b])

    def issue_all(s_ref, b):
        for r in range(TM):
            row_copy(s_ref, b, r).start(priority=r % 2)

    @pl.when(i == 0)
    def _():
        issue_all(slots_ref, 0)

    @pl.when(i + 1 < pl.num_programs(0))
    def _():
        issue_all(next_slots_ref, 1 - cur)

    for r in range(TM):
        row_copy(slots_ref, cur, r).wait()
    buf = buf_ref.at[cur]
    y_lo = _load_row_tiles(buf, TM, stride=2 * SUBLANES)
    y_hi = _load_row_tiles(buf, TM, stride=2 * SUBLANES, first=SUBLANES)
    y = wts_ref[:, 0:1] * y_lo + wts_ref[:, 1:2] * y_hi
    xn, _ = _post_norm_mod(x_ref[...], y, m_ref, lng_ref[...], lnb_ref[...], 5, None, None)
    o_ref[...] = xn


def _combine(ys, slots3, wts, x, mods, lng, lnb, cond_of, n_tiles):
    d = x.shape[1]
    grid_spec = pltpu.PrefetchScalarGridSpec(
        num_scalar_prefetch=0,
        grid=(n_tiles,),
        in_specs=[pl.BlockSpec((1, 1, TM), lambda i: (i, 0, 0), memory_space=pltpu.SMEM),
                  pl.BlockSpec((1, 1, TM), lambda i: (jnp.minimum(i + 1, n_tiles - 1), 0, 0),
                               memory_space=pltpu.SMEM),
                  pl.BlockSpec(memory_space=pl.ANY),
                  pl.BlockSpec((TM, 2), lambda i: (i, 0)),
                  pl.BlockSpec((TM, d), lambda i: (i, 0)),
                  pl.BlockSpec((1, 6, d), lambda i: (cond_of(i), 0, 0)),
                  pl.BlockSpec((1, d), lambda i: (0, 0)),
                  pl.BlockSpec((1, d), lambda i: (0, 0))],
        out_specs=pl.BlockSpec((TM, d), lambda i: (i, 0)),
        scratch_shapes=[pltpu.VMEM((2, 2 * TM * SUBLANES, LANES), F32), pltpu.SemaphoreType.DMA((2,))],
    )
    return pl.pallas_call(
        _combine_kernel,
        out_shape=jax.ShapeDtypeStruct((n_tiles * TM, d), F32),
        grid_spec=grid_spec,
        compiler_params=_cparams("arbitrary"),
        name="moe_combine",
    )(slots3, slots3, ys, wts, x, mods, lng, lnb)


def _moe(h2, mi, mf, cnt, x, cond_of, mods, lng, lnb, layer, wg, wu, wd):
    t = x.shape[0]
    n_tiles = t // TM
    counts = cnt[:N_CLASSES, 0].astype(jnp.int32)
    padded = ((counts + TE - 1) // TE) * TE
    ends = jnp.cumsum(padded)
    offs = ends - padded
    cls = mi[0]
    cids = jnp.arange(N_CLASSES, dtype=jnp.int32)[:, None]
    slots = jnp.sum(jnp.where(cls[None] == cids, offs[:, None], 0), axis=0) + mi[1]
    slots3 = slots.reshape(n_tiles, 1, TM)
    n_cls_tiles = t // TE + N_CLASSES
    total_tiles = ends[-1] // TE
    tile_ids = jnp.arange(n_cls_tiles, dtype=jnp.int32)
    tile_valid = (tile_ids < total_tiles).astype(jnp.int32)
    tile_last = jnp.minimum(tile_ids, total_tiles - 1)
    tile_class = jnp.sum((tile_last[:, None] * TE >= ends[None, :]).astype(jnp.int32), axis=1)
    tile_class = jnp.minimum(tile_class, N_CLASSES - 1)
    pair_lo = jnp.asarray([p[0] for p in PAIRS], jnp.int32)
    pair_hi = jnp.asarray([p[1] for p in PAIRS], jnp.int32)
    pair_sel = (tile_class % len(PAIRS))[:, None] == jnp.arange(len(PAIRS), dtype=jnp.int32)[None, :]
    group0 = (tile_class // len(PAIRS)) * EXPERTS_PER_GROUP
    tile_ea = (group0 + jnp.sum(jnp.where(pair_sel, pair_lo[None, :], 0), axis=1)).astype(jnp.int32)
    tile_eb = (group0 + jnp.sum(jnp.where(pair_sel, pair_hi[None, :], 0), axis=1)).astype(jnp.int32)
    zoff = jnp.concatenate([jnp.maximum(ends // TE - 1, 0), total_tiles[None], padded]).astype(jnp.int32)
    xs = _dispatch(h2, slots3, zoff, n_cls_tiles * TE)
    ys = _experts(xs, tile_ea + layer * N_EXPERTS, tile_eb + layer * N_EXPERTS, tile_valid, wg, wu, wd)
    wts = jnp.transpose(mf[0:2])
    return _combine(ys, slots3, wts, x, mods, lng, lnb, cond_of, n_tiles)


def _chunk_cumsum(x, tri):
    hi = x.astype(BF16)
    r1 = x - hi.astype(F32)
    mid = r1.astype(BF16)
    lo = (r1 - mid.astype(F32)).astype(BF16)
    return _dot(tri, hi) + _dot(tri, mid) + _dot(tri, lo)


def _gla_kernel(*refs, nb):
    f_in, b_in, (of_ref, ob_ref, sf_ref, sb_ref) = refs[:6], refs[6:12], refs[12:]

    @pl.when(pl.program_id(0) == 0)
    def _():
        sf_ref[...] = jnp.zeros_like(sf_ref)
        sb_ref[...] = jnp.zeros_like(sb_ref)

    _gla_dir(*f_in, of_ref, sf_ref, False, nb)
    _gla_dir(*b_in, ob_ref, sb_ref, True, nb)


def _gla_dir(q_ref, k_ref, v_ref, lr_ref, gw_ref, gb_ref, o_ref, s_ref, reverse, nb):
    C = GLA_CHUNK
    nchunk = TM // C
    ii = lax.broadcasted_iota(jnp.int32, (C, C), 0)
    jj = lax.broadcasted_iota(jnp.int32, (C, C), 1)
    keep = (jj >= ii) if reverse else (jj <= ii)
    tri = jnp.where(keep, 1.0, 0.0).astype(BF16)
    order = tuple(reversed(range(nchunk))) if reverse else tuple(range(nchunk))
    end_row = 0 if reverse else C - 1
    gw = gw_ref[...]
    gb = gb_ref[...]
    for b in range(nb):
        pre = _dot(lr_ref[b], gw) + gb
        la_all = (jnp.minimum(pre, 0.0) - jnp.log(1.0 + jnp.exp(-jnp.abs(pre)))) * (1.0 / GLA_TAU)
        for c in order:
            rows = slice(c * C, (c + 1) * C)
            bc = _chunk_cumsum(la_all[rows, :], tri)
            bend = bc[end_row:end_row + 1, :]
            eb = jnp.exp(bc)
            enb = jnp.exp(-bc)
            ekb = jnp.exp(bend - bc)
            ebend = jnp.exp(bend)
            for h in range(GLA_HEADS):
                kc = slice(h * GLA_DK, (h + 1) * GLA_DK)
                vc = slice(h * GLA_DV, (h + 1) * GLA_DV)
                q = q_ref[b, rows, kc].astype(F32) * (GLA_DK ** -0.5)
                k = k_ref[b, rows, kc].astype(F32)
                v = v_ref[b, rows, vc]
                qd = (q * eb[:, kc]).astype(BF16)
                kd = (k * enb[:, kc]).astype(BF16)
                st = s_ref[b, h]
                scores = jnp.where(keep, _dot_nt(qd, kd), 0.0)
                o = _dot(scores.astype(BF16), v) + _dot_nt(qd, st.astype(BF16))
                s_ref[b, h] = st * ebend[:, kc] + _dot_tn(v, (k * ekb[:, kc]).astype(BF16))
                o_ref[b, rows, vc] = o.astype(BF16)


def _gla(z3, gw_f, gb_f, gw_b, gb_b):
    nb, s, _ = z3.shape
    nt = s // TM
    kw = GLA_HEADS * GLA_DK
    vw = GLA_HEADS * GLA_DV
    lr_blk = (2 * kw + 2 * vw) // LANES

    def dir_specs(jmap):
        return [pl.BlockSpec((nb, TM, kw), lambda j: (0, jmap(j), 0)),
                pl.BlockSpec((nb, TM, kw), lambda j: (0, jmap(j), 1)),
                pl.BlockSpec((nb, TM, vw), lambda j: (0, jmap(j), 1)),
                pl.BlockSpec((nb, TM, LANES), lambda j: (0, jmap(j), lr_blk)),
                pl.BlockSpec((LANES, kw), lambda j: (0, 0)),
                pl.BlockSpec((1, kw), lambda j: (0, 0))]

    fwd, bwd = _scan_tile_orders(nt)
    state = pltpu.VMEM((nb, GLA_HEADS, GLA_DV, GLA_DK), F32)
    out = jax.ShapeDtypeStruct((nb, s, vw), BF16)
    return pl.pallas_call(
        functools.partial(_gla_kernel, nb=nb),
        out_shape=(out, out),
        grid=(nt,),
        in_specs=dir_specs(fwd) + dir_specs(bwd),
        out_specs=(pl.BlockSpec((nb, TM, vw), lambda j: (0, fwd(j), 0)),
                   pl.BlockSpec((nb, TM, vw), lambda j: (0, bwd(j), 0))),
        scratch_shapes=[state, state],
        compiler_params=_cparams("arbitrary"),
        name="gla",
    )(z3, z3, z3, z3, gw_f, gb_f, z3, z3, z3, z3, gw_b, gb_b)


def kernel(x, c, ctx, c_ctx, mod_w, mod_b, ln_g, ln_b, ab_w_in, ret_log_decay_f, ret_log_decay_b, hy_conv_w, hy_conv_b, hy_w1, hy_b1, hy_w2, hy_b2, hy_w3, hy_b3, hy_w4, hy_freq, hy_skip, ab_w_out, gla_w_in, gla_gate_w_f, gla_gate_b_f, gla_gate_w_b, gla_gate_b_b, gla_w_out, router_w, router_bias, exp_w_gate, exp_w_up, exp_w_down):
    nb, length, d = x.shape
    ctx_len = ctx.shape[1]
    assert nb == 2 and ctx_len == TM and length % TM == 0 and length % (FFT_N2 * 2) == 0
    s = length + ctx_len
    tpb = s // TM
    r = nb * s
    lat_tiles = tpb - 1

    cond = jnp.zeros((8, d), F32).at[0].set(c[0]).at[1].set(c_ctx).at[2].set(c[1]).at[3].set(c_ctx)
    mods = _ada(cond, mod_w, mod_b).reshape(DEPTH, 8, 6, d)

    def cond_of(i):
        return 2 * (i // tpb) + jnp.where(i % tpb == tpb - 1, 1, 0)

    x_lat = x.reshape(nb * length, d)
    x_ctx = ctx.reshape(nb * ctx_len, d)

    ret_w = RET_HEADS * RET_HD
    z0, *hy_lat, hy_ctx = _inproj0(x_lat, x_ctx, mods[0], ab_w_in[0].astype(BF16), tpb, r, 4 * ret_w)
    z0_3 = z0.reshape(nb, s, -1)
    cosf, sinf = _rope_tables(length, ctx_len)
    o_f, o_b = _retention(z0_3, ret_log_decay_f[0], ret_log_decay_b[0], cosf, sinf)

    cw3 = jnp.concatenate([hy_conv_w[0].T, hy_conv_b[0][:, None]], axis=1).reshape(3, HY_WIDTH, 4)
    na = length // FFT_N2
    tb = _dft_tables(na)
    filt_w = (hy_w1[0], hy_b1[0], hy_w2[0], hy_b2[0], hy_w3[0], hy_b3[0], hy_w4[0], hy_freq[0])
    skip = hy_skip[0]

    filt, ss = _hyena_filters_raw(length, *filt_w)
    filt5 = filt.reshape(2, 2, HY_WIDTH, na, FFT_N2)
    ss5 = ss.reshape(2, 2, HY_WIDTH, 1, 1)
    hy_lat = [a.reshape(nb, length, HY_WIDTH) for a in hy_lat]
    zcur = _to_freq_layout(hy_lat[2], na)
    for n in range(2):
        hspec = _spectrum(filt5, ss5, n, tb, na)
        gate = _to_freq_layout(hy_lat[n], na)
        zcur = _fftconv(zcur, gate, hspec, skip[n].reshape(-1, 1, 1), cw3[2].reshape(-1, 1, 4),
                        cw3[n].reshape(-1, 1, 4), tb, na, conv_u=(n == 0))
    y_hy_lat = _from_freq_layout(zcur)

    filt_c, ss_c = _hyena_filters_raw(ctx_len, *filt_w)
    filt_c = filt_c.reshape(2, 2, HY_WIDTH, ctx_len)
    ss_c = ss_c.reshape(2, 2, HY_WIDTH, 1)
    u_ctx = jnp.transpose(hy_ctx.reshape(nb, ctx_len, 3, HY_WIDTH), (2, 3, 0, 1))
    y_ctx_t = _ctxconv(u_ctx.reshape(3, HY_WIDTH, nb * ctx_len), cw3, filt_c[:, 0], filt_c[:, 1], ss_c[:, 0],
                       ss_c[:, 1], skip.reshape(2, HY_WIDTH, 1), _ctx_dft_tables(ctx_len))
    y_hy_ctx = jnp.transpose(y_ctx_t.reshape(HY_WIDTH, nb, ctx_len), (1, 2, 0))

    gate_blk = 3

    def specs0(h):
        tile = lambda i: 2 * i + h
        trow = lambda i: (tile(i), 0)
        return [pl.BlockSpec((TM, ret_w), trow), pl.BlockSpec((TM, ret_w), trow),
                pl.BlockSpec((TM, ret_w), lambda i: (tile(i), gate_blk)),
                *_lat_or_ctx_specs(HY_WIDTH, tpb, tile), *_lat_or_ctx_specs(d, tpb, tile),
                pl.BlockSpec((1, 6, d), lambda i: (cond_of(tile(i)), 0, 0))]

    x_mid, h2, mi, mf, cnt = _outproj_call(
        True, tpb, r // TM, d, specs0,
        (o_f.reshape(r, ret_w), o_b.reshape(r, ret_w), z0, y_hy_lat.reshape(nb * length, HY_WIDTH),
         y_hy_ctx.reshape(nb * ctx_len, HY_WIDTH), x_lat, x_ctx, mods[0]),
        ab_w_out[0].astype(BF16), ln_g[0, 0].reshape(1, d), ln_b[0, 0].reshape(1, d), router_w, router_bias)

    stack = lambda w: w.astype(BF16).reshape((DEPTH * N_EXPERTS,) + w.shape[2:])
    exp_w = (stack(exp_w_gate), stack(exp_w_up), stack(exp_w_down))
    x1 = _moe(h2, mi, mf, cnt, x_mid, cond_of, mods[0], ln_g[0, 1].reshape(1, d), ln_b[0, 1].reshape(1, d),
              0, *exp_w)

    kw = GLA_HEADS * GLA_DK
    vw = GLA_HEADS * GLA_DV
    n_in = gla_w_in.shape[2]
    n_pad = 2 * kw + 2 * vw + LANES
    w_in1 = jnp.pad(gla_w_in[0], ((0, 0), (0, n_pad - n_in))).astype(BF16)
    z1 = _inproj(x1, mods[1], w_in1, tpb)
    z1_3 = z1.reshape(nb, s, n_pad)
    gw_f = jnp.zeros((LANES, kw), F32).at[:GLA_RANK].set(gla_gate_w_f[0]).astype(BF16)
    gw_b = jnp.zeros((LANES, kw), F32).at[GLA_RANK:2 * GLA_RANK].set(gla_gate_w_b[0]).astype(BF16)
    g_f, g_b = _gla(z1_3, gw_f, gla_gate_b_f[0].reshape(1, kw), gw_b, gla_gate_b_b[0].reshape(1, kw))

    n_lat = nb * lat_tiles

    def lat_tile(i):
        return (i // lat_tiles) * tpb + i % lat_tiles

    def lat_cond(i):
        return 2 * (i // lat_tiles)

    def specs1(h):
        lrow = lambda i: (lat_tile(2 * i + h), 0)
        return [pl.BlockSpec((TM, vw), lrow), pl.BlockSpec((TM, vw), lrow),
                pl.BlockSpec((TM, vw), lambda i: (lat_tile(2 * i + h), 2)),
                pl.BlockSpec((TM, d), lrow),
                pl.BlockSpec((1, 6, d), lambda i: (lat_cond(2 * i + h), 0, 0))]

    x_mid1, h2_1, mi1, mf1, cnt1 = _outproj_call(
        False, tpb, n_lat, d, specs1, (g_f.reshape(r, vw), g_b.reshape(r, vw), z1, x1, mods[1]),
        gla_w_out[0].astype(BF16), ln_g[1, 0].reshape(1, d), ln_b[1, 0].reshape(1, d), router_w, router_bias)

    out = _moe(h2_1, mi1, mf1, cnt1, x_mid1, lat_cond, mods[1], ln_g[1, 1].reshape(1, d), ln_b[1, 1].reshape(1, d),
               1, *exp_w)
    return out.reshape(nb, length, d)
```

```python
import functools
import math

import numpy as np
import jax
import jax.numpy as jnp
from jax import lax
from jax.experimental import pallas as pl
from jax.experimental.pallas import tpu as pltpu

F32 = jnp.float32
BF16 = jnp.bfloat16

GRID_W = 64
RET_HEADS = 4
RET_HD = 128
RET_CHUNK = 128
ROPE_BASE = 10000.0
HY_WIDTH = 512
HY_EMB = 33
HY_FFN = 64
HY_SHORT_DECAY_PCT = 0.3
HY_LONG_DECAY_PCT = 1.5
HY_TARGET = 1e-2
GLA_HEADS = 4
GLA_DK = 128
GLA_DV = 256
GLA_RANK = 16
GLA_TAU = 16.0
GLA_CHUNK = 64
N_EXPERTS = 16
N_GROUPS = 4
EXPERTS_PER_GROUP = 4
LN_EPS = 1e-5
DEPTH = 2
DN_ALPHA = (2.0 * DEPTH) ** 0.25

PAIRS = ((0, 1), (0, 2), (0, 3), (1, 2), (1, 3), (2, 3))
N_CLASSES = N_GROUPS * len(PAIRS)
CLASS_ROWS = 32

LANES = 128
SUBLANES = 8
TM = 256
TE = 256
FFT_N2 = 128
FFT_CB = 16
VMEM_LIMIT = 48 * 1024 * 1024

HIGHEST = lax.Precision.HIGHEST


def _cparams(*sem):
    return pltpu.CompilerParams(dimension_semantics=sem, vmem_limit_bytes=VMEM_LIMIT)


def _silu(v):
    return v * (1.0 / (1.0 + jnp.exp(-v)))


def _dot(a, b):
    return jnp.dot(a, b, preferred_element_type=F32)


def _dot_nt(a, b):
    return lax.dot_general(a, b, (((1,), (1,)), ((), ())), preferred_element_type=F32)


def _dot_tn(a, b):
    return lax.dot_general(a, b, (((0,), (0,)), ((), ())), preferred_element_type=F32)


def _split2(a):
    hi = a.astype(BF16)
    return hi, (a - hi.astype(F32)).astype(BF16)


def _dot3(a, b, dot=_dot):
    ah, al = _split2(a)
    bh, bl = _split2(b)
    return dot(ah, bh) + dot(ah, bl) + dot(al, bh)


def _ada_kernel(c_ref, w_ref, b_ref, o_ref):
    a = _silu(c_ref[...])
    o_ref[0] = _dot3(a, w_ref[0]) + b_ref[0]


def _ada(cond, mod_w, mod_b):
    depth, d, n = mod_w.shape
    nt = n // 4
    return pl.pallas_call(
        _ada_kernel,
        out_shape=jax.ShapeDtypeStruct((depth, 8, n), F32),
        grid=(depth, n // nt),
        in_specs=[pl.BlockSpec((8, d), lambda l, j: (0, 0)),
                  pl.BlockSpec((1, d, nt), lambda l, j: (l, 0, j)),
                  pl.BlockSpec((1, 1, nt), lambda l, j: (l, 0, j))],
        out_specs=pl.BlockSpec((1, 8, nt), lambda l, j: (l, 0, j)),
        compiler_params=_cparams("arbitrary", "arbitrary"),
        name="ada_mod",
    )(cond, mod_w, mod_b.reshape(depth, 1, n))


def _inproj_kernel(x_ref, m_ref, w_ref, o_ref, *, chunks):
    sh = m_ref[0, 0:1, :]
    sc = m_ref[0, 1:2, :]
    h = (x_ref[...] * (1.0 + sc) + sh).astype(BF16)
    for lo, hi in chunks:
        o_ref[:, lo:hi] = _dot(h, w_ref[:, lo:hi]).astype(BF16)


def _lat_or_ctx_specs(width, tpb, tile_of=lambda i: i):
    lat_tiles = tpb - 1
    lat = lambda i: ((tile_of(i) // tpb) * lat_tiles + jnp.minimum(tile_of(i) % tpb, lat_tiles - 1), 0)
    ctx = lambda i: (tile_of(i) // tpb, 0)
    return pl.BlockSpec((TM, width), lat), pl.BlockSpec((TM, width), ctx)


def _col_chunks(n, width=512):
    out, lo = [], 0
    while lo < n:
        hi = min(lo + width, n)
        out.append((lo, hi))
        lo = hi
    return tuple(out)


def _inproj(xu, mods, w, tiles_per_batch):
    r, d = xu.shape
    n = w.shape[1]
    tpb = tiles_per_batch

    def cond_of(i):
        return 2 * (i // tpb) + jnp.where(i % tpb == tpb - 1, 1, 0)

    return pl.pallas_call(
        functools.partial(_inproj_kernel, chunks=_col_chunks(n)),
        out_shape=jax.ShapeDtypeStruct((r, n), BF16),
        grid=(r // TM,),
        in_specs=[pl.BlockSpec((TM, d), lambda i: (i, 0)),
                  pl.BlockSpec((1, 6, d), lambda i: (cond_of(i), 0, 0)),
                  pl.BlockSpec((d, n), lambda i: (0, 0))],
        out_specs=pl.BlockSpec((TM, n), lambda i: (i, 0)),
        compiler_params=_cparams("arbitrary"),
        name="inproj1",
    )(xu, mods, w)


def _inproj0_kernel(x_ref, c_ref, m_ref, w_ref, z_ref, x1_ref, x2_ref, v_ref, hc_ref, *, tpb, ret_w):
    is_ctx = pl.program_id(0) % tpb == tpb - 1
    x = jnp.where(is_ctx, c_ref[...], x_ref[...])
    h = (x * (1.0 + m_ref[0, 1:2, :]) + m_ref[0, 0:1, :]).astype(BF16)
    for lo, hi in _col_chunks(ret_w):
        z_ref[:, lo:hi] = _dot(h, w_ref[:, lo:hi]).astype(BF16)
    hy = [_dot(h, w_ref[:, ret_w + j * HY_WIDTH:ret_w + (j + 1) * HY_WIDTH]).astype(BF16) for j in range(3)]

    @pl.when(jnp.logical_not(is_ctx))
    def _():
        for ref, val in zip((x1_ref, x2_ref, v_ref), hy):
            ref[...] = val

    @pl.when(is_ctx)
    def _():
        for j, val in enumerate(hy):
            hc_ref[:, j * HY_WIDTH:(j + 1) * HY_WIDTH] = val


def _inproj0(x_lat, x_ctx, mods, w, tpb, r, ret_w):
    d = w.shape[0]
    lat_tiles = tpb - 1

    def cond_of(i):
        return 2 * (i // tpb) + jnp.where(i % tpb == tpb - 1, 1, 0)

    lat_spec, ctx_spec = _lat_or_ctx_specs(d, tpb)
    lat_out = pl.BlockSpec((TM, HY_WIDTH), lambda i: ((i // tpb) * lat_tiles + jnp.minimum(i % tpb, lat_tiles - 1), 0))
    hy_lat = jax.ShapeDtypeStruct(((r // TM // tpb) * lat_tiles * TM, HY_WIDTH), BF16)
    return pl.pallas_call(
        functools.partial(_inproj0_kernel, tpb=tpb, ret_w=ret_w),
        out_shape=(jax.ShapeDtypeStruct((r, ret_w), BF16), hy_lat, hy_lat, hy_lat,
                   jax.ShapeDtypeStruct(((r // TM // tpb) * TM, 3 * HY_WIDTH), BF16)),
        grid=(r // TM,),
        in_specs=[lat_spec, ctx_spec, pl.BlockSpec((1, 6, d), lambda i: (cond_of(i), 0, 0)),
                  pl.BlockSpec(w.shape, lambda i: (0, 0))],
        out_specs=(pl.BlockSpec((TM, ret_w), lambda i: (i, 0)), lat_out, lat_out, lat_out,
                   pl.BlockSpec((TM, 3 * HY_WIDTH), lambda i: (i // tpb, 0))),
        compiler_params=_cparams("arbitrary"),
        name="inproj0",
    )(x_lat, x_ctx, mods, w)


def _ret_kernel(lgf_ref, lgb_ref, *refs, nb):
    f_in, b_in, (of_ref, ob_ref, sf_ref, sb_ref) = refs[:5], refs[5:10], refs[10:]

    @pl.when(pl.program_id(0) == 0)
    def _():
        sf_ref[...] = jnp.zeros_like(sf_ref)
        sb_ref[...] = jnp.zeros_like(sb_ref)

    _ret_dir(lgf_ref, *f_in, of_ref, sf_ref, False, nb)
    _ret_dir(lgb_ref, *b_in, ob_ref, sb_ref, True, nb)


def _ret_dir(lg_ref, q_ref, k_ref, v_ref, cos_ref, sin_ref, o_ref, s_ref, reverse, nb):
    C = RET_CHUNK
    ii = lax.broadcasted_iota(jnp.int32, (C, C), 0).astype(F32)
    jj = lax.broadcasted_iota(jnp.int32, (C, C), 1).astype(F32)
    ci = lax.broadcasted_iota(jnp.int32, (C, 1), 0).astype(F32)
    diff = (jj - ii) if reverse else (ii - jj)
    order = (1, 0) if reverse else (0, 1)
    for h in range(RET_HEADS):
        lg = lg_ref[h]
        intra = jnp.where(diff >= 0, jnp.exp(jnp.maximum(diff, 0.0) * lg), 0.0)
        if reverse:
            q_dec = jnp.exp((C - ci) * lg)
            k_dec = jnp.exp(ci * lg)
        else:
            q_dec = jnp.exp((ci + 1.0) * lg)
            k_dec = jnp.exp((C - 1.0 - ci) * lg)
        c_dec = jnp.exp(jnp.zeros((1, RET_HD), F32) + C * lg)
        cols = slice(h * RET_HD, (h + 1) * RET_HD)
        for b in range(nb):
            for c in order:
                rows = slice(c * C, (c + 1) * C)
                cosf = cos_ref[rows, :]
                sinf = sin_ref[rows, :]
                q = q_ref[b, rows, cols].astype(F32)
                k = k_ref[b, rows, cols].astype(F32)
                v = v_ref[b, rows, cols]
                q = q * cosf + pltpu.roll(q, RET_HD // 2, 1) * sinf
                k = (k * cosf + pltpu.roll(k, RET_HD // 2, 1) * sinf) * (RET_HD ** -0.5)
                st = s_ref[b, h]
                scores = _dot_nt(q.astype(BF16), k.astype(BF16)) * intra
                o = _dot(scores.astype(BF16), v) + _dot_nt((q * q_dec).astype(BF16), st.astype(BF16))
                s_ref[b, h] = c_dec * st + _dot_tn(v, (k * k_dec).astype(BF16))
                o_ref[b, rows, cols] = o.astype(BF16)


def _scan_tile_orders(nt):
    fwd = lambda j: (j + nt - 1) % nt
    bwd = lambda j: jnp.where(j == 0, nt - 1, nt - 1 - j)
    return fwd, bwd


def _retention(z3, lg_f, lg_b, cosf, sinf):
    nb, s, _ = z3.shape
    nt = s // TM
    w = RET_HEADS * RET_HD
    smem = pl.BlockSpec(memory_space=pltpu.SMEM)

    def dir_specs(jmap):
        return [pl.BlockSpec((nb, TM, w), lambda j: (0, jmap(j), 0)),
                pl.BlockSpec((nb, TM, w), lambda j: (0, jmap(j), 1)),
                pl.BlockSpec((nb, TM, w), lambda j: (0, jmap(j), 2)),
                pl.BlockSpec((TM, RET_HD), lambda j: (jmap(j), 0)),
                pl.BlockSpec((TM, RET_HD), lambda j: (jmap(j), 0))]

    fwd, bwd = _scan_tile_orders(nt)
    state = pltpu.VMEM((nb, RET_HEADS, RET_HD, RET_HD), F32)
    out = jax.ShapeDtypeStruct((nb, s, w), BF16)
    grid_spec = pltpu.PrefetchScalarGridSpec(
        num_scalar_prefetch=0,
        grid=(nt,),
        in_specs=[smem, smem] + dir_specs(fwd) + dir_specs(bwd),
        out_specs=(pl.BlockSpec((nb, TM, w), lambda j: (0, fwd(j), 0)),
                   pl.BlockSpec((nb, TM, w), lambda j: (0, bwd(j), 0))),
        scratch_shapes=[state, state],
    )
    return pl.pallas_call(
        functools.partial(_ret_kernel, nb=nb),
        out_shape=(out, out),
        grid_spec=grid_spec,
        compiler_params=_cparams("arbitrary"),
        name="retention",
    )(lg_f, lg_b, z3, z3, z3, cosf, sinf, z3, z3, z3, cosf, sinf)


def _rope_tables(length, ctx_len):
    rows = length // GRID_W
    quarter = RET_HD // 4
    inv = ROPE_BASE ** (-jnp.arange(quarter, dtype=F32) / quarter)
    def expand(fn):
        by_row = fn(jnp.arange(rows, dtype=F32)[:, None] * inv)
        by_col = fn(jnp.arange(GRID_W, dtype=F32)[:, None] * inv)
        by_row = jnp.broadcast_to(by_row[:, None, :], (rows, GRID_W, quarter)).reshape(length, quarter)
        by_col = jnp.broadcast_to(by_col[None, :, :], (rows, GRID_W, quarter)).reshape(length, quarter)
        return jnp.concatenate([by_row, by_col], axis=-1)

    cos, sin = expand(jnp.cos), expand(jnp.sin)
    cosf = jnp.concatenate([cos, cos], axis=-1)
    sinf = jnp.concatenate([-sin, sin], axis=-1)
    cosf = jnp.concatenate([cosf, jnp.ones((ctx_len, RET_HD), F32)], axis=0)
    sinf = jnp.concatenate([sinf, jnp.zeros((ctx_len, RET_HD), F32)], axis=0)
    return cosf, sinf


def _shortconv_freq(x, cw, na):
    row = lax.broadcasted_iota(jnp.int32, (1, FFT_N2, 1), 1)
    lane = lax.broadcasted_iota(jnp.int32, (1, 1, 2 * na), 2)
    wrap_prev = pltpu.roll(x[:, FFT_N2 - 1:FFT_N2, :], 1, 2)
    prev = jnp.where(row == 0, jnp.where((lane == 0) | (lane == na), 0.0, wrap_prev), pltpu.roll(x, 1, 1))
    wrap_next = pltpu.roll(x[:, 0:1, :], 2 * na - 1, 2)
    nxt = jnp.where(row == FFT_N2 - 1, jnp.where((lane == na - 1) | (lane == 2 * na - 1), 0.0, wrap_next),
                    pltpu.roll(x, FFT_N2 - 1, 1))
    return cw[..., 0:1] * prev + cw[..., 1:2] * x + cw[..., 2:3] * nxt + cw[..., 3:4]


def _shortconv_lanes(x, cw, n):
    lane = lax.broadcasted_iota(jnp.int32, (1, 2 * n), 1)
    prev = jnp.where((lane == 0) | (lane == n), 0.0, pltpu.roll(x, 1, 1))
    nxt = jnp.where((lane == n - 1) | (lane == 2 * n - 1), 0.0, pltpu.roll(x, 2 * n - 1, 1))
    return cw[:, 0:1] * prev + cw[:, 1:2] * x + cw[:, 2:3] * nxt + cw[:, 3:4]


FEAT_ROWS = 40


def _filter_kernel(w1_ref, b1_ref, w2_ref, b2_ref, w3_ref, b3_ref, w4_ref, fr_ref, dl_ref,
                   f_ref, ss_ref, *, length, pb):
    i = pl.program_id(0)
    pos = (lax.broadcasted_iota(jnp.int32, (1, pb), 1) + i * pb).astype(F32)
    t = pos * (1.0 / (length - 1))
    bands = (HY_EMB - 1) // 2
    w = (2.0 * math.pi) * pos / length
    sub = lax.broadcasted_iota(jnp.int32, (FEAT_ROWS, 1), 0)
    band = jnp.where(sub <= bands, sub - 1, sub - 1 - bands).astype(F32)
    f = 1e-4 + band * ((bands - 1 - 1e-4) / (bands - 1))
    fw = f * w
    feats = jnp.where(sub == 0, t, jnp.where(sub <= bands, jnp.cos(fw),
                                             jnp.where(sub <= 2 * bands, -jnp.sin(fw), 0.0)))
    feats = jnp.concatenate([feats, jnp.zeros((LANES - FEAT_ROWS, pb), F32)], axis=0)
    hdot = _dot3
    a = jnp.sin(fr_ref[:, 0:1] * (hdot(w1_ref[...], feats) + b1_ref[...]))
    a = jnp.sin(fr_ref[:, 1:2] * (hdot(w2_ref[...], a) + b2_ref[...]))
    a = jnp.sin(fr_ref[:, 2:3] * (hdot(w3_ref[...], a) + b3_ref[...]))

    @pl.when(i == 0)
    def _():
        ss_ref[...] = jnp.zeros_like(ss_ref)

    nout = w4_ref.shape[0]
    window = jnp.exp(-dl_ref[...] * t)
    a_bf = a.astype(BF16)
    for cb in range(nout // HY_WIDTH):
        rows = slice(cb * HY_WIDTH, (cb + 1) * HY_WIDTH)
        filt = _dot(w4_ref[rows, :].astype(BF16), a_bf) * window
        for j in range(pb // LANES):
            f_ref[rows, j, :] = filt[:, j * LANES:(j + 1) * LANES]
        ss_ref[rows, :] += jnp.sum(filt * filt, axis=1, keepdims=True)


def _hyena_filters_raw(length, w1, b1, w2, b2, w3, b3, w4, freq):
    nout = w4.shape[1]
    pb = min(length, 1024)
    max_decay = math.log(HY_TARGET) / HY_SHORT_DECAY_PCT
    min_decay = math.log(HY_TARGET) / HY_LONG_DECAY_PCT
    deltas = jnp.abs(jnp.linspace(min_decay, max_decay, HY_WIDTH, dtype=F32))
    dl = deltas.reshape(HY_WIDTH, 1)
    w1p = jnp.pad(w1.T, ((0, 0), (0, LANES - w1.shape[0])))
    col = lambda b: b.reshape(-1, 1)
    full = lambda a: pl.BlockSpec(a.shape, lambda i: tuple(0 for _ in a.shape))
    args = (w1p, col(b1), w2.T, col(b2), w3.T, col(b3), w4.T, freq.T, dl)
    return pl.pallas_call(
        functools.partial(_filter_kernel, length=length, pb=pb),
        out_shape=(jax.ShapeDtypeStruct((nout, length // LANES, LANES), F32),
                   jax.ShapeDtypeStruct((nout, 1), F32)),
        grid=(length // pb,),
        in_specs=[full(a) for a in args],
        out_specs=(pl.BlockSpec((nout, pb // LANES, LANES), lambda i: (0, i, 0)),
                   pl.BlockSpec((nout, 1), lambda i: (0, 0))),
        compiler_params=_cparams("arbitrary"),
        name="hyena_filter_mlp",
    )(*args)


def _dft_tables(na):
    k1n = 2 * na
    n = k1n * FFT_N2
    n1 = np.arange(na)[:, None]
    k1 = np.arange(k1n)[None, :]
    ang = 2.0 * np.pi * ((n1 * k1) % k1n) / k1n
    c, s = np.cos(ang), np.sin(ang)
    ma = np.block([[c, -s], [s, c]])
    n1f = np.arange(FFT_N2)[:, None]
    angf = 2.0 * np.pi * ((n1f * k1) % k1n) / k1n
    live = (n1f < na).astype(np.float64)
    ma_f = np.concatenate([np.cos(angf), -np.sin(angf)], axis=1) * live
    ma_b = np.concatenate([np.cos(angf), np.sin(angf)], axis=1) * live
    n2 = np.arange(FFT_N2)[:, None]
    angt = 2.0 * np.pi * ((n2 * k1) % n) / n
    twr, twi = np.cos(angt), -np.sin(angt)
    k2 = np.arange(FFT_N2)[None, :]
    angb = 2.0 * np.pi * ((n2 * k2) % FFT_N2) / FFT_N2
    cb, sb = np.cos(angb), np.sin(angb)
    mb = np.block([[cb, -sb], [sb, cb]])
    mc = np.block([[cb, sb], [-sb, cb]])
    angd = 2.0 * np.pi * ((np.arange(k1n)[:, None] * np.arange(na)[None, :]) % k1n) / k1n
    cd, sd = np.cos(angd) / n, np.sin(angd) / n
    md = np.block([[cd, sd], [-sd, cd]])
    as_bf = lambda a: jnp.asarray(a, dtype=F32).astype(BF16)
    as_f = lambda a: jnp.asarray(a, dtype=F32)
    mb2 = np.concatenate([mb, mc], axis=0)
    return dict(ma=as_bf(ma), ma_f=as_bf(ma_f), ma_b=as_bf(ma_b), twr=as_f(twr), twi=as_f(twi),
                twr_t=as_f(twr.T), twi_t=as_f(twi.T), mb=as_bf(mb), mb2=as_bf(mb2), mc=as_bf(mc), md=as_bf(md))


def _fwd_stages(x2, ma, twr, twi, mb, cb, k1n):
    a = _dot(x2, ma).reshape(cb, FFT_N2, 2 * k1n)
    ar, ai = a[..., :k1n], a[..., k1n:]
    ar2 = ar * twr - ai * twi
    ai2 = ar * twi + ai * twr
    xt = jnp.concatenate([jnp.swapaxes(ar2, 1, 2), jnp.swapaxes(ai2, 1, 2)], axis=-1)
    return _dot(xt.astype(BF16).reshape(cb * k1n, 2 * FFT_N2), mb).reshape(cb, k1n, 2 * FFT_N2)


def _spectrum_kernel(hf_ref, hb_ref, ssf_ref, ssb_ref, maf_ref, mab_ref, twr_ref, twi_ref, mb2_ref, h_ref, *, na):
    k1n = 2 * na
    cb = h_ref.shape[0]

    def stage_a(ref, ss_ref, m_ref):
        x = ref[0, 0] * lax.rsqrt(ss_ref[0, 0])
        x = jnp.concatenate([x, jnp.zeros((cb, FFT_N2 - na, FFT_N2), F32)], axis=1)
        xt = jnp.swapaxes(x, 1, 2).astype(BF16).reshape(cb * FFT_N2, FFT_N2)
        a = _dot(xt, m_ref[...]).reshape(cb, FFT_N2, 2 * k1n)
        return a[..., :k1n], a[..., k1n:]

    fr, fi = stage_a(hf_ref, ssf_ref, maf_ref)
    br, bi = stage_a(hb_ref, ssb_ref, mab_ref)
    twr, twi = twr_ref[...], twi_ref[...]
    parts = (fr * twr - fi * twi, fr * twi + fi * twr, br * twr + bi * twi, bi * twr - br * twi)
    xt = jnp.concatenate([jnp.swapaxes(p, 1, 2) for p in parts], axis=-1)
    h = _dot(xt.astype(BF16).reshape(cb * k1n, 4 * FFT_N2), mb2_ref[...])
    h_ref[...] = h.reshape(cb, k1n, 2 * FFT_N2).astype(BF16)


def _spectrum(filt5, ss5, order, tb, na):
    c = filt5.shape[2]
    k1n = 2 * na
    cb = FFT_CB
    full = lambda a: pl.BlockSpec(a.shape, lambda i: tuple(0 for _ in a.shape))
    fblk = lambda d: pl.BlockSpec((1, 1, cb, na, FFT_N2), lambda i: (order, d, i, 0, 0))
    sblk = lambda d: pl.BlockSpec((1, 1, cb, 1, 1), lambda i: (order, d, i, 0, 0))
    return pl.pallas_call(
        functools.partial(_spectrum_kernel, na=na),
        out_shape=jax.ShapeDtypeStruct((c, k1n, 2 * FFT_N2), BF16),
        grid=(c // cb,),
        in_specs=[fblk(0), fblk(1), sblk(0), sblk(1),
                  full(tb["ma_f"]), full(tb["ma_b"]), full(tb["twr"]), full(tb["twi"]), full(tb["mb2"])],
        out_specs=pl.BlockSpec((cb, k1n, 2 * FFT_N2), lambda i: (i, 0, 0)),
        compiler_params=_cparams("arbitrary"),
        name="hyena_filter_spectrum",
    )(filt5, filt5, ss5, ss5, tb["ma_f"], tb["ma_b"], tb["twr"], tb["twi"], tb["mb2"])


def _fftconv_kernel(u_ref, g_ref, h_ref, sk_ref, cwu_ref, cwg_ref, ma_ref, twr_ref, twi_ref, mb_ref, mc_ref,
                    twrt_ref, twit_ref, md_ref, o_ref, *, na, conv_u):
    k1n = 2 * na
    cb = u_ref.shape[0]
    uf = u_ref[...].astype(F32)
    if conv_u:
        uf = _shortconv_freq(uf, cwu_ref[...], na)
    gate = _shortconv_freq(g_ref[...].astype(F32), cwg_ref[...], na)
    x = _fwd_stages(uf.astype(BF16).reshape(cb * FFT_N2, k1n), ma_ref[...], twr_ref[...], twi_ref[...],
                    mb_ref[...], cb, k1n)
    xr, xi = x[..., :FFT_N2], x[..., FFT_N2:]
    h = h_ref[...].astype(F32)
    hr, hi = h[..., :FFT_N2], h[..., FFT_N2:]
    y = jnp.concatenate([xr * hr - xi * hi, xr * hi + xi * hr], axis=-1)
    c = _dot(y.astype(BF16).reshape(cb * k1n, 2 * FFT_N2), mc_ref[...]).reshape(cb, k1n, 2 * FFT_N2)
    cr, ci = c[..., :FFT_N2], c[..., FFT_N2:]
    twrt, twit = twrt_ref[...], twit_ref[...]
    cr2 = cr * twrt + ci * twit
    ci2 = ci * twrt - cr * twit
    ct = jnp.concatenate([jnp.swapaxes(cr2, 1, 2), jnp.swapaxes(ci2, 1, 2)], axis=-1)
    d = _dot(ct.astype(BF16).reshape(cb * FFT_N2, 2 * k1n), md_ref[...]).reshape(cb, FFT_N2, k1n)
    o_ref[...] = (gate * (d + sk_ref[...] * uf)).astype(BF16)


def _fftconv(u, gate, h, skip, cw_u, cw_g, tb, na, conv_u):
    c = u.shape[0]
    k1n = 2 * na
    cb = FFT_CB
    full = lambda a: pl.BlockSpec(a.shape, lambda i: tuple(0 for _ in a.shape))
    blk = pl.BlockSpec((cb, FFT_N2, k1n), lambda i: (i, 0, 0))
    cwblk = pl.BlockSpec((cb, 1, 4), lambda i: (i, 0, 0))
    return pl.pallas_call(
        functools.partial(_fftconv_kernel, na=na, conv_u=conv_u),
        out_shape=jax.ShapeDtypeStruct(u.shape, BF16),
        grid=(c // cb,),
        in_specs=[blk, blk,
                  pl.BlockSpec((cb, k1n, 2 * FFT_N2), lambda i: (i, 0, 0)),
                  pl.BlockSpec((cb, 1, 1), lambda i: (i, 0, 0)), cwblk, cwblk,
                  full(tb["ma"]), full(tb["twr"]), full(tb["twi"]), full(tb["mb"]), full(tb["mc"]),
                  full(tb["twr_t"]), full(tb["twi_t"]), full(tb["md"])],
        out_specs=blk,
        compiler_params=_cparams("arbitrary"),
        name="hyena_fftconv",
    )(u, gate, h, skip, cw_u, cw_g, tb["ma"], tb["twr"], tb["twi"], tb["mb"], tb["mc"], tb["twr_t"], tb["twi_t"],
      tb["md"])


def _to_freq_layout(a, na):
    nb, _, c = a.shape
    a = a.reshape(nb, na, FFT_N2, c)
    return jnp.transpose(a, (3, 2, 0, 1)).reshape(c, FFT_N2, nb * na)


def _from_freq_layout(a):
    c, _, lanes = a.shape
    na = lanes // 2
    a = a.reshape(c, FFT_N2, 2, na)
    return jnp.transpose(a, (2, 3, 1, 0)).reshape(2, na * FFT_N2, c)


def _ctx_dft_tables(n):
    big = 2 * n
    j = np.arange(n)[:, None]
    k = np.arange(big)[None, :]
    ang = 2.0 * np.pi * ((j * k) % big) / big
    c, s = np.cos(ang), np.sin(ang)
    fh = np.block([[c, -s], [c, s]])
    ff = np.block([[c, -s], [s, c]])
    fi = np.block([[c.T, s.T], [-s.T, c.T]]) / big
    as_bf = lambda a: jnp.asarray(a, dtype=F32).astype(BF16)
    return as_bf(fh), as_bf(ff), as_bf(fi)


def _ctxconv_kernel(u_ref, cw_ref, hf_ref, hb_ref, ssf_ref, ssb_ref, sk_ref, fh_ref, ff_ref, fi_ref, o_ref):
    n_tok = u_ref.shape[2] // 2
    short = lambda j: _shortconv_lanes(u_ref[j].astype(F32), cw_ref[j], n_tok)
    gates = (short(0), short(1))
    z = short(2)
    half = fh_ref.shape[1] // 2
    for n in range(2):
        taps = jnp.concatenate([hf_ref[n] * lax.rsqrt(ssf_ref[n]), hb_ref[n] * lax.rsqrt(ssb_ref[n])], axis=-1)
        h = _dot(taps.astype(BF16), fh_ref[...])
        x = _dot(z.astype(BF16), ff_ref[...])
        hr, hi = h[:, :half], h[:, half:]
        xr, xi = x[:, :half], x[:, half:]
        y = jnp.concatenate([xr * hr - xi * hi, xr * hi + xi * hr], axis=-1)
        conv = _dot(y.astype(BF16), fi_ref[...])
        z = gates[n] * (conv + sk_ref[n] * z)
    o_ref[...] = z.astype(BF16)


def _ctxconv(u3, cw3, hf, hb, ssf, ssb, skip, tables):
    _, c, lanes = u3.shape
    args = (u3, cw3, hf, hb, ssf, ssb, skip) + tuple(tables)
    full = lambda a: pl.BlockSpec(a.shape, lambda i: tuple(0 for _ in a.shape))
    return pl.pallas_call(
        _ctxconv_kernel,
        out_shape=jax.ShapeDtypeStruct((c, lanes), BF16),
        grid=(1,),
        in_specs=[full(a) for a in args],
        out_specs=pl.BlockSpec((c, lanes), lambda i: (0, 0)),
        compiler_params=_cparams("arbitrary"),
        name="hyena_ctxconv",
    )(*args)


def _load_row_tiles(ref, n, stride=SUBLANES, first=0):
    return jnp.concatenate([ref[pl.ds(first + j, n, stride=stride), :] for j in range(SUBLANES)], axis=-1)


def _store_row_tiles(ref, val, n, stride=SUBLANES, first=0):
    for j in range(SUBLANES):
        ref[pl.ds(first + j, n, stride=stride), :] = val[:, j * LANES:(j + 1) * LANES]


def _row_tile(ref, idx, tiles=1):
    size = tiles * SUBLANES
    return ref.at[pl.ds(pl.multiple_of(idx * size, size), size)]


def _top2_rows(vals):
    n = len(vals)
    best_v, best_i = vals[0], jnp.zeros_like(vals[0], dtype=jnp.int32)
    for e in range(1, n):
        take = vals[e] > best_v
        best_v = jnp.where(take, vals[e], best_v)
        best_i = jnp.where(take, e, best_i)
    sec_v = jnp.full_like(vals[0], -jnp.inf)
    sec_i = jnp.zeros_like(best_i)
    for e in range(n):
        take = (best_i != e) & (vals[e] > sec_v)
        sec_v = jnp.where(take, vals[e], sec_v)
        sec_i = jnp.where(take, e, sec_i)
    return best_v, best_i, sec_v, sec_i


def _route(h2, rw_ref, rb_ref, carry_ref, mi_ref, mf_ref, cnt_ref, cols):
    logits = _dot3(rw_ref[...], h2, dot=_dot_nt)
    s = 1.0 / (1.0 + jnp.exp(-logits))
    sel = s + rb_ref[...]
    ng = N_GROUPS
    sel4 = [sel[j * ng:(j + 1) * ng, :] for j in range(EXPERTS_PER_GROUP)]
    s4 = [s[j * ng:(j + 1) * ng, :] for j in range(EXPERTS_PER_GROUP)]
    bv, _, sv, _ = _top2_rows(sel4)
    gscore = bv + sv
    gidx = lax.broadcasted_iota(jnp.int32, (ng, TM), 0)
    gmax = jnp.max(gscore, axis=0, keepdims=True)
    best_g = jnp.min(jnp.where(gscore == gmax, gidx, ng), axis=0, keepdims=True)
    in_best = gidx == best_g
    pick = lambda slab: jnp.sum(jnp.where(in_best, slab, 0.0), axis=0, keepdims=True)
    cand_sel = [pick(v) for v in sel4]
    cand_s = [pick(v) for v in s4]
    _, i1, _, i2 = _top2_rows(cand_sel)
    a1, a2 = cand_s[0], cand_s[0]
    for j in range(1, EXPERTS_PER_GROUP):
        a1 = jnp.where(i1 == j, cand_s[j], a1)
        a2 = jnp.where(i2 == j, cand_s[j], a2)
    den = a1 + a2
    w1, w2 = a1 / den, a2 / den
    swap = i2 < i1
    lo = jnp.where(swap, i2, i1)
    hi = jnp.where(swap, i1, i2)
    w_lo = jnp.where(swap, w2, w1)
    w_hi = jnp.where(swap, w1, w2)
    pair = jnp.where(lo == 0, 0, jnp.where(lo == 1, 3, 5)) + (hi - lo - 1)
    cls = best_g * len(PAIRS) + pair
    cidx = lax.broadcasted_iota(jnp.int32, (CLASS_ROWS, TM), 0)
    hit = cidx == cls
    onehot = jnp.where(hit, 1.0, 0.0)
    tri = jnp.where(lax.broadcasted_iota(jnp.int32, (TM, TM), 0) < lax.broadcasted_iota(jnp.int32, (TM, TM), 1),
                    1.0, 0.0).astype(BF16)
    rank = _dot(onehot.astype(BF16), tri) + carry_ref[:, 0:1]
    rk = jnp.sum(jnp.where(hit, rank, 0.0), axis=0, keepdims=True)
    carry_ref[...] = carry_ref[...] + jnp.sum(onehot, axis=1, keepdims=True)
    mi_ref[:, cols] = jnp.concatenate([cls, rk.astype(jnp.int32), jnp.zeros((6, TM), jnp.int32)], axis=0)
    mf_ref[:, cols] = jnp.concatenate([w_lo, w_hi, jnp.zeros((6, TM), F32)], axis=0)
    cnt_ref[...] = carry_ref[...]


def _post_norm_mod(x, y, m_ref, lng, lnb, gate_row, sh_row, sc_row):
    u = DN_ALPHA * x + m_ref[0, gate_row:gate_row + 1, :] * y
    mu = jnp.mean(u, axis=-1, keepdims=True)
    var = jnp.mean(jnp.square(u - mu), axis=-1, keepdims=True)
    xn = (u - mu) * lax.rsqrt(var + LN_EPS) * lng + lnb
    if sh_row is None:
        return xn, None
    return xn, xn * (1.0 + m_ref[0, sc_row:sc_row + 1, :]) + m_ref[0, sh_row:sh_row + 1, :]


def _outproj_kernel(*refs, layer0, tpb):
    nd = 8 if layer0 else 5
    halves = (refs[:nd], refs[nd:2 * nd])
    w_ref, lng_ref, lnb_ref, rw_ref, rb_ref, xo_ref, h2_ref, mi_ref, mf_ref, cnt_ref, carry_ref = refs[2 * nd:]
    i = pl.program_id(0)

    @pl.when(i == 0)
    def _():
        carry_ref[...] = jnp.zeros_like(carry_ref)

    for half, data in enumerate(halves):
        if layer0:
            of_ref, ob_ref, g_ref, hyl_ref, hyc_ref, xl_ref, xc_ref, m_ref = data
            is_ctx = (2 * i + half) % tpb == tpb - 1
            x = jnp.where(is_ctx, xc_ref[...], xl_ref[...])
            hy = jnp.where(is_ctx, hyc_ref[...], hyl_ref[...])
            o = of_ref[...].astype(F32) + ob_ref[...].astype(F32)
            parts = []
            for h in range(RET_HEADS):
                oh = o[:, h * RET_HD:(h + 1) * RET_HD]
                mu = jnp.mean(oh, axis=-1, keepdims=True)
                var = jnp.mean(jnp.square(oh - mu), axis=-1, keepdims=True)
                parts.append((oh - mu) * lax.rsqrt(var + LN_EPS))
            yret = (jnp.concatenate(parts, axis=-1) * _silu(g_ref[...].astype(F32))).astype(BF16)
            wr = yret.shape[1]
            y = _dot(yret, w_ref[:wr, :]) + _dot(hy, w_ref[wr:, :])
        else:
            of_ref, ob_ref, g_ref, x_ref, m_ref = data
            x = x_ref[...]
            o = of_ref[...].astype(F32) + ob_ref[...].astype(F32)
            parts = []
            for h in range(GLA_HEADS):
                oh = o[:, h * GLA_DV:(h + 1) * GLA_DV]
                parts.append(oh * lax.rsqrt(jnp.mean(jnp.square(oh), axis=-1, keepdims=True) + LN_EPS))
            a = (jnp.concatenate(parts, axis=-1) * _silu(g_ref[...].astype(F32))).astype(BF16)
            y = _dot(a, w_ref[...])
        rows = slice(half * TM, (half + 1) * TM)
        xn, h2 = _post_norm_mod(x, y, m_ref, lng_ref[...], lnb_ref[...], 2, 3, 4)
        xo_ref[rows, :] = xn
        _store_row_tiles(h2_ref, h2, TM, first=half * TM * SUBLANES)
        _route(h2, rw_ref, rb_ref, carry_ref, mi_ref, mf_ref, cnt_ref, rows)


def _outproj_call(layer0, tpb, n_tiles, d, half_specs, half_args, w, lng, lnb, router_w, router_bias):
    rows = n_tiles * TM
    full2 = lambda a: pl.BlockSpec(a.shape, lambda i: (0, 0))
    regroup = lambda a: jnp.swapaxes(a.reshape(N_GROUPS, EXPERTS_PER_GROUP, -1), 0, 1).reshape(N_EXPERTS, -1)
    rw_t = regroup(router_w.T)
    rb = regroup(router_bias.reshape(N_EXPERTS, 1))
    return pl.pallas_call(
        functools.partial(_outproj_kernel, layer0=layer0, tpb=tpb),
        out_shape=(jax.ShapeDtypeStruct((rows, d), F32),
                   jax.ShapeDtypeStruct((rows * SUBLANES, LANES), F32),
                   jax.ShapeDtypeStruct((8, rows), jnp.int32),
                   jax.ShapeDtypeStruct((8, rows), F32),
                   jax.ShapeDtypeStruct((CLASS_ROWS, LANES), F32)),
        grid=(n_tiles // 2,),
        in_specs=list(half_specs(0)) + list(half_specs(1)) + [full2(w), full2(lng), full2(lnb), full2(rw_t), full2(rb)],
        out_specs=(pl.BlockSpec((2 * TM, d), lambda i: (i, 0)),
                   pl.BlockSpec((2 * TM * SUBLANES, LANES), lambda i: (i, 0)),
                   pl.BlockSpec((8, 2 * TM), lambda i: (0, i)),
                   pl.BlockSpec((8, 2 * TM), lambda i: (0, i)),
                   pl.BlockSpec((CLASS_ROWS, LANES), lambda i: (0, 0))),
        scratch_shapes=[pltpu.VMEM((CLASS_ROWS, LANES), F32)],
        compiler_params=_cparams("arbitrary"),
        name="outproj_norm0" if layer0 else "outproj_norm1",
    )(*half_args, *half_args, w, lng, lnb, rw_t, rb)


def _dispatch_kernel(slots_ref, zoff_ref, h_ref, xs_ref, zero_ref, sem):
    i = pl.program_id(0)

    @pl.when(i == 0)
    def _():
        zero_ref[...] = jnp.zeros_like(zero_ref)
        def fill_class(e, go):
            @pl.when(zoff_ref[N_CLASSES + 1 + e] > 0)
            def _():
                go(pltpu.make_async_copy(zero_ref, _row_tile(xs_ref, zoff_ref[e], TE), sem))

        for e in range(N_CLASSES):
            fill_class(e, lambda cp: cp.start())
        for e in range(N_CLASSES):
            fill_class(e, lambda cp: cp.wait())
        first_free = zoff_ref[N_CLASSES]
        n_all = xs_ref.shape[0] // (TE * SUBLANES)

        def fill(t, carry):
            pltpu.make_async_copy(zero_ref, _row_tile(xs_ref, t, TE), sem).start()
            return carry

        def fill_wait(t, carry):
            pltpu.make_async_copy(zero_ref, _row_tile(xs_ref, t, TE), sem).wait()
            return carry

        lax.fori_loop(first_free, n_all, fill, 0)
        lax.fori_loop(first_free, n_all, fill_wait, 0)

    def row_copy(r):
        return pltpu.make_async_copy(_row_tile(h_ref, r), _row_tile(xs_ref, slots_ref[0, 0, r]), sem)

    for r in range(TM):
        row_copy(r).start(priority=r % 2)
    for r in range(TM):
        row_copy(r).wait()


def _dispatch(h2, slots3, zoff, p_rows):
    n_tiles = h2.shape[0] // (TM * SUBLANES)
    grid_spec = pltpu.PrefetchScalarGridSpec(
        num_scalar_prefetch=0,
        grid=(n_tiles,),
        in_specs=[pl.BlockSpec((1, 1, TM), lambda i: (i, 0, 0), memory_space=pltpu.SMEM),
                  pl.BlockSpec(memory_space=pltpu.SMEM),
                  pl.BlockSpec((TM * SUBLANES, LANES), lambda i: (i, 0))],
        out_specs=pl.BlockSpec(memory_space=pl.ANY),
        scratch_shapes=[pltpu.VMEM((TE * SUBLANES, LANES), F32), pltpu.SemaphoreType.DMA(())],
    )
    return pl.pallas_call(
        _dispatch_kernel,
        out_shape=jax.ShapeDtypeStruct((p_rows * SUBLANES, LANES), F32),
        grid_spec=grid_spec,
        compiler_params=_cparams("arbitrary"),
        name="moe_dispatch",
    )(slots3, zoff, h2)


def _experts_kernel(ea_ref, eb_ref, tv_ref, x_ref, wga_ref, wua_ref, wda_ref, wgb_ref, wub_ref, wdb_ref, y_ref):
    j = pl.program_id(0)

    @pl.when(tv_ref[j] > 0)
    def _():
        x = _load_row_tiles(x_ref, TE).astype(BF16)
        for which, (wg, wu, wd) in enumerate(((wga_ref, wua_ref, wda_ref), (wgb_ref, wub_ref, wdb_ref))):
            g = _dot(x, wg[0])
            u = _dot(x, wu[0])
            y = _dot((_silu(g) * u).astype(BF16), wd[0])
            _store_row_tiles(y_ref, y, TE, stride=2 * SUBLANES, first=which * SUBLANES)

    @pl.when(tv_ref[j] == 0)
    def _():
        y_ref[...] = jnp.zeros_like(y_ref)


def _experts(xs, tile_ea, tile_eb, tile_valid, wg, wu, wd):
    n_tiles = xs.shape[0] // (TE * SUBLANES)
    d, hdim = wg.shape[1], wg.shape[2]
    wa = lambda shp: pl.BlockSpec(shp, lambda j, ea, eb, tv: (ea[j], 0, 0))
    wb = lambda shp: pl.BlockSpec(shp, lambda j, ea, eb, tv: (eb[j], 0, 0))
    grid_spec = pltpu.PrefetchScalarGridSpec(
        num_scalar_prefetch=3,
        grid=(n_tiles,),
        in_specs=[pl.BlockSpec((TE * SUBLANES, LANES), lambda j, ea, eb, tv: (j, 0)),
                  wa((1, d, hdim)), wa((1, d, hdim)), wa((1, hdim, d)),
                  wb((1, d, hdim)), wb((1, d, hdim)), wb((1, hdim, d))],
        out_specs=pl.BlockSpec((2 * TE * SUBLANES, LANES), lambda j, ea, eb, tv: (j, 0)),
    )
    return pl.pallas_call(
        _experts_kernel,
        out_shape=jax.ShapeDtypeStruct((2 * xs.shape[0], LANES), F32),
        grid_spec=grid_spec,
        compiler_params=_cparams("arbitrary"),
        name="moe_experts",
    )(tile_ea, tile_eb, tile_valid, xs, wg, wu, wd, wg, wu, wd)


def _combine_kernel(slots_ref, next_slots_ref, ys_ref, wts_ref, x_ref, m_ref, lng_ref, lnb_ref, o_ref, buf_ref, sem):
    i = pl.program_id(0)
    cur = i % 2

    def row_copy(s_ref, b, r):
        return pltpu.make_async_copy(_row_tile(ys_ref, s_ref[0, 0, r], 2), _row_tile(buf_ref.at[b], r, 2), sem.at[b])

    def issue_all(s_ref, b):
        for r in range(TM):
            row_copy(s_ref, b, r).start(priority=r % 2)

    @pl.when(i == 0)
    def _():
        issue_all(slots_ref, 0)

    @pl.when(i + 1 < pl.num_programs(0))
    def _():
        issue_all(next_slots_ref, 1 - cur)

    for r in range(TM):
        row_copy(slots_ref, cur, r).wait()
    buf = buf_ref.at[cur]
    y_lo = _load_row_tiles(buf, TM, stride=2 * SUBLANES)
    y_hi = _load_row_tiles(buf, TM, stride=2 * SUBLANES, first=SUBLANES)
    y = wts_ref[:, 0:1] * y_lo + wts_ref[:, 1:2] * y_hi
    xn, _ = _post_norm_mod(x_ref[...], y, m_ref, lng_ref[...], lnb_ref[...], 5, None, None)
    o_ref[...] = xn


def _combine(ys, slots3, wts, x, mods, lng, lnb, cond_of, n_tiles):
    d = x.shape[1]
    grid_spec = pltpu.PrefetchScalarGridSpec(
        num_scalar_prefetch=0,
        grid=(n_tiles,),
        in_specs=[pl.BlockSpec((1, 1, TM), lambda i: (i, 0, 0), memory_space=pltpu.SMEM),
                  pl.BlockSpec((1, 1, TM), lambda i: (jnp.minimum(i + 1, n_tiles - 1), 0, 0),
                               memory_space=pltpu.SMEM),
                  pl.BlockSpec(memory_space=pl.ANY),
                  pl.BlockSpec((TM, 2), lambda i: (i, 0)),
                  pl.BlockSpec((TM, d), lambda i: (i, 0)),
                  pl.BlockSpec((1, 6, d), lambda i: (cond_of(i), 0, 0)),
                  pl.BlockSpec((1, d), lambda i: (0, 0)),
                  pl.BlockSpec((1, d), lambda i: (0, 0))],
        out_specs=pl.BlockSpec((TM, d), lambda i: (i, 0)),
        scratch_shapes=[pltpu.VMEM((2, 2 * TM * SUBLANES, LANES), F32), pltpu.SemaphoreType.DMA((2,))],
    )
    return pl.pallas_call(
        _combine_kernel,
        out_shape=jax.ShapeDtypeStruct((n_tiles * TM, d), F32),
        grid_spec=grid_spec,
        compiler_params=_cparams("arbitrary"),
        name="moe_combine",
    )(slots3, slots3, ys, wts, x, mods, lng, lnb)


def _moe(h2, mi, mf, cnt, x, cond_of, mods, lng, lnb, layer, wg, wu, wd):
    t = x.shape[0]
    n_tiles = t // TM
    counts = cnt[:N_CLASSES, 0].astype(jnp.int32)
    padded = ((counts + TE - 1) // TE) * TE
    ends = jnp.cumsum(padded)
    offs = ends - padded
    cls = mi[0]
    cids = jnp.arange(N_CLASSES, dtype=jnp.int32)[:, None]
    slots = jnp.sum(jnp.where(cls[None] == cids, offs[:, None], 0), axis=0) + mi[1]
    slots3 = slots.reshape(n_tiles, 1, TM)
    n_cls_tiles = t // TE + N_CLASSES
    total_tiles = ends[-1] // TE
    tile_ids = jnp.arange(n_cls_tiles, dtype=jnp.int32)
    tile_valid = (tile_ids < total_tiles).astype(jnp.int32)
    tile_last = jnp.minimum(tile_ids, total_tiles - 1)
    tile_class = jnp.sum((tile_last[:, None] * TE >= ends[None, :]).astype(jnp.int32), axis=1)
    tile_class = jnp.minimum(tile_class, N_CLASSES - 1)
    pair_lo = jnp.asarray([p[0] for p in PAIRS], jnp.int32)
    pair_hi = jnp.asarray([p[1] for p in PAIRS], jnp.int32)
    pair_sel = (tile_class % len(PAIRS))[:, None] == jnp.arange(len(PAIRS), dtype=jnp.int32)[None, :]
    group0 = (tile_class // len(PAIRS)) * EXPERTS_PER_GROUP
    tile_ea = (group0 + jnp.sum(jnp.where(pair_sel, pair_lo[None, :], 0), axis=1)).astype(jnp.int32)
    tile_eb = (group0 + jnp.sum(jnp.where(pair_sel, pair_hi[None, :], 0), axis=1)).astype(jnp.int32)
    zoff = jnp.concatenate([jnp.maximum(ends // TE - 1, 0), total_tiles[None], padded]).astype(jnp.int32)
    xs = _dispatch(h2, slots3, zoff, n_cls_tiles * TE)
    ys = _experts(xs, tile_ea + layer * N_EXPERTS, tile_eb + layer * N_EXPERTS, tile_valid, wg, wu, wd)
    wts = jnp.transpose(mf[0:2])
    return _combine(ys, slots3, wts, x, mods, lng, lnb, cond_of, n_tiles)


def _chunk_cumsum(x, tri):
    hi = x.astype(BF16)
    r1 = x - hi.astype(F32)
    mid = r1.astype(BF16)
    lo = (r1 - mid.astype(F32)).astype(BF16)
    return _dot(tri, hi) + _dot(tri, mid) + _dot(tri, lo)


def _gla_kernel(*refs, nb):
    f_in, b_in, (of_ref, ob_ref, sf_ref, sb_ref) = refs[:6], refs[6:12], refs[12:]

    @pl.when(pl.program_id(0) == 0)
    def _():
        sf_ref[...] = jnp.zeros_like(sf_ref)
        sb_ref[...] = jnp.zeros_like(sb_ref)

    _gla_dir(*f_in, of_ref, sf_ref, False, nb)
    _gla_dir(*b_in, ob_ref, sb_ref, True, nb)


def _gla_dir(q_ref, k_ref, v_ref, lr_ref, gw_ref, gb_ref, o_ref, s_ref, reverse, nb):
    C = GLA_CHUNK
    nchunk = TM // C
    ii = lax.broadcasted_iota(jnp.int32, (C, C), 0)
    jj = lax.broadcasted_iota(jnp.int32, (C, C), 1)
    keep = (jj >= ii) if reverse else (jj <= ii)
    tri = jnp.where(keep, 1.0, 0.0).astype(BF16)
    order = tuple(reversed(range(nchunk))) if reverse else tuple(range(nchunk))
    end_row = 0 if reverse else C - 1
    gw = gw_ref[...]
    gb = gb_ref[...]
    for b in range(nb):
        pre = _dot(lr_ref[b], gw) + gb
        la_all = (jnp.minimum(pre, 0.0) - jnp.log(1.0 + jnp.exp(-jnp.abs(pre)))) * (1.0 / GLA_TAU)
        for c in order:
            rows = slice(c * C, (c + 1) * C)
            bc = _chunk_cumsum(la_all[rows, :], tri)
            bend = bc[end_row:end_row + 1, :]
            eb = jnp.exp(bc)
            enb = jnp.exp(-bc)
            ekb = jnp.exp(bend - bc)
            ebend = jnp.exp(bend)
            for h in range(GLA_HEADS):
                kc = slice(h * GLA_DK, (h + 1) * GLA_DK)
                vc = slice(h * GLA_DV, (h + 1) * GLA_DV)
                q = q_ref[b, rows, kc].astype(F32) * (GLA_DK ** -0.5)
                k = k_ref[b, rows, kc].astype(F32)
                v = v_ref[b, rows, vc]
                qd = (q * eb[:, kc]).astype(BF16)
                kd = (k * enb[:, kc]).astype(BF16)
                st = s_ref[b, h]
                scores = jnp.where(keep, _dot_nt(qd, kd), 0.0)
                o = _dot(scores.astype(BF16), v) + _dot_nt(qd, st.astype(BF16))
                s_ref[b, h] = st * ebend[:, kc] + _dot_tn(v, (k * ekb[:, kc]).astype(BF16))
                o_ref[b, rows, vc] = o.astype(BF16)


def _gla(z3, gw_f, gb_f, gw_b, gb_b):
    nb, s, _ = z3.shape
    nt = s // TM
    kw = GLA_HEADS * GLA_DK
    vw = GLA_HEADS * GLA_DV
    lr_blk = (2 * kw + 2 * vw) // LANES

    def dir_specs(jmap):
        return [pl.BlockSpec((nb, TM, kw), lambda j: (0, jmap(j), 0)),
                pl.BlockSpec((nb, TM, kw), lambda j: (0, jmap(j), 1)),
                pl.BlockSpec((nb, TM, vw), lambda j: (0, jmap(j), 1)),
                pl.BlockSpec((nb, TM, LANES), lambda j: (0, jmap(j), lr_blk)),
                pl.BlockSpec((LANES, kw), lambda j: (0, 0)),
                pl.BlockSpec((1, kw), lambda j: (0, 0))]

    fwd, bwd = _scan_tile_orders(nt)
    state = pltpu.VMEM((nb, GLA_HEADS, GLA_DV, GLA_DK), F32)
    out = jax.ShapeDtypeStruct((nb, s, vw), BF16)
    return pl.pallas_call(
        functools.partial(_gla_kernel, nb=nb),
        out_shape=(out, out),
        grid=(nt,),
        in_specs=dir_specs(fwd) + dir_specs(bwd),
        out_specs=(pl.BlockSpec((nb, TM, vw), lambda j: (0, fwd(j), 0)),
                   pl.BlockSpec((nb, TM, vw), lambda j: (0, bwd(j), 0))),
        scratch_shapes=[state, state],
        compiler_params=_cparams("arbitrary"),
        name="gla",
    )(z3, z3, z3, z3, gw_f, gb_f, z3, z3, z3, z3, gw_b, gb_b)


def kernel(x, c, ctx, c_ctx, mod_w, mod_b, ln_g, ln_b, ab_w_in, ret_log_decay_f, ret_log_decay_b, hy_conv_w, hy_conv_b, hy_w1, hy_b1, hy_w2, hy_b2, hy_w3, hy_b3, hy_w4, hy_freq, hy_skip, ab_w_out, gla_w_in, gla_gate_w_f, gla_gate_b_f, gla_gate_w_b, gla_gate_b_b, gla_w_out, router_w, router_bias, exp_w_gate, exp_w_up, exp_w_down):
    nb, length, d = x.shape
    ctx_len = ctx.shape[1]
    assert nb == 2 and ctx_len == TM and length % TM == 0 and length % (FFT_N2 * 2) == 0
    s = length + ctx_len
    tpb = s // TM
    r = nb * s
    lat_tiles = tpb - 1

    cond = jnp.zeros((8, d), F32).at[0].set(c[0]).at[1].set(c_ctx).at[2].set(c[1]).at[3].set(c_ctx)
    mods = _ada(cond, mod_w, mod_b).reshape(DEPTH, 8, 6, d)

    def cond_of(i):
        return 2 * (i // tpb) + jnp.where(i % tpb == tpb - 1, 1, 0)

    x_lat = x.reshape(nb * length, d)
    x_ctx = ctx.reshape(nb * ctx_len, d)

    ret_w = RET_HEADS * RET_HD
    z0, *hy_lat, hy_ctx = _inproj0(x_lat, x_ctx, mods[0], ab_w_in[0].astype(BF16), tpb, r, 4 * ret_w)
    z0_3 = z0.reshape(nb, s, -1)
    cosf, sinf = _rope_tables(length, ctx_len)
    o_f, o_b = _retention(z0_3, ret_log_decay_f[0], ret_log_decay_b[0], cosf, sinf)

    cw3 = jnp.concatenate([hy_conv_w[0].T, hy_conv_b[0][:, None]], axis=1).reshape(3, HY_WIDTH, 4)
    na = length // FFT_N2
    tb = _dft_tables(na)
    filt_w = (hy_w1[0], hy_b1[0], hy_w2[0], hy_b2[0], hy_w3[0], hy_b3[0], hy_w4[0], hy_freq[0])
    skip = hy_skip[0]

    filt, ss = _hyena_filters_raw(length, *filt_w)
    filt5 = filt.reshape(2, 2, HY_WIDTH, na, FFT_N2)
    ss5 = ss.reshape(2, 2, HY_WIDTH, 1, 1)
    hy_lat = [a.reshape(nb, length, HY_WIDTH) for a in hy_lat]
    zcur = _to_freq_layout(hy_lat[2], na)
    for n in range(2):
        hspec = _spectrum(filt5, ss5, n, tb, na)
        gate = _to_freq_layout(hy_lat[n], na)
        zcur = _fftconv(zcur, gate, hspec, skip[n].reshape(-1, 1, 1), cw3[2].reshape(-1, 1, 4),
                        cw3[n].reshape(-1, 1, 4), tb, na, conv_u=(n == 0))
    y_hy_lat = _from_freq_layout(zcur)

    filt_c, ss_c = _hyena_filters_raw(ctx_len, *filt_w)
    filt_c = filt_c.reshape(2, 2, HY_WIDTH, ctx_len)
    ss_c = ss_c.reshape(2, 2, HY_WIDTH, 1)
    u_ctx = jnp.transpose(hy_ctx.reshape(nb, ctx_len, 3, HY_WIDTH), (2, 3, 0, 1))
    y_ctx_t = _ctxconv(u_ctx.reshape(3, HY_WIDTH, nb * ctx_len), cw3, filt_c[:, 0], filt_c[:, 1], ss_c[:, 0],
                       ss_c[:, 1], skip.reshape(2, HY_WIDTH, 1), _ctx_dft_tables(ctx_len))
    y_hy_ctx = jnp.transpose(y_ctx_t.reshape(HY_WIDTH, nb, ctx_len), (1, 2, 0))

    gate_blk = 3

    def specs0(h):
        tile = lambda i: 2 * i + h
        trow = lambda i: (tile(i), 0)
        return [pl.BlockSpec((TM, ret_w), trow), pl.BlockSpec((TM, ret_w), trow),
                pl.BlockSpec((TM, ret_w), lambda i: (tile(i), gate_blk)),
                *_lat_or_ctx_specs(HY_WIDTH, tpb, tile), *_lat_or_ctx_specs(d, tpb, tile),
                pl.BlockSpec((1, 6, d), lambda i: (cond_of(tile(i)), 0, 0))]

    x_mid, h2, mi, mf, cnt = _outproj_call(
        True, tpb, r // TM, d, specs0,
        (o_f.reshape(r, ret_w), o_b.reshape(r, ret_w), z0, y_hy_lat.reshape(nb * length, HY_WIDTH),
         y_hy_ctx.reshape(nb * ctx_len, HY_WIDTH), x_lat, x_ctx, mods[0]),
        ab_w_out[0].astype(BF16), ln_g[0, 0].reshape(1, d), ln_b[0, 0].reshape(1, d), router_w, router_bias)

    stack = lambda w: w.astype(BF16).reshape((DEPTH * N_EXPERTS,) + w.shape[2:])
    exp_w = (stack(exp_w_gate), stack(exp_w_up), stack(exp_w_down))
    x1 = _moe(h2, mi, mf, cnt, x_mid, cond_of, mods[0], ln_g[0, 1].reshape(1, d), ln_b[0, 1].reshape(1, d),
              0, *exp_w)

    kw = GLA_HEADS * GLA_DK
    vw = GLA_HEADS * GLA_DV
    n_in = gla_w_in.shape[2]
    n_pad = 2 * kw + 2 * vw + LANES
    w_in1 = jnp.pad(gla_w_in[0], ((0, 0), (0, n_pad - n_in))).astype(BF16)
    z1 = _inproj(x1, mods[1], w_in1, tpb)
    z1_3 = z1.reshape(nb, s, n_pad)
    gw_f = jnp.zeros((LANES, kw), F32).at[:GLA_RANK].set(gla_gate_w_f[0]).astype(BF16)
    gw_b = jnp.zeros((LANES, kw), F32).at[GLA_RANK:2 * GLA_RANK].set(gla_gate_w_b[0]).astype(BF16)
    g_f, g_b = _gla(z1_3, gw_f, gla_gate_b_f[0].reshape(1, kw), gw_b, gla_gate_b_b[0].reshape(1, kw))

    n_lat = nb * lat_tiles

    def lat_tile(i):
        return (i // lat_tiles) * tpb + i % lat_tiles

    def lat_cond(i):
        return 2 * (i // lat_tiles)

    def specs1(h):
        lrow = lambda i: (lat_tile(2 * i + h), 0)
        return [pl.BlockSpec((TM, vw), lrow), pl.BlockSpec((TM, vw), lrow),
                pl.BlockSpec((TM, vw), lambda i: (lat_tile(2 * i + h), 2)),
                pl.BlockSpec((TM, d), lrow),
                pl.BlockSpec((1, 6, d), lambda i: (lat_cond(2 * i + h), 0, 0))]

    x_mid1, h2_1, mi1, mf1, cnt1 = _outproj_call(
        False, tpb, n_lat, d, specs1, (g_f.reshape(r, vw), g_b.reshape(r, vw), z1, x1, mods[1]),
        gla_w_out[0].astype(BF16), ln_g[1, 0].reshape(1, d), ln_b[1, 0].reshape(1, d), router_w, router_bias)

    out = _moe(h2_1, mi1, mf1, cnt1, x_mid1, lat_cond, mods[1], ln_g[1, 1].reshape(1, d), ln_b[1, 1].reshape(1, d),
               1, *exp_w)
    return out.reshape(nb, length, d)
```

```python
import functools
import math

import numpy as np
import jax
import jax.numpy as jnp
from jax import lax
from jax.experimental import pallas as pl
from jax.experimental.pallas import tpu as pltpu

F32 = jnp.float32
BF16 = jnp.bfloat16

GRID_W = 64
RET_HEADS = 4
RET_HD = 128
RET_CHUNK = 128
ROPE_BASE = 10000.0
HY_WIDTH = 512
HY_EMB = 33
HY_FFN = 64
HY_SHORT_DECAY_PCT = 0.3
HY_LONG_DECAY_PCT = 1.5
HY_TARGET = 1e-2
GLA_HEADS = 4
GLA_DK = 128
GLA_DV = 256
GLA_RANK = 16
GLA_TAU = 16.0
GLA_CHUNK = 64
N_EXPERTS = 16
N_GROUPS = 4
EXPERTS_PER_GROUP = 4
LN_EPS = 1e-5
DEPTH = 2
DN_ALPHA = (2.0 * DEPTH) ** 0.25

PAIRS = ((0, 1), (0, 2), (0, 3), (1, 2), (1, 3), (2, 3))
N_CLASSES = N_GROUPS * len(PAIRS)
CLASS_ROWS = 32

LANES = 128
SUBLANES = 8
TM = 256
TE = 256
FFT_N2 = 128
FFT_CB = 16
VMEM_LIMIT = 48 * 1024 * 1024


def _cparams(*sem):
    return pltpu.CompilerParams(dimension_semantics=sem, vmem_limit_bytes=VMEM_LIMIT)


def _silu(v):
    return v * (1.0 / (1.0 + jnp.exp(-v)))


def _dot(a, b):
    return jnp.dot(a, b, preferred_element_type=F32)


def _dot_nt(a, b):
    return lax.dot_general(a, b, (((1,), (1,)), ((), ())), preferred_element_type=F32)


def _dot_tn(a, b):
    return lax.dot_general(a, b, (((0,), (0,)), ((), ())), preferred_element_type=F32)


def _split2(a):
    hi = a.astype(BF16)
    return hi, (a - hi.astype(F32)).astype(BF16)


def _dot3(a, b, dot=_dot):
    ah, al = _split2(a)
    bh, bl = _split2(b)
    return dot(ah, bh) + dot(ah, bl) + dot(al, bh)


def _ada_kernel(c_ref, w_ref, b_ref, o_ref):
    a = _silu(c_ref[...])
    o_ref[0] = _dot3(a, w_ref[0]) + b_ref[0]


def _ada(cond, mod_w, mod_b):
    depth, d, n = mod_w.shape
    nt = n // 4
    return pl.pallas_call(
        _ada_kernel,
        out_shape=jax.ShapeDtypeStruct((depth, 8, n), F32),
        grid=(depth, n // nt),
        in_specs=[pl.BlockSpec((8, d), lambda l, j: (0, 0)),
                  pl.BlockSpec((1, d, nt), lambda l, j: (l, 0, j)),
                  pl.BlockSpec((1, 1, nt), lambda l, j: (l, 0, j))],
        out_specs=pl.BlockSpec((1, 8, nt), lambda l, j: (l, 0, j)),
        compiler_params=_cparams("arbitrary", "arbitrary"),
        name="ada_mod",
    )(cond, mod_w, mod_b.reshape(depth, 1, n))


def _inproj_kernel(x_ref, ma_ref, mb_ref, w_ref, o_ref, *, chunks):
    hs = []
    for half, m_ref in enumerate((ma_ref, mb_ref)):
        x = x_ref[half * TM:(half + 1) * TM, :]
        hs.append((x * (1.0 + m_ref[0, 1:2, :]) + m_ref[0, 0:1, :]).astype(BF16))
    h = jnp.concatenate(hs, axis=0)
    for lo, hi in chunks:
        o_ref[:, lo:hi] = _dot(h, w_ref[:, lo:hi]).astype(BF16)


def _lat_or_ctx_specs(width, tpb, tile_of=lambda i: i):
    lat_tiles = tpb - 1
    lat = lambda i: ((tile_of(i) // tpb) * lat_tiles + jnp.minimum(tile_of(i) % tpb, lat_tiles - 1), 0)
    ctx = lambda i: (tile_of(i) // tpb, 0)
    return pl.BlockSpec((TM, width), lat), pl.BlockSpec((TM, width), ctx)


def _col_chunks(n, width=512):
    out, lo = [], 0
    while lo < n:
        hi = min(lo + width, n)
        out.append((lo, hi))
        lo = hi
    return tuple(out)


def _inproj(xu, mods, w, tiles_per_batch):
    r, d = xu.shape
    n = w.shape[1]
    tpb = tiles_per_batch

    def cond_of(i):
        return 2 * (i // tpb) + jnp.where(i % tpb == tpb - 1, 1, 0)

    return pl.pallas_call(
        functools.partial(_inproj_kernel, chunks=_col_chunks(n)),
        out_shape=jax.ShapeDtypeStruct((r, n), BF16),
        grid=(r // (2 * TM),),
        in_specs=[pl.BlockSpec((2 * TM, d), lambda i: (i, 0)),
                  pl.BlockSpec((1, 6, d), lambda i: (cond_of(2 * i), 0, 0)),
                  pl.BlockSpec((1, 6, d), lambda i: (cond_of(2 * i + 1), 0, 0)),
                  pl.BlockSpec((d, n), lambda i: (0, 0))],
        out_specs=pl.BlockSpec((2 * TM, n), lambda i: (i, 0)),
        compiler_params=_cparams("arbitrary"),
        name="inproj1",
    )(xu, mods, mods, w)


def _inproj0_kernel(x_ref, c_ref, m_ref, w_ref, z_ref, x1_ref, x2_ref, v_ref, hc_ref, *, tpb, ret_w):
    is_ctx = pl.program_id(0) % tpb == tpb - 1
    x = jnp.where(is_ctx, c_ref[...], x_ref[...])
    h = (x * (1.0 + m_ref[0, 1:2, :]) + m_ref[0, 0:1, :]).astype(BF16)
    for lo, hi in _col_chunks(ret_w):
        z_ref[:, lo:hi] = _dot(h, w_ref[:, lo:hi]).astype(BF16)
    hy = [_dot(h, w_ref[:, ret_w + j * HY_WIDTH:ret_w + (j + 1) * HY_WIDTH]).astype(BF16) for j in range(3)]

    @pl.when(jnp.logical_not(is_ctx))
    def _():
        for ref, val in zip((x1_ref, x2_ref, v_ref), hy):
            ref[...] = val

    @pl.when(is_ctx)
    def _():
        for j, val in enumerate(hy):
            hc_ref[:, j * HY_WIDTH:(j + 1) * HY_WIDTH] = val


def _inproj0(x_lat, x_ctx, mods, w, tpb, r, ret_w):
    d = w.shape[0]
    lat_tiles = tpb - 1

    def cond_of(i):
        return 2 * (i // tpb) + jnp.where(i % tpb == tpb - 1, 1, 0)

    lat_spec, ctx_spec = _lat_or_ctx_specs(d, tpb)
    lat_out = pl.BlockSpec((TM, HY_WIDTH), lambda i: ((i // tpb) * lat_tiles + jnp.minimum(i % tpb, lat_tiles - 1), 0))
    hy_lat = jax.ShapeDtypeStruct(((r // TM // tpb) * lat_tiles * TM, HY_WIDTH), BF16)
    return pl.pallas_call(
        functools.partial(_inproj0_kernel, tpb=tpb, ret_w=ret_w),
        out_shape=(jax.ShapeDtypeStruct((r, ret_w), BF16), hy_lat, hy_lat, hy_lat,
                   jax.ShapeDtypeStruct(((r // TM // tpb) * TM, 3 * HY_WIDTH), BF16)),
        grid=(r // TM,),
        in_specs=[lat_spec, ctx_spec, pl.BlockSpec((1, 6, d), lambda i: (cond_of(i), 0, 0)),
                  pl.BlockSpec(w.shape, lambda i: (0, 0))],
        out_specs=(pl.BlockSpec((TM, ret_w), lambda i: (i, 0)), lat_out, lat_out, lat_out,
                   pl.BlockSpec((TM, 3 * HY_WIDTH), lambda i: (i // tpb, 0))),
        compiler_params=_cparams("arbitrary"),
        name="inproj0",
    )(x_lat, x_ctx, mods, w)


def _ret_kernel(lgf_ref, lgb_ref, *refs, nb):
    f_in, b_in, (of_ref, ob_ref, sf_ref, sb_ref) = refs[:5], refs[5:10], refs[10:]

    @pl.when(pl.program_id(0) == 0)
    def _():
        sf_ref[...] = jnp.zeros_like(sf_ref)
        sb_ref[...] = jnp.zeros_like(sb_ref)

    _ret_dir(lgf_ref, *f_in, of_ref, sf_ref, False, nb)
    _ret_dir(lgb_ref, *b_in, ob_ref, sb_ref, True, nb)


def _ret_dir(lg_ref, q_ref, k_ref, v_ref, cos_ref, sin_ref, o_ref, s_ref, reverse, nb):
    C = RET_CHUNK
    ii = lax.broadcasted_iota(jnp.int32, (C, C), 0).astype(F32)
    jj = lax.broadcasted_iota(jnp.int32, (C, C), 1).astype(F32)
    ci = lax.broadcasted_iota(jnp.int32, (C, 1), 0).astype(F32)
    diff = (jj - ii) if reverse else (ii - jj)
    order = (1, 0) if reverse else (0, 1)
    for h in range(RET_HEADS):
        lg = lg_ref[h]
        intra = jnp.where(diff >= 0, jnp.exp(jnp.maximum(diff, 0.0) * lg), 0.0)
        if reverse:
            q_dec = jnp.exp((C - ci) * lg)
            k_dec = jnp.exp(ci * lg)
        else:
            q_dec = jnp.exp((ci + 1.0) * lg)
            k_dec = jnp.exp((C - 1.0 - ci) * lg)
        c_dec = jnp.exp(jnp.zeros((1, RET_HD), F32) + C * lg)
        cols = slice(h * RET_HD, (h + 1) * RET_HD)
        for b in range(nb):
            for c in order:
                rows = slice(c * C, (c + 1) * C)
                cosf = cos_ref[rows, :]
                sinf = sin_ref[rows, :]
                q = q_ref[b, rows, cols].astype(F32)
                k = k_ref[b, rows, cols].astype(F32)
                v = v_ref[b, rows, cols]
                q = q * cosf + pltpu.roll(q, RET_HD // 2, 1) * sinf
                k = (k * cosf + pltpu.roll(k, RET_HD // 2, 1) * sinf) * (RET_HD ** -0.5)
                st = s_ref[b, h]
                scores = _dot_nt(q.astype(BF16), k.astype(BF16)) * intra
                o = _dot(scores.astype(BF16), v) + _dot_nt((q * q_dec).astype(BF16), st.astype(BF16))
                s_ref[b, h] = c_dec * st + _dot_tn(v, (k * k_dec).astype(BF16))
                o_ref[b, rows, cols] = o.astype(BF16)


def _scan_tile_orders(nt):
    fwd = lambda j: (j + nt - 1) % nt
    bwd = lambda j: jnp.where(j == 0, nt - 1, nt - 1 - j)
    return fwd, bwd


def _retention(z3, lg_f, lg_b, cosf, sinf):
    nb, s, _ = z3.shape
    nt = s // TM
    w = RET_HEADS * RET_HD
    smem = pl.BlockSpec(memory_space=pltpu.SMEM)

    def dir_specs(jmap):
        return [pl.BlockSpec((nb, TM, w), lambda j: (0, jmap(j), 0)),
                pl.BlockSpec((nb, TM, w), lambda j: (0, jmap(j), 1)),
                pl.BlockSpec((nb, TM, w), lambda j: (0, jmap(j), 2)),
                pl.BlockSpec((TM, RET_HD), lambda j: (jmap(j), 0)),
                pl.BlockSpec((TM, RET_HD), lambda j: (jmap(j), 0))]

    fwd, bwd = _scan_tile_orders(nt)
    state = pltpu.VMEM((nb, RET_HEADS, RET_HD, RET_HD), F32)
    out = jax.ShapeDtypeStruct((nb, s, w), BF16)
    grid_spec = pltpu.PrefetchScalarGridSpec(
        num_scalar_prefetch=0,
        grid=(nt,),
        in_specs=[smem, smem] + dir_specs(fwd) + dir_specs(bwd),
        out_specs=(pl.BlockSpec((nb, TM, w), lambda j: (0, fwd(j), 0)),
                   pl.BlockSpec((nb, TM, w), lambda j: (0, bwd(j), 0))),
        scratch_shapes=[state, state],
    )
    return pl.pallas_call(
        functools.partial(_ret_kernel, nb=nb),
        out_shape=(out, out),
        grid_spec=grid_spec,
        compiler_params=_cparams("arbitrary"),
        name="retention",
    )(lg_f, lg_b, z3, z3, z3, cosf, sinf, z3, z3, z3, cosf, sinf)


def _rope_tables(length, ctx_len):
    rows = length // GRID_W
    quarter = RET_HD // 4
    inv = ROPE_BASE ** (-jnp.arange(quarter, dtype=F32) / quarter)
    def expand(fn):
        by_row = fn(jnp.arange(rows, dtype=F32)[:, None] * inv)
        by_col = fn(jnp.arange(GRID_W, dtype=F32)[:, None] * inv)
        by_row = jnp.broadcast_to(by_row[:, None, :], (rows, GRID_W, quarter)).reshape(length, quarter)
        by_col = jnp.broadcast_to(by_col[None, :, :], (rows, GRID_W, quarter)).reshape(length, quarter)
        return jnp.concatenate([by_row, by_col], axis=-1)

    cos, sin = expand(jnp.cos), expand(jnp.sin)
    cosf = jnp.concatenate([cos, cos], axis=-1)
    sinf = jnp.concatenate([-sin, sin], axis=-1)
    cosf = jnp.concatenate([cosf, jnp.ones((ctx_len, RET_HD), F32)], axis=0)
    sinf = jnp.concatenate([sinf, jnp.zeros((ctx_len, RET_HD), F32)], axis=0)
    return cosf, sinf


def _shortconv_freq(x, cw, na):
    row = lax.broadcasted_iota(jnp.int32, (1, FFT_N2, 1), 1)
    lane = lax.broadcasted_iota(jnp.int32, (1, 1, 2 * na), 2)
    wrap_prev = pltpu.roll(x[:, FFT_N2 - 1:FFT_N2, :], 1, 2)
    prev = jnp.where(row == 0, jnp.where((lane == 0) | (lane == na), 0.0, wrap_prev), pltpu.roll(x, 1, 1))
    wrap_next = pltpu.roll(x[:, 0:1, :], 2 * na - 1, 2)
    nxt = jnp.where(row == FFT_N2 - 1, jnp.where((lane == na - 1) | (lane == 2 * na - 1), 0.0, wrap_next),
                    pltpu.roll(x, FFT_N2 - 1, 1))
    return cw[..., 0:1] * prev + cw[..., 1:2] * x + cw[..., 2:3] * nxt + cw[..., 3:4]


def _shortconv_lanes(x, cw, n):
    lane = lax.broadcasted_iota(jnp.int32, (1, 2 * n), 1)
    prev = jnp.where((lane == 0) | (lane == n), 0.0, pltpu.roll(x, 1, 1))
    nxt = jnp.where((lane == n - 1) | (lane == 2 * n - 1), 0.0, pltpu.roll(x, 2 * n - 1, 1))
    return cw[:, 0:1] * prev + cw[:, 1:2] * x + cw[:, 2:3] * nxt + cw[:, 3:4]


FEAT_ROWS = 40


def _filter_kernel(w1_ref, b1_ref, w2_ref, b2_ref, w3_ref, b3_ref, w4_ref, fr_ref, dl_ref,
                   f_ref, ss_ref, *, length, pb):
    i = pl.program_id(0)
    pos = (lax.broadcasted_iota(jnp.int32, (1, pb), 1) + i * pb).astype(F32)
    t = pos * (1.0 / (length - 1))
    bands = (HY_EMB - 1) // 2
    w = (2.0 * math.pi) * pos / length
    sub = lax.broadcasted_iota(jnp.int32, (FEAT_ROWS, 1), 0)
    band = jnp.where(sub <= bands, sub - 1, sub - 1 - bands).astype(F32)
    f = 1e-4 + band * ((bands - 1 - 1e-4) / (bands - 1))
    fw = f * w
    feats = jnp.where(sub == 0, t, jnp.where(sub <= bands, jnp.cos(fw),
                                             jnp.where(sub <= 2 * bands, -jnp.sin(fw), 0.0)))
    feats = jnp.concatenate([feats, jnp.zeros((LANES - FEAT_ROWS, pb), F32)], axis=0)
    hdot = _dot3
    a = jnp.sin(fr_ref[:, 0:1] * (hdot(w1_ref[...], feats) + b1_ref[...]))
    a = jnp.sin(fr_ref[:, 1:2] * (hdot(w2_ref[...], a) + b2_ref[...]))
    a = jnp.sin(fr_ref[:, 2:3] * (hdot(w3_ref[...], a) + b3_ref[...]))

    @pl.when(i == 0)
    def _():
        ss_ref[...] = jnp.zeros_like(ss_ref)

    nout = w4_ref.shape[0]
    window = jnp.exp(-dl_ref[...] * t)
    a_bf = a.astype(BF16)
    for cb in range(nout // HY_WIDTH):
        rows = slice(cb * HY_WIDTH, (cb + 1) * HY_WIDTH)
        filt = _dot(w4_ref[rows, :].astype(BF16), a_bf) * window
        for j in range(pb // LANES):
            f_ref[rows, j, :] = filt[:, j * LANES:(j + 1) * LANES]
        ss_ref[rows, :] += jnp.sum(filt * filt, axis=1, keepdims=True)


def _hyena_filters_raw(length, w1, b1, w2, b2, w3, b3, w4, freq):
    nout = w4.shape[1]
    pb = min(length, 1024)
    max_decay = math.log(HY_TARGET) / HY_SHORT_DECAY_PCT
    min_decay = math.log(HY_TARGET) / HY_LONG_DECAY_PCT
    deltas = jnp.abs(jnp.linspace(min_decay, max_decay, HY_WIDTH, dtype=F32))
    dl = deltas.reshape(HY_WIDTH, 1)
    w1p = jnp.pad(w1.T, ((0, 0), (0, LANES - w1.shape[0])))
    col = lambda b: b.reshape(-1, 1)
    full = lambda a: pl.BlockSpec(a.shape, lambda i: tuple(0 for _ in a.shape))
    args = (w1p, col(b1), w2.T, col(b2), w3.T, col(b3), w4.T, freq.T, dl)
    return pl.pallas_call(
        functools.partial(_filter_kernel, length=length, pb=pb),
        out_shape=(jax.ShapeDtypeStruct((nout, length // LANES, LANES), F32),
                   jax.ShapeDtypeStruct((nout, 1), F32)),
        grid=(length // pb,),
        in_specs=[full(a) for a in args],
        out_specs=(pl.BlockSpec((nout, pb // LANES, LANES), lambda i: (0, i, 0)),
                   pl.BlockSpec((nout, 1), lambda i: (0, 0))),
        compiler_params=_cparams("arbitrary"),
        name="hyena_filter_mlp",
    )(*args)


def _dft_tables(na):
    k1n = 2 * na
    n = k1n * FFT_N2
    n1 = np.arange(na)[:, None]
    k1 = np.arange(k1n)[None, :]
    ang = 2.0 * np.pi * ((n1 * k1) % k1n) / k1n
    c, s = np.cos(ang), np.sin(ang)
    ma = np.block([[c, -s], [s, c]])
    n1f = np.arange(FFT_N2)[:, None]
    angf = 2.0 * np.pi * ((n1f * k1) % k1n) / k1n
    live = (n1f < na).astype(np.float64)
    ma_f = np.concatenate([np.cos(angf), -np.sin(angf)], axis=1) * live
    ma_b = np.concatenate([np.cos(angf), np.sin(angf)], axis=1) * live
    n2 = np.arange(FFT_N2)[:, None]
    angt = 2.0 * np.pi * ((n2 * k1) % n) / n
    twr, twi = np.cos(angt), -np.sin(angt)
    k2 = np.arange(FFT_N2)[None, :]
    angb = 2.0 * np.pi * ((n2 * k2) % FFT_N2) / FFT_N2
    cb, sb = np.cos(angb), np.sin(angb)
    mb = np.block([[cb, -sb], [sb, cb]])
    mc = np.block([[cb, sb], [-sb, cb]])
    angd = 2.0 * np.pi * ((np.arange(k1n)[:, None] * np.arange(na)[None, :]) % k1n) / k1n
    cd, sd = np.cos(angd) / n, np.sin(angd) / n
    md = np.block([[cd, sd], [-sd, cd]])
    as_bf = lambda a: jnp.asarray(a, dtype=F32).astype(BF16)
    as_f = lambda a: jnp.asarray(a, dtype=F32)
    mb2 = np.concatenate([mb, mc], axis=0)
    return dict(ma=as_bf(ma), ma_f=as_bf(ma_f), ma_b=as_bf(ma_b), twr=as_f(twr), twi=as_f(twi),
                twr_t=as_f(twr.T), twi_t=as_f(twi.T), mb=as_bf(mb), mb2=as_bf(mb2), mc=as_bf(mc), md=as_bf(md))


def _fwd_stages(x2, ma, twr, twi, mb, cb, k1n):
    a = _dot(x2, ma).reshape(cb, FFT_N2, 2 * k1n)
    ar, ai = a[..., :k1n], a[..., k1n:]
    ar2 = ar * twr - ai * twi
    ai2 = ar * twi + ai * twr
    xt = jnp.concatenate([jnp.swapaxes(ar2, 1, 2), jnp.swapaxes(ai2, 1, 2)], axis=-1)
    return _dot(xt.astype(BF16).reshape(cb * k1n, 2 * FFT_N2), mb).reshape(cb, k1n, 2 * FFT_N2)


def _spectrum_kernel(hf_ref, hb_ref, ssf_ref, ssb_ref, maf_ref, mab_ref, twr_ref, twi_ref, mb2_ref, h_ref, *, na):
    k1n = 2 * na
    cb = h_ref.shape[0]

    def stage_a(ref, ss_ref, m_ref):
        x = ref[0, 0] * lax.rsqrt(ss_ref[0, 0])
        x = jnp.concatenate([x, jnp.zeros((cb, FFT_N2 - na, FFT_N2), F32)], axis=1)
        xt = jnp.swapaxes(x, 1, 2).astype(BF16).reshape(cb * FFT_N2, FFT_N2)
        a = _dot(xt, m_ref[...]).reshape(cb, FFT_N2, 2 * k1n)
        return a[..., :k1n], a[..., k1n:]

    fr, fi = stage_a(hf_ref, ssf_ref, maf_ref)
    br, bi = stage_a(hb_ref, ssb_ref, mab_ref)
    twr, twi = twr_ref[...], twi_ref[...]
    parts = (fr * twr - fi * twi, fr * twi + fi * twr, br * twr + bi * twi, bi * twr - br * twi)
    xt = jnp.concatenate([jnp.swapaxes(p, 1, 2) for p in parts], axis=-1)
    h = _dot(xt.astype(BF16).reshape(cb * k1n, 4 * FFT_N2), mb2_ref[...])
    h_ref[...] = h.reshape(cb, k1n, 2 * FFT_N2).astype(BF16)


def _spectrum(filt5, ss5, order, tb, na):
    c = filt5.shape[2]
    k1n = 2 * na
    cb = FFT_CB
    full = lambda a: pl.BlockSpec(a.shape, lambda i: tuple(0 for _ in a.shape))
    fblk = lambda d: pl.BlockSpec((1, 1, cb, na, FFT_N2), lambda i: (order, d, i, 0, 0))
    sblk = lambda d: pl.BlockSpec((1, 1, cb, 1, 1), lambda i: (order, d, i, 0, 0))
    return pl.pallas_call(
        functools.partial(_spectrum_kernel, na=na),
        out_shape=jax.ShapeDtypeStruct((c, k1n, 2 * FFT_N2), BF16),
        grid=(c // cb,),
        in_specs=[fblk(0), fblk(1), sblk(0), sblk(1),
                  full(tb["ma_f"]), full(tb["ma_b"]), full(tb["twr"]), full(tb["twi"]), full(tb["mb2"])],
        out_specs=pl.BlockSpec((cb, k1n, 2 * FFT_N2), lambda i: (i, 0, 0)),
        compiler_params=_cparams("arbitrary"),
        name="hyena_filter_spectrum",
    )(filt5, filt5, ss5, ss5, tb["ma_f"], tb["ma_b"], tb["twr"], tb["twi"], tb["mb2"])


def _fftconv_kernel(u_ref, g_ref, h_ref, sk_ref, cwu_ref, cwg_ref, ma_ref, twr_ref, twi_ref, mb_ref, mc_ref,
                    twrt_ref, twit_ref, md_ref, o_ref, *, na, conv_u):
    k1n = 2 * na
    cb = u_ref.shape[0]
    uf = u_ref[...].astype(F32)
    if conv_u:
        uf = _shortconv_freq(uf, cwu_ref[...], na)
    gate = _shortconv_freq(g_ref[...].astype(F32), cwg_ref[...], na)
    x = _fwd_stages(uf.astype(BF16).reshape(cb * FFT_N2, k1n), ma_ref[...], twr_ref[...], twi_ref[...],
                    mb_ref[...], cb, k1n)
    xr, xi = x[..., :FFT_N2], x[..., FFT_N2:]
    h = h_ref[...].astype(F32)
    hr, hi = h[..., :FFT_N2], h[..., FFT_N2:]
    y = jnp.concatenate([xr * hr - xi * hi, xr * hi + xi * hr], axis=-1)
    c = _dot(y.astype(BF16).reshape(cb * k1n, 2 * FFT_N2), mc_ref[...]).reshape(cb, k1n, 2 * FFT_N2)
    cr, ci = c[..., :FFT_N2], c[..., FFT_N2:]
    twrt, twit = twrt_ref[...], twit_ref[...]
    cr2 = cr * twrt + ci * twit
    ci2 = ci * twrt - cr * twit
    ct = jnp.concatenate([jnp.swapaxes(cr2, 1, 2), jnp.swapaxes(ci2, 1, 2)], axis=-1)
    d = _dot(ct.astype(BF16).reshape(cb * FFT_N2, 2 * k1n), md_ref[...]).reshape(cb, FFT_N2, k1n)
    o_ref[...] = (gate * (d + sk_ref[...] * uf)).astype(BF16)


def _fftconv(u, gate, h, skip, cw_u, cw_g, tb, na, conv_u):
    c = u.shape[0]
    k1n = 2 * na
    cb = FFT_CB
    full = lambda a: pl.BlockSpec(a.shape, lambda i: tuple(0 for _ in a.shape))
    blk = pl.BlockSpec((cb, FFT_N2, k1n), lambda i: (i, 0, 0))
    cwblk = pl.BlockSpec((cb, 1, 4), lambda i: (i, 0, 0))
    return pl.pallas_call(
        functools.partial(_fftconv_kernel, na=na, conv_u=conv_u),
        out_shape=jax.ShapeDtypeStruct(u.shape, BF16),
        grid=(c // cb,),
        in_specs=[blk, blk,
                  pl.BlockSpec((cb, k1n, 2 * FFT_N2), lambda i: (i, 0, 0)),
                  pl.BlockSpec((cb, 1, 1), lambda i: (i, 0, 0)), cwblk, cwblk,
                  full(tb["ma"]), full(tb["twr"]), full(tb["twi"]), full(tb["mb"]), full(tb["mc"]),
                  full(tb["twr_t"]), full(tb["twi_t"]), full(tb["md"])],
        out_specs=blk,
        compiler_params=_cparams("arbitrary"),
        name="hyena_fftconv",
    )(u, gate, h, skip, cw_u, cw_g, tb["ma"], tb["twr"], tb["twi"], tb["mb"], tb["mc"], tb["twr_t"], tb["twi_t"],
      tb["md"])


def _to_freq_layout(a, na):
    nb, _, c = a.shape
    a = a.reshape(nb, na, FFT_N2, c)
    return jnp.transpose(a, (3, 2, 0, 1)).reshape(c, FFT_N2, nb * na)


def _from_freq_layout(a):
    c, _, lanes = a.shape
    na = lanes // 2
    a = a.reshape(c, FFT_N2, 2, na)
    return jnp.transpose(a, (2, 3, 1, 0)).reshape(2, na * FFT_N2, c)


def _ctx_dft_tables(n):
    big = 2 * n
    j = np.arange(n)[:, None]
    k = np.arange(big)[None, :]
    ang = 2.0 * np.pi * ((j * k) % big) / big
    c, s = np.cos(ang), np.sin(ang)
    fh = np.block([[c, -s], [c, s]])
    ff = np.block([[c, -s], [s, c]])
    fi = np.block([[c.T, s.T], [-s.T, c.T]]) / big
    as_bf = lambda a: jnp.asarray(a, dtype=F32).astype(BF16)
    return as_bf(fh), as_bf(ff), as_bf(fi)


def _ctxconv_kernel(u_ref, cw_ref, hf_ref, hb_ref, ssf_ref, ssb_ref, sk_ref, fh_ref, ff_ref, fi_ref, o_ref):
    n_tok = u_ref.shape[2] // 2
    short = lambda j: _shortconv_lanes(u_ref[j].astype(F32), cw_ref[j], n_tok)
    gates = (short(0), short(1))
    z = short(2)
    half = fh_ref.shape[1] // 2
    for n in range(2):
        taps = jnp.concatenate([hf_ref[n] * lax.rsqrt(ssf_ref[n]), hb_ref[n] * lax.rsqrt(ssb_ref[n])], axis=-1)
        h = _dot(taps.astype(BF16), fh_ref[...])
        x = _dot(z.astype(BF16), ff_ref[...])
        hr, hi = h[:, :half], h[:, half:]
        xr, xi = x[:, :half], x[:, half:]
        y = jnp.concatenate([xr * hr - xi * hi, xr * hi + xi * hr], axis=-1)
        conv = _dot(y.astype(BF16), fi_ref[...])
        z = gates[n] * (conv + sk_ref[n] * z)
    o_ref[...] = z.astype(BF16)


def _ctxconv(u3, cw3, hf, hb, ssf, ssb, skip, tables):
    _, c, lanes = u3.shape
    args = (u3, cw3, hf, hb, ssf, ssb, skip) + tuple(tables)
    full = lambda a: pl.BlockSpec(a.shape, lambda i: tuple(0 for _ in a.shape))
    return pl.pallas_call(
        _ctxconv_kernel,
        out_shape=jax.ShapeDtypeStruct((c, lanes), BF16),
        grid=(1,),
        in_specs=[full(a) for a in args],
        out_specs=pl.BlockSpec((c, lanes), lambda i: (0, 0)),
        compiler_params=_cparams("arbitrary"),
        name="hyena_ctxconv",
    )(*args)


def _load_row_tiles(ref, n, stride=SUBLANES, first=0):
    return jnp.concatenate([ref[pl.ds(first + j, n, stride=stride), :] for j in range(SUBLANES)], axis=-1)


def _store_row_tiles(ref, val, n, stride=SUBLANES, first=0):
    for j in range(SUBLANES):
        ref[pl.ds(first + j, n, stride=stride), :] = val[:, j * LANES:(j + 1) * LANES]


def _row_tile(ref, idx, tiles=1):
    size = tiles * SUBLANES
    return ref.at[pl.ds(pl.multiple_of(idx * size, size), size)]


def _top2_rows(vals):
    n = len(vals)
    best_v, best_i = vals[0], jnp.zeros_like(vals[0], dtype=jnp.int32)
    for e in range(1, n):
        take = vals[e] > best_v
        best_v = jnp.where(take, vals[e], best_v)
        best_i = jnp.where(take, e, best_i)
    sec_v = jnp.full_like(vals[0], -jnp.inf)
    sec_i = jnp.zeros_like(best_i)
    for e in range(n):
        take = (best_i != e) & (vals[e] > sec_v)
        sec_v = jnp.where(take, vals[e], sec_v)
        sec_i = jnp.where(take, e, sec_i)
    return best_v, best_i, sec_v, sec_i


def _route(h2, rw_ref, rb_ref, carry_ref, mi_ref, mf_ref, cnt_ref, cols):
    logits = _dot3(rw_ref[...], h2, dot=_dot_nt)
    s = 1.0 / (1.0 + jnp.exp(-logits))
    sel = s + rb_ref[...]
    ng = N_GROUPS
    sel4 = [sel[j * ng:(j + 1) * ng, :] for j in range(EXPERTS_PER_GROUP)]
    s4 = [s[j * ng:(j + 1) * ng, :] for j in range(EXPERTS_PER_GROUP)]
    bv, _, sv, _ = _top2_rows(sel4)
    gscore = bv + sv
    gidx = lax.broadcasted_iota(jnp.int32, (ng, TM), 0)
    gmax = jnp.max(gscore, axis=0, keepdims=True)
    best_g = jnp.min(jnp.where(gscore == gmax, gidx, ng), axis=0, keepdims=True)
    in_best = gidx == best_g
    pick = lambda slab: jnp.sum(jnp.where(in_best, slab, 0.0), axis=0, keepdims=True)
    cand_sel = [pick(v) for v in sel4]
    cand_s = [pick(v) for v in s4]
    _, i1, _, i2 = _top2_rows(cand_sel)
    a1, a2 = cand_s[0], cand_s[0]
    for j in range(1, EXPERTS_PER_GROUP):
        a1 = jnp.where(i1 == j, cand_s[j], a1)
        a2 = jnp.where(i2 == j, cand_s[j], a2)
    den = a1 + a2
    w1, w2 = a1 / den, a2 / den
    swap = i2 < i1
    lo = jnp.where(swap, i2, i1)
    hi = jnp.where(swap, i1, i2)
    w_lo = jnp.where(swap, w2, w1)
    w_hi = jnp.where(swap, w1, w2)
    pair = jnp.where(lo == 0, 0, jnp.where(lo == 1, 3, 5)) + (hi - lo - 1)
    cls = best_g * len(PAIRS) + pair
    cidx = lax.broadcasted_iota(jnp.int32, (CLASS_ROWS, TM), 0)
    hit = cidx == cls
    onehot = jnp.where(hit, 1.0, 0.0)
    tri = jnp.where(lax.broadcasted_iota(jnp.int32, (TM, TM), 0) < lax.broadcasted_iota(jnp.int32, (TM, TM), 1),
                    1.0, 0.0).astype(BF16)
    rank = _dot(onehot.astype(BF16), tri) + carry_ref[:, 0:1]
    rk = jnp.sum(jnp.where(hit, rank, 0.0), axis=0, keepdims=True)
    carry_ref[...] = carry_ref[...] + jnp.sum(onehot, axis=1, keepdims=True)
    mi_ref[:, cols] = jnp.concatenate([cls, rk.astype(jnp.int32), jnp.zeros((6, TM), jnp.int32)], axis=0)
    mf_ref[:, cols] = jnp.concatenate([w_lo, w_hi, jnp.zeros((6, TM), F32)], axis=0)
    cnt_ref[...] = carry_ref[...]


def _post_norm_mod(x, y, m_ref, lng, lnb, gate_row, sh_row, sc_row):
    u = DN_ALPHA * x + m_ref[0, gate_row:gate_row + 1, :] * y
    mu = jnp.mean(u, axis=-1, keepdims=True)
    var = jnp.mean(jnp.square(u - mu), axis=-1, keepdims=True)
    xn = (u - mu) * lax.rsqrt(var + LN_EPS) * lng + lnb
    if sh_row is None:
        return xn, None
    return xn, xn * (1.0 + m_ref[0, sc_row:sc_row + 1, :]) + m_ref[0, sh_row:sh_row + 1, :]


def _outproj_kernel(*refs, layer0, tpb, parts):
    nd = 8 if layer0 else 5
    halves = tuple(refs[p * nd:(p + 1) * nd] for p in range(parts))
    (w_ref, lng_ref, lnb_ref, rw_ref, rb_ref, xo_ref, h2_ref, mi_ref, mf_ref, cnt_ref,
     carry_ref) = refs[parts * nd:]
    i = pl.program_id(0)

    @pl.when(i == 0)
    def _():
        carry_ref[...] = jnp.zeros_like(carry_ref)

    for half, data in enumerate(halves):
        if layer0:
            of_ref, ob_ref, g_ref, hyl_ref, hyc_ref, xl_ref, xc_ref, m_ref = data
            is_ctx = (parts * i + half) % tpb == tpb - 1
            x = jnp.where(is_ctx, xc_ref[...], xl_ref[...])
            hy = jnp.where(is_ctx, hyc_ref[...], hyl_ref[...])
            o = of_ref[...].astype(F32) + ob_ref[...].astype(F32)
            normed = []
            for h in range(RET_HEADS):
                oh = o[:, h * RET_HD:(h + 1) * RET_HD]
                mu = jnp.mean(oh, axis=-1, keepdims=True)
                var = jnp.mean(jnp.square(oh - mu), axis=-1, keepdims=True)
                normed.append((oh - mu) * lax.rsqrt(var + LN_EPS))
            yret = (jnp.concatenate(normed, axis=-1) * _silu(g_ref[...].astype(F32))).astype(BF16)
            wr = yret.shape[1]
            y = _dot(yret, w_ref[:wr, :]) + _dot(hy, w_ref[wr:, :])
        else:
            of_ref, ob_ref, g_ref, x_ref, m_ref = data
            x = x_ref[...]
            o = of_ref[...].astype(F32) + ob_ref[...].astype(F32)
            normed = []
            for h in range(GLA_HEADS):
                oh = o[:, h * GLA_DV:(h + 1) * GLA_DV]
                normed.append(oh * lax.rsqrt(jnp.mean(jnp.square(oh), axis=-1, keepdims=True) + LN_EPS))
            a = (jnp.concatenate(normed, axis=-1) * _silu(g_ref[...].astype(F32))).astype(BF16)
            y = _dot(a, w_ref[...])
        rows = slice(half * TM, (half + 1) * TM)
        xn, h2 = _post_norm_mod(x, y, m_ref, lng_ref[...], lnb_ref[...], 2, 3, 4)
        xo_ref[rows, :] = xn
        _store_row_tiles(h2_ref, h2, TM, first=half * TM * SUBLANES)
        _route(h2, rw_ref, rb_ref, carry_ref, mi_ref, mf_ref, cnt_ref, rows)


def _outproj_call(layer0, tpb, n_tiles, d, half_specs, half_args, w, lng, lnb, router_w, router_bias):
    rows = n_tiles * TM
    parts = next(p for p in (4, 3, 2, 1) if n_tiles % p == 0)
    full2 = lambda a: pl.BlockSpec(a.shape, lambda i: (0, 0))
    regroup = lambda a: jnp.swapaxes(a.reshape(N_GROUPS, EXPERTS_PER_GROUP, -1), 0, 1).reshape(N_EXPERTS, -1)
    rw_t = regroup(router_w.T)
    rb = regroup(router_bias.reshape(N_EXPERTS, 1))
    return pl.pallas_call(
        functools.partial(_outproj_kernel, layer0=layer0, tpb=tpb, parts=parts),
        out_shape=(jax.ShapeDtypeStruct((rows, d), F32),
                   jax.ShapeDtypeStruct((rows * SUBLANES, LANES), F32),
                   jax.ShapeDtypeStruct((8, rows), jnp.int32),
                   jax.ShapeDtypeStruct((8, rows), F32),
                   jax.ShapeDtypeStruct((CLASS_ROWS, LANES), F32)),
        grid=(n_tiles // parts,),
        in_specs=[s for h in range(parts) for s in half_specs(h, parts)]
        + [full2(w), full2(lng), full2(lnb), full2(rw_t), full2(rb)],
        out_specs=(pl.BlockSpec((parts * TM, d), lambda i: (i, 0)),
                   pl.BlockSpec((parts * TM * SUBLANES, LANES), lambda i: (i, 0)),
                   pl.BlockSpec((8, parts * TM), lambda i: (0, i)),
                   pl.BlockSpec((8, parts * TM), lambda i: (0, i)),
                   pl.BlockSpec((CLASS_ROWS, LANES), lambda i: (0, 0))),
        scratch_shapes=[pltpu.VMEM((CLASS_ROWS, LANES), F32)],
        compiler_params=_cparams("arbitrary"),
        name="outproj_norm0" if layer0 else "outproj_norm1",
    )(*(tuple(half_args) * parts), w, lng, lnb, rw_t, rb)


def _dispatch_kernel(slots_ref, zoff_ref, h_ref, xs_ref, zero_ref, sem):
    i = pl.program_id(0)

    @pl.when(i == 0)
    def _():
        zero_ref[...] = jnp.zeros_like(zero_ref)
        def fill_class(e, go):
            @pl.when(zoff_ref[N_CLASSES + 1 + e] > 0)
            def _():
                go(pltpu.make_async_copy(zero_ref, _row_tile(xs_ref, zoff_ref[e], TE), sem))

        for e in range(N_CLASSES):
            fill_class(e, lambda cp: cp.start())
        for e in range(N_CLASSES):
            fill_class(e, lambda cp: cp.wait())
        first_free = zoff_ref[N_CLASSES]
        n_all = xs_ref.shape[0] // (TE * SUBLANES)

        def fill(t, carry):
            pltpu.make_async_copy(zero_ref, _row_tile(xs_ref, t, TE), sem).start()
            return carry

        def fill_wait(t, carry):
            pltpu.make_async_copy(zero_ref, _row_tile(xs_ref, t, TE), sem).wait()
            return carry

        lax.fori_loop(first_free, n_all, fill, 0)
        lax.fori_loop(first_free, n_all, fill_wait, 0)

    def row_copy(r):
        return pltpu.make_async_copy(_row_tile(h_ref, r), _row_tile(xs_ref, slots_ref[0, 0, r]), sem)

    for r in range(TM):
        row_copy(r).start(priority=r % 2)
    for r in range(TM):
        row_copy(r).wait()


def _dispatch(h2, slots3, zoff, p_rows):
    n_tiles = h2.shape[0] // (TM * SUBLANES)
    grid_spec = pltpu.PrefetchScalarGridSpec(
        num_scalar_prefetch=0,
        grid=(n_tiles,),
        in_specs=[pl.BlockSpec((1, 1, TM), lambda i: (i, 0, 0), memory_space=pltpu.SMEM),
                  pl.BlockSpec(memory_space=pltpu.SMEM),
                  pl.BlockSpec((TM * SUBLANES, LANES), lambda i: (i, 0))],
        out_specs=pl.BlockSpec(memory_space=pl.ANY),
        scratch_shapes=[pltpu.VMEM((TE * SUBLANES, LANES), F32), pltpu.SemaphoreType.DMA(())],
    )
    return pl.pallas_call(
        _dispatch_kernel,
        out_shape=jax.ShapeDtypeStruct((p_rows * SUBLANES, LANES), F32),
        grid_spec=grid_spec,
        compiler_params=_cparams("arbitrary"),
        name="moe_dispatch",
    )(slots3, zoff, h2)


def _experts_kernel(ea_ref, eb_ref, tv_ref, x_ref, wga_ref, wua_ref, wda_ref, wgb_ref, wub_ref, wdb_ref, y_ref):
    j = pl.program_id(0)

    @pl.when(tv_ref[j] > 0)
    def _():
        x = _load_row_tiles(x_ref, TE).astype(BF16)
        for which, (wg, wu, wd) in enumerate(((wga_ref, wua_ref, wda_ref), (wgb_ref, wub_ref, wdb_ref))):
            g = _dot(x, wg[0])
            u = _dot(x, wu[0])
            y = _dot((_silu(g) * u).astype(BF16), wd[0])
            _store_row_tiles(y_ref, y, TE, stride=2 * SUBLANES, first=which * SUBLANES)

    @pl.when(tv_ref[j] == 0)
    def _():
        y_ref[...] = jnp.zeros_like(y_ref)


def _experts(xs, tile_ea, tile_eb, tile_valid, wg, wu, wd):
    n_tiles = xs.shape[0] // (TE * SUBLANES)
    d, hdim = wg.shape[1], wg.shape[2]
    wa = lambda shp: pl.BlockSpec(shp, lambda j, ea, eb, tv: (ea[j], 0, 0))
    wb = lambda shp: pl.BlockSpec(shp, lambda j, ea, eb, tv: (eb[j], 0, 0))
    grid_spec = pltpu.PrefetchScalarGridSpec(
        num_scalar_prefetch=3,
        grid=(n_tiles,),
        in_specs=[pl.BlockSpec((TE * SUBLANES, LANES), lambda j, ea, eb, tv: (j, 0)),
                  wa((1, d, hdim)), wa((1, d, hdim)), wa((1, hdim, d)),
                  wb((1, d, hdim)), wb((1, d, hdim)), wb((1, hdim, d))],
        out_specs=pl.BlockSpec((2 * TE * SUBLANES, LANES), lambda j, ea, eb, tv: (j, 0)),
    )
    return pl.pallas_call(
        _experts_kernel,
        out_shape=jax.ShapeDtypeStruct((2 * xs.shape[0], LANES), F32),
        grid_spec=grid_spec,
        compiler_params=_cparams("arbitrary"),
        name="moe_experts",
    )(tile_ea, tile_eb, tile_valid, xs, wg, wu, wd, wg, wu, wd)


def _combine_kernel(slots_ref, next_slots_ref, ys_ref, wts_ref, x_ref, m_ref, lng_ref, lnb_ref, o_ref, buf_ref, sem):
    i = pl.program_id(0)
    cur = i % 2

    def row_copy(s_ref, b, r):
        return pltpu.make_async_copy(_row_tile(ys_ref, s_ref[0, 0, r], 2), _row_tile(buf_ref.at[b], r, 2), sem.at[b])

    def issue_all(s_ref, b):
        for r in range(TM):
            row_copy(s_ref, b, r).start(priority=r % 2)

    @pl.when(i == 0)
    def _():
        issue_all(slots_ref, 0)

    @pl.when(i + 1 < pl.num_programs(0))
    def _():
        issue_all(next_slots_ref, 1 - cur)

    for r in range(TM):
        row_copy(slots_ref, cur, r).wait()
    buf = buf_ref.at[cur]
    y_lo = _load_row_tiles(buf, TM, stride=2 * SUBLANES)
    y_hi = _load_row_tiles(buf, TM, stride=2 * SUBLANES, first=SUBLANES)
    y = wts_ref[:, 0:1] * y_lo + wts_ref[:, 1:2] * y_hi
    xn, _ = _post_norm_mod(x_ref[...], y, m_ref, lng_ref[...], lnb_ref[...], 5, None, None)
    o_ref[...] = xn


def _combine(ys, slots3, wts, x, mods, lng, lnb, cond_of, n_tiles):
    d = x.shape[1]
    grid_spec = pltpu.PrefetchScalarGridSpec(
        num_scalar_prefetch=0,
        grid=(n_tiles,),
        in_specs=[pl.BlockSpec((1, 1, TM), lambda i: (i, 0, 0), memory_space=pltpu.SMEM),
                  pl.BlockSpec((1, 1, TM), lambda i: (jnp.minimum(i + 1, n_tiles - 1), 0, 0),
                               memory_space=pltpu.SMEM),
                  pl.BlockSpec(memory_space=pl.ANY),
                  pl.BlockSpec((TM, 2), lambda i: (i, 0)),
                  pl.BlockSpec((TM, d), lambda i: (i, 0)),
                  pl.BlockSpec((1, 6, d), lambda i: (cond_of(i), 0, 0)),
                  pl.BlockSpec((1, d), lambda i: (0, 0)),
                  pl.BlockSpec((1, d), lambda i: (0, 0))],
        out_specs=pl.BlockSpec((TM, d), lambda i: (i, 0)),
        scratch_shapes=[pltpu.VMEM((2, 2 * TM * SUBLANES, LANES), F32), pltpu.SemaphoreType.DMA((2,))],
    )
    return pl.pallas_call(
        _combine_kernel,
        out_shape=jax.ShapeDtypeStruct((n_tiles * TM, d), F32),
        grid_spec=grid_spec,
        compiler_params=_cparams("arbitrary"),
        name="moe_combine",
    )(slots3, slots3, ys, wts, x, mods, lng, lnb)


def _moe(h2, mi, mf, cnt, x, cond_of, mods, lng, lnb, layer, wg, wu, wd):
    t = x.shape[0]
    n_tiles = t // TM
    counts = cnt[:N_CLASSES, 0].astype(jnp.int32)
    padded = ((counts + TE - 1) // TE) * TE
    ends = jnp.cumsum(padded)
    offs = ends - padded
    cls = mi[0]
    cids = jnp.arange(N_CLASSES, dtype=jnp.int32)[:, None]
    slots = jnp.sum(jnp.where(cls[None] == cids, offs[:, None], 0), axis=0) + mi[1]
    slots3 = slots.reshape(n_tiles, 1, TM)
    n_cls_tiles = t // TE + N_CLASSES
    total_tiles = ends[-1] // TE
    tile_ids = jnp.arange(n_cls_tiles, dtype=jnp.int32)
    tile_valid = (tile_ids < total_tiles).astype(jnp.int32)
    tile_last = jnp.minimum(tile_ids, total_tiles - 1)
    tile_class = jnp.sum((tile_last[:, None] * TE >= ends[None, :]).astype(jnp.int32), axis=1)
    tile_class = jnp.minimum(tile_class, N_CLASSES - 1)
    pair_lo = jnp.asarray([p[0] for p in PAIRS], jnp.int32)
    pair_hi = jnp.asarray([p[1] for p in PAIRS], jnp.int32)
    pair_sel = (tile_class % len(PAIRS))[:, None] == jnp.arange(len(PAIRS), dtype=jnp.int32)[None, :]
    group0 = (tile_class // len(PAIRS)) * EXPERTS_PER_GROUP
    tile_ea = (group0 + jnp.sum(jnp.where(pair_sel, pair_lo[None, :], 0), axis=1)).astype(jnp.int32)
    tile_eb = (group0 + jnp.sum(jnp.where(pair_sel, pair_hi[None, :], 0), axis=1)).astype(jnp.int32)
    zoff = jnp.concatenate([jnp.maximum(ends // TE - 1, 0), total_tiles[None], padded]).astype(jnp.int32)
    xs = _dispatch(h2, slots3, zoff, n_cls_tiles * TE)
    ys = _experts(xs, tile_ea + layer * N_EXPERTS, tile_eb + layer * N_EXPERTS, tile_valid, wg, wu, wd)
    wts = jnp.transpose(mf[0:2])
    return _combine(ys, slots3, wts, x, mods, lng, lnb, cond_of, n_tiles)


def _chunk_cumsum(x, tri):
    hi = x.astype(BF16)
    r1 = x - hi.astype(F32)
    mid = r1.astype(BF16)
    lo = (r1 - mid.astype(F32)).astype(BF16)
    return _dot(tri, hi) + _dot(tri, mid) + _dot(tri, lo)


def _gla_kernel(*refs, nb):
    f_in, b_in, (of_ref, ob_ref, sf_ref, sb_ref) = refs[:6], refs[6:12], refs[12:]

    @pl.when(pl.program_id(0) == 0)
    def _():
        sf_ref[...] = jnp.zeros_like(sf_ref)
        sb_ref[...] = jnp.zeros_like(sb_ref)

    _gla_dir(*f_in, of_ref, sf_ref, False, nb)
    _gla_dir(*b_in, ob_ref, sb_ref, True, nb)


def _gla_dir(q_ref, k_ref, v_ref, lr_ref, gw_ref, gb_ref, o_ref, s_ref, reverse, nb):
    C = GLA_CHUNK
    nchunk = TM // C
    ii = lax.broadcasted_iota(jnp.int32, (C, C), 0)
    jj = lax.broadcasted_iota(jnp.int32, (C, C), 1)
    keep = (jj >= ii) if reverse else (jj <= ii)
    tri = jnp.where(keep, 1.0, 0.0).astype(BF16)
    order = tuple(reversed(range(nchunk))) if reverse else tuple(range(nchunk))
    end_row = 0 if reverse else C - 1
    gw = gw_ref[...]
    gb = gb_ref[...]
    for b in range(nb):
        pre = _dot(lr_ref[b], gw) + gb
        la_all = (jnp.minimum(pre, 0.0) - jnp.log(1.0 + jnp.exp(-jnp.abs(pre)))) * (1.0 / GLA_TAU)
        for c in order:
            rows = slice(c * C, (c + 1) * C)
            bc = _chunk_cumsum(la_all[rows, :], tri)
            bend = bc[end_row:end_row + 1, :]
            eb = jnp.exp(bc)
            enb = jnp.exp(-bc)
            ekb = jnp.exp(bend - bc)
            ebend = jnp.exp(bend)
            for h in range(GLA_HEADS):
                kc = slice(h * GLA_DK, (h + 1) * GLA_DK)
                vc = slice(h * GLA_DV, (h + 1) * GLA_DV)
                q = q_ref[b, rows, kc].astype(F32) * (GLA_DK ** -0.5)
                k = k_ref[b, rows, kc].astype(F32)
                v = v_ref[b, rows, vc]
                qd = (q * eb[:, kc]).astype(BF16)
                kd = (k * enb[:, kc]).astype(BF16)
                st = s_ref[b, h]
                scores = jnp.where(keep, _dot_nt(qd, kd), 0.0)
                o = _dot(scores.astype(BF16), v) + _dot_nt(qd, st.astype(BF16))
                s_ref[b, h] = st * ebend[:, kc] + _dot_tn(v, (k * ekb[:, kc]).astype(BF16))
                o_ref[b, rows, vc] = o.astype(BF16)


def _gla(z3, gw_f, gb_f, gw_b, gb_b):
    nb, s, _ = z3.shape
    nt = s // TM
    kw = GLA_HEADS * GLA_DK
    vw = GLA_HEADS * GLA_DV
    lr_blk = (2 * kw + 2 * vw) // LANES

    def dir_specs(jmap):
        return [pl.BlockSpec((nb, TM, kw), lambda j: (0, jmap(j), 0)),
                pl.BlockSpec((nb, TM, kw), lambda j: (0, jmap(j), 1)),
                pl.BlockSpec((nb, TM, vw), lambda j: (0, jmap(j), 1)),
                pl.BlockSpec((nb, TM, LANES), lambda j: (0, jmap(j), lr_blk)),
                pl.BlockSpec((LANES, kw), lambda j: (0, 0)),
                pl.BlockSpec((1, kw), lambda j: (0, 0))]

    fwd, bwd = _scan_tile_orders(nt)
    state = pltpu.VMEM((nb, GLA_HEADS, GLA_DV, GLA_DK), F32)
    out = jax.ShapeDtypeStruct((nb, s, vw), BF16)
    return pl.pallas_call(
        functools.partial(_gla_kernel, nb=nb),
        out_shape=(out, out),
        grid=(nt,),
        in_specs=dir_specs(fwd) + dir_specs(bwd),
        out_specs=(pl.BlockSpec((nb, TM, vw), lambda j: (0, fwd(j), 0)),
                   pl.BlockSpec((nb, TM, vw), lambda j: (0, bwd(j), 0))),
        scratch_shapes=[state, state],
        compiler_params=_cparams("arbitrary"),
        name="gla",
    )(z3, z3, z3, z3, gw_f, gb_f, z3, z3, z3, z3, gw_b, gb_b)


def kernel(x, c, ctx, c_ctx, mod_w, mod_b, ln_g, ln_b, ab_w_in, ret_log_decay_f, ret_log_decay_b, hy_conv_w, hy_conv_b, hy_w1, hy_b1, hy_w2, hy_b2, hy_w3, hy_b3, hy_w4, hy_freq, hy_skip, ab_w_out, gla_w_in, gla_gate_w_f, gla_gate_b_f, gla_gate_w_b, gla_gate_b_b, gla_w_out, router_w, router_bias, exp_w_gate, exp_w_up, exp_w_down):
    nb, length, d = x.shape
    ctx_len = ctx.shape[1]
    assert nb == 2 and ctx_len == TM and length % TM == 0 and length % (FFT_N2 * 2) == 0
    s = length + ctx_len
    tpb = s // TM
    r = nb * s
    lat_tiles = tpb - 1

    cond = jnp.zeros((8, d), F32).at[0].set(c[0]).at[1].set(c_ctx).at[2].set(c[1]).at[3].set(c_ctx)
    mods = _ada(cond, mod_w, mod_b).reshape(DEPTH, 8, 6, d)

    def cond_of(i):
        return 2 * (i // tpb) + jnp.where(i % tpb == tpb - 1, 1, 0)

    x_lat = x.reshape(nb * length, d)
    x_ctx = ctx.reshape(nb * ctx_len, d)

    ret_w = RET_HEADS * RET_HD
    z0, *hy_lat, hy_ctx = _inproj0(x_lat, x_ctx, mods[0], ab_w_in[0].astype(BF16), tpb, r, 4 * ret_w)
    z0_3 = z0.reshape(nb, s, -1)
    cosf, sinf = _rope_tables(length, ctx_len)
    o_f, o_b = _retention(z0_3, ret_log_decay_f[0], ret_log_decay_b[0], cosf, sinf)

    cw3 = jnp.concatenate([hy_conv_w[0].T, hy_conv_b[0][:, None]], axis=1).reshape(3, HY_WIDTH, 4)
    na = length // FFT_N2
    tb = _dft_tables(na)
    filt_w = (hy_w1[0], hy_b1[0], hy_w2[0], hy_b2[0], hy_w3[0], hy_b3[0], hy_w4[0], hy_freq[0])
    skip = hy_skip[0]

    filt, ss = _hyena_filters_raw(length, *filt_w)
    filt5 = filt.reshape(2, 2, HY_WIDTH, na, FFT_N2)
    ss5 = ss.reshape(2, 2, HY_WIDTH, 1, 1)
    hy_lat = [a.reshape(nb, length, HY_WIDTH) for a in hy_lat]
    zcur = _to_freq_layout(hy_lat[2], na)
    for n in range(2):
        hspec = _spectrum(filt5, ss5, n, tb, na)
        gate = _to_freq_layout(hy_lat[n], na)
        zcur = _fftconv(zcur, gate, hspec, skip[n].reshape(-1, 1, 1), cw3[2].reshape(-1, 1, 4),
                        cw3[n].reshape(-1, 1, 4), tb, na, conv_u=(n == 0))
    y_hy_lat = _from_freq_layout(zcur)

    filt_c, ss_c = _hyena_filters_raw(ctx_len, *filt_w)
    filt_c = filt_c.reshape(2, 2, HY_WIDTH, ctx_len)
    ss_c = ss_c.reshape(2, 2, HY_WIDTH, 1)
    u_ctx = jnp.transpose(hy_ctx.reshape(nb, ctx_len, 3, HY_WIDTH), (2, 3, 0, 1))
    y_ctx_t = _ctxconv(u_ctx.reshape(3, HY_WIDTH, nb * ctx_len), cw3, filt_c[:, 0], filt_c[:, 1], ss_c[:, 0],
                       ss_c[:, 1], skip.reshape(2, HY_WIDTH, 1), _ctx_dft_tables(ctx_len))
    y_hy_ctx = jnp.transpose(y_ctx_t.reshape(HY_WIDTH, nb, ctx_len), (1, 2, 0))

    gate_blk = 3

    def specs0(h, parts):
        tile = lambda i: parts * i + h
        trow = lambda i: (tile(i), 0)
        return [pl.BlockSpec((TM, ret_w), trow), pl.BlockSpec((TM, ret_w), trow),
                pl.BlockSpec((TM, ret_w), lambda i: (tile(i), gate_blk)),
                *_lat_or_ctx_specs(HY_WIDTH, tpb, tile), *_lat_or_ctx_specs(d, tpb, tile),
                pl.BlockSpec((1, 6, d), lambda i: (cond_of(tile(i)), 0, 0))]

    x_mid, h2, mi, mf, cnt = _outproj_call(
        True, tpb, r // TM, d, specs0,
        (o_f.reshape(r, ret_w), o_b.reshape(r, ret_w), z0, y_hy_lat.reshape(nb * length, HY_WIDTH),
         y_hy_ctx.reshape(nb * ctx_len, HY_WIDTH), x_lat, x_ctx, mods[0]),
        ab_w_out[0].astype(BF16), ln_g[0, 0].reshape(1, d), ln_b[0, 0].reshape(1, d), router_w, router_bias)

    stack = lambda w: w.astype(BF16).reshape((DEPTH * N_EXPERTS,) + w.shape[2:])
    exp_w = (stack(exp_w_gate), stack(exp_w_up), stack(exp_w_down))
    x1 = _moe(h2, mi, mf, cnt, x_mid, cond_of, mods[0], ln_g[0, 1].reshape(1, d), ln_b[0, 1].reshape(1, d),
              0, *exp_w)

    kw = GLA_HEADS * GLA_DK
    vw = GLA_HEADS * GLA_DV
    n_in = gla_w_in.shape[2]
    n_pad = 2 * kw + 2 * vw + LANES
    w_in1 = jnp.pad(gla_w_in[0], ((0, 0), (0, n_pad - n_in))).astype(BF16)
    z1 = _inproj(x1, mods[1], w_in1, tpb)
    z1_3 = z1.reshape(nb, s, n_pad)
    gw_f = jnp.zeros((LANES, kw), F32).at[:GLA_RANK].set(gla_gate_w_f[0]).astype(BF16)
    gw_b = jnp.zeros((LANES, kw), F32).at[GLA_RANK:2 * GLA_RANK].set(gla_gate_w_b[0]).astype(BF16)
    g_f, g_b = _gla(z1_3, gw_f, gla_gate_b_f[0].reshape(1, kw), gw_b, gla_gate_b_b[0].reshape(1, kw))

    n_lat = nb * lat_tiles

    def lat_tile(i):
        return (i // lat_tiles) * tpb + i % lat_tiles

    def lat_cond(i):
        return 2 * (i // lat_tiles)

    def specs1(h, parts):
        tile = lambda i: parts * i + h
        lrow = lambda i: (lat_tile(tile(i)), 0)
        return [pl.BlockSpec((TM, vw), lrow), pl.BlockSpec((TM, vw), lrow),
                pl.BlockSpec((TM, vw), lambda i: (lat_tile(tile(i)), 2)),
                pl.BlockSpec((TM, d), lrow),
                pl.BlockSpec((1, 6, d), lambda i: (lat_cond(tile(i)), 0, 0))]

    x_mid1, h2_1, mi1, mf1, cnt1 = _outproj_call(
        False, tpb, n_lat, d, specs1, (g_f.reshape(r, vw), g_b.reshape(r, vw), z1, x1, mods[1]),
        gla_w_out[0].astype(BF16), ln_g[1, 0].reshape(1, d), ln_b[1, 0].reshape(1, d), router_w, router_bias)

    out = _moe(h2_1, mi1, mf1, cnt1, x_mid1, lat_cond, mods[1], ln_g[1, 1].reshape(1, d), ln_b[1, 1].reshape(1, d),
               1, *exp_w)
    return out.reshape(nb, length, d)
```

```python
import functools
import math

import numpy as np
import jax
import jax.numpy as jnp
from jax import lax
from jax.experimental import pallas as pl
from jax.experimental.pallas import tpu as pltpu

F32 = jnp.float32
BF16 = jnp.bfloat16

GRID_W = 64
RET_HEADS = 4
RET_HD = 128
RET_CHUNK = 128
ROPE_BASE = 10000.0
HY_WIDTH = 512
HY_EMB = 33
HY_FFN = 64
HY_SHORT_DECAY_PCT = 0.3
HY_LONG_DECAY_PCT = 1.5
HY_TARGET = 1e-2
GLA_HEADS = 4
GLA_DK = 128
GLA_DV = 256
GLA_RANK = 16
GLA_TAU = 16.0
GLA_CHUNK = 64
N_EXPERTS = 16
N_GROUPS = 4
EXPERTS_PER_GROUP = 4
LN_EPS = 1e-5
DEPTH = 2
DN_ALPHA = (2.0 * DEPTH) ** 0.25

PAIRS = ((0, 1), (0, 2), (0, 3), (1, 2), (1, 3), (2, 3))
N_CLASSES = N_GROUPS * len(PAIRS)
CLASS_ROWS = 32

LANES = 128
SUBLANES = 8
TM = 256
TE = 256
DISPATCH_ROWS = 2 * TM
FFT_N2 = 128
FFT_CB = 16
VMEM_LIMIT = 48 * 1024 * 1024


def _cparams(*sem):
    return pltpu.CompilerParams(dimension_semantics=sem, vmem_limit_bytes=VMEM_LIMIT)


def _silu(v):
    return v * (1.0 / (1.0 + jnp.exp(-v)))


def _dot(a, b):
    return jnp.dot(a, b, preferred_element_type=F32)


def _dot_nt(a, b):
    return lax.dot_general(a, b, (((1,), (1,)), ((), ())), preferred_element_type=F32)


def _dot_tn(a, b):
    return lax.dot_general(a, b, (((0,), (0,)), ((), ())), preferred_element_type=F32)


def _split2(a):
    hi = a.astype(BF16)
    return hi, (a - hi.astype(F32)).astype(BF16)


def _dot3(a, b, dot=_dot):
    ah, al = _split2(a)
    bh, bl = _split2(b)
    return dot(ah, bh) + dot(ah, bl) + dot(al, bh)


def _ada_kernel(c_ref, w_ref, b_ref, o_ref):
    a = _silu(c_ref[...])
    o_ref[0] = _dot3(a, w_ref[0]) + b_ref[0]


def _ada(cond, mod_w, mod_b):
    depth, d, n = mod_w.shape
    nt = n // 4
    return pl.pallas_call(
        _ada_kernel,
        out_shape=jax.ShapeDtypeStruct((depth, 8, n), F32),
        grid=(depth, n // nt),
        in_specs=[pl.BlockSpec((8, d), lambda l, j: (0, 0)),
                  pl.BlockSpec((1, d, nt), lambda l, j: (l, 0, j)),
                  pl.BlockSpec((1, 1, nt), lambda l, j: (l, 0, j))],
        out_specs=pl.BlockSpec((1, 8, nt), lambda l, j: (l, 0, j)),
        compiler_params=_cparams("arbitrary", "arbitrary"),
        name="ada_mod",
    )(cond, mod_w, mod_b.reshape(depth, 1, n))


def _inproj_kernel(x_ref, ma_ref, mb_ref, w_ref, o_ref, *, chunks):
    hs = []
    for half, m_ref in enumerate((ma_ref, mb_ref)):
        x = x_ref[half * TM:(half + 1) * TM, :]
        hs.append((x * (1.0 + m_ref[0, 1:2, :]) + m_ref[0, 0:1, :]).astype(BF16))
    h = jnp.concatenate(hs, axis=0)
    for lo, hi in chunks:
        o_ref[:, lo:hi] = _dot(h, w_ref[:, lo:hi]).astype(BF16)


def _lat_or_ctx_specs(width, tpb, tile_of=lambda i: i):
    lat_tiles = tpb - 1
    lat = lambda i: ((tile_of(i) // tpb) * lat_tiles + jnp.minimum(tile_of(i) % tpb, lat_tiles - 1), 0)
    ctx = lambda i: (tile_of(i) // tpb, 0)
    return pl.BlockSpec((TM, width), lat), pl.BlockSpec((TM, width), ctx)


def _col_chunks(n, width=512):
    out, lo = [], 0
    while lo < n:
        hi = min(lo + width, n)
        out.append((lo, hi))
        lo = hi
    return tuple(out)


def _inproj(xu, mods, w, tiles_per_batch):
    r, d = xu.shape
    n = w.shape[1]
    tpb = tiles_per_batch

    def cond_of(i):
        return 2 * (i // tpb) + jnp.where(i % tpb == tpb - 1, 1, 0)

    return pl.pallas_call(
        functools.partial(_inproj_kernel, chunks=_col_chunks(n)),
        out_shape=jax.ShapeDtypeStruct((r, n), BF16),
        grid=(r // (2 * TM),),
        in_specs=[pl.BlockSpec((2 * TM, d), lambda i: (i, 0)),
                  pl.BlockSpec((1, 6, d), lambda i: (cond_of(2 * i), 0, 0)),
                  pl.BlockSpec((1, 6, d), lambda i: (cond_of(2 * i + 1), 0, 0)),
                  pl.BlockSpec((d, n), lambda i: (0, 0))],
        out_specs=pl.BlockSpec((2 * TM, n), lambda i: (i, 0)),
        compiler_params=_cparams("arbitrary"),
        name="inproj1",
    )(xu, mods, mods, w)


def _inproj0_kernel(x_ref, c_ref, m_ref, w_ref, z_ref, x1_ref, x2_ref, v_ref, hc_ref, *, tpb, ret_w):
    is_ctx = pl.program_id(0) % tpb == tpb - 1
    x = jnp.where(is_ctx, c_ref[...], x_ref[...])
    h = (x * (1.0 + m_ref[0, 1:2, :]) + m_ref[0, 0:1, :]).astype(BF16)
    for lo, hi in _col_chunks(ret_w):
        z_ref[:, lo:hi] = _dot(h, w_ref[:, lo:hi]).astype(BF16)
    hy = [_dot(h, w_ref[:, ret_w + j * HY_WIDTH:ret_w + (j + 1) * HY_WIDTH]).astype(BF16) for j in range(3)]

    @pl.when(jnp.logical_not(is_ctx))
    def _():
        for ref, val in zip((x1_ref, x2_ref, v_ref), hy):
            ref[...] = val

    @pl.when(is_ctx)
    def _():
        for j, val in enumerate(hy):
            hc_ref[:, j * HY_WIDTH:(j + 1) * HY_WIDTH] = val


def _inproj0(x_lat, x_ctx, mods, w, tpb, r, ret_w):
    d = w.shape[0]
    lat_tiles = tpb - 1

    def cond_of(i):
        return 2 * (i // tpb) + jnp.where(i % tpb == tpb - 1, 1, 0)

    lat_spec, ctx_spec = _lat_or_ctx_specs(d, tpb)
    lat_out = pl.BlockSpec((TM, HY_WIDTH), lambda i: ((i // tpb) * lat_tiles + jnp.minimum(i % tpb, lat_tiles - 1), 0))
    hy_lat = jax.ShapeDtypeStruct(((r // TM // tpb) * lat_tiles * TM, HY_WIDTH), BF16)
    return pl.pallas_call(
        functools.partial(_inproj0_kernel, tpb=tpb, ret_w=ret_w),
        out_shape=(jax.ShapeDtypeStruct((r, ret_w), BF16), hy_lat, hy_lat, hy_lat,
                   jax.ShapeDtypeStruct(((r // TM // tpb) * TM, 3 * HY_WIDTH), BF16)),
        grid=(r // TM,),
        in_specs=[lat_spec, ctx_spec, pl.BlockSpec((1, 6, d), lambda i: (cond_of(i), 0, 0)),
                  pl.BlockSpec(w.shape, lambda i: (0, 0))],
        out_specs=(pl.BlockSpec((TM, ret_w), lambda i: (i, 0)), lat_out, lat_out, lat_out,
                   pl.BlockSpec((TM, 3 * HY_WIDTH), lambda i: (i // tpb, 0))),
        compiler_params=_cparams("arbitrary"),
        name="inproj0",
    )(x_lat, x_ctx, mods, w)


def _ret_kernel(lgf_ref, lgb_ref, *refs, nb):
    f_in, b_in, (of_ref, ob_ref, sf_ref, sb_ref) = refs[:5], refs[5:10], refs[10:]

    @pl.when(pl.program_id(0) == 0)
    def _():
        sf_ref[...] = jnp.zeros_like(sf_ref)
        sb_ref[...] = jnp.zeros_like(sb_ref)

    _ret_dir(lgf_ref, *f_in, of_ref, sf_ref, False, nb)
    _ret_dir(lgb_ref, *b_in, ob_ref, sb_ref, True, nb)


def _ret_dir(lg_ref, q_ref, k_ref, v_ref, cos_ref, sin_ref, o_ref, s_ref, reverse, nb):
    C = RET_CHUNK
    ii = lax.broadcasted_iota(jnp.int32, (C, C), 0).astype(F32)
    jj = lax.broadcasted_iota(jnp.int32, (C, C), 1).astype(F32)
    ci = lax.broadcasted_iota(jnp.int32, (C, 1), 0).astype(F32)
    diff = (jj - ii) if reverse else (ii - jj)
    order = (1, 0) if reverse else (0, 1)
    for h in range(RET_HEADS):
        lg = lg_ref[h]
        intra = jnp.where(diff >= 0, jnp.exp(jnp.maximum(diff, 0.0) * lg), 0.0)
        if reverse:
            q_dec = jnp.exp((C - ci) * lg)
            k_dec = jnp.exp(ci * lg)
        else:
            q_dec = jnp.exp((ci + 1.0) * lg)
            k_dec = jnp.exp((C - 1.0 - ci) * lg)
        c_dec = jnp.exp(jnp.zeros((1, RET_HD), F32) + C * lg)
        cols = slice(h * RET_HD, (h + 1) * RET_HD)
        for b in range(nb):
            for c in order:
                rows = slice(c * C, (c + 1) * C)
                cosf = cos_ref[rows, :]
                sinf = sin_ref[rows, :]
                q = q_ref[b, rows, cols].astype(F32)
                k = k_ref[b, rows, cols].astype(F32)
                v = v_ref[b, rows, cols]
                q = q * cosf + pltpu.roll(q, RET_HD // 2, 1) * sinf
                k = (k * cosf + pltpu.roll(k, RET_HD // 2, 1) * sinf) * (RET_HD ** -0.5)
                st = s_ref[b, h]
                scores = _dot_nt(q.astype(BF16), k.astype(BF16)) * intra
                o = _dot(scores.astype(BF16), v) + _dot_nt((q * q_dec).astype(BF16), st.astype(BF16))
                s_ref[b, h] = c_dec * st + _dot_tn(v, (k * k_dec).astype(BF16))
                o_ref[b, rows, cols] = o.astype(BF16)


def _scan_tile_orders(nt):
    fwd = lambda j: (j + nt - 1) % nt
    bwd = lambda j: jnp.where(j == 0, nt - 1, nt - 1 - j)
    return fwd, bwd


def _retention(z3, lg_f, lg_b, cosf, sinf):
    nb, s, _ = z3.shape
    nt = s // TM
    w = RET_HEADS * RET_HD
    smem = pl.BlockSpec(memory_space=pltpu.SMEM)

    def dir_specs(jmap):
        return [pl.BlockSpec((nb, TM, w), lambda j: (0, jmap(j), 0)),
                pl.BlockSpec((nb, TM, w), lambda j: (0, jmap(j), 1)),
                pl.BlockSpec((nb, TM, w), lambda j: (0, jmap(j), 2)),
                pl.BlockSpec((TM, RET_HD), lambda j: (jmap(j), 0)),
                pl.BlockSpec((TM, RET_HD), lambda j: (jmap(j), 0))]

    fwd, bwd = _scan_tile_orders(nt)
    state = pltpu.VMEM((nb, RET_HEADS, RET_HD, RET_HD), F32)
    out = jax.ShapeDtypeStruct((nb, s, w), BF16)
    grid_spec = pltpu.PrefetchScalarGridSpec(
        num_scalar_prefetch=0,
        grid=(nt,),
        in_specs=[smem, smem] + dir_specs(fwd) + dir_specs(bwd),
        out_specs=(pl.BlockSpec((nb, TM, w), lambda j: (0, fwd(j), 0)),
                   pl.BlockSpec((nb, TM, w), lambda j: (0, bwd(j), 0))),
        scratch_shapes=[state, state],
    )
    return pl.pallas_call(
        functools.partial(_ret_kernel, nb=nb),
        out_shape=(out, out),
        grid_spec=grid_spec,
        compiler_params=_cparams("arbitrary"),
        name="retention",
    )(lg_f, lg_b, z3, z3, z3, cosf, sinf, z3, z3, z3, cosf, sinf)


def _rope_tables(length, ctx_len):
    rows = length // GRID_W
    quarter = RET_HD // 4
    inv = ROPE_BASE ** (-jnp.arange(quarter, dtype=F32) / quarter)
    def expand(fn):
        by_row = fn(jnp.arange(rows, dtype=F32)[:, None] * inv)
        by_col = fn(jnp.arange(GRID_W, dtype=F32)[:, None] * inv)
        by_row = jnp.broadcast_to(by_row[:, None, :], (rows, GRID_W, quarter)).reshape(length, quarter)
        by_col = jnp.broadcast_to(by_col[None, :, :], (rows, GRID_W, quarter)).reshape(length, quarter)
        return jnp.concatenate([by_row, by_col], axis=-1)

    cos, sin = expand(jnp.cos), expand(jnp.sin)
    cosf = jnp.concatenate([cos, cos], axis=-1)
    sinf = jnp.concatenate([-sin, sin], axis=-1)
    cosf = jnp.concatenate([cosf, jnp.ones((ctx_len, RET_HD), F32)], axis=0)
    sinf = jnp.concatenate([sinf, jnp.zeros((ctx_len, RET_HD), F32)], axis=0)
    return cosf, sinf


def _shortconv_freq(x, cw, na):
    row = lax.broadcasted_iota(jnp.int32, (1, FFT_N2, 1), 1)
    lane = lax.broadcasted_iota(jnp.int32, (1, 1, 2 * na), 2)
    wrap_prev = pltpu.roll(x[:, FFT_N2 - 1:FFT_N2, :], 1, 2)
    prev = jnp.where(row == 0, jnp.where((lane == 0) | (lane == na), 0.0, wrap_prev), pltpu.roll(x, 1, 1))
    wrap_next = pltpu.roll(x[:, 0:1, :], 2 * na - 1, 2)
    nxt = jnp.where(row == FFT_N2 - 1, jnp.where((lane == na - 1) | (lane == 2 * na - 1), 0.0, wrap_next),
                    pltpu.roll(x, FFT_N2 - 1, 1))
    return cw[..., 0:1] * prev + cw[..., 1:2] * x + cw[..., 2:3] * nxt + cw[..., 3:4]


def _shortconv_lanes(x, cw, n):
    lane = lax.broadcasted_iota(jnp.int32, (1, 2 * n), 1)
    prev = jnp.where((lane == 0) | (lane == n), 0.0, pltpu.roll(x, 1, 1))
    nxt = jnp.where((lane == n - 1) | (lane == 2 * n - 1), 0.0, pltpu.roll(x, 2 * n - 1, 1))
    return cw[:, 0:1] * prev + cw[:, 1:2] * x + cw[:, 2:3] * nxt + cw[:, 3:4]


FEAT_ROWS = 40


def _filter_kernel(w1_ref, b1_ref, w2_ref, b2_ref, w3_ref, b3_ref, w4_ref, fr_ref, dl_ref,
                   f_ref, ss_ref, *, length, pb):
    i = pl.program_id(0)
    pos = (lax.broadcasted_iota(jnp.int32, (1, pb), 1) + i * pb).astype(F32)
    t = pos * (1.0 / (length - 1))
    bands = (HY_EMB - 1) // 2
    w = (2.0 * math.pi) * pos / length
    sub = lax.broadcasted_iota(jnp.int32, (FEAT_ROWS, 1), 0)
    band = jnp.where(sub <= bands, sub - 1, sub - 1 - bands).astype(F32)
    f = 1e-4 + band * ((bands - 1 - 1e-4) / (bands - 1))
    fw = f * w
    feats = jnp.where(sub == 0, t, jnp.where(sub <= bands, jnp.cos(fw),
                                             jnp.where(sub <= 2 * bands, -jnp.sin(fw), 0.0)))
    feats = jnp.concatenate([feats, jnp.zeros((LANES - FEAT_ROWS, pb), F32)], axis=0)
    hdot = _dot3
    a = jnp.sin(fr_ref[:, 0:1] * (hdot(w1_ref[...], feats) + b1_ref[...]))
    a = jnp.sin(fr_ref[:, 1:2] * (hdot(w2_ref[...], a) + b2_ref[...]))
    a = jnp.sin(fr_ref[:, 2:3] * (hdot(w3_ref[...], a) + b3_ref[...]))

    @pl.when(i == 0)
    def _():
        ss_ref[...] = jnp.zeros_like(ss_ref)

    nout = w4_ref.shape[0]
    window = jnp.exp(-dl_ref[...] * t)
    a_bf = a.astype(BF16)
    for cb in range(nout // HY_WIDTH):
        rows = slice(cb * HY_WIDTH, (cb + 1) * HY_WIDTH)
        filt = _dot(w4_ref[rows, :].astype(BF16), a_bf) * window
        for j in range(pb // LANES):
            f_ref[rows, j, :] = filt[:, j * LANES:(j + 1) * LANES]
        ss_ref[rows, :] += jnp.sum(filt * filt, axis=1, keepdims=True)


def _hyena_filters_raw(length, w1, b1, w2, b2, w3, b3, w4, freq):
    nout = w4.shape[1]
    pb = min(length, 1024)
    max_decay = math.log(HY_TARGET) / HY_SHORT_DECAY_PCT
    min_decay = math.log(HY_TARGET) / HY_LONG_DECAY_PCT
    deltas = jnp.abs(jnp.linspace(min_decay, max_decay, HY_WIDTH, dtype=F32))
    dl = deltas.reshape(HY_WIDTH, 1)
    w1p = jnp.pad(w1.T, ((0, 0), (0, LANES - w1.shape[0])))
    col = lambda b: b.reshape(-1, 1)
    full = lambda a: pl.BlockSpec(a.shape, lambda i: tuple(0 for _ in a.shape))
    args = (w1p, col(b1), w2.T, col(b2), w3.T, col(b3), w4.T, freq.T, dl)
    return pl.pallas_call(
        functools.partial(_filter_kernel, length=length, pb=pb),
        out_shape=(jax.ShapeDtypeStruct((nout, length // LANES, LANES), F32),
                   jax.ShapeDtypeStruct((nout, 1), F32)),
        grid=(length // pb,),
        in_specs=[full(a) for a in args],
        out_specs=(pl.BlockSpec((nout, pb // LANES, LANES), lambda i: (0, i, 0)),
                   pl.BlockSpec((nout, 1), lambda i: (0, 0))),
        compiler_params=_cparams("arbitrary"),
        name="hyena_filter_mlp",
    )(*args)


def _dft_tables(na):
    k1n = 2 * na
    n = k1n * FFT_N2
    n1 = np.arange(na)[:, None]
    k1 = np.arange(k1n)[None, :]
    ang = 2.0 * np.pi * ((n1 * k1) % k1n) / k1n
    c, s = np.cos(ang), np.sin(ang)
    ma = np.block([[c, -s], [s, c]])
    n1f = np.arange(FFT_N2)[:, None]
    angf = 2.0 * np.pi * ((n1f * k1) % k1n) / k1n
    live = (n1f < na).astype(np.float64)
    ma_f = np.concatenate([np.cos(angf), -np.sin(angf)], axis=1) * live
    ma_b = np.concatenate([np.cos(angf), np.sin(angf)], axis=1) * live
    n2 = np.arange(FFT_N2)[:, None]
    angt = 2.0 * np.pi * ((n2 * k1) % n) / n
    twr, twi = np.cos(angt), -np.sin(angt)
    k2 = np.arange(FFT_N2)[None, :]
    angb = 2.0 * np.pi * ((n2 * k2) % FFT_N2) / FFT_N2
    cb, sb = np.cos(angb), np.sin(angb)
    mb = np.block([[cb, -sb], [sb, cb]])
    mc = np.block([[cb, sb], [-sb, cb]])
    angd = 2.0 * np.pi * ((np.arange(k1n)[:, None] * np.arange(na)[None, :]) % k1n) / k1n
    cd, sd = np.cos(angd) / n, np.sin(angd) / n
    md = np.block([[cd, sd], [-sd, cd]])
    as_bf = lambda a: jnp.asarray(a, dtype=F32).astype(BF16)
    as_f = lambda a: jnp.asarray(a, dtype=F32)
    mb2 = np.concatenate([mb, mc], axis=0)
    return dict(ma=as_bf(ma), ma_f=as_bf(ma_f), ma_b=as_bf(ma_b), twr=as_f(twr), twi=as_f(twi),
                twr_t=as_f(twr.T), twi_t=as_f(twi.T), mb=as_bf(mb), mb2=as_bf(mb2), mc=as_bf(mc), md=as_bf(md))


def _fwd_stages(x2, ma, twr, twi, mb, cb, k1n):
    a = _dot(x2, ma).reshape(cb, FFT_N2, 2 * k1n)
    ar, ai = a[..., :k1n], a[..., k1n:]
    ar2 = ar * twr - ai * twi
    ai2 = ar * twi + ai * twr
    xt = jnp.concatenate([jnp.swapaxes(ar2, 1, 2), jnp.swapaxes(ai2, 1, 2)], axis=-1)
    return _dot(xt.astype(BF16).reshape(cb * k1n, 2 * FFT_N2), mb).reshape(cb, k1n, 2 * FFT_N2)


def _spectrum_kernel(hf_ref, hb_ref, ssf_ref, ssb_ref, maf_ref, mab_ref, twr_ref, twi_ref, mb2_ref, h_ref, *, na):
    k1n = 2 * na
    cb = h_ref.shape[0]

    def stage_a(ref, ss_ref, m_ref):
        x = ref[0, 0] * lax.rsqrt(ss_ref[0, 0])
        x = jnp.concatenate([x, jnp.zeros((cb, FFT_N2 - na, FFT_N2), F32)], axis=1)
        xt = jnp.swapaxes(x, 1, 2).astype(BF16).reshape(cb * FFT_N2, FFT_N2)
        a = _dot(xt, m_ref[...]).reshape(cb, FFT_N2, 2 * k1n)
        return a[..., :k1n], a[..., k1n:]

    fr, fi = stage_a(hf_ref, ssf_ref, maf_ref)
    br, bi = stage_a(hb_ref, ssb_ref, mab_ref)
    twr, twi = twr_ref[...], twi_ref[...]
    parts = (fr * twr - fi * twi, fr * twi + fi * twr, br * twr + bi * twi, bi * twr - br * twi)
    xt = jnp.concatenate([jnp.swapaxes(p, 1, 2) for p in parts], axis=-1)
    h = _dot(xt.astype(BF16).reshape(cb * k1n, 4 * FFT_N2), mb2_ref[...])
    h_ref[...] = h.reshape(cb, k1n, 2 * FFT_N2).astype(BF16)


def _spectrum(filt5, ss5, order, tb, na):
    c = filt5.shape[2]
    k1n = 2 * na
    cb = FFT_CB
    full = lambda a: pl.BlockSpec(a.shape, lambda i: tuple(0 for _ in a.shape))
    fblk = lambda d: pl.BlockSpec((1, 1, cb, na, FFT_N2), lambda i: (order, d, i, 0, 0))
    sblk = lambda d: pl.BlockSpec((1, 1, cb, 1, 1), lambda i: (order, d, i, 0, 0))
    return pl.pallas_call(
        functools.partial(_spectrum_kernel, na=na),
        out_shape=jax.ShapeDtypeStruct((c, k1n, 2 * FFT_N2), BF16),
        grid=(c // cb,),
        in_specs=[fblk(0), fblk(1), sblk(0), sblk(1),
                  full(tb["ma_f"]), full(tb["ma_b"]), full(tb["twr"]), full(tb["twi"]), full(tb["mb2"])],
        out_specs=pl.BlockSpec((cb, k1n, 2 * FFT_N2), lambda i: (i, 0, 0)),
        compiler_params=_cparams("arbitrary"),
        name="hyena_filter_spectrum",
    )(filt5, filt5, ss5, ss5, tb["ma_f"], tb["ma_b"], tb["twr"], tb["twi"], tb["mb2"])


def _fftconv_kernel(u_ref, g_ref, h_ref, sk_ref, cwu_ref, cwg_ref, ma_ref, twr_ref, twi_ref, mb_ref, mc_ref,
                    twrt_ref, twit_ref, md_ref, o_ref, *, na, conv_u):
    k1n = 2 * na
    cb = u_ref.shape[0]
    uf = u_ref[...].astype(F32)
    if conv_u:
        uf = _shortconv_freq(uf, cwu_ref[...], na)
    gate = _shortconv_freq(g_ref[...].astype(F32), cwg_ref[...], na)
    x = _fwd_stages(uf.astype(BF16).reshape(cb * FFT_N2, k1n), ma_ref[...], twr_ref[...], twi_ref[...],
                    mb_ref[...], cb, k1n)
    xr, xi = x[..., :FFT_N2], x[..., FFT_N2:]
    h = h_ref[...].astype(F32)
    hr, hi = h[..., :FFT_N2], h[..., FFT_N2:]
    y = jnp.concatenate([xr * hr - xi * hi, xr * hi + xi * hr], axis=-1)
    c = _dot(y.astype(BF16).reshape(cb * k1n, 2 * FFT_N2), mc_ref[...]).reshape(cb, k1n, 2 * FFT_N2)
    cr, ci = c[..., :FFT_N2], c[..., FFT_N2:]
    twrt, twit = twrt_ref[...], twit_ref[...]
    cr2 = cr * twrt + ci * twit
    ci2 = ci * twrt - cr * twit
    ct = jnp.concatenate([jnp.swapaxes(cr2, 1, 2), jnp.swapaxes(ci2, 1, 2)], axis=-1)
    d = _dot(ct.astype(BF16).reshape(cb * FFT_N2, 2 * k1n), md_ref[...]).reshape(cb, FFT_N2, k1n)
    o_ref[...] = (gate * (d + sk_ref[...] * uf)).astype(BF16)


def _fftconv(u, gate, h, skip, cw_u, cw_g, tb, na, conv_u):
    c = u.shape[0]
    k1n = 2 * na
    cb = FFT_CB
    full = lambda a: pl.BlockSpec(a.shape, lambda i: tuple(0 for _ in a.shape))
    blk = pl.BlockSpec((cb, FFT_N2, k1n), lambda i: (i, 0, 0))
    cwblk = pl.BlockSpec((cb, 1, 4), lambda i: (i, 0, 0))
    return pl.pallas_call(
        functools.partial(_fftconv_kernel, na=na, conv_u=conv_u),
        out_shape=jax.ShapeDtypeStruct(u.shape, BF16),
        grid=(c // cb,),
        in_specs=[blk, blk,
                  pl.BlockSpec((cb, k1n, 2 * FFT_N2), lambda i: (i, 0, 0)),
                  pl.BlockSpec((cb, 1, 1), lambda i: (i, 0, 0)), cwblk, cwblk,
                  full(tb["ma"]), full(tb["twr"]), full(tb["twi"]), full(tb["mb"]), full(tb["mc"]),
                  full(tb["twr_t"]), full(tb["twi_t"]), full(tb["md"])],
        out_specs=blk,
        compiler_params=_cparams("arbitrary"),
        name="hyena_fftconv",
    )(u, gate, h, skip, cw_u, cw_g, tb["ma"], tb["twr"], tb["twi"], tb["mb"], tb["mc"], tb["twr_t"], tb["twi_t"],
      tb["md"])


def _to_freq_layout(a, na):
    nb, _, c = a.shape
    a = a.reshape(nb, na, FFT_N2, c)
    return jnp.transpose(a, (3, 2, 0, 1)).reshape(c, FFT_N2, nb * na)


def _from_freq_layout(a):
    c, _, lanes = a.shape
    na = lanes // 2
    a = a.reshape(c, FFT_N2, 2, na)
    return jnp.transpose(a, (2, 3, 1, 0)).reshape(2, na * FFT_N2, c)


def _ctx_dft_tables(n):
    big = 2 * n
    j = np.arange(n)[:, None]
    k = np.arange(big)[None, :]
    ang = 2.0 * np.pi * ((j * k) % big) / big
    c, s = np.cos(ang), np.sin(ang)
    fh = np.block([[c, -s], [c, s]])
    ff = np.block([[c, -s], [s, c]])
    fi = np.block([[c.T, s.T], [-s.T, c.T]]) / big
    as_bf = lambda a: jnp.asarray(a, dtype=F32).astype(BF16)
    return as_bf(fh), as_bf(ff), as_bf(fi)


def _ctxconv_kernel(u_ref, cw_ref, hf_ref, hb_ref, ssf_ref, ssb_ref, sk_ref, fh_ref, ff_ref, fi_ref, o_ref):
    n_tok = u_ref.shape[2] // 2
    short = lambda j: _shortconv_lanes(u_ref[j].astype(F32), cw_ref[j], n_tok)
    gates = (short(0), short(1))
    z = short(2)
    half = fh_ref.shape[1] // 2
    for n in range(2):
        taps = jnp.concatenate([hf_ref[n] * lax.rsqrt(ssf_ref[n]), hb_ref[n] * lax.rsqrt(ssb_ref[n])], axis=-1)
        h = _dot(taps.astype(BF16), fh_ref[...])
        x = _dot(z.astype(BF16), ff_ref[...])
        hr, hi = h[:, :half], h[:, half:]
        xr, xi = x[:, :half], x[:, half:]
        y = jnp.concatenate([xr * hr - xi * hi, xr * hi + xi * hr], axis=-1)
        conv = _dot(y.astype(BF16), fi_ref[...])
        z = gates[n] * (conv + sk_ref[n] * z)
    o_ref[...] = z.astype(BF16)


def _ctxconv(u3, cw3, hf, hb, ssf, ssb, skip, tables):
    _, c, lanes = u3.shape
    args = (u3, cw3, hf, hb, ssf, ssb, skip) + tuple(tables)
    full = lambda a: pl.BlockSpec(a.shape, lambda i: tuple(0 for _ in a.shape))
    return pl.pallas_call(
        _ctxconv_kernel,
        out_shape=jax.ShapeDtypeStruct((c, lanes), BF16),
        grid=(1,),
        in_specs=[full(a) for a in args],
        out_specs=pl.BlockSpec((c, lanes), lambda i: (0, 0)),
        compiler_params=_cparams("arbitrary"),
        name="hyena_ctxconv",
    )(*args)


def _load_row_tiles(ref, n, stride=SUBLANES, first=0):
    return jnp.concatenate([ref[pl.ds(first + j, n, stride=stride), :] for j in range(SUBLANES)], axis=-1)


def _store_row_tiles(ref, val, n, stride=SUBLANES, first=0):
    for j in range(SUBLANES):
        ref[pl.ds(first + j, n, stride=stride), :] = val[:, j * LANES:(j + 1) * LANES]


def _row_tile(ref, idx, tiles=1):
    size = tiles * SUBLANES
    return ref.at[pl.ds(pl.multiple_of(idx * size, size), size)]


def _top2_rows(vals):
    n = len(vals)
    best_v, best_i = vals[0], jnp.zeros_like(vals[0], dtype=jnp.int32)
    for e in range(1, n):
        take = vals[e] > best_v
        best_v = jnp.where(take, vals[e], best_v)
        best_i = jnp.where(take, e, best_i)
    sec_v = jnp.full_like(vals[0], -jnp.inf)
    sec_i = jnp.zeros_like(best_i)
    for e in range(n):
        take = (best_i != e) & (vals[e] > sec_v)
        sec_v = jnp.where(take, vals[e], sec_v)
        sec_i = jnp.where(take, e, sec_i)
    return best_v, best_i, sec_v, sec_i


def _route(h2, rw_ref, rb_ref, carry_ref, mi_ref, mf_ref, cnt_ref, cols):
    logits = _dot3(rw_ref[...], h2, dot=_dot_nt)
    s = 1.0 / (1.0 + jnp.exp(-logits))
    sel = s + rb_ref[...]
    ng = N_GROUPS
    sel4 = [sel[j * ng:(j + 1) * ng, :] for j in range(EXPERTS_PER_GROUP)]
    s4 = [s[j * ng:(j + 1) * ng, :] for j in range(EXPERTS_PER_GROUP)]
    bv, _, sv, _ = _top2_rows(sel4)
    gscore = bv + sv
    gidx = lax.broadcasted_iota(jnp.int32, (ng, TM), 0)
    gmax = jnp.max(gscore, axis=0, keepdims=True)
    best_g = jnp.min(jnp.where(gscore == gmax, gidx, ng), axis=0, keepdims=True)
    in_best = gidx == best_g
    pick = lambda slab: jnp.sum(jnp.where(in_best, slab, 0.0), axis=0, keepdims=True)
    cand_sel = [pick(v) for v in sel4]
    cand_s = [pick(v) for v in s4]
    _, i1, _, i2 = _top2_rows(cand_sel)
    a1, a2 = cand_s[0], cand_s[0]
    for j in range(1, EXPERTS_PER_GROUP):
        a1 = jnp.where(i1 == j, cand_s[j], a1)
        a2 = jnp.where(i2 == j, cand_s[j], a2)
    den = a1 + a2
    w1, w2 = a1 / den, a2 / den
    swap = i2 < i1
    lo = jnp.where(swap, i2, i1)
    hi = jnp.where(swap, i1, i2)
    w_lo = jnp.where(swap, w2, w1)
    w_hi = jnp.where(swap, w1, w2)
    pair = jnp.where(lo == 0, 0, jnp.where(lo == 1, 3, 5)) + (hi - lo - 1)
    cls = best_g * len(PAIRS) + pair
    cidx = lax.broadcasted_iota(jnp.int32, (CLASS_ROWS, TM), 0)
    hit = cidx == cls
    onehot = jnp.where(hit, 1.0, 0.0)
    tri = jnp.where(lax.broadcasted_iota(jnp.int32, (TM, TM), 0) < lax.broadcasted_iota(jnp.int32, (TM, TM), 1),
                    1.0, 0.0).astype(BF16)
    rank = _dot(onehot.astype(BF16), tri) + carry_ref[:, 0:1]
    rk = jnp.sum(jnp.where(hit, rank, 0.0), axis=0, keepdims=True)
    carry_ref[...] = carry_ref[...] + jnp.sum(onehot, axis=1, keepdims=True)
    mi_ref[:, cols] = jnp.concatenate([cls, rk.astype(jnp.int32), jnp.zeros((6, TM), jnp.int32)], axis=0)
    mf_ref[:, cols] = jnp.concatenate([w_lo, w_hi, jnp.zeros((6, TM), F32)], axis=0)
    cnt_ref[...] = carry_ref[...]


def _post_norm_mod(x, y, m_ref, lng, lnb, gate_row, sh_row, sc_row):
    u = DN_ALPHA * x + m_ref[0, gate_row:gate_row + 1, :] * y
    mu = jnp.mean(u, axis=-1, keepdims=True)
    var = jnp.mean(jnp.square(u - mu), axis=-1, keepdims=True)
    xn = (u - mu) * lax.rsqrt(var + LN_EPS) * lng + lnb
    if sh_row is None:
        return xn, None
    return xn, xn * (1.0 + m_ref[0, sc_row:sc_row + 1, :]) + m_ref[0, sh_row:sh_row + 1, :]


def _outproj_kernel(*refs, layer0, tpb, parts):
    nd = 8 if layer0 else 5
    halves = tuple(refs[p * nd:(p + 1) * nd] for p in range(parts))
    (w_ref, lng_ref, lnb_ref, rw_ref, rb_ref, xo_ref, h2_ref, mi_ref, mf_ref, cnt_ref,
     carry_ref) = refs[parts * nd:]
    i = pl.program_id(0)

    @pl.when(i == 0)
    def _():
        carry_ref[...] = jnp.zeros_like(carry_ref)

    for half, data in enumerate(halves):
        if layer0:
            of_ref, ob_ref, g_ref, hyl_ref, hyc_ref, xl_ref, xc_ref, m_ref = data
            is_ctx = (parts * i + half) % tpb == tpb - 1
            x = jnp.where(is_ctx, xc_ref[...], xl_ref[...])
            hy = jnp.where(is_ctx, hyc_ref[...], hyl_ref[...])
            o = of_ref[...].astype(F32) + ob_ref[...].astype(F32)
            normed = []
            for h in range(RET_HEADS):
                oh = o[:, h * RET_HD:(h + 1) * RET_HD]
                mu = jnp.mean(oh, axis=-1, keepdims=True)
                var = jnp.mean(jnp.square(oh - mu), axis=-1, keepdims=True)
                normed.append((oh - mu) * lax.rsqrt(var + LN_EPS))
            yret = (jnp.concatenate(normed, axis=-1) * _silu(g_ref[...].astype(F32))).astype(BF16)
            wr = yret.shape[1]
            y = _dot(yret, w_ref[:wr, :]) + _dot(hy, w_ref[wr:, :])
        else:
            of_ref, ob_ref, g_ref, x_ref, m_ref = data
            x = x_ref[...]
            o = of_ref[...].astype(F32) + ob_ref[...].astype(F32)
            normed = []
            for h in range(GLA_HEADS):
                oh = o[:, h * GLA_DV:(h + 1) * GLA_DV]
                normed.append(oh * lax.rsqrt(jnp.mean(jnp.square(oh), axis=-1, keepdims=True) + LN_EPS))
            a = (jnp.concatenate(normed, axis=-1) * _silu(g_ref[...].astype(F32))).astype(BF16)
            y = _dot(a, w_ref[...])
        rows = slice(half * TM, (half + 1) * TM)
        xn, h2 = _post_norm_mod(x, y, m_ref, lng_ref[...], lnb_ref[...], 2, 3, 4)
        xo_ref[rows, :] = xn
        _store_row_tiles(h2_ref, h2, TM, first=half * TM * SUBLANES)
        _route(h2, rw_ref, rb_ref, carry_ref, mi_ref, mf_ref, cnt_ref, rows)


def _outproj_call(layer0, tpb, n_tiles, d, half_specs, half_args, w, lng, lnb, router_w, router_bias):
    rows = n_tiles * TM
    parts = next(p for p in (4, 3, 2, 1) if n_tiles % p == 0)
    full2 = lambda a: pl.BlockSpec(a.shape, lambda i: (0, 0))
    regroup = lambda a: jnp.swapaxes(a.reshape(N_GROUPS, EXPERTS_PER_GROUP, -1), 0, 1).reshape(N_EXPERTS, -1)
    rw_t = regroup(router_w.T)
    rb = regroup(router_bias.reshape(N_EXPERTS, 1))
    return pl.pallas_call(
        functools.partial(_outproj_kernel, layer0=layer0, tpb=tpb, parts=parts),
        out_shape=(jax.ShapeDtypeStruct((rows, d), F32),
                   jax.ShapeDtypeStruct((rows * SUBLANES, LANES), F32),
                   jax.ShapeDtypeStruct((8, rows), jnp.int32),
                   jax.ShapeDtypeStruct((8, rows), F32),
                   jax.ShapeDtypeStruct((CLASS_ROWS, LANES), F32)),
        grid=(n_tiles // parts,),
        in_specs=[s for h in range(parts) for s in half_specs(h, parts)]
        + [full2(w), full2(lng), full2(lnb), full2(rw_t), full2(rb)],
        out_specs=(pl.BlockSpec((parts * TM, d), lambda i: (i, 0)),
                   pl.BlockSpec((parts * TM * SUBLANES, LANES), lambda i: (i, 0)),
                   pl.BlockSpec((8, parts * TM), lambda i: (0, i)),
                   pl.BlockSpec((8, parts * TM), lambda i: (0, i)),
                   pl.BlockSpec((CLASS_ROWS, LANES), lambda i: (0, 0))),
        scratch_shapes=[pltpu.VMEM((CLASS_ROWS, LANES), F32)],
        compiler_params=_cparams("arbitrary"),
        name="outproj_norm0" if layer0 else "outproj_norm1",
    )(*(tuple(half_args) * parts), w, lng, lnb, rw_t, rb)


def _dispatch_kernel(slots_ref, zoff_ref, h_ref, xs_ref, zero_ref, sem):
    i = pl.program_id(0)

    @pl.when(i == 0)
    def _():
        zero_ref[...] = jnp.zeros_like(zero_ref)
        def fill_class(e, go):
            @pl.when(zoff_ref[N_CLASSES + 1 + e] > 0)
            def _():
                go(pltpu.make_async_copy(zero_ref, _row_tile(xs_ref, zoff_ref[e], TE), sem))

        for e in range(N_CLASSES):
            fill_class(e, lambda cp: cp.start())
        for e in range(N_CLASSES):
            fill_class(e, lambda cp: cp.wait())
        first_free = zoff_ref[N_CLASSES]
        n_all = xs_ref.shape[0] // (TE * SUBLANES)

        def fill(t, carry):
            pltpu.make_async_copy(zero_ref, _row_tile(xs_ref, t, TE), sem).start()
            return carry

        def fill_wait(t, carry):
            pltpu.make_async_copy(zero_ref, _row_tile(xs_ref, t, TE), sem).wait()
            return carry

        lax.fori_loop(first_free, n_all, fill, 0)
        lax.fori_loop(first_free, n_all, fill_wait, 0)

    def row_copy(r):
        return pltpu.make_async_copy(_row_tile(h_ref, r), _row_tile(xs_ref, slots_ref[0, 0, r]), sem)

    for r in range(DISPATCH_ROWS):
        row_copy(r).start(priority=r % 2)
    for r in range(DISPATCH_ROWS):
        row_copy(r).wait()


def _dispatch(h2, slots3, zoff, p_rows):
    n_steps = h2.shape[0] // (DISPATCH_ROWS * SUBLANES)
    grid_spec = pltpu.PrefetchScalarGridSpec(
        num_scalar_prefetch=0,
        grid=(n_steps,),
        in_specs=[pl.BlockSpec((1, 1, DISPATCH_ROWS), lambda i: (i, 0, 0), memory_space=pltpu.SMEM),
                  pl.BlockSpec(memory_space=pltpu.SMEM),
                  pl.BlockSpec((DISPATCH_ROWS * SUBLANES, LANES), lambda i: (i, 0))],
        out_specs=pl.BlockSpec(memory_space=pl.ANY),
        scratch_shapes=[pltpu.VMEM((TE * SUBLANES, LANES), F32), pltpu.SemaphoreType.DMA(())],
    )
    return pl.pallas_call(
        _dispatch_kernel,
        out_shape=jax.ShapeDtypeStruct((p_rows * SUBLANES, LANES), F32),
        grid_spec=grid_spec,
        compiler_params=_cparams("arbitrary"),
        name="moe_dispatch",
    )(slots3, zoff, h2)


def _experts_kernel(ea_ref, eb_ref, tv_ref, x_ref, wga_ref, wua_ref, wda_ref, wgb_ref, wub_ref, wdb_ref, y_ref):
    j = pl.program_id(0)

    @pl.when(tv_ref[j] > 0)
    def _():
        x = _load_row_tiles(x_ref, TE).astype(BF16)
        for which, (wg, wu, wd) in enumerate(((wga_ref, wua_ref, wda_ref), (wgb_ref, wub_ref, wdb_ref))):
            g = _dot(x, wg[0])
            u = _dot(x, wu[0])
            y = _dot((_silu(g) * u).astype(BF16), wd[0])
            _store_row_tiles(y_ref, y, TE, stride=2 * SUBLANES, first=which * SUBLANES)

    @pl.when(tv_ref[j] == 0)
    def _():
        y_ref[...] = jnp.zeros_like(y_ref)


def _experts(xs, tile_ea, tile_eb, tile_valid, wg, wu, wd):
    n_tiles = xs.shape[0] // (TE * SUBLANES)
    d, hdim = wg.shape[1], wg.shape[2]
    wa = lambda shp: pl.BlockSpec(shp, lambda j, ea, eb, tv: (ea[j], 0, 0))
    wb = lambda shp: pl.BlockSpec(shp, lambda j, ea, eb, tv: (eb[j], 0, 0))
    grid_spec = pltpu.PrefetchScalarGridSpec(
        num_scalar_prefetch=3,
        grid=(n_tiles,),
        in_specs=[pl.BlockSpec((TE * SUBLANES, LANES), lambda j, ea, eb, tv: (j, 0)),
                  wa((1, d, hdim)), wa((1, d, hdim)), wa((1, hdim, d)),
                  wb((1, d, hdim)), wb((1, d, hdim)), wb((1, hdim, d))],
        out_specs=pl.BlockSpec((2 * TE * SUBLANES, LANES), lambda j, ea, eb, tv: (j, 0)),
    )
    return pl.pallas_call(
        _experts_kernel,
        out_shape=jax.ShapeDtypeStruct((2 * xs.shape[0], LANES), F32),
        grid_spec=grid_spec,
        compiler_params=_cparams("arbitrary"),
        name="moe_experts",
    )(tile_ea, tile_eb, tile_valid, xs, wg, wu, wd, wg, wu, wd)


def _combine_kernel(slots_ref, next_slots_ref, ys_ref, wts_ref, x_ref, m_ref, lng_ref, lnb_ref, o_ref, buf_ref, sem):
    i = pl.program_id(0)
    cur = i % 2

    def row_copy(s_ref, b, r):
        return pltpu.make_async_copy(_row_tile(ys_ref, s_ref[0, 0, r], 2), _row_tile(buf_ref.at[b], r, 2), sem.at[b])

    def issue_all(s_ref, b):
        for r in range(TM):
            row_copy(s_ref, b, r).start(priority=r % 2)

    @pl.when(i == 0)
    def _():
        issue_all(slots_ref, 0)

    @pl.when(i + 1 < pl.num_programs(0))
    def _():
        issue_all(next_slots_ref, 1 - cur)

    for r in range(TM):
        row_copy(slots_ref, cur, r).wait()
    buf = buf_ref.at[cur]
    y_lo = _load_row_tiles(buf, TM, stride=2 * SUBLANES)
    y_hi = _load_row_tiles(buf, TM, stride=2 * SUBLANES, first=SUBLANES)
    y = wts_ref[:, 0:1] * y_lo + wts_ref[:, 1:2] * y_hi
    xn, _ = _post_norm_mod(x_ref[...], y, m_ref, lng_ref[...], lnb_ref[...], 5, None, None)
    o_ref[...] = xn


def _combine(ys, slots3, wts, x, mods, lng, lnb, cond_of, n_tiles):
    d = x.shape[1]
    grid_spec = pltpu.PrefetchScalarGridSpec(
        num_scalar_prefetch=0,
        grid=(n_tiles,),
        in_specs=[pl.BlockSpec((1, 1, TM), lambda i: (i, 0, 0), memory_space=pltpu.SMEM),
                  pl.BlockSpec((1, 1, TM), lambda i: (jnp.minimum(i + 1, n_tiles - 1), 0, 0),
                               memory_space=pltpu.SMEM),
                  pl.BlockSpec(memory_space=pl.ANY),
                  pl.BlockSpec((TM, 2), lambda i: (i, 0)),
                  pl.BlockSpec((TM, d), lambda i: (i, 0)),
                  pl.BlockSpec((1, 6, d), lambda i: (cond_of(i), 0, 0)),
                  pl.BlockSpec((1, d), lambda i: (0, 0)),
                  pl.BlockSpec((1, d), lambda i: (0, 0))],
        out_specs=pl.BlockSpec((TM, d), lambda i: (i, 0)),
        scratch_shapes=[pltpu.VMEM((2, 2 * TM * SUBLANES, LANES), F32), pltpu.SemaphoreType.DMA((2,))],
    )
    return pl.pallas_call(
        _combine_kernel,
        out_shape=jax.ShapeDtypeStruct((n_tiles * TM, d), F32),
        grid_spec=grid_spec,
        compiler_params=_cparams("arbitrary"),
        name="moe_combine",
    )(slots3, slots3, ys, wts, x, mods, lng, lnb)


def _moe(h2, mi, mf, cnt, x, cond_of, mods, lng, lnb, layer, wg, wu, wd):
    t = x.shape[0]
    n_tiles = t // TM
    counts = cnt[:N_CLASSES, 0].astype(jnp.int32)
    padded = ((counts + TE - 1) // TE) * TE
    ends = jnp.cumsum(padded)
    offs = ends - padded
    cls = mi[0]
    cids = jnp.arange(N_CLASSES, dtype=jnp.int32)[:, None]
    slots = jnp.sum(jnp.where(cls[None] == cids, offs[:, None], 0), axis=0) + mi[1]
    slots3 = slots.reshape(n_tiles, 1, TM)
    n_cls_tiles = t // TE + N_CLASSES
    total_tiles = ends[-1] // TE
    tile_ids = jnp.arange(n_cls_tiles, dtype=jnp.int32)
    tile_valid = (tile_ids < total_tiles).astype(jnp.int32)
    tile_last = jnp.minimum(tile_ids, total_tiles - 1)
    tile_class = jnp.sum((tile_last[:, None] * TE >= ends[None, :]).astype(jnp.int32), axis=1)
    tile_class = jnp.minimum(tile_class, N_CLASSES - 1)
    pair_lo = jnp.asarray([p[0] for p in PAIRS], jnp.int32)
    pair_hi = jnp.asarray([p[1] for p in PAIRS], jnp.int32)
    pair_sel = (tile_class % len(PAIRS))[:, None] == jnp.arange(len(PAIRS), dtype=jnp.int32)[None, :]
    group0 = (tile_class // len(PAIRS)) * EXPERTS_PER_GROUP
    tile_ea = (group0 + jnp.sum(jnp.where(pair_sel, pair_lo[None, :], 0), axis=1)).astype(jnp.int32)
    tile_eb = (group0 + jnp.sum(jnp.where(pair_sel, pair_hi[None, :], 0), axis=1)).astype(jnp.int32)
    zoff = jnp.concatenate([jnp.maximum(ends // TE - 1, 0), total_tiles[None], padded]).astype(jnp.int32)
    xs = _dispatch(h2, slots.reshape(t // DISPATCH_ROWS, 1, DISPATCH_ROWS), zoff, n_cls_tiles * TE)
    ys = _experts(xs, tile_ea + layer * N_EXPERTS, tile_eb + layer * N_EXPERTS, tile_valid, wg, wu, wd)
    wts = jnp.transpose(mf[0:2])
    return _combine(ys, slots3, wts, x, mods, lng, lnb, cond_of, n_tiles)


def _chunk_cumsum(x, tri):
    hi = x.astype(BF16)
    r1 = x - hi.astype(F32)
    mid = r1.astype(BF16)
    lo = (r1 - mid.astype(F32)).astype(BF16)
    return _dot(tri, hi) + _dot(tri, mid) + _dot(tri, lo)


def _gla_kernel(*refs, nb):
    f_in, b_in, (of_ref, ob_ref, sf_ref, sb_ref) = refs[:6], refs[6:12], refs[12:]

    @pl.when(pl.program_id(0) == 0)
    def _():
        sf_ref[...] = jnp.zeros_like(sf_ref)
        sb_ref[...] = jnp.zeros_like(sb_ref)

    _gla_dir(*f_in, of_ref, sf_ref, False, nb)
    _gla_dir(*b_in, ob_ref, sb_ref, True, nb)


def _gla_dir(q_ref, k_ref, v_ref, lr_ref, gw_ref, gb_ref, o_ref, s_ref, reverse, nb):
    C = GLA_CHUNK
    nchunk = TM // C
    ii = lax.broadcasted_iota(jnp.int32, (C, C), 0)
    jj = lax.broadcasted_iota(jnp.int32, (C, C), 1)
    keep = (jj >= ii) if reverse else (jj <= ii)
    tri = jnp.where(keep, 1.0, 0.0).astype(BF16)
    order = tuple(reversed(range(nchunk))) if reverse else tuple(range(nchunk))
    end_row = 0 if reverse else C - 1
    gw = gw_ref[...]
    gb = gb_ref[...]
    for b in range(nb):
        pre = _dot(lr_ref[b], gw) + gb
        la_all = (jnp.minimum(pre, 0.0) - jnp.log(1.0 + jnp.exp(-jnp.abs(pre)))) * (1.0 / GLA_TAU)
        for c in order:
            rows = slice(c * C, (c + 1) * C)
            bc = _chunk_cumsum(la_all[rows, :], tri)
            bend = bc[end_row:end_row + 1, :]
            eb = jnp.exp(bc)
            enb = jnp.exp(-bc)
            ekb = jnp.exp(bend - bc)
            ebend = jnp.exp(bend)
            for h in range(GLA_HEADS):
                kc = slice(h * GLA_DK, (h + 1) * GLA_DK)
                vc = slice(h * GLA_DV, (h + 1) * GLA_DV)
                q = q_ref[b, rows, kc].astype(F32) * (GLA_DK ** -0.5)
                k = k_ref[b, rows, kc].astype(F32)
                v = v_ref[b, rows, vc]
                qd = (q * eb[:, kc]).astype(BF16)
                kd = (k * enb[:, kc]).astype(BF16)
                st = s_ref[b, h]
                scores = jnp.where(keep, _dot_nt(qd, kd), 0.0)
                o = _dot(scores.astype(BF16), v) + _dot_nt(qd, st.astype(BF16))
                s_ref[b, h] = st * ebend[:, kc] + _dot_tn(v, (k * ekb[:, kc]).astype(BF16))
                o_ref[b, rows, vc] = o.astype(BF16)


def _gla(z3, gw_f, gb_f, gw_b, gb_b):
    nb, s, _ = z3.shape
    nt = s // TM
    kw = GLA_HEADS * GLA_DK
    vw = GLA_HEADS * GLA_DV
    lr_blk = (2 * kw + 2 * vw) // LANES

    def dir_specs(jmap):
        return [pl.BlockSpec((nb, TM, kw), lambda j: (0, jmap(j), 0)),
                pl.BlockSpec((nb, TM, kw), lambda j: (0, jmap(j), 1)),
                pl.BlockSpec((nb, TM, vw), lambda j: (0, jmap(j), 1)),
                pl.BlockSpec((nb, TM, LANES), lambda j: (0, jmap(j), lr_blk)),
                pl.BlockSpec((LANES, kw), lambda j: (0, 0)),
                pl.BlockSpec((1, kw), lambda j: (0, 0))]

    fwd, bwd = _scan_tile_orders(nt)
    state = pltpu.VMEM((nb, GLA_HEADS, GLA_DV, GLA_DK), F32)
    out = jax.ShapeDtypeStruct((nb, s, vw), BF16)
    return pl.pallas_call(
        functools.partial(_gla_kernel, nb=nb),
        out_shape=(out, out),
        grid=(nt,),
        in_specs=dir_specs(fwd) + dir_specs(bwd),
        out_specs=(pl.BlockSpec((nb, TM, vw), lambda j: (0, fwd(j), 0)),
                   pl.BlockSpec((nb, TM, vw), lambda j: (0, bwd(j), 0))),
        scratch_shapes=[state, state],
        compiler_params=_cparams("arbitrary"),
        name="gla",
    )(z3, z3, z3, z3, gw_f, gb_f, z3, z3, z3, z3, gw_b, gb_b)


def kernel(x, c, ctx, c_ctx, mod_w, mod_b, ln_g, ln_b, ab_w_in, ret_log_decay_f, ret_log_decay_b, hy_conv_w, hy_conv_b, hy_w1, hy_b1, hy_w2, hy_b2, hy_w3, hy_b3, hy_w4, hy_freq, hy_skip, ab_w_out, gla_w_in, gla_gate_w_f, gla_gate_b_f, gla_gate_w_b, gla_gate_b_b, gla_w_out, router_w, router_bias, exp_w_gate, exp_w_up, exp_w_down):
    nb, length, d = x.shape
    ctx_len = ctx.shape[1]
    assert nb == 2 and ctx_len == TM and length % TM == 0 and length % (FFT_N2 * 2) == 0
    s = length + ctx_len
    tpb = s // TM
    r = nb * s
    lat_tiles = tpb - 1

    cond = jnp.zeros((8, d), F32).at[0].set(c[0]).at[1].set(c_ctx).at[2].set(c[1]).at[3].set(c_ctx)
    mods = _ada(cond, mod_w, mod_b).reshape(DEPTH, 8, 6, d)

    def cond_of(i):
        return 2 * (i // tpb) + jnp.where(i % tpb == tpb - 1, 1, 0)

    x_lat = x.reshape(nb * length, d)
    x_ctx = ctx.reshape(nb * ctx_len, d)

    ret_w = RET_HEADS * RET_HD
    z0, *hy_lat, hy_ctx = _inproj0(x_lat, x_ctx, mods[0], ab_w_in[0].astype(BF16), tpb, r, 4 * ret_w)
    z0_3 = z0.reshape(nb, s, -1)
    cosf, sinf = _rope_tables(length, ctx_len)
    o_f, o_b = _retention(z0_3, ret_log_decay_f[0], ret_log_decay_b[0], cosf, sinf)

    cw3 = jnp.concatenate([hy_conv_w[0].T, hy_conv_b[0][:, None]], axis=1).reshape(3, HY_WIDTH, 4)
    na = length // FFT_N2
    tb = _dft_tables(na)
    filt_w = (hy_w1[0], hy_b1[0], hy_w2[0], hy_b2[0], hy_w3[0], hy_b3[0], hy_w4[0], hy_freq[0])
    skip = hy_skip[0]

    filt, ss = _hyena_filters_raw(length, *filt_w)
    filt5 = filt.reshape(2, 2, HY_WIDTH, na, FFT_N2)
    ss5 = ss.reshape(2, 2, HY_WIDTH, 1, 1)
    hy_lat = [a.reshape(nb, length, HY_WIDTH) for a in hy_lat]
    zcur = _to_freq_layout(hy_lat[2], na)
    for n in range(2):
        hspec = _spectrum(filt5, ss5, n, tb, na)
        gate = _to_freq_layout(hy_lat[n], na)
        zcur = _fftconv(zcur, gate, hspec, skip[n].reshape(-1, 1, 1), cw3[2].reshape(-1, 1, 4),
                        cw3[n].reshape(-1, 1, 4), tb, na, conv_u=(n == 0))
    y_hy_lat = _from_freq_layout(zcur)

    filt_c, ss_c = _hyena_filters_raw(ctx_len, *filt_w)
    filt_c = filt_c.reshape(2, 2, HY_WIDTH, ctx_len)
    ss_c = ss_c.reshape(2, 2, HY_WIDTH, 1)
    u_ctx = jnp.transpose(hy_ctx.reshape(nb, ctx_len, 3, HY_WIDTH), (2, 3, 0, 1))
    y_ctx_t = _ctxconv(u_ctx.reshape(3, HY_WIDTH, nb * ctx_len), cw3, filt_c[:, 0], filt_c[:, 1], ss_c[:, 0],
                       ss_c[:, 1], skip.reshape(2, HY_WIDTH, 1), _ctx_dft_tables(ctx_len))
    y_hy_ctx = jnp.transpose(y_ctx_t.reshape(HY_WIDTH, nb, ctx_len), (1, 2, 0))

    gate_blk = 3

    def specs0(h, parts):
        tile = lambda i: parts * i + h
        trow = lambda i: (tile(i), 0)
        return [pl.BlockSpec((TM, ret_w), trow), pl.BlockSpec((TM, ret_w), trow),
                pl.BlockSpec((TM, ret_w), lambda i: (tile(i), gate_blk)),
                *_lat_or_ctx_specs(HY_WIDTH, tpb, tile), *_lat_or_ctx_specs(d, tpb, tile),
                pl.BlockSpec((1, 6, d), lambda i: (cond_of(tile(i)), 0, 0))]

    x_mid, h2, mi, mf, cnt = _outproj_call(
        True, tpb, r // TM, d, specs0,
        (o_f.reshape(r, ret_w), o_b.reshape(r, ret_w), z0, y_hy_lat.reshape(nb * length, HY_WIDTH),
         y_hy_ctx.reshape(nb * ctx_len, HY_WIDTH), x_lat, x_ctx, mods[0]),
        ab_w_out[0].astype(BF16), ln_g[0, 0].reshape(1, d), ln_b[0, 0].reshape(1, d), router_w, router_bias)

    stack = lambda w: w.astype(BF16).reshape((DEPTH * N_EXPERTS,) + w.shape[2:])
    exp_w = (stack(exp_w_gate), stack(exp_w_up), stack(exp_w_down))
    x1 = _moe(h2, mi, mf, cnt, x_mid, cond_of, mods[0], ln_g[0, 1].reshape(1, d), ln_b[0, 1].reshape(1, d),
              0, *exp_w)

    kw = GLA_HEADS * GLA_DK
    vw = GLA_HEADS * GLA_DV
    n_in = gla_w_in.shape[2]
    n_pad = 2 * kw + 2 * vw + LANES
    w_in1 = jnp.pad(gla_w_in[0], ((0, 0), (0, n_pad - n_in))).astype(BF16)
    z1 = _inproj(x1, mods[1], w_in1, tpb)
    z1_3 = z1.reshape(nb, s, n_pad)
    gw_f = jnp.zeros((LANES, kw), F32).at[:GLA_RANK].set(gla_gate_w_f[0]).astype(BF16)
    gw_b = jnp.zeros((LANES, kw), F32).at[GLA_RANK:2 * GLA_RANK].set(gla_gate_w_b[0]).astype(BF16)
    g_f, g_b = _gla(z1_3, gw_f, gla_gate_b_f[0].reshape(1, kw), gw_b, gla_gate_b_b[0].reshape(1, kw))

    n_lat = nb * lat_tiles

    def lat_tile(i):
        return (i // lat_tiles) * tpb + i % lat_tiles

    def lat_cond(i):
        return 2 * (i // lat_tiles)

    def specs1(h, parts):
        tile = lambda i: parts * i + h
        lrow = lambda i: (lat_tile(tile(i)), 0)
        return [pl.BlockSpec((TM, vw), lrow), pl.BlockSpec((TM, vw), lrow),
                pl.BlockSpec((TM, vw), lambda i: (lat_tile(tile(i)), 2)),
                pl.BlockSpec((TM, d), lrow),
                pl.BlockSpec((1, 6, d), lambda i: (lat_cond(tile(i)), 0, 0))]

    x_mid1, h2_1, mi1, mf1, cnt1 = _outproj_call(
        False, tpb, n_lat, d, specs1, (g_f.reshape(r, vw), g_b.reshape(r, vw), z1, x1, mods[1]),
        gla_w_out[0].astype(BF16), ln_g[1, 0].reshape(1, d), ln_b[1, 0].reshape(1, d), router_w, router_bias)

    out = _moe(h2_1, mi1, mf1, cnt1, x_mid1, lat_cond, mods[1], ln_g[1, 1].reshape(1, d), ln_b[1, 1].reshape(1, d),
               1, *exp_w)
    return out.reshape(nb, length, d)
```

```python
import functools
import math

import numpy as np
import jax
import jax.numpy as jnp
from jax import lax
from jax.experimental import pallas as pl
from jax.experimental.pallas import tpu as pltpu

F32 = jnp.float32
BF16 = jnp.bfloat16

GRID_W = 64
RET_HEADS = 4
RET_HD = 128
RET_CHUNK = 128
ROPE_BASE = 10000.0
HY_WIDTH = 512
HY_EMB = 33
HY_FFN = 64
HY_SHORT_DECAY_PCT = 0.3
HY_LONG_DECAY_PCT = 1.5
HY_TARGET = 1e-2
GLA_HEADS = 4
GLA_DK = 128
GLA_DV = 256
GLA_RANK = 16
GLA_TAU = 16.0
GLA_CHUNK = 64
N_EXPERTS = 16
N_GROUPS = 4
EXPERTS_PER_GROUP = 4
LN_EPS = 1e-5
DEPTH = 2
DN_ALPHA = (2.0 * DEPTH) ** 0.25

PAIRS = ((0, 1), (0, 2), (0, 3), (1, 2), (1, 3), (2, 3))
N_CLASSES = N_GROUPS * len(PAIRS)
CLASS_ROWS = 32

LANES = 128
SUBLANES = 8
TM = 256
TE = 256
DISPATCH_ROWS = (6 * TM, 4 * TM, 2 * TM, TM)
FFT_N2 = 128
FFT_CB = 16
VMEM_LIMIT = 48 * 1024 * 1024


def _cparams(*sem):
    return pltpu.CompilerParams(dimension_semantics=sem, vmem_limit_bytes=VMEM_LIMIT)


def _silu(v):
    return v * (1.0 / (1.0 + jnp.exp(-v)))


def _dot(a, b):
    return jnp.dot(a, b, preferred_element_type=F32)


def _dot_nt(a, b):
    return lax.dot_general(a, b, (((1,), (1,)), ((), ())), preferred_element_type=F32)


def _dot_tn(a, b):
    return lax.dot_general(a, b, (((0,), (0,)), ((), ())), preferred_element_type=F32)


def _split2(a):
    hi = a.astype(BF16)
    return hi, (a - hi.astype(F32)).astype(BF16)


def _dot3(a, b, dot=_dot):
    ah, al = _split2(a)
    bh, bl = _split2(b)
    return dot(ah, bh) + dot(ah, bl) + dot(al, bh)


def _ada_kernel(c_ref, w_ref, b_ref, o_ref):
    a = _silu(c_ref[...])
    o_ref[0] = _dot3(a, w_ref[0]) + b_ref[0]


def _ada(cond, mod_w, mod_b):
    depth, d, n = mod_w.shape
    nt = n // 4
    return pl.pallas_call(
        _ada_kernel,
        out_shape=jax.ShapeDtypeStruct((depth, 8, n), F32),
        grid=(depth, n // nt),
        in_specs=[pl.BlockSpec((8, d), lambda l, j: (0, 0)),
                  pl.BlockSpec((1, d, nt), lambda l, j: (l, 0, j)),
                  pl.BlockSpec((1, 1, nt), lambda l, j: (l, 0, j))],
        out_specs=pl.BlockSpec((1, 8, nt), lambda l, j: (l, 0, j)),
        compiler_params=_cparams("arbitrary", "arbitrary"),
        name="ada_mod",
    )(cond, mod_w, mod_b.reshape(depth, 1, n))


def _inproj_kernel(x_ref, ma_ref, mb_ref, w_ref, o_ref, *, chunks):
    hs = []
    for half, m_ref in enumerate((ma_ref, mb_ref)):
        x = x_ref[half * TM:(half + 1) * TM, :]
        hs.append((x * (1.0 + m_ref[0, 1:2, :]) + m_ref[0, 0:1, :]).astype(BF16))
    h = jnp.concatenate(hs, axis=0)
    for lo, hi in chunks:
        o_ref[:, lo:hi] = _dot(h, w_ref[:, lo:hi]).astype(BF16)


def _lat_or_ctx_specs(width, tpb, tile_of=lambda i: i):
    lat_tiles = tpb - 1
    lat = lambda i: ((tile_of(i) // tpb) * lat_tiles + jnp.minimum(tile_of(i) % tpb, lat_tiles - 1), 0)
    ctx = lambda i: (tile_of(i) // tpb, 0)
    return pl.BlockSpec((TM, width), lat), pl.BlockSpec((TM, width), ctx)


def _col_chunks(n, width=512):
    out, lo = [], 0
    while lo < n:
        hi = min(lo + width, n)
        out.append((lo, hi))
        lo = hi
    return tuple(out)


def _inproj(xu, mods, w, tiles_per_batch):
    r, d = xu.shape
    n = w.shape[1]
    tpb = tiles_per_batch

    def cond_of(i):
        return 2 * (i // tpb) + jnp.where(i % tpb == tpb - 1, 1, 0)

    return pl.pallas_call(
        functools.partial(_inproj_kernel, chunks=_col_chunks(n)),
        out_shape=jax.ShapeDtypeStruct((r, n), BF16),
        grid=(r // (2 * TM),),
        in_specs=[pl.BlockSpec((2 * TM, d), lambda i: (i, 0)),
                  pl.BlockSpec((1, 6, d), lambda i: (cond_of(2 * i), 0, 0)),
                  pl.BlockSpec((1, 6, d), lambda i: (cond_of(2 * i + 1), 0, 0)),
                  pl.BlockSpec((d, n), lambda i: (0, 0))],
        out_specs=pl.BlockSpec((2 * TM, n), lambda i: (i, 0)),
        compiler_params=_cparams("arbitrary"),
        name="inproj1",
    )(xu, mods, mods, w)


def _inproj0_kernel(x_ref, c_ref, m_ref, w_ref, z_ref, x1_ref, x2_ref, v_ref, hc_ref, *, tpb, ret_w):
    is_ctx = pl.program_id(0) % tpb == tpb - 1
    x = jnp.where(is_ctx, c_ref[...], x_ref[...])
    h = (x * (1.0 + m_ref[0, 1:2, :]) + m_ref[0, 0:1, :]).astype(BF16)
    for lo, hi in _col_chunks(ret_w):
        z_ref[:, lo:hi] = _dot(h, w_ref[:, lo:hi]).astype(BF16)
    hy = [_dot(h, w_ref[:, ret_w + j * HY_WIDTH:ret_w + (j + 1) * HY_WIDTH]).astype(BF16) for j in range(3)]

    @pl.when(jnp.logical_not(is_ctx))
    def _():
        for ref, val in zip((x1_ref, x2_ref, v_ref), hy):
            ref[...] = val

    @pl.when(is_ctx)
    def _():
        for j, val in enumerate(hy):
            hc_ref[:, j * HY_WIDTH:(j + 1) * HY_WIDTH] = val


def _inproj0(x_lat, x_ctx, mods, w, tpb, r, ret_w):
    d = w.shape[0]
    lat_tiles = tpb - 1

    def cond_of(i):
        return 2 * (i // tpb) + jnp.where(i % tpb == tpb - 1, 1, 0)

    lat_spec, ctx_spec = _lat_or_ctx_specs(d, tpb)
    lat_out = pl.BlockSpec((TM, HY_WIDTH), lambda i: ((i // tpb) * lat_tiles + jnp.minimum(i % tpb, lat_tiles - 1), 0))
    hy_lat = jax.ShapeDtypeStruct(((r // TM // tpb) * lat_tiles * TM, HY_WIDTH), BF16)
    return pl.pallas_call(
        functools.partial(_inproj0_kernel, tpb=tpb, ret_w=ret_w),
        out_shape=(jax.ShapeDtypeStruct((r, ret_w), BF16), hy_lat, hy_lat, hy_lat,
                   jax.ShapeDtypeStruct(((r // TM // tpb) * TM, 3 * HY_WIDTH), BF16)),
        grid=(r // TM,),
        in_specs=[lat_spec, ctx_spec, pl.BlockSpec((1, 6, d), lambda i: (cond_of(i), 0, 0)),
                  pl.BlockSpec(w.shape, lambda i: (0, 0))],
        out_specs=(pl.BlockSpec((TM, ret_w), lambda i: (i, 0)), lat_out, lat_out, lat_out,
                   pl.BlockSpec((TM, 3 * HY_WIDTH), lambda i: (i // tpb, 0))),
        compiler_params=_cparams("arbitrary"),
        name="inproj0",
    )(x_lat, x_ctx, mods, w)


def _ret_kernel(lgf_ref, lgb_ref, *refs, nb):
    f_in, b_in, (of_ref, ob_ref, sf_ref, sb_ref) = refs[:5], refs[5:10], refs[10:]

    @pl.when(pl.program_id(0) == 0)
    def _():
        sf_ref[...] = jnp.zeros_like(sf_ref)
        sb_ref[...] = jnp.zeros_like(sb_ref)

    _ret_dir(lgf_ref, *f_in, of_ref, sf_ref, False, nb)
    _ret_dir(lgb_ref, *b_in, ob_ref, sb_ref, True, nb)


def _ret_dir(lg_ref, q_ref, k_ref, v_ref, cos_ref, sin_ref, o_ref, s_ref, reverse, nb):
    C = RET_CHUNK
    ii = lax.broadcasted_iota(jnp.int32, (C, C), 0).astype(F32)
    jj = lax.broadcasted_iota(jnp.int32, (C, C), 1).astype(F32)
    ci = lax.broadcasted_iota(jnp.int32, (C, 1), 0).astype(F32)
    diff = (jj - ii) if reverse else (ii - jj)
    order = (1, 0) if reverse else (0, 1)
    for h in range(RET_HEADS):
        lg = lg_ref[h]
        intra = jnp.where(diff >= 0, jnp.exp(jnp.maximum(diff, 0.0) * lg), 0.0)
        if reverse:
            q_dec = jnp.exp((C - ci) * lg)
            k_dec = jnp.exp(ci * lg)
        else:
            q_dec = jnp.exp((ci + 1.0) * lg)
            k_dec = jnp.exp((C - 1.0 - ci) * lg)
        c_dec = jnp.exp(jnp.zeros((1, RET_HD), F32) + C * lg)
        cols = slice(h * RET_HD, (h + 1) * RET_HD)
        for b in range(nb):
            for c in order:
                rows = slice(c * C, (c + 1) * C)
                cosf = cos_ref[rows, :]
                sinf = sin_ref[rows, :]
                q = q_ref[b, rows, cols].astype(F32)
                k = k_ref[b, rows, cols].astype(F32)
                v = v_ref[b, rows, cols]
                q = q * cosf + pltpu.roll(q, RET_HD // 2, 1) * sinf
                k = (k * cosf + pltpu.roll(k, RET_HD // 2, 1) * sinf) * (RET_HD ** -0.5)
                st = s_ref[b, h]
                scores = _dot_nt(q.astype(BF16), k.astype(BF16)) * intra
                o = _dot(scores.astype(BF16), v) + _dot_nt((q * q_dec).astype(BF16), st.astype(BF16))
                s_ref[b, h] = c_dec * st + _dot_tn(v, (k * k_dec).astype(BF16))
                o_ref[b, rows, cols] = o.astype(BF16)


def _scan_tile_orders(nt):
    fwd = lambda j: (j + nt - 1) % nt
    bwd = lambda j: jnp.where(j == 0, nt - 1, nt - 1 - j)
    return fwd, bwd


def _retention(z3, lg_f, lg_b, cosf, sinf):
    nb, s, _ = z3.shape
    nt = s // TM
    w = RET_HEADS * RET_HD
    smem = pl.BlockSpec(memory_space=pltpu.SMEM)

    def dir_specs(jmap):
        return [pl.BlockSpec((nb, TM, w), lambda j: (0, jmap(j), 0)),
                pl.BlockSpec((nb, TM, w), lambda j: (0, jmap(j), 1)),
                pl.BlockSpec((nb, TM, w), lambda j: (0, jmap(j), 2)),
                pl.BlockSpec((TM, RET_HD), lambda j: (jmap(j), 0)),
                pl.BlockSpec((TM, RET_HD), lambda j: (jmap(j), 0))]

    fwd, bwd = _scan_tile_orders(nt)
    state = pltpu.VMEM((nb, RET_HEADS, RET_HD, RET_HD), F32)
    out = jax.ShapeDtypeStruct((nb, s, w), BF16)
    grid_spec = pltpu.PrefetchScalarGridSpec(
        num_scalar_prefetch=0,
        grid=(nt,),
        in_specs=[smem, smem] + dir_specs(fwd) + dir_specs(bwd),
        out_specs=(pl.BlockSpec((nb, TM, w), lambda j: (0, fwd(j), 0)),
                   pl.BlockSpec((nb, TM, w), lambda j: (0, bwd(j), 0))),
        scratch_shapes=[state, state],
    )
    return pl.pallas_call(
        functools.partial(_ret_kernel, nb=nb),
        out_shape=(out, out),
        grid_spec=grid_spec,
        compiler_params=_cparams("arbitrary"),
        name="retention",
    )(lg_f, lg_b, z3, z3, z3, cosf, sinf, z3, z3, z3, cosf, sinf)


def _rope_tables(length, ctx_len):
    rows = length // GRID_W
    quarter = RET_HD // 4
    inv = ROPE_BASE ** (-jnp.arange(quarter, dtype=F32) / quarter)
    def expand(fn):
        by_row = fn(jnp.arange(rows, dtype=F32)[:, None] * inv)
        by_col = fn(jnp.arange(GRID_W, dtype=F32)[:, None] * inv)
        by_row = jnp.broadcast_to(by_row[:, None, :], (rows, GRID_W, quarter)).reshape(length, quarter)
        by_col = jnp.broadcast_to(by_col[None, :, :], (rows, GRID_W, quarter)).reshape(length, quarter)
        return jnp.concatenate([by_row, by_col], axis=-1)

    cos, sin = expand(jnp.cos), expand(jnp.sin)
    cosf = jnp.concatenate([cos, cos], axis=-1)
    sinf = jnp.concatenate([-sin, sin], axis=-1)
    cosf = jnp.concatenate([cosf, jnp.ones((ctx_len, RET_HD), F32)], axis=0)
    sinf = jnp.concatenate([sinf, jnp.zeros((ctx_len, RET_HD), F32)], axis=0)
    return cosf, sinf


def _shortconv_freq(x, cw, na):
    row = lax.broadcasted_iota(jnp.int32, (1, FFT_N2, 1), 1)
    lane = lax.broadcasted_iota(jnp.int32, (1, 1, 2 * na), 2)
    wrap_prev = pltpu.roll(x[:, FFT_N2 - 1:FFT_N2, :], 1, 2)
    prev = jnp.where(row == 0, jnp.where((lane == 0) | (lane == na), 0.0, wrap_prev), pltpu.roll(x, 1, 1))
    wrap_next = pltpu.roll(x[:, 0:1, :], 2 * na - 1, 2)
    nxt = jnp.where(row == FFT_N2 - 1, jnp.where((lane == na - 1) | (lane == 2 * na - 1), 0.0, wrap_next),
                    pltpu.roll(x, FFT_N2 - 1, 1))
    return cw[..., 0:1] * prev + cw[..., 1:2] * x + cw[..., 2:3] * nxt + cw[..., 3:4]


def _shortconv_lanes(x, cw, n):
    lane = lax.broadcasted_iota(jnp.int32, (1, 2 * n), 1)
    prev = jnp.where((lane == 0) | (lane == n), 0.0, pltpu.roll(x, 1, 1))
    nxt = jnp.where((lane == n - 1) | (lane == 2 * n - 1), 0.0, pltpu.roll(x, 2 * n - 1, 1))
    return cw[:, 0:1] * prev + cw[:, 1:2] * x + cw[:, 2:3] * nxt + cw[:, 3:4]


FEAT_ROWS = 40


def _filter_kernel(w1_ref, b1_ref, w2_ref, b2_ref, w3_ref, b3_ref, w4_ref, fr_ref, dl_ref,
                   f_ref, ss_ref, *, length, pb):
    i = pl.program_id(0)
    pos = (lax.broadcasted_iota(jnp.int32, (1, pb), 1) + i * pb).astype(F32)
    t = pos * (1.0 / (length - 1))
    bands = (HY_EMB - 1) // 2
    w = (2.0 * math.pi) * pos / length
    sub = lax.broadcasted_iota(jnp.int32, (FEAT_ROWS, 1), 0)
    band = jnp.where(sub <= bands, sub - 1, sub - 1 - bands).astype(F32)
    f = 1e-4 + band * ((bands - 1 - 1e-4) / (bands - 1))
    fw = f * w
    feats = jnp.where(sub == 0, t, jnp.where(sub <= bands, jnp.cos(fw),
                                             jnp.where(sub <= 2 * bands, -jnp.sin(fw), 0.0)))
    feats = jnp.concatenate([feats, jnp.zeros((LANES - FEAT_ROWS, pb), F32)], axis=0)
    hdot = _dot3
    a = jnp.sin(fr_ref[:, 0:1] * (hdot(w1_ref[...], feats) + b1_ref[...]))
    a = jnp.sin(fr_ref[:, 1:2] * (hdot(w2_ref[...], a) + b2_ref[...]))
    a = jnp.sin(fr_ref[:, 2:3] * (hdot(w3_ref[...], a) + b3_ref[...]))

    @pl.when(i == 0)
    def _():
        ss_ref[...] = jnp.zeros_like(ss_ref)

    nout = w4_ref.shape[0]
    window = jnp.exp(-dl_ref[...] * t)
    a_bf = a.astype(BF16)
    for cb in range(nout // HY_WIDTH):
        rows = slice(cb * HY_WIDTH, (cb + 1) * HY_WIDTH)
        filt = _dot(w4_ref[rows, :].astype(BF16), a_bf) * window
        for j in range(pb // LANES):
            f_ref[rows, j, :] = filt[:, j * LANES:(j + 1) * LANES]
        ss_ref[rows, :] += jnp.sum(filt * filt, axis=1, keepdims=True)


def _hyena_filters_raw(length, w1, b1, w2, b2, w3, b3, w4, freq):
    nout = w4.shape[1]
    pb = min(length, 1024)
    max_decay = math.log(HY_TARGET) / HY_SHORT_DECAY_PCT
    min_decay = math.log(HY_TARGET) / HY_LONG_DECAY_PCT
    deltas = jnp.abs(jnp.linspace(min_decay, max_decay, HY_WIDTH, dtype=F32))
    dl = deltas.reshape(HY_WIDTH, 1)
    w1p = jnp.pad(w1.T, ((0, 0), (0, LANES - w1.shape[0])))
    col = lambda b: b.reshape(-1, 1)
    full = lambda a: pl.BlockSpec(a.shape, lambda i: tuple(0 for _ in a.shape))
    args = (w1p, col(b1), w2.T, col(b2), w3.T, col(b3), w4.T, freq.T, dl)
    return pl.pallas_call(
        functools.partial(_filter_kernel, length=length, pb=pb),
        out_shape=(jax.ShapeDtypeStruct((nout, length // LANES, LANES), F32),
                   jax.ShapeDtypeStruct((nout, 1), F32)),
        grid=(length // pb,),
        in_specs=[full(a) for a in args],
        out_specs=(pl.BlockSpec((nout, pb // LANES, LANES), lambda i: (0, i, 0)),
                   pl.BlockSpec((nout, 1), lambda i: (0, 0))),
        compiler_params=_cparams("arbitrary"),
        name="hyena_filter_mlp",
    )(*args)


def _dft_tables(na):
    k1n = 2 * na
    n = k1n * FFT_N2
    n1 = np.arange(na)[:, None]
    k1 = np.arange(k1n)[None, :]
    ang = 2.0 * np.pi * ((n1 * k1) % k1n) / k1n
    c, s = np.cos(ang), np.sin(ang)
    ma = np.block([[c, -s], [s, c]])
    n1f = np.arange(FFT_N2)[:, None]
    angf = 2.0 * np.pi * ((n1f * k1) % k1n) / k1n
    live = (n1f < na).astype(np.float64)
    ma_f = np.concatenate([np.cos(angf), -np.sin(angf)], axis=1) * live
    ma_b = np.concatenate([np.cos(angf), np.sin(angf)], axis=1) * live
    n2 = np.arange(FFT_N2)[:, None]
    angt = 2.0 * np.pi * ((n2 * k1) % n) / n
    twr, twi = np.cos(angt), -np.sin(angt)
    k2 = np.arange(FFT_N2)[None, :]
    angb = 2.0 * np.pi * ((n2 * k2) % FFT_N2) / FFT_N2
    cb, sb = np.cos(angb), np.sin(angb)
    mb = np.block([[cb, -sb], [sb, cb]])
    mc = np.block([[cb, sb], [-sb, cb]])
    angd = 2.0 * np.pi * ((np.arange(k1n)[:, None] * np.arange(na)[None, :]) % k1n) / k1n
    cd, sd = np.cos(angd) / n, np.sin(angd) / n
    md = np.block([[cd, sd], [-sd, cd]])
    as_bf = lambda a: jnp.asarray(a, dtype=F32).astype(BF16)
    as_f = lambda a: jnp.asarray(a, dtype=F32)
    mb2 = np.concatenate([mb, mc], axis=0)
    return dict(ma=as_bf(ma), ma_f=as_bf(ma_f), ma_b=as_bf(ma_b), twr=as_f(twr), twi=as_f(twi),
                twr_t=as_f(twr.T), twi_t=as_f(twi.T), mb=as_bf(mb), mb2=as_bf(mb2), mc=as_bf(mc), md=as_bf(md))


def _fwd_stages(x2, ma, twr, twi, mb, cb, k1n):
    a = _dot(x2, ma).reshape(cb, FFT_N2, 2 * k1n)
    ar, ai = a[..., :k1n], a[..., k1n:]
    ar2 = ar * twr - ai * twi
    ai2 = ar * twi + ai * twr
    xt = jnp.concatenate([jnp.swapaxes(ar2, 1, 2), jnp.swapaxes(ai2, 1, 2)], axis=-1)
    return _dot(xt.astype(BF16).reshape(cb * k1n, 2 * FFT_N2), mb).reshape(cb, k1n, 2 * FFT_N2)


def _spectrum_kernel(hf_ref, hb_ref, ssf_ref, ssb_ref, maf_ref, mab_ref, twr_ref, twi_ref, mb2_ref, h_ref, *, na):
    k1n = 2 * na
    cb = h_ref.shape[0]

    def stage_a(ref, ss_ref, m_ref):
        x = ref[0, 0] * lax.rsqrt(ss_ref[0, 0])
        x = jnp.concatenate([x, jnp.zeros((cb, FFT_N2 - na, FFT_N2), F32)], axis=1)
        xt = jnp.swapaxes(x, 1, 2).astype(BF16).reshape(cb * FFT_N2, FFT_N2)
        a = _dot(xt, m_ref[...]).reshape(cb, FFT_N2, 2 * k1n)
        return a[..., :k1n], a[..., k1n:]

    fr, fi = stage_a(hf_ref, ssf_ref, maf_ref)
    br, bi = stage_a(hb_ref, ssb_ref, mab_ref)
    twr, twi = twr_ref[...], twi_ref[...]
    parts = (fr * twr - fi * twi, fr * twi + fi * twr, br * twr + bi * twi, bi * twr - br * twi)
    xt = jnp.concatenate([jnp.swapaxes(p, 1, 2) for p in parts], axis=-1)
    h = _dot(xt.astype(BF16).reshape(cb * k1n, 4 * FFT_N2), mb2_ref[...])
    h_ref[...] = h.reshape(cb, k1n, 2 * FFT_N2).astype(BF16)


def _spectrum(filt5, ss5, order, tb, na):
    c = filt5.shape[2]
    k1n = 2 * na
    cb = FFT_CB
    full = lambda a: pl.BlockSpec(a.shape, lambda i: tuple(0 for _ in a.shape))
    fblk = lambda d: pl.BlockSpec((1, 1, cb, na, FFT_N2), lambda i: (order, d, i, 0, 0))
    sblk = lambda d: pl.BlockSpec((1, 1, cb, 1, 1), lambda i: (order, d, i, 0, 0))
    return pl.pallas_call(
        functools.partial(_spectrum_kernel, na=na),
        out_shape=jax.ShapeDtypeStruct((c, k1n, 2 * FFT_N2), BF16),
        grid=(c // cb,),
        in_specs=[fblk(0), fblk(1), sblk(0), sblk(1),
                  full(tb["ma_f"]), full(tb["ma_b"]), full(tb["twr"]), full(tb["twi"]), full(tb["mb2"])],
        out_specs=pl.BlockSpec((cb, k1n, 2 * FFT_N2), lambda i: (i, 0, 0)),
        compiler_params=_cparams("arbitrary"),
        name="hyena_filter_spectrum",
    )(filt5, filt5, ss5, ss5, tb["ma_f"], tb["ma_b"], tb["twr"], tb["twi"], tb["mb2"])


def _fftconv_kernel(u_ref, g_ref, h_ref, sk_ref, cwu_ref, cwg_ref, ma_ref, twr_ref, twi_ref, mb_ref, mc_ref,
                    twrt_ref, twit_ref, md_ref, o_ref, *, na, conv_u):
    k1n = 2 * na
    cb = u_ref.shape[0]
    uf = u_ref[...].astype(F32)
    if conv_u:
        uf = _shortconv_freq(uf, cwu_ref[...], na)
    gate = _shortconv_freq(g_ref[...].astype(F32), cwg_ref[...], na)
    x = _fwd_stages(uf.astype(BF16).reshape(cb * FFT_N2, k1n), ma_ref[...], twr_ref[...], twi_ref[...],
                    mb_ref[...], cb, k1n)
    xr, xi = x[..., :FFT_N2], x[..., FFT_N2:]
    h = h_ref[...].astype(F32)
    hr, hi = h[..., :FFT_N2], h[..., FFT_N2:]
    y = jnp.concatenate([xr * hr - xi * hi, xr * hi + xi * hr], axis=-1)
    c = _dot(y.astype(BF16).reshape(cb * k1n, 2 * FFT_N2), mc_ref[...]).reshape(cb, k1n, 2 * FFT_N2)
    cr, ci = c[..., :FFT_N2], c[..., FFT_N2:]
    twrt, twit = twrt_ref[...], twit_ref[...]
    cr2 = cr * twrt + ci * twit
    ci2 = ci * twrt - cr * twit
    ct = jnp.concatenate([jnp.swapaxes(cr2, 1, 2), jnp.swapaxes(ci2, 1, 2)], axis=-1)
    d = _dot(ct.astype(BF16).reshape(cb * FFT_N2, 2 * k1n), md_ref[...]).reshape(cb, FFT_N2, k1n)
    o_ref[...] = (gate * (d + sk_ref[...] * uf)).astype(BF16)


def _fftconv(u, gate, h, skip, cw_u, cw_g, tb, na, conv_u):
    c = u.shape[0]
    k1n = 2 * na
    cb = FFT_CB
    full = lambda a: pl.BlockSpec(a.shape, lambda i: tuple(0 for _ in a.shape))
    blk = pl.BlockSpec((cb, FFT_N2, k1n), lambda i: (i, 0, 0))
    cwblk = pl.BlockSpec((cb, 1, 4), lambda i: (i, 0, 0))
    return pl.pallas_call(
        functools.partial(_fftconv_kernel, na=na, conv_u=conv_u),
        out_shape=jax.ShapeDtypeStruct(u.shape, BF16),
        grid=(c // cb,),
        in_specs=[blk, blk,
                  pl.BlockSpec((cb, k1n, 2 * FFT_N2), lambda i: (i, 0, 0)),
                  pl.BlockSpec((cb, 1, 1), lambda i: (i, 0, 0)), cwblk, cwblk,
                  full(tb["ma"]), full(tb["twr"]), full(tb["twi"]), full(tb["mb"]), full(tb["mc"]),
                  full(tb["twr_t"]), full(tb["twi_t"]), full(tb["md"])],
        out_specs=blk,
        compiler_params=_cparams("arbitrary"),
        name="hyena_fftconv",
    )(u, gate, h, skip, cw_u, cw_g, tb["ma"], tb["twr"], tb["twi"], tb["mb"], tb["mc"], tb["twr_t"], tb["twi_t"],
      tb["md"])


def _to_freq_layout(a, na):
    nb, _, c = a.shape
    a = a.reshape(nb, na, FFT_N2, c)
    return jnp.transpose(a, (3, 2, 0, 1)).reshape(c, FFT_N2, nb * na)


def _from_freq_layout(a):
    c, _, lanes = a.shape
    na = lanes // 2
    a = a.reshape(c, FFT_N2, 2, na)
    return jnp.transpose(a, (2, 3, 1, 0)).reshape(2, na * FFT_N2, c)


def _ctx_dft_tables(n):
    big = 2 * n
    j = np.arange(n)[:, None]
    k = np.arange(big)[None, :]
    ang = 2.0 * np.pi * ((j * k) % big) / big
    c, s = np.cos(ang), np.sin(ang)
    fh = np.block([[c, -s], [c, s]])
    ff = np.block([[c, -s], [s, c]])
    fi = np.block([[c.T, s.T], [-s.T, c.T]]) / big
    as_bf = lambda a: jnp.asarray(a, dtype=F32).astype(BF16)
    return as_bf(fh), as_bf(ff), as_bf(fi)


def _ctxconv_kernel(u_ref, cw_ref, hf_ref, hb_ref, ssf_ref, ssb_ref, sk_ref, fh_ref, ff_ref, fi_ref, o_ref):
    n_tok = u_ref.shape[2] // 2
    short = lambda j: _shortconv_lanes(u_ref[j].astype(F32), cw_ref[j], n_tok)
    gates = (short(0), short(1))
    z = short(2)
    half = fh_ref.shape[1] // 2
    for n in range(2):
        taps = jnp.concatenate([hf_ref[n] * lax.rsqrt(ssf_ref[n]), hb_ref[n] * lax.rsqrt(ssb_ref[n])], axis=-1)
        h = _dot(taps.astype(BF16), fh_ref[...])
        x = _dot(z.astype(BF16), ff_ref[...])
        hr, hi = h[:, :half], h[:, half:]
        xr, xi = x[:, :half], x[:, half:]
        y = jnp.concatenate([xr * hr - xi * hi, xr * hi + xi * hr], axis=-1)
        conv = _dot(y.astype(BF16), fi_ref[...])
        z = gates[n] * (conv + sk_ref[n] * z)
    o_ref[...] = z.astype(BF16)


def _ctxconv(u3, cw3, hf, hb, ssf, ssb, skip, tables):
    _, c, lanes = u3.shape
    args = (u3, cw3, hf, hb, ssf, ssb, skip) + tuple(tables)
    full = lambda a: pl.BlockSpec(a.shape, lambda i: tuple(0 for _ in a.shape))
    return pl.pallas_call(
        _ctxconv_kernel,
        out_shape=jax.ShapeDtypeStruct((c, lanes), BF16),
        grid=(1,),
        in_specs=[full(a) for a in args],
        out_specs=pl.BlockSpec((c, lanes), lambda i: (0, 0)),
        compiler_params=_cparams("arbitrary"),
        name="hyena_ctxconv",
    )(*args)


def _load_row_tiles(ref, n, stride=SUBLANES, first=0):
    return jnp.concatenate([ref[pl.ds(first + j, n, stride=stride), :] for j in range(SUBLANES)], axis=-1)


def _store_row_tiles(ref, val, n, stride=SUBLANES, first=0):
    for j in range(SUBLANES):
        ref[pl.ds(first + j, n, stride=stride), :] = val[:, j * LANES:(j + 1) * LANES]


def _row_tile(ref, idx, tiles=1):
    size = tiles * SUBLANES
    return ref.at[pl.ds(pl.multiple_of(idx * size, size), size)]


def _top2_rows(vals):
    n = len(vals)
    best_v, best_i = vals[0], jnp.zeros_like(vals[0], dtype=jnp.int32)
    for e in range(1, n):
        take = vals[e] > best_v
        best_v = jnp.where(take, vals[e], best_v)
        best_i = jnp.where(take, e, best_i)
    sec_v = jnp.full_like(vals[0], -jnp.inf)
    sec_i = jnp.zeros_like(best_i)
    for e in range(n):
        take = (best_i != e) & (vals[e] > sec_v)
        sec_v = jnp.where(take, vals[e], sec_v)
        sec_i = jnp.where(take, e, sec_i)
    return best_v, best_i, sec_v, sec_i


def _route(h2, rw_ref, rb_ref, carry_ref, mi_ref, mf_ref, cnt_ref, cols):
    logits = _dot3(rw_ref[...], h2, dot=_dot_nt)
    s = 1.0 / (1.0 + jnp.exp(-logits))
    sel = s + rb_ref[...]
    ng = N_GROUPS
    sel4 = [sel[j * ng:(j + 1) * ng, :] for j in range(EXPERTS_PER_GROUP)]
    s4 = [s[j * ng:(j + 1) * ng, :] for j in range(EXPERTS_PER_GROUP)]
    bv, _, sv, _ = _top2_rows(sel4)
    gscore = bv + sv
    gidx = lax.broadcasted_iota(jnp.int32, (ng, TM), 0)
    gmax = jnp.max(gscore, axis=0, keepdims=True)
    best_g = jnp.min(jnp.where(gscore == gmax, gidx, ng), axis=0, keepdims=True)
    in_best = gidx == best_g
    pick = lambda slab: jnp.sum(jnp.where(in_best, slab, 0.0), axis=0, keepdims=True)
    cand_sel = [pick(v) for v in sel4]
    cand_s = [pick(v) for v in s4]
    _, i1, _, i2 = _top2_rows(cand_sel)
    a1, a2 = cand_s[0], cand_s[0]
    for j in range(1, EXPERTS_PER_GROUP):
        a1 = jnp.where(i1 == j, cand_s[j], a1)
        a2 = jnp.where(i2 == j, cand_s[j], a2)
    den = a1 + a2
    w1, w2 = a1 / den, a2 / den
    swap = i2 < i1
    lo = jnp.where(swap, i2, i1)
    hi = jnp.where(swap, i1, i2)
    w_lo = jnp.where(swap, w2, w1)
    w_hi = jnp.where(swap, w1, w2)
    pair = jnp.where(lo == 0, 0, jnp.where(lo == 1, 3, 5)) + (hi - lo - 1)
    cls = best_g * len(PAIRS) + pair
    cidx = lax.broadcasted_iota(jnp.int32, (CLASS_ROWS, TM), 0)
    hit = cidx == cls
    onehot = jnp.where(hit, 1.0, 0.0)
    tri = jnp.where(lax.broadcasted_iota(jnp.int32, (TM, TM), 0) < lax.broadcasted_iota(jnp.int32, (TM, TM), 1),
                    1.0, 0.0).astype(BF16)
    rank = _dot(onehot.astype(BF16), tri) + carry_ref[:, 0:1]
    rk = jnp.sum(jnp.where(hit, rank, 0.0), axis=0, keepdims=True)
    carry_ref[...] = carry_ref[...] + jnp.sum(onehot, axis=1, keepdims=True)
    mi_ref[:, cols] = jnp.concatenate([cls, rk.astype(jnp.int32), jnp.zeros((6, TM), jnp.int32)], axis=0)
    mf_ref[:, cols] = jnp.concatenate([w_lo, w_hi, jnp.zeros((6, TM), F32)], axis=0)
    cnt_ref[...] = carry_ref[...]


def _post_norm_mod(x, y, m_ref, lng, lnb, gate_row, sh_row, sc_row):
    u = DN_ALPHA * x + m_ref[0, gate_row:gate_row + 1, :] * y
    mu = jnp.mean(u, axis=-1, keepdims=True)
    var = jnp.mean(jnp.square(u - mu), axis=-1, keepdims=True)
    xn = (u - mu) * lax.rsqrt(var + LN_EPS) * lng + lnb
    if sh_row is None:
        return xn, None
    return xn, xn * (1.0 + m_ref[0, sc_row:sc_row + 1, :]) + m_ref[0, sh_row:sh_row + 1, :]


def _outproj_kernel(*refs, layer0, tpb, parts):
    nd = 8 if layer0 else 5
    halves = tuple(refs[p * nd:(p + 1) * nd] for p in range(parts))
    (w_ref, lng_ref, lnb_ref, rw_ref, rb_ref, xo_ref, h2_ref, mi_ref, mf_ref, cnt_ref,
     carry_ref) = refs[parts * nd:]
    i = pl.program_id(0)

    @pl.when(i == 0)
    def _():
        carry_ref[...] = jnp.zeros_like(carry_ref)

    for half, data in enumerate(halves):
        if layer0:
            of_ref, ob_ref, g_ref, hyl_ref, hyc_ref, xl_ref, xc_ref, m_ref = data
            is_ctx = (parts * i + half) % tpb == tpb - 1
            x = jnp.where(is_ctx, xc_ref[...], xl_ref[...])
            hy = jnp.where(is_ctx, hyc_ref[...], hyl_ref[...])
            o = of_ref[...].astype(F32) + ob_ref[...].astype(F32)
            normed = []
            for h in range(RET_HEADS):
                oh = o[:, h * RET_HD:(h + 1) * RET_HD]
                mu = jnp.mean(oh, axis=-1, keepdims=True)
                var = jnp.mean(jnp.square(oh - mu), axis=-1, keepdims=True)
                normed.append((oh - mu) * lax.rsqrt(var + LN_EPS))
            yret = (jnp.concatenate(normed, axis=-1) * _silu(g_ref[...].astype(F32))).astype(BF16)
            wr = yret.shape[1]
            y = _dot(yret, w_ref[:wr, :]) + _dot(hy, w_ref[wr:, :])
        else:
            of_ref, ob_ref, g_ref, x_ref, m_ref = data
            x = x_ref[...]
            o = of_ref[...].astype(F32) + ob_ref[...].astype(F32)
            normed = []
            for h in range(GLA_HEADS):
                oh = o[:, h * GLA_DV:(h + 1) * GLA_DV]
                normed.append(oh * lax.rsqrt(jnp.mean(jnp.square(oh), axis=-1, keepdims=True) + LN_EPS))
            a = (jnp.concatenate(normed, axis=-1) * _silu(g_ref[...].astype(F32))).astype(BF16)
            y = _dot(a, w_ref[...])
        rows = slice(half * TM, (half + 1) * TM)
        xn, h2 = _post_norm_mod(x, y, m_ref, lng_ref[...], lnb_ref[...], 2, 3, 4)
        xo_ref[rows, :] = xn
        _store_row_tiles(h2_ref, h2, TM, first=half * TM * SUBLANES)
        _route(h2, rw_ref, rb_ref, carry_ref, mi_ref, mf_ref, cnt_ref, rows)


def _outproj_call(layer0, tpb, n_tiles, d, half_specs, half_args, w, lng, lnb, router_w, router_bias):
    rows = n_tiles * TM
    parts = next(p for p in (4, 3, 2, 1) if n_tiles % p == 0)
    full2 = lambda a: pl.BlockSpec(a.shape, lambda i: (0, 0))
    regroup = lambda a: jnp.swapaxes(a.reshape(N_GROUPS, EXPERTS_PER_GROUP, -1), 0, 1).reshape(N_EXPERTS, -1)
    rw_t = regroup(router_w.T)
    rb = regroup(router_bias.reshape(N_EXPERTS, 1))
    return pl.pallas_call(
        functools.partial(_outproj_kernel, layer0=layer0, tpb=tpb, parts=parts),
        out_shape=(jax.ShapeDtypeStruct((rows, d), F32),
                   jax.ShapeDtypeStruct((rows * SUBLANES, LANES), F32),
                   jax.ShapeDtypeStruct((8, rows), jnp.int32),
                   jax.ShapeDtypeStruct((8, rows), F32),
                   jax.ShapeDtypeStruct((CLASS_ROWS, LANES), F32)),
        grid=(n_tiles // parts,),
        in_specs=[s for h in range(parts) for s in half_specs(h, parts)]
        + [full2(w), full2(lng), full2(lnb), full2(rw_t), full2(rb)],
        out_specs=(pl.BlockSpec((parts * TM, d), lambda i: (i, 0)),
                   pl.BlockSpec((parts * TM * SUBLANES, LANES), lambda i: (i, 0)),
                   pl.BlockSpec((8, parts * TM), lambda i: (0, i)),
                   pl.BlockSpec((8, parts * TM), lambda i: (0, i)),
                   pl.BlockSpec((CLASS_ROWS, LANES), lambda i: (0, 0))),
        scratch_shapes=[pltpu.VMEM((CLASS_ROWS, LANES), F32)],
        compiler_params=_cparams("arbitrary"),
        name="outproj_norm0" if layer0 else "outproj_norm1",
    )(*(tuple(half_args) * parts), w, lng, lnb, rw_t, rb)


def _dispatch_kernel(slots_ref, zoff_ref, h_ref, xs_ref, zero_ref, sem):
    i = pl.program_id(0)

    @pl.when(i == 0)
    def _():
        zero_ref[...] = jnp.zeros_like(zero_ref)
        def fill_class(e, go):
            @pl.when(zoff_ref[N_CLASSES + 1 + e] > 0)
            def _():
                go(pltpu.make_async_copy(zero_ref, _row_tile(xs_ref, zoff_ref[e], TE), sem))

        for e in range(N_CLASSES):
            fill_class(e, lambda cp: cp.start())
        for e in range(N_CLASSES):
            fill_class(e, lambda cp: cp.wait())
        first_free = zoff_ref[N_CLASSES]
        n_all = xs_ref.shape[0] // (TE * SUBLANES)

        def fill(t, carry):
            pltpu.make_async_copy(zero_ref, _row_tile(xs_ref, t, TE), sem).start()
            return carry

        def fill_wait(t, carry):
            pltpu.make_async_copy(zero_ref, _row_tile(xs_ref, t, TE), sem).wait()
            return carry

        lax.fori_loop(first_free, n_all, fill, 0)
        lax.fori_loop(first_free, n_all, fill_wait, 0)

    def row_copy(r):
        return pltpu.make_async_copy(_row_tile(h_ref, r), _row_tile(xs_ref, slots_ref[0, 0, r]), sem)

    n_rows = slots_ref.shape[2]
    for r in range(n_rows):
        row_copy(r).start(priority=r % 2)
    for r in range(n_rows):
        row_copy(r).wait()


def _dispatch(h2, slots, zoff, p_rows):
    t = slots.shape[0]
    rows = next(c for c in DISPATCH_ROWS if t % c == 0)
    grid_spec = pltpu.PrefetchScalarGridSpec(
        num_scalar_prefetch=0,
        grid=(t // rows,),
        in_specs=[pl.BlockSpec((1, 1, rows), lambda i: (i, 0, 0), memory_space=pltpu.SMEM),
                  pl.BlockSpec(memory_space=pltpu.SMEM),
                  pl.BlockSpec((rows * SUBLANES, LANES), lambda i: (i, 0))],
        out_specs=pl.BlockSpec(memory_space=pl.ANY),
        scratch_shapes=[pltpu.VMEM((TE * SUBLANES, LANES), F32), pltpu.SemaphoreType.DMA(())],
    )
    return pl.pallas_call(
        _dispatch_kernel,
        out_shape=jax.ShapeDtypeStruct((p_rows * SUBLANES, LANES), F32),
        grid_spec=grid_spec,
        compiler_params=_cparams("arbitrary"),
        name="moe_dispatch",
    )(slots.reshape(t // rows, 1, rows), zoff, h2)


def _experts_kernel(ea_ref, eb_ref, tv_ref, x_ref, wga_ref, wua_ref, wda_ref, wgb_ref, wub_ref, wdb_ref, y_ref):
    j = pl.program_id(0)

    @pl.when(tv_ref[j] > 0)
    def _():
        x = _load_row_tiles(x_ref, TE).astype(BF16)
        for which, (wg, wu, wd) in enumerate(((wga_ref, wua_ref, wda_ref), (wgb_ref, wub_ref, wdb_ref))):
            g = _dot(x, wg[0])
            u = _dot(x, wu[0])
            y = _dot((_silu(g) * u).astype(BF16), wd[0])
            _store_row_tiles(y_ref, y, TE, stride=2 * SUBLANES, first=which * SUBLANES)

    @pl.when(tv_ref[j] == 0)
    def _():
        y_ref[...] = jnp.zeros_like(y_ref)


def _experts(xs, tile_ea, tile_eb, tile_valid, wg, wu, wd):
    n_tiles = xs.shape[0] // (TE * SUBLANES)
    d, hdim = wg.shape[1], wg.shape[2]
    wa = lambda shp: pl.BlockSpec(shp, lambda j, ea, eb, tv: (ea[j], 0, 0))
    wb = lambda shp: pl.BlockSpec(shp, lambda j, ea, eb, tv: (eb[j], 0, 0))
    grid_spec = pltpu.PrefetchScalarGridSpec(
        num_scalar_prefetch=3,
        grid=(n_tiles,),
        in_specs=[pl.BlockSpec((TE * SUBLANES, LANES), lambda j, ea, eb, tv: (j, 0)),
                  wa((1, d, hdim)), wa((1, d, hdim)), wa((1, hdim, d)),
                  wb((1, d, hdim)), wb((1, d, hdim)), wb((1, hdim, d))],
        out_specs=pl.BlockSpec((2 * TE * SUBLANES, LANES), lambda j, ea, eb, tv: (j, 0)),
    )
    return pl.pallas_call(
        _experts_kernel,
        out_shape=jax.ShapeDtypeStruct((2 * xs.shape[0], LANES), F32),
        grid_spec=grid_spec,
        compiler_params=_cparams("arbitrary"),
        name="moe_experts",
    )(tile_ea, tile_eb, tile_valid, xs, wg, wu, wd, wg, wu, wd)


def _combine_kernel(slots_ref, next_slots_ref, ys_ref, wts_ref, x_ref, m_ref, lng_ref, lnb_ref, o_ref, buf_ref, sem):
    i = pl.program_id(0)
    cur = i % 2

    def row_copy(s_ref, b, r):
        return pltpu.make_async_copy(_row_tile(ys_ref, s_ref[0, 0, r], 2), _row_tile(buf_ref.at[b], r, 2), sem.at[b])

    def issue_all(s_ref, b):
        for r in range(TM):
            row_copy(s_ref, b, r).start(priority=r % 2)

    @pl.when(i == 0)
    def _():
        issue_all(slots_ref, 0)

    @pl.when(i + 1 < pl.num_programs(0))
    def _():
        issue_all(next_slots_ref, 1 - cur)

    for r in range(TM):
        row_copy(slots_ref, cur, r).wait()
    buf = buf_ref.at[cur]
    y_lo = _load_row_tiles(buf, TM, stride=2 * SUBLANES)
    y_hi = _load_row_tiles(buf, TM, stride=2 * SUBLANES, first=SUBLANES)
    y = wts_ref[:, 0:1] * y_lo + wts_ref[:, 1:2] * y_hi
    xn, _ = _post_norm_mod(x_ref[...], y, m_ref, lng_ref[...], lnb_ref[...], 5, None, None)
    o_ref[...] = xn


def _combine(ys, slots3, wts, x, mods, lng, lnb, cond_of, n_tiles):
    d = x.shape[1]
    grid_spec = pltpu.PrefetchScalarGridSpec(
        num_scalar_prefetch=0,
        grid=(n_tiles,),
        in_specs=[pl.BlockSpec((1, 1, TM), lambda i: (i, 0, 0), memory_space=pltpu.SMEM),
                  pl.BlockSpec((1, 1, TM), lambda i: (jnp.minimum(i + 1, n_tiles - 1), 0, 0),
                               memory_space=pltpu.SMEM),
                  pl.BlockSpec(memory_space=pl.ANY),
                  pl.BlockSpec((TM, 2), lambda i: (i, 0)),
                  pl.BlockSpec((TM, d), lambda i: (i, 0)),
                  pl.BlockSpec((1, 6, d), lambda i: (cond_of(i), 0, 0)),
                  pl.BlockSpec((1, d), lambda i: (0, 0)),
                  pl.BlockSpec((1, d), lambda i: (0, 0))],
        out_specs=pl.BlockSpec((TM, d), lambda i: (i, 0)),
        scratch_shapes=[pltpu.VMEM((2, 2 * TM * SUBLANES, LANES), F32), pltpu.SemaphoreType.DMA((2,))],
    )
    return pl.pallas_call(
        _combine_kernel,
        out_shape=jax.ShapeDtypeStruct((n_tiles * TM, d), F32),
        grid_spec=grid_spec,
        compiler_params=_cparams("arbitrary"),
        name="moe_combine",
    )(slots3, slots3, ys, wts, x, mods, lng, lnb)


def _moe(h2, mi, mf, cnt, x, cond_of, mods, lng, lnb, layer, wg, wu, wd):
    t = x.shape[0]
    n_tiles = t // TM
    counts = cnt[:N_CLASSES, 0].astype(jnp.int32)
    padded = ((counts + TE - 1) // TE) * TE
    ends = jnp.cumsum(padded)
    offs = ends - padded
    cls = mi[0]
    cids = jnp.arange(N_CLASSES, dtype=jnp.int32)[:, None]
    slots = jnp.sum(jnp.where(cls[None] == cids, offs[:, None], 0), axis=0) + mi[1]
    slots3 = slots.reshape(n_tiles, 1, TM)
    n_cls_tiles = t // TE + N_CLASSES
    total_tiles = ends[-1] // TE
    tile_ids = jnp.arange(n_cls_tiles, dtype=jnp.int32)
    tile_valid = (tile_ids < total_tiles).astype(jnp.int32)
    tile_last = jnp.minimum(tile_ids, total_tiles - 1)
    tile_class = jnp.sum((tile_last[:, None] * TE >= ends[None, :]).astype(jnp.int32), axis=1)
    tile_class = jnp.minimum(tile_class, N_CLASSES - 1)
    pair_lo = jnp.asarray([p[0] for p in PAIRS], jnp.int32)
    pair_hi = jnp.asarray([p[1] for p in PAIRS], jnp.int32)
    pair_sel = (tile_class % len(PAIRS))[:, None] == jnp.arange(len(PAIRS), dtype=jnp.int32)[None, :]
    group0 = (tile_class // len(PAIRS)) * EXPERTS_PER_GROUP
    tile_ea = (group0 + jnp.sum(jnp.where(pair_sel, pair_lo[None, :], 0), axis=1)).astype(jnp.int32)
    tile_eb = (group0 + jnp.sum(jnp.where(pair_sel, pair_hi[None, :], 0), axis=1)).astype(jnp.int32)
    zoff = jnp.concatenate([jnp.maximum(ends // TE - 1, 0), total_tiles[None], padded]).astype(jnp.int32)
    xs = _dispatch(h2, slots, zoff, n_cls_tiles * TE)
    ys = _experts(xs, tile_ea + layer * N_EXPERTS, tile_eb + layer * N_EXPERTS, tile_valid, wg, wu, wd)
    wts = jnp.transpose(mf[0:2])
    return _combine(ys, slots3, wts, x, mods, lng, lnb, cond_of, n_tiles)


def _chunk_cumsum(x, tri):
    hi = x.astype(BF16)
    r1 = x - hi.astype(F32)
    mid = r1.astype(BF16)
    lo = (r1 - mid.astype(F32)).astype(BF16)
    return _dot(tri, hi) + _dot(tri, mid) + _dot(tri, lo)


def _gla_kernel(*refs, nb):
    f_in, b_in, (of_ref, ob_ref, sf_ref, sb_ref) = refs[:6], refs[6:12], refs[12:]

    @pl.when(pl.program_id(0) == 0)
    def _():
        sf_ref[...] = jnp.zeros_like(sf_ref)
        sb_ref[...] = jnp.zeros_like(sb_ref)

    _gla_dir(*f_in, of_ref, sf_ref, False, nb)
    _gla_dir(*b_in, ob_ref, sb_ref, True, nb)


def _gla_dir(q_ref, k_ref, v_ref, lr_ref, gw_ref, gb_ref, o_ref, s_ref, reverse, nb):
    C = GLA_CHUNK
    nchunk = TM // C
    ii = lax.broadcasted_iota(jnp.int32, (C, C), 0)
    jj = lax.broadcasted_iota(jnp.int32, (C, C), 1)
    keep = (jj >= ii) if reverse else (jj <= ii)
    tri = jnp.where(keep, 1.0, 0.0).astype(BF16)
    order = tuple(reversed(range(nchunk))) if reverse else tuple(range(nchunk))
    end_row = 0 if reverse else C - 1
    gw = gw_ref[...]
    gb = gb_ref[...]
    for b in range(nb):
        pre = _dot(lr_ref[b], gw) + gb
        la_all = (jnp.minimum(pre, 0.0) - jnp.log(1.0 + jnp.exp(-jnp.abs(pre)))) * (1.0 / GLA_TAU)
        for c in order:
            rows = slice(c * C, (c + 1) * C)
            bc = _chunk_cumsum(la_all[rows, :], tri)
            bend = bc[end_row:end_row + 1, :]
            eb = jnp.exp(bc)
            enb = jnp.exp(-bc)
            ekb = jnp.exp(bend - bc)
            ebend = jnp.exp(bend)
            for h in range(GLA_HEADS):
                kc = slice(h * GLA_DK, (h + 1) * GLA_DK)
                vc = slice(h * GLA_DV, (h + 1) * GLA_DV)
                q = q_ref[b, rows, kc].astype(F32) * (GLA_DK ** -0.5)
                k = k_ref[b, rows, kc].astype(F32)
                v = v_ref[b, rows, vc]
                qd = (q * eb[:, kc]).astype(BF16)
                kd = (k * enb[:, kc]).astype(BF16)
                st = s_ref[b, h]
                scores = jnp.where(keep, _dot_nt(qd, kd), 0.0)
                o = _dot(scores.astype(BF16), v) + _dot_nt(qd, st.astype(BF16))
                s_ref[b, h] = st * ebend[:, kc] + _dot_tn(v, (k * ekb[:, kc]).astype(BF16))
                o_ref[b, rows, vc] = o.astype(BF16)


def _gla(z3, gw_f, gb_f, gw_b, gb_b):
    nb, s, _ = z3.shape
    nt = s // TM
    kw = GLA_HEADS * GLA_DK
    vw = GLA_HEADS * GLA_DV
    lr_blk = (2 * kw + 2 * vw) // LANES

    def dir_specs(jmap):
        return [pl.BlockSpec((nb, TM, kw), lambda j: (0, jmap(j), 0)),
                pl.BlockSpec((nb, TM, kw), lambda j: (0, jmap(j), 1)),
                pl.BlockSpec((nb, TM, vw), lambda j: (0, jmap(j), 1)),
                pl.BlockSpec((nb, TM, LANES), lambda j: (0, jmap(j), lr_blk)),
                pl.BlockSpec((LANES, kw), lambda j: (0, 0)),
                pl.BlockSpec((1, kw), lambda j: (0, 0))]

    fwd, bwd = _scan_tile_orders(nt)
    state = pltpu.VMEM((nb, GLA_HEADS, GLA_DV, GLA_DK), F32)
    out = jax.ShapeDtypeStruct((nb, s, vw), BF16)
    return pl.pallas_call(
        functools.partial(_gla_kernel, nb=nb),
        out_shape=(out, out),
        grid=(nt,),
        in_specs=dir_specs(fwd) + dir_specs(bwd),
        out_specs=(pl.BlockSpec((nb, TM, vw), lambda j: (0, fwd(j), 0)),
                   pl.BlockSpec((nb, TM, vw), lambda j: (0, bwd(j), 0))),
        scratch_shapes=[state, state],
        compiler_params=_cparams("arbitrary"),
        name="gla",
    )(z3, z3, z3, z3, gw_f, gb_f, z3, z3, z3, z3, gw_b, gb_b)


def kernel(x, c, ctx, c_ctx, mod_w, mod_b, ln_g, ln_b, ab_w_in, ret_log_decay_f, ret_log_decay_b, hy_conv_w, hy_conv_b, hy_w1, hy_b1, hy_w2, hy_b2, hy_w3, hy_b3, hy_w4, hy_freq, hy_skip, ab_w_out, gla_w_in, gla_gate_w_f, gla_gate_b_f, gla_gate_w_b, gla_gate_b_b, gla_w_out, router_w, router_bias, exp_w_gate, exp_w_up, exp_w_down):
    nb, length, d = x.shape
    ctx_len = ctx.shape[1]
    assert nb == 2 and ctx_len == TM and length % TM == 0 and length % (FFT_N2 * 2) == 0
    s = length + ctx_len
    tpb = s // TM
    r = nb * s
    lat_tiles = tpb - 1

    cond = jnp.zeros((8, d), F32).at[0].set(c[0]).at[1].set(c_ctx).at[2].set(c[1]).at[3].set(c_ctx)
    mods = _ada(cond, mod_w, mod_b).reshape(DEPTH, 8, 6, d)

    def cond_of(i):
        return 2 * (i // tpb) + jnp.where(i % tpb == tpb - 1, 1, 0)

    x_lat = x.reshape(nb * length, d)
    x_ctx = ctx.reshape(nb * ctx_len, d)

    ret_w = RET_HEADS * RET_HD
    z0, *hy_lat, hy_ctx = _inproj0(x_lat, x_ctx, mods[0], ab_w_in[0].astype(BF16), tpb, r, 4 * ret_w)
    z0_3 = z0.reshape(nb, s, -1)
    cosf, sinf = _rope_tables(length, ctx_len)
    o_f, o_b = _retention(z0_3, ret_log_decay_f[0], ret_log_decay_b[0], cosf, sinf)

    cw3 = jnp.concatenate([hy_conv_w[0].T, hy_conv_b[0][:, None]], axis=1).reshape(3, HY_WIDTH, 4)
    na = length // FFT_N2
    tb = _dft_tables(na)
    filt_w = (hy_w1[0], hy_b1[0], hy_w2[0], hy_b2[0], hy_w3[0], hy_b3[0], hy_w4[0], hy_freq[0])
    skip = hy_skip[0]

    filt, ss = _hyena_filters_raw(length, *filt_w)
    filt5 = filt.reshape(2, 2, HY_WIDTH, na, FFT_N2)
    ss5 = ss.reshape(2, 2, HY_WIDTH, 1, 1)
    hy_lat = [a.reshape(nb, length, HY_WIDTH) for a in hy_lat]
    zcur = _to_freq_layout(hy_lat[2], na)
    for n in range(2):
        hspec = _spectrum(filt5, ss5, n, tb, na)
        gate = _to_freq_layout(hy_lat[n], na)
        zcur = _fftconv(zcur, gate, hspec, skip[n].reshape(-1, 1, 1), cw3[2].reshape(-1, 1, 4),
                        cw3[n].reshape(-1, 1, 4), tb, na, conv_u=(n == 0))
    y_hy_lat = _from_freq_layout(zcur)

    filt_c, ss_c = _hyena_filters_raw(ctx_len, *filt_w)
    filt_c = filt_c.reshape(2, 2, HY_WIDTH, ctx_len)
    ss_c = ss_c.reshape(2, 2, HY_WIDTH, 1)
    u_ctx = jnp.transpose(hy_ctx.reshape(nb, ctx_len, 3, HY_WIDTH), (2, 3, 0, 1))
    y_ctx_t = _ctxconv(u_ctx.reshape(3, HY_WIDTH, nb * ctx_len), cw3, filt_c[:, 0], filt_c[:, 1], ss_c[:, 0],
                       ss_c[:, 1], skip.reshape(2, HY_WIDTH, 1), _ctx_dft_tables(ctx_len))
    y_hy_ctx = jnp.transpose(y_ctx_t.reshape(HY_WIDTH, nb, ctx_len), (1, 2, 0))

    gate_blk = 3

    def specs0(h, parts):
        tile = lambda i: parts * i + h
        trow = lambda i: (tile(i), 0)
        return [pl.BlockSpec((TM, ret_w), trow), pl.BlockSpec((TM, ret_w), trow),
                pl.BlockSpec((TM, ret_w), lambda i: (tile(i), gate_blk)),
                *_lat_or_ctx_specs(HY_WIDTH, tpb, tile), *_lat_or_ctx_specs(d, tpb, tile),
                pl.BlockSpec((1, 6, d), lambda i: (cond_of(tile(i)), 0, 0))]

    x_mid, h2, mi, mf, cnt = _outproj_call(
        True, tpb, r // TM, d, specs0,
        (o_f.reshape(r, ret_w), o_b.reshape(r, ret_w), z0, y_hy_lat.reshape(nb * length, HY_WIDTH),
         y_hy_ctx.reshape(nb * ctx_len, HY_WIDTH), x_lat, x_ctx, mods[0]),
        ab_w_out[0].astype(BF16), ln_g[0, 0].reshape(1, d), ln_b[0, 0].reshape(1, d), router_w, router_bias)

    stack = lambda w: w.astype(BF16).reshape((DEPTH * N_EXPERTS,) + w.shape[2:])
    exp_w = (stack(exp_w_gate), stack(exp_w_up), stack(exp_w_down))
    x1 = _moe(h2, mi, mf, cnt, x_mid, cond_of, mods[0], ln_g[0, 1].reshape(1, d), ln_b[0, 1].reshape(1, d),
              0, *exp_w)

    kw = GLA_HEADS * GLA_DK
    vw = GLA_HEADS * GLA_DV
    n_in = gla_w_in.shape[2]
    n_pad = 2 * kw + 2 * vw + LANES
    w_in1 = jnp.pad(gla_w_in[0], ((0, 0), (0, n_pad - n_in))).astype(BF16)
    z1 = _inproj(x1, mods[1], w_in1, tpb)
    z1_3 = z1.reshape(nb, s, n_pad)
    gw_f = jnp.zeros((LANES, kw), F32).at[:GLA_RANK].set(gla_gate_w_f[0]).astype(BF16)
    gw_b = jnp.zeros((LANES, kw), F32).at[GLA_RANK:2 * GLA_RANK].set(gla_gate_w_b[0]).astype(BF16)
    g_f, g_b = _gla(z1_3, gw_f, gla_gate_b_f[0].reshape(1, kw), gw_b, gla_gate_b_b[0].reshape(1, kw))

    n_lat = nb * lat_tiles

    def lat_tile(i):
        return (i // lat_tiles) * tpb + i % lat_tiles

    def lat_cond(i):
        return 2 * (i // lat_tiles)

    def specs1(h, parts):
        tile = lambda i: parts * i + h
        lrow = lambda i: (lat_tile(tile(i)), 0)
        return [pl.BlockSpec((TM, vw), lrow), pl.BlockSpec((TM, vw), lrow),
                pl.BlockSpec((TM, vw), lambda i: (lat_tile(tile(i)), 2)),
                pl.BlockSpec((TM, d), lrow),
                pl.BlockSpec((1, 6, d), lambda i: (lat_cond(tile(i)), 0, 0))]

    x_mid1, h2_1, mi1, mf1, cnt1 = _outproj_call(
        False, tpb, n_lat, d, specs1, (g_f.reshape(r, vw), g_b.reshape(r, vw), z1, x1, mods[1]),
        gla_w_out[0].astype(BF16), ln_g[1, 0].reshape(1, d), ln_b[1, 0].reshape(1, d), router_w, router_bias)

    out = _moe(h2_1, mi1, mf1, cnt1, x_mid1, lat_cond, mods[1], ln_g[1, 1].reshape(1, d), ln_b[1, 1].reshape(1, d),
               1, *exp_w)
    return out.reshape(nb, length, d)
```

```python
import functools
import math

import numpy as np
import jax
import jax.numpy as jnp
from jax import lax
from jax.experimental import pallas as pl
from jax.experimental.pallas import tpu as pltpu

F32 = jnp.float32
BF16 = jnp.bfloat16

GRID_W = 64
RET_HEADS = 4
RET_HD = 128
RET_CHUNK = 128
ROPE_BASE = 10000.0
HY_WIDTH = 512
HY_EMB = 33
HY_FFN = 64
HY_SHORT_DECAY_PCT = 0.3
HY_LONG_DECAY_PCT = 1.5
HY_TARGET = 1e-2
GLA_HEADS = 4
GLA_DK = 128
GLA_DV = 256
GLA_RANK = 16
GLA_TAU = 16.0
GLA_CHUNK = 64
N_EXPERTS = 16
N_GROUPS = 4
EXPERTS_PER_GROUP = 4
LN_EPS = 1e-5
DEPTH = 2
DN_ALPHA = (2.0 * DEPTH) ** 0.25

PAIRS = ((0, 1), (0, 2), (0, 3), (1, 2), (1, 3), (2, 3))
N_CLASSES = N_GROUPS * len(PAIRS)
CLASS_ROWS = 32

LANES = 128
SUBLANES = 8
TM = 256
TE = 256
DISPATCH_ROWS = (6 * TM, 4 * TM, 2 * TM, TM)
FFT_N2 = 128
FFT_CB = 32
VMEM_LIMIT = 48 * 1024 * 1024


def _cparams(*sem):
    return pltpu.CompilerParams(dimension_semantics=sem, vmem_limit_bytes=VMEM_LIMIT)


def _silu(v):
    return v * (1.0 / (1.0 + jnp.exp(-v)))


def _dot(a, b):
    return jnp.dot(a, b, preferred_element_type=F32)


def _dot_nt(a, b):
    return lax.dot_general(a, b, (((1,), (1,)), ((), ())), preferred_element_type=F32)


def _dot_tn(a, b):
    return lax.dot_general(a, b, (((0,), (0,)), ((), ())), preferred_element_type=F32)


def _split2(a):
    hi = a.astype(BF16)
    return hi, (a - hi.astype(F32)).astype(BF16)


def _dot3(a, b, dot=_dot):
    ah, al = _split2(a)
    bh, bl = _split2(b)
    return dot(ah, bh) + dot(ah, bl) + dot(al, bh)


def _ada_kernel(c_ref, w_ref, b_ref, o_ref):
    a = _silu(c_ref[...])
    o_ref[0] = _dot3(a, w_ref[0]) + b_ref[0]


def _ada(cond, mod_w, mod_b):
    depth, d, n = mod_w.shape
    nt = n // 4
    return pl.pallas_call(
        _ada_kernel,
        out_shape=jax.ShapeDtypeStruct((depth, 8, n), F32),
        grid=(depth, n // nt),
        in_specs=[pl.BlockSpec((8, d), lambda l, j: (0, 0)),
                  pl.BlockSpec((1, d, nt), lambda l, j: (l, 0, j)),
                  pl.BlockSpec((1, 1, nt), lambda l, j: (l, 0, j))],
        out_specs=pl.BlockSpec((1, 8, nt), lambda l, j: (l, 0, j)),
        compiler_params=_cparams("arbitrary", "arbitrary"),
        name="ada_mod",
    )(cond, mod_w, mod_b.reshape(depth, 1, n))


def _inproj_kernel(x_ref, ma_ref, mb_ref, w_ref, o_ref, *, chunks):
    hs = []
    for half, m_ref in enumerate((ma_ref, mb_ref)):
        x = x_ref[half * TM:(half + 1) * TM, :]
        hs.append((x * (1.0 + m_ref[0, 1:2, :]) + m_ref[0, 0:1, :]).astype(BF16))
    h = jnp.concatenate(hs, axis=0)
    for lo, hi in chunks:
        o_ref[:, lo:hi] = _dot(h, w_ref[:, lo:hi]).astype(BF16)


def _lat_or_ctx_specs(width, tpb, tile_of=lambda i: i):
    lat_tiles = tpb - 1
    lat = lambda i: ((tile_of(i) // tpb) * lat_tiles + jnp.minimum(tile_of(i) % tpb, lat_tiles - 1), 0)
    ctx = lambda i: (tile_of(i) // tpb, 0)
    return pl.BlockSpec((TM, width), lat), pl.BlockSpec((TM, width), ctx)


def _col_chunks(n, width=512):
    out, lo = [], 0
    while lo < n:
        hi = min(lo + width, n)
        out.append((lo, hi))
        lo = hi
    return tuple(out)


def _inproj(xu, mods, w, tiles_per_batch):
    r, d = xu.shape
    n = w.shape[1]
    tpb = tiles_per_batch

    def cond_of(i):
        return 2 * (i // tpb) + jnp.where(i % tpb == tpb - 1, 1, 0)

    return pl.pallas_call(
        functools.partial(_inproj_kernel, chunks=_col_chunks(n)),
        out_shape=jax.ShapeDtypeStruct((r, n), BF16),
        grid=(r // (2 * TM),),
        in_specs=[pl.BlockSpec((2 * TM, d), lambda i: (i, 0)),
                  pl.BlockSpec((1, 6, d), lambda i: (cond_of(2 * i), 0, 0)),
                  pl.BlockSpec((1, 6, d), lambda i: (cond_of(2 * i + 1), 0, 0)),
                  pl.BlockSpec((d, n), lambda i: (0, 0))],
        out_specs=pl.BlockSpec((2 * TM, n), lambda i: (i, 0)),
        compiler_params=_cparams("arbitrary"),
        name="inproj1",
    )(xu, mods, mods, w)


def _inproj0_kernel(x_ref, c_ref, m_ref, w_ref, z_ref, x1_ref, x2_ref, v_ref, hc_ref, *, tpb, ret_w):
    is_ctx = pl.program_id(0) % tpb == tpb - 1
    x = jnp.where(is_ctx, c_ref[...], x_ref[...])
    h = (x * (1.0 + m_ref[0, 1:2, :]) + m_ref[0, 0:1, :]).astype(BF16)
    for lo, hi in _col_chunks(ret_w):
        z_ref[:, lo:hi] = _dot(h, w_ref[:, lo:hi]).astype(BF16)
    hy = [_dot(h, w_ref[:, ret_w + j * HY_WIDTH:ret_w + (j + 1) * HY_WIDTH]).astype(BF16) for j in range(3)]

    @pl.when(jnp.logical_not(is_ctx))
    def _():
        for ref, val in zip((x1_ref, x2_ref, v_ref), hy):
            ref[...] = val

    @pl.when(is_ctx)
    def _():
        for j, val in enumerate(hy):
            hc_ref[:, j * HY_WIDTH:(j + 1) * HY_WIDTH] = val


def _inproj0(x_lat, x_ctx, mods, w, tpb, r, ret_w):
    d = w.shape[0]
    lat_tiles = tpb - 1

    def cond_of(i):
        return 2 * (i // tpb) + jnp.where(i % tpb == tpb - 1, 1, 0)

    lat_spec, ctx_spec = _lat_or_ctx_specs(d, tpb)
    lat_out = pl.BlockSpec((TM, HY_WIDTH), lambda i: ((i // tpb) * lat_tiles + jnp.minimum(i % tpb, lat_tiles - 1), 0))
    hy_lat = jax.ShapeDtypeStruct(((r // TM // tpb) * lat_tiles * TM, HY_WIDTH), BF16)
    return pl.pallas_call(
        functools.partial(_inproj0_kernel, tpb=tpb, ret_w=ret_w),
        out_shape=(jax.ShapeDtypeStruct((r, ret_w), BF16), hy_lat, hy_lat, hy_lat,
                   jax.ShapeDtypeStruct(((r // TM // tpb) * TM, 3 * HY_WIDTH), BF16)),
        grid=(r // TM,),
        in_specs=[lat_spec, ctx_spec, pl.BlockSpec((1, 6, d), lambda i: (cond_of(i), 0, 0)),
                  pl.BlockSpec(w.shape, lambda i: (0, 0))],
        out_specs=(pl.BlockSpec((TM, ret_w), lambda i: (i, 0)), lat_out, lat_out, lat_out,
                   pl.BlockSpec((TM, 3 * HY_WIDTH), lambda i: (i // tpb, 0))),
        compiler_params=_cparams("arbitrary"),
        name="inproj0",
    )(x_lat, x_ctx, mods, w)


def _ret_kernel(lgf_ref, lgb_ref, *refs, nb):
    f_in, b_in, (of_ref, ob_ref, sf_ref, sb_ref) = refs[:5], refs[5:10], refs[10:]

    @pl.when(pl.program_id(0) == 0)
    def _():
        sf_ref[...] = jnp.zeros_like(sf_ref)
        sb_ref[...] = jnp.zeros_like(sb_ref)

    _ret_dir(lgf_ref, *f_in, of_ref, sf_ref, False, nb)
    _ret_dir(lgb_ref, *b_in, ob_ref, sb_ref, True, nb)


def _ret_dir(lg_ref, q_ref, k_ref, v_ref, cos_ref, sin_ref, o_ref, s_ref, reverse, nb):
    C = RET_CHUNK
    ii = lax.broadcasted_iota(jnp.int32, (C, C), 0).astype(F32)
    jj = lax.broadcasted_iota(jnp.int32, (C, C), 1).astype(F32)
    ci = lax.broadcasted_iota(jnp.int32, (C, 1), 0).astype(F32)
    diff = (jj - ii) if reverse else (ii - jj)
    order = (1, 0) if reverse else (0, 1)
    for h in range(RET_HEADS):
        lg = lg_ref[h]
        intra = jnp.where(diff >= 0, jnp.exp(jnp.maximum(diff, 0.0) * lg), 0.0)
        if reverse:
            q_dec = jnp.exp((C - ci) * lg)
            k_dec = jnp.exp(ci * lg)
        else:
            q_dec = jnp.exp((ci + 1.0) * lg)
            k_dec = jnp.exp((C - 1.0 - ci) * lg)
        c_dec = jnp.exp(jnp.zeros((1, RET_HD), F32) + C * lg)
        cols = slice(h * RET_HD, (h + 1) * RET_HD)
        for b in range(nb):
            for c in order:
                rows = slice(c * C, (c + 1) * C)
                cosf = cos_ref[rows, :]
                sinf = sin_ref[rows, :]
                q = q_ref[b, rows, cols].astype(F32)
                k = k_ref[b, rows, cols].astype(F32)
                v = v_ref[b, rows, cols]
                q = q * cosf + pltpu.roll(q, RET_HD // 2, 1) * sinf
                k = (k * cosf + pltpu.roll(k, RET_HD // 2, 1) * sinf) * (RET_HD ** -0.5)
                st = s_ref[b, h]
                scores = _dot_nt(q.astype(BF16), k.astype(BF16)) * intra
                o = _dot(scores.astype(BF16), v) + _dot_nt((q * q_dec).astype(BF16), st.astype(BF16))
                s_ref[b, h] = c_dec * st + _dot_tn(v, (k * k_dec).astype(BF16))
                o_ref[b, rows, cols] = o.astype(BF16)


def _scan_tile_orders(nt):
    fwd = lambda j: (j + nt - 1) % nt
    bwd = lambda j: jnp.where(j == 0, nt - 1, nt - 1 - j)
    return fwd, bwd


def _retention(z3, lg_f, lg_b, cosf, sinf):
    nb, s, _ = z3.shape
    nt = s // TM
    w = RET_HEADS * RET_HD
    smem = pl.BlockSpec(memory_space=pltpu.SMEM)

    def dir_specs(jmap):
        return [pl.BlockSpec((nb, TM, w), lambda j: (0, jmap(j), 0)),
                pl.BlockSpec((nb, TM, w), lambda j: (0, jmap(j), 1)),
                pl.BlockSpec((nb, TM, w), lambda j: (0, jmap(j), 2)),
                pl.BlockSpec((TM, RET_HD), lambda j: (jmap(j), 0)),
                pl.BlockSpec((TM, RET_HD), lambda j: (jmap(j), 0))]

    fwd, bwd = _scan_tile_orders(nt)
    state = pltpu.VMEM((nb, RET_HEADS, RET_HD, RET_HD), F32)
    out = jax.ShapeDtypeStruct((nb, s, w), BF16)
    grid_spec = pltpu.PrefetchScalarGridSpec(
        num_scalar_prefetch=0,
        grid=(nt,),
        in_specs=[smem, smem] + dir_specs(fwd) + dir_specs(bwd),
        out_specs=(pl.BlockSpec((nb, TM, w), lambda j: (0, fwd(j), 0)),
                   pl.BlockSpec((nb, TM, w), lambda j: (0, bwd(j), 0))),
        scratch_shapes=[state, state],
    )
    return pl.pallas_call(
        functools.partial(_ret_kernel, nb=nb),
        out_shape=(out, out),
        grid_spec=grid_spec,
        compiler_params=_cparams("arbitrary"),
        name="retention",
    )(lg_f, lg_b, z3, z3, z3, cosf, sinf, z3, z3, z3, cosf, sinf)


def _rope_tables(length, ctx_len):
    rows = length // GRID_W
    quarter = RET_HD // 4
    inv = ROPE_BASE ** (-jnp.arange(quarter, dtype=F32) / quarter)
    def expand(fn):
        by_row = fn(jnp.arange(rows, dtype=F32)[:, None] * inv)
        by_col = fn(jnp.arange(GRID_W, dtype=F32)[:, None] * inv)
        by_row = jnp.broadcast_to(by_row[:, None, :], (rows, GRID_W, quarter)).reshape(length, quarter)
        by_col = jnp.broadcast_to(by_col[None, :, :], (rows, GRID_W, quarter)).reshape(length, quarter)
        return jnp.concatenate([by_row, by_col], axis=-1)

    cos, sin = expand(jnp.cos), expand(jnp.sin)
    cosf = jnp.concatenate([cos, cos], axis=-1)
    sinf = jnp.concatenate([-sin, sin], axis=-1)
    cosf = jnp.concatenate([cosf, jnp.ones((ctx_len, RET_HD), F32)], axis=0)
    sinf = jnp.concatenate([sinf, jnp.zeros((ctx_len, RET_HD), F32)], axis=0)
    return cosf, sinf


def _shortconv_freq(x, cw, na):
    row = lax.broadcasted_iota(jnp.int32, (1, FFT_N2, 1), 1)
    lane = lax.broadcasted_iota(jnp.int32, (1, 1, 2 * na), 2)
    wrap_prev = pltpu.roll(x[:, FFT_N2 - 1:FFT_N2, :], 1, 2)
    prev = jnp.where(row == 0, jnp.where((lane == 0) | (lane == na), 0.0, wrap_prev), pltpu.roll(x, 1, 1))
    wrap_next = pltpu.roll(x[:, 0:1, :], 2 * na - 1, 2)
    nxt = jnp.where(row == FFT_N2 - 1, jnp.where((lane == na - 1) | (lane == 2 * na - 1), 0.0, wrap_next),
                    pltpu.roll(x, FFT_N2 - 1, 1))
    return cw[..., 0:1] * prev + cw[..., 1:2] * x + cw[..., 2:3] * nxt + cw[..., 3:4]


def _shortconv_lanes(x, cw, n):
    lane = lax.broadcasted_iota(jnp.int32, (1, 2 * n), 1)
    prev = jnp.where((lane == 0) | (lane == n), 0.0, pltpu.roll(x, 1, 1))
    nxt = jnp.where((lane == n - 1) | (lane == 2 * n - 1), 0.0, pltpu.roll(x, 2 * n - 1, 1))
    return cw[:, 0:1] * prev + cw[:, 1:2] * x + cw[:, 2:3] * nxt + cw[:, 3:4]


FEAT_ROWS = 40


def _filter_kernel(w1_ref, b1_ref, w2_ref, b2_ref, w3_ref, b3_ref, w4_ref, fr_ref, dl_ref,
                   f_ref, ss_ref, *, length, pb):
    i = pl.program_id(0)
    pos = (lax.broadcasted_iota(jnp.int32, (1, pb), 1) + i * pb).astype(F32)
    t = pos * (1.0 / (length - 1))
    bands = (HY_EMB - 1) // 2
    w = (2.0 * math.pi) * pos / length
    sub = lax.broadcasted_iota(jnp.int32, (FEAT_ROWS, 1), 0)
    band = jnp.where(sub <= bands, sub - 1, sub - 1 - bands).astype(F32)
    f = 1e-4 + band * ((bands - 1 - 1e-4) / (bands - 1))
    fw = f * w
    feats = jnp.where(sub == 0, t, jnp.where(sub <= bands, jnp.cos(fw),
                                             jnp.where(sub <= 2 * bands, -jnp.sin(fw), 0.0)))
    feats = jnp.concatenate([feats, jnp.zeros((LANES - FEAT_ROWS, pb), F32)], axis=0)
    hdot = _dot3
    a = jnp.sin(fr_ref[:, 0:1] * (hdot(w1_ref[...], feats) + b1_ref[...]))
    a = jnp.sin(fr_ref[:, 1:2] * (hdot(w2_ref[...], a) + b2_ref[...]))
    a = jnp.sin(fr_ref[:, 2:3] * (hdot(w3_ref[...], a) + b3_ref[...]))

    @pl.when(i == 0)
    def _():
        ss_ref[...] = jnp.zeros_like(ss_ref)

    nout = w4_ref.shape[0]
    window = jnp.exp(-dl_ref[...] * t)
    a_bf = a.astype(BF16)
    for cb in range(nout // HY_WIDTH):
        rows = slice(cb * HY_WIDTH, (cb + 1) * HY_WIDTH)
        filt = _dot(w4_ref[rows, :].astype(BF16), a_bf) * window
        for j in range(pb // LANES):
            f_ref[rows, j, :] = filt[:, j * LANES:(j + 1) * LANES]
        ss_ref[rows, :] += jnp.sum(filt * filt, axis=1, keepdims=True)


def _hyena_filters_raw(length, w1, b1, w2, b2, w3, b3, w4, freq):
    nout = w4.shape[1]
    pb = min(length, 1024)
    max_decay = math.log(HY_TARGET) / HY_SHORT_DECAY_PCT
    min_decay = math.log(HY_TARGET) / HY_LONG_DECAY_PCT
    deltas = jnp.abs(jnp.linspace(min_decay, max_decay, HY_WIDTH, dtype=F32))
    dl = deltas.reshape(HY_WIDTH, 1)
    w1p = jnp.pad(w1.T, ((0, 0), (0, LANES - w1.shape[0])))
    col = lambda b: b.reshape(-1, 1)
    full = lambda a: pl.BlockSpec(a.shape, lambda i: tuple(0 for _ in a.shape))
    args = (w1p, col(b1), w2.T, col(b2), w3.T, col(b3), w4.T, freq.T, dl)
    return pl.pallas_call(
        functools.partial(_filter_kernel, length=length, pb=pb),
        out_shape=(jax.ShapeDtypeStruct((nout, length // LANES, LANES), F32),
                   jax.ShapeDtypeStruct((nout, 1), F32)),
        grid=(length // pb,),
        in_specs=[full(a) for a in args],
        out_specs=(pl.BlockSpec((nout, pb // LANES, LANES), lambda i: (0, i, 0)),
                   pl.BlockSpec((nout, 1), lambda i: (0, 0))),
        compiler_params=_cparams("arbitrary"),
        name="hyena_filter_mlp",
    )(*args)


def _dft_tables(na):
    k1n = 2 * na
    n = k1n * FFT_N2
    n1 = np.arange(na)[:, None]
    k1 = np.arange(k1n)[None, :]
    ang = 2.0 * np.pi * ((n1 * k1) % k1n) / k1n
    c, s = np.cos(ang), np.sin(ang)
    ma = np.block([[c, -s], [s, c]])
    n1f = np.arange(FFT_N2)[:, None]
    angf = 2.0 * np.pi * ((n1f * k1) % k1n) / k1n
    live = (n1f < na).astype(np.float64)
    ma_f = np.concatenate([np.cos(angf), -np.sin(angf)], axis=1) * live
    ma_b = np.concatenate([np.cos(angf), np.sin(angf)], axis=1) * live
    n2 = np.arange(FFT_N2)[:, None]
    angt = 2.0 * np.pi * ((n2 * k1) % n) / n
    twr, twi = np.cos(angt), -np.sin(angt)
    k2 = np.arange(FFT_N2)[None, :]
    angb = 2.0 * np.pi * ((n2 * k2) % FFT_N2) / FFT_N2
    cb, sb = np.cos(angb), np.sin(angb)
    mb = np.block([[cb, -sb], [sb, cb]])
    mc = np.block([[cb, sb], [-sb, cb]])
    angd = 2.0 * np.pi * ((np.arange(k1n)[:, None] * np.arange(na)[None, :]) % k1n) / k1n
    cd, sd = np.cos(angd) / n, np.sin(angd) / n
    md = np.block([[cd, sd], [-sd, cd]])
    as_bf = lambda a: jnp.asarray(a, dtype=F32).astype(BF16)
    as_f = lambda a: jnp.asarray(a, dtype=F32)
    mb2 = np.concatenate([mb, mc], axis=0)
    return dict(ma=as_bf(ma), ma_f=as_bf(ma_f), ma_b=as_bf(ma_b), twr=as_f(twr), twi=as_f(twi),
                twr_t=as_f(twr.T), twi_t=as_f(twi.T), mb=as_bf(mb), mb2=as_bf(mb2), mc=as_bf(mc), md=as_bf(md))


def _fwd_stages(x2, ma, twr, twi, mb, cb, k1n):
    a = _dot(x2, ma).reshape(cb, FFT_N2, 2 * k1n)
    ar, ai = a[..., :k1n], a[..., k1n:]
    ar2 = ar * twr - ai * twi
    ai2 = ar * twi + ai * twr
    xt = jnp.concatenate([jnp.swapaxes(ar2, 1, 2), jnp.swapaxes(ai2, 1, 2)], axis=-1)
    return _dot(xt.astype(BF16).reshape(cb * k1n, 2 * FFT_N2), mb).reshape(cb, k1n, 2 * FFT_N2)


def _spectrum_kernel(hf_ref, hb_ref, ssf_ref, ssb_ref, maf_ref, mab_ref, twr_ref, twi_ref, mb2_ref, h_ref, *, na):
    k1n = 2 * na
    cb = h_ref.shape[0]

    def stage_a(ref, ss_ref, m_ref):
        x = ref[0, 0] * lax.rsqrt(ss_ref[0, 0])
        x = jnp.concatenate([x, jnp.zeros((cb, FFT_N2 - na, FFT_N2), F32)], axis=1)
        xt = jnp.swapaxes(x, 1, 2).astype(BF16).reshape(cb * FFT_N2, FFT_N2)
        a = _dot(xt, m_ref[...]).reshape(cb, FFT_N2, 2 * k1n)
        return a[..., :k1n], a[..., k1n:]

    fr, fi = stage_a(hf_ref, ssf_ref, maf_ref)
    br, bi = stage_a(hb_ref, ssb_ref, mab_ref)
    twr, twi = twr_ref[...], twi_ref[...]
    parts = (fr * twr - fi * twi, fr * twi + fi * twr, br * twr + bi * twi, bi * twr - br * twi)
    xt = jnp.concatenate([jnp.swapaxes(p, 1, 2) for p in parts], axis=-1)
    h = _dot(xt.astype(BF16).reshape(cb * k1n, 4 * FFT_N2), mb2_ref[...])
    h_ref[...] = h.reshape(cb, k1n, 2 * FFT_N2).astype(BF16)


def _spectrum(filt5, ss5, order, tb, na):
    c = filt5.shape[2]
    k1n = 2 * na
    cb = FFT_CB
    full = lambda a: pl.BlockSpec(a.shape, lambda i: tuple(0 for _ in a.shape))
    fblk = lambda d: pl.BlockSpec((1, 1, cb, na, FFT_N2), lambda i: (order, d, i, 0, 0))
    sblk = lambda d: pl.BlockSpec((1, 1, cb, 1, 1), lambda i: (order, d, i, 0, 0))
    return pl.pallas_call(
        functools.partial(_spectrum_kernel, na=na),
        out_shape=jax.ShapeDtypeStruct((c, k1n, 2 * FFT_N2), BF16),
        grid=(c // cb,),
        in_specs=[fblk(0), fblk(1), sblk(0), sblk(1),
                  full(tb["ma_f"]), full(tb["ma_b"]), full(tb["twr"]), full(tb["twi"]), full(tb["mb2"])],
        out_specs=pl.BlockSpec((cb, k1n, 2 * FFT_N2), lambda i: (i, 0, 0)),
        compiler_params=_cparams("arbitrary"),
        name="hyena_filter_spectrum",
    )(filt5, filt5, ss5, ss5, tb["ma_f"], tb["ma_b"], tb["twr"], tb["twi"], tb["mb2"])


def _fftconv_kernel(u_ref, g_ref, h_ref, sk_ref, cwu_ref, cwg_ref, ma_ref, twr_ref, twi_ref, mb_ref, mc_ref,
                    twrt_ref, twit_ref, md_ref, o_ref, *, na, conv_u):
    k1n = 2 * na
    cb = u_ref.shape[0]
    uf = u_ref[...].astype(F32)
    if conv_u:
        uf = _shortconv_freq(uf, cwu_ref[...], na)
    gate = _shortconv_freq(g_ref[...].astype(F32), cwg_ref[...], na)
    x = _fwd_stages(uf.astype(BF16).reshape(cb * FFT_N2, k1n), ma_ref[...], twr_ref[...], twi_ref[...],
                    mb_ref[...], cb, k1n)
    xr, xi = x[..., :FFT_N2], x[..., FFT_N2:]
    h = h_ref[...].astype(F32)
    hr, hi = h[..., :FFT_N2], h[..., FFT_N2:]
    y = jnp.concatenate([xr * hr - xi * hi, xr * hi + xi * hr], axis=-1)
    c = _dot(y.astype(BF16).reshape(cb * k1n, 2 * FFT_N2), mc_ref[...]).reshape(cb, k1n, 2 * FFT_N2)
    cr, ci = c[..., :FFT_N2], c[..., FFT_N2:]
    twrt, twit = twrt_ref[...], twit_ref[...]
    cr2 = cr * twrt + ci * twit
    ci2 = ci * twrt - cr * twit
    ct = jnp.concatenate([jnp.swapaxes(cr2, 1, 2), jnp.swapaxes(ci2, 1, 2)], axis=-1)
    d = _dot(ct.astype(BF16).reshape(cb * FFT_N2, 2 * k1n), md_ref[...]).reshape(cb, FFT_N2, k1n)
    o_ref[...] = (gate * (d + sk_ref[...] * uf)).astype(BF16)


def _fftconv(u, gate, h, skip, cw_u, cw_g, tb, na, conv_u):
    c = u.shape[0]
    k1n = 2 * na
    cb = FFT_CB
    full = lambda a: pl.BlockSpec(a.shape, lambda i: tuple(0 for _ in a.shape))
    blk = pl.BlockSpec((cb, FFT_N2, k1n), lambda i: (i, 0, 0))
    cwblk = pl.BlockSpec((cb, 1, 4), lambda i: (i, 0, 0))
    return pl.pallas_call(
        functools.partial(_fftconv_kernel, na=na, conv_u=conv_u),
        out_shape=jax.ShapeDtypeStruct(u.shape, BF16),
        grid=(c // cb,),
        in_specs=[blk, blk,
                  pl.BlockSpec((cb, k1n, 2 * FFT_N2), lambda i: (i, 0, 0)),
                  pl.BlockSpec((cb, 1, 1), lambda i: (i, 0, 0)), cwblk, cwblk,
                  full(tb["ma"]), full(tb["twr"]), full(tb["twi"]), full(tb["mb"]), full(tb["mc"]),
                  full(tb["twr_t"]), full(tb["twi_t"]), full(tb["md"])],
        out_specs=blk,
        compiler_params=_cparams("arbitrary"),
        name="hyena_fftconv",
    )(u, gate, h, skip, cw_u, cw_g, tb["ma"], tb["twr"], tb["twi"], tb["mb"], tb["mc"], tb["twr_t"], tb["twi_t"],
      tb["md"])


def _to_freq_layout(a, na):
    nb, _, c = a.shape
    a = a.reshape(nb, na, FFT_N2, c)
    return jnp.transpose(a, (3, 2, 0, 1)).reshape(c, FFT_N2, nb * na)


def _from_freq_layout(a):
    c, _, lanes = a.shape
    na = lanes // 2
    a = a.reshape(c, FFT_N2, 2, na)
    return jnp.transpose(a, (2, 3, 1, 0)).reshape(2, na * FFT_N2, c)


def _ctx_dft_tables(n):
    big = 2 * n
    j = np.arange(n)[:, None]
    k = np.arange(big)[None, :]
    ang = 2.0 * np.pi * ((j * k) % big) / big
    c, s = np.cos(ang), np.sin(ang)
    fh = np.block([[c, -s], [c, s]])
    ff = np.block([[c, -s], [s, c]])
    fi = np.block([[c.T, s.T], [-s.T, c.T]]) / big
    as_bf = lambda a: jnp.asarray(a, dtype=F32).astype(BF16)
    return as_bf(fh), as_bf(ff), as_bf(fi)


def _ctxconv_kernel(u_ref, cw_ref, hf_ref, hb_ref, ssf_ref, ssb_ref, sk_ref, fh_ref, ff_ref, fi_ref, o_ref):
    n_tok = u_ref.shape[2] // 2
    short = lambda j: _shortconv_lanes(u_ref[j].astype(F32), cw_ref[j], n_tok)
    gates = (short(0), short(1))
    z = short(2)
    half = fh_ref.shape[1] // 2
    for n in range(2):
        taps = jnp.concatenate([hf_ref[n] * lax.rsqrt(ssf_ref[n]), hb_ref[n] * lax.rsqrt(ssb_ref[n])], axis=-1)
        h = _dot(taps.astype(BF16), fh_ref[...])
        x = _dot(z.astype(BF16), ff_ref[...])
        hr, hi = h[:, :half], h[:, half:]
        xr, xi = x[:, :half], x[:, half:]
        y = jnp.concatenate([xr * hr - xi * hi, xr * hi + xi * hr], axis=-1)
        conv = _dot(y.astype(BF16), fi_ref[...])
        z = gates[n] * (conv + sk_ref[n] * z)
    o_ref[...] = z.astype(BF16)


def _ctxconv(u3, cw3, hf, hb, ssf, ssb, skip, tables):
    _, c, lanes = u3.shape
    args = (u3, cw3, hf, hb, ssf, ssb, skip) + tuple(tables)
    full = lambda a: pl.BlockSpec(a.shape, lambda i: tuple(0 for _ in a.shape))
    return pl.pallas_call(
        _ctxconv_kernel,
        out_shape=jax.ShapeDtypeStruct((c, lanes), BF16),
        grid=(1,),
        in_specs=[full(a) for a in args],
        out_specs=pl.BlockSpec((c, lanes), lambda i: (0, 0)),
        compiler_params=_cparams("arbitrary"),
        name="hyena_ctxconv",
    )(*args)


def _load_row_tiles(ref, n, stride=SUBLANES, first=0):
    return jnp.concatenate([ref[pl.ds(first + j, n, stride=stride), :] for j in range(SUBLANES)], axis=-1)


def _store_row_tiles(ref, val, n, stride=SUBLANES, first=0):
    for j in range(SUBLANES):
        ref[pl.ds(first + j, n, stride=stride), :] = val[:, j * LANES:(j + 1) * LANES]


def _row_tile(ref, idx, tiles=1):
    size = tiles * SUBLANES
    return ref.at[pl.ds(pl.multiple_of(idx * size, size), size)]


def _top2_rows(vals):
    n = len(vals)
    best_v, best_i = vals[0], jnp.zeros_like(vals[0], dtype=jnp.int32)
    for e in range(1, n):
        take = vals[e] > best_v
        best_v = jnp.where(take, vals[e], best_v)
        best_i = jnp.where(take, e, best_i)
    sec_v = jnp.full_like(vals[0], -jnp.inf)
    sec_i = jnp.zeros_like(best_i)
    for e in range(n):
        take = (best_i != e) & (vals[e] > sec_v)
        sec_v = jnp.where(take, vals[e], sec_v)
        sec_i = jnp.where(take, e, sec_i)
    return best_v, best_i, sec_v, sec_i


def _route(h2, rw_ref, rb_ref, carry_ref, mi_ref, mf_ref, cnt_ref, cols):
    logits = _dot3(rw_ref[...], h2, dot=_dot_nt)
    s = 1.0 / (1.0 + jnp.exp(-logits))
    sel = s + rb_ref[...]
    ng = N_GROUPS
    sel4 = [sel[j * ng:(j + 1) * ng, :] for j in range(EXPERTS_PER_GROUP)]
    s4 = [s[j * ng:(j + 1) * ng, :] for j in range(EXPERTS_PER_GROUP)]
    bv, _, sv, _ = _top2_rows(sel4)
    gscore = bv + sv
    gidx = lax.broadcasted_iota(jnp.int32, (ng, TM), 0)
    gmax = jnp.max(gscore, axis=0, keepdims=True)
    best_g = jnp.min(jnp.where(gscore == gmax, gidx, ng), axis=0, keepdims=True)
    in_best = gidx == best_g
    pick = lambda slab: jnp.sum(jnp.where(in_best, slab, 0.0), axis=0, keepdims=True)
    cand_sel = [pick(v) for v in sel4]
    cand_s = [pick(v) for v in s4]
    _, i1, _, i2 = _top2_rows(cand_sel)
    a1, a2 = cand_s[0], cand_s[0]
    for j in range(1, EXPERTS_PER_GROUP):
        a1 = jnp.where(i1 == j, cand_s[j], a1)
        a2 = jnp.where(i2 == j, cand_s[j], a2)
    den = a1 + a2
    w1, w2 = a1 / den, a2 / den
    swap = i2 < i1
    lo = jnp.where(swap, i2, i1)
    hi = jnp.where(swap, i1, i2)
    w_lo = jnp.where(swap, w2, w1)
    w_hi = jnp.where(swap, w1, w2)
    pair = jnp.where(lo == 0, 0, jnp.where(lo == 1, 3, 5)) + (hi - lo - 1)
    cls = best_g * len(PAIRS) + pair
    cidx = lax.broadcasted_iota(jnp.int32, (CLASS_ROWS, TM), 0)
    hit = cidx == cls
    onehot = jnp.where(hit, 1.0, 0.0)
    tri = jnp.where(lax.broadcasted_iota(jnp.int32, (TM, TM), 0) < lax.broadcasted_iota(jnp.int32, (TM, TM), 1),
                    1.0, 0.0).astype(BF16)
    rank = _dot(onehot.astype(BF16), tri) + carry_ref[:, 0:1]
    rk = jnp.sum(jnp.where(hit, rank, 0.0), axis=0, keepdims=True)
    carry_ref[...] = carry_ref[...] + jnp.sum(onehot, axis=1, keepdims=True)
    mi_ref[:, cols] = jnp.concatenate([cls, rk.astype(jnp.int32), jnp.zeros((6, TM), jnp.int32)], axis=0)
    mf_ref[:, cols] = jnp.concatenate([w_lo, w_hi, jnp.zeros((6, TM), F32)], axis=0)
    cnt_ref[...] = carry_ref[...]


def _post_norm_mod(x, y, m_ref, lng, lnb, gate_row, sh_row, sc_row):
    u = DN_ALPHA * x + m_ref[0, gate_row:gate_row + 1, :] * y
    mu = jnp.mean(u, axis=-1, keepdims=True)
    var = jnp.mean(jnp.square(u - mu), axis=-1, keepdims=True)
    xn = (u - mu) * lax.rsqrt(var + LN_EPS) * lng + lnb
    if sh_row is None:
        return xn, None
    return xn, xn * (1.0 + m_ref[0, sc_row:sc_row + 1, :]) + m_ref[0, sh_row:sh_row + 1, :]


def _outproj_kernel(*refs, layer0, tpb, parts):
    nd = 8 if layer0 else 5
    halves = tuple(refs[p * nd:(p + 1) * nd] for p in range(parts))
    (w_ref, lng_ref, lnb_ref, rw_ref, rb_ref, xo_ref, h2_ref, mi_ref, mf_ref, cnt_ref,
     carry_ref) = refs[parts * nd:]
    i = pl.program_id(0)

    @pl.when(i == 0)
    def _():
        carry_ref[...] = jnp.zeros_like(carry_ref)

    for half, data in enumerate(halves):
        if layer0:
            of_ref, ob_ref, g_ref, hyl_ref, hyc_ref, xl_ref, xc_ref, m_ref = data
            is_ctx = (parts * i + half) % tpb == tpb - 1
            x = jnp.where(is_ctx, xc_ref[...], xl_ref[...])
            hy = jnp.where(is_ctx, hyc_ref[...], hyl_ref[...])
            o = of_ref[...].astype(F32) + ob_ref[...].astype(F32)
            normed = []
            for h in range(RET_HEADS):
                oh = o[:, h * RET_HD:(h + 1) * RET_HD]
                mu = jnp.mean(oh, axis=-1, keepdims=True)
                var = jnp.mean(jnp.square(oh - mu), axis=-1, keepdims=True)
                normed.append((oh - mu) * lax.rsqrt(var + LN_EPS))
            yret = (jnp.concatenate(normed, axis=-1) * _silu(g_ref[...].astype(F32))).astype(BF16)
            wr = yret.shape[1]
            y = _dot(yret, w_ref[:wr, :]) + _dot(hy, w_ref[wr:, :])
        else:
            of_ref, ob_ref, g_ref, x_ref, m_ref = data
            x = x_ref[...]
            o = of_ref[...].astype(F32) + ob_ref[...].astype(F32)
            normed = []
            for h in range(GLA_HEADS):
                oh = o[:, h * GLA_DV:(h + 1) * GLA_DV]
                normed.append(oh * lax.rsqrt(jnp.mean(jnp.square(oh), axis=-1, keepdims=True) + LN_EPS))
            a = (jnp.concatenate(normed, axis=-1) * _silu(g_ref[...].astype(F32))).astype(BF16)
            y = _dot(a, w_ref[...])
        rows = slice(half * TM, (half + 1) * TM)
        xn, h2 = _post_norm_mod(x, y, m_ref, lng_ref[...], lnb_ref[...], 2, 3, 4)
        xo_ref[rows, :] = xn
        _store_row_tiles(h2_ref, h2, TM, first=half * TM * SUBLANES)
        _route(h2, rw_ref, rb_ref, carry_ref, mi_ref, mf_ref, cnt_ref, rows)


def _outproj_call(layer0, tpb, n_tiles, d, half_specs, half_args, w, lng, lnb, router_w, router_bias):
    rows = n_tiles * TM
    parts = next(p for p in (4, 3, 2, 1) if n_tiles % p == 0)
    full2 = lambda a: pl.BlockSpec(a.shape, lambda i: (0, 0))
    regroup = lambda a: jnp.swapaxes(a.reshape(N_GROUPS, EXPERTS_PER_GROUP, -1), 0, 1).reshape(N_EXPERTS, -1)
    rw_t = regroup(router_w.T)
    rb = regroup(router_bias.reshape(N_EXPERTS, 1))
    return pl.pallas_call(
        functools.partial(_outproj_kernel, layer0=layer0, tpb=tpb, parts=parts),
        out_shape=(jax.ShapeDtypeStruct((rows, d), F32),
                   jax.ShapeDtypeStruct((rows * SUBLANES, LANES), F32),
                   jax.ShapeDtypeStruct((8, rows), jnp.int32),
                   jax.ShapeDtypeStruct((8, rows), F32),
                   jax.ShapeDtypeStruct((CLASS_ROWS, LANES), F32)),
        grid=(n_tiles // parts,),
        in_specs=[s for h in range(parts) for s in half_specs(h, parts)]
        + [full2(w), full2(lng), full2(lnb), full2(rw_t), full2(rb)],
        out_specs=(pl.BlockSpec((parts * TM, d), lambda i: (i, 0)),
                   pl.BlockSpec((parts * TM * SUBLANES, LANES), lambda i: (i, 0)),
                   pl.BlockSpec((8, parts * TM), lambda i: (0, i)),
                   pl.BlockSpec((8, parts * TM), lambda i: (0, i)),
                   pl.BlockSpec((CLASS_ROWS, LANES), lambda i: (0, 0))),
        scratch_shapes=[pltpu.VMEM((CLASS_ROWS, LANES), F32)],
        compiler_params=_cparams("arbitrary"),
        name="outproj_norm0" if layer0 else "outproj_norm1",
    )(*(tuple(half_args) * parts), w, lng, lnb, rw_t, rb)


def _dispatch_kernel(slots_ref, zoff_ref, h_ref, xs_ref, zero_ref, sem):
    i = pl.program_id(0)

    @pl.when(i == 0)
    def _():
        zero_ref[...] = jnp.zeros_like(zero_ref)
        def fill_class(e, go):
            @pl.when(zoff_ref[N_CLASSES + 1 + e] > 0)
            def _():
                go(pltpu.make_async_copy(zero_ref, _row_tile(xs_ref, zoff_ref[e], TE), sem))

        for e in range(N_CLASSES):
            fill_class(e, lambda cp: cp.start())
        for e in range(N_CLASSES):
            fill_class(e, lambda cp: cp.wait())
        first_free = zoff_ref[N_CLASSES]
        n_all = xs_ref.shape[0] // (TE * SUBLANES)

        def fill(t, carry):
            pltpu.make_async_copy(zero_ref, _row_tile(xs_ref, t, TE), sem).start()
            return carry

        def fill_wait(t, carry):
            pltpu.make_async_copy(zero_ref, _row_tile(xs_ref, t, TE), sem).wait()
            return carry

        lax.fori_loop(first_free, n_all, fill, 0)
        lax.fori_loop(first_free, n_all, fill_wait, 0)

    def row_copy(r):
        return pltpu.make_async_copy(_row_tile(h_ref, r), _row_tile(xs_ref, slots_ref[0, 0, r]), sem)

    n_rows = slots_ref.shape[2]
    for r in range(n_rows):
        row_copy(r).start(priority=r % 2)
    for r in range(n_rows):
        row_copy(r).wait()


def _dispatch(h2, slots, zoff, p_rows):
    t = slots.shape[0]
    rows = next(c for c in DISPATCH_ROWS if t % c == 0)
    grid_spec = pltpu.PrefetchScalarGridSpec(
        num_scalar_prefetch=0,
        grid=(t // rows,),
        in_specs=[pl.BlockSpec((1, 1, rows), lambda i: (i, 0, 0), memory_space=pltpu.SMEM),
                  pl.BlockSpec(memory_space=pltpu.SMEM),
                  pl.BlockSpec((rows * SUBLANES, LANES), lambda i: (i, 0))],
        out_specs=pl.BlockSpec(memory_space=pl.ANY),
        scratch_shapes=[pltpu.VMEM((TE * SUBLANES, LANES), F32), pltpu.SemaphoreType.DMA(())],
    )
    return pl.pallas_call(
        _dispatch_kernel,
        out_shape=jax.ShapeDtypeStruct((p_rows * SUBLANES, LANES), F32),
        grid_spec=grid_spec,
        compiler_params=_cparams("arbitrary"),
        name="moe_dispatch",
    )(slots.reshape(t // rows, 1, rows), zoff, h2)


def _experts_kernel(ea_ref, eb_ref, tv_ref, x_ref, wga_ref, wua_ref, wda_ref, wgb_ref, wub_ref, wdb_ref, y_ref):
    j = pl.program_id(0)

    @pl.when(tv_ref[j] > 0)
    def _():
        x = _load_row_tiles(x_ref, TE).astype(BF16)
        for which, (wg, wu, wd) in enumerate(((wga_ref, wua_ref, wda_ref), (wgb_ref, wub_ref, wdb_ref))):
            g = _dot(x, wg[0])
            u = _dot(x, wu[0])
            y = _dot((_silu(g) * u).astype(BF16), wd[0])
            _store_row_tiles(y_ref, y, TE, stride=2 * SUBLANES, first=which * SUBLANES)

    @pl.when(tv_ref[j] == 0)
    def _():
        y_ref[...] = jnp.zeros_like(y_ref)


def _experts(xs, tile_ea, tile_eb, tile_valid, wg, wu, wd):
    n_tiles = xs.shape[0] // (TE * SUBLANES)
    d, hdim = wg.shape[1], wg.shape[2]
    wa = lambda shp: pl.BlockSpec(shp, lambda j, ea, eb, tv: (ea[j], 0, 0))
    wb = lambda shp: pl.BlockSpec(shp, lambda j, ea, eb, tv: (eb[j], 0, 0))
    grid_spec = pltpu.PrefetchScalarGridSpec(
        num_scalar_prefetch=3,
        grid=(n_tiles,),
        in_specs=[pl.BlockSpec((TE * SUBLANES, LANES), lambda j, ea, eb, tv: (j, 0)),
                  wa((1, d, hdim)), wa((1, d, hdim)), wa((1, hdim, d)),
                  wb((1, d, hdim)), wb((1, d, hdim)), wb((1, hdim, d))],
        out_specs=pl.BlockSpec((2 * TE * SUBLANES, LANES), lambda j, ea, eb, tv: (j, 0)),
    )
    return pl.pallas_call(
        _experts_kernel,
        out_shape=jax.ShapeDtypeStruct((2 * xs.shape[0], LANES), F32),
        grid_spec=grid_spec,
        compiler_params=_cparams("arbitrary"),
        name="moe_experts",
    )(tile_ea, tile_eb, tile_valid, xs, wg, wu, wd, wg, wu, wd)


def _combine_kernel(slots_ref, next_slots_ref, ys_ref, wts_ref, x_ref, m_ref, lng_ref, lnb_ref, o_ref, buf_ref, sem):
    i = pl.program_id(0)
    cur = i % 2

    def row_copy(s_ref, b, r):
        return pltpu.make_async_copy(_row_tile(ys_ref, s_ref[0, 0, r], 2), _row_tile(buf_ref.at[b], r, 2), sem.at[b])

    def issue_all(s_ref, b):
        for r in range(TM):
            row_copy(s_ref, b, r).start(priority=r % 2)

    @pl.when(i == 0)
    def _():
        issue_all(slots_ref, 0)

    @pl.when(i + 1 < pl.num_programs(0))
    def _():
        issue_all(next_slots_ref, 1 - cur)

    for r in range(TM):
        row_copy(slots_ref, cur, r).wait()
    buf = buf_ref.at[cur]
    y_lo = _load_row_tiles(buf, TM, stride=2 * SUBLANES)
    y_hi = _load_row_tiles(buf, TM, stride=2 * SUBLANES, first=SUBLANES)
    y = wts_ref[:, 0:1] * y_lo + wts_ref[:, 1:2] * y_hi
    xn, _ = _post_norm_mod(x_ref[...], y, m_ref, lng_ref[...], lnb_ref[...], 5, None, None)
    o_ref[...] = xn


def _combine(ys, slots3, wts, x, mods, lng, lnb, cond_of, n_tiles):
    d = x.shape[1]
    grid_spec = pltpu.PrefetchScalarGridSpec(
        num_scalar_prefetch=0,
        grid=(n_tiles,),
        in_specs=[pl.BlockSpec((1, 1, TM), lambda i: (i, 0, 0), memory_space=pltpu.SMEM),
                  pl.BlockSpec((1, 1, TM), lambda i: (jnp.minimum(i + 1, n_tiles - 1), 0, 0),
                               memory_space=pltpu.SMEM),
                  pl.BlockSpec(memory_space=pl.ANY),
                  pl.BlockSpec((TM, 2), lambda i: (i, 0)),
                  pl.BlockSpec((TM, d), lambda i: (i, 0)),
                  pl.BlockSpec((1, 6, d), lambda i: (cond_of(i), 0, 0)),
                  pl.BlockSpec((1, d), lambda i: (0, 0)),
                  pl.BlockSpec((1, d), lambda i: (0, 0))],
        out_specs=pl.BlockSpec((TM, d), lambda i: (i, 0)),
        scratch_shapes=[pltpu.VMEM((2, 2 * TM * SUBLANES, LANES), F32), pltpu.SemaphoreType.DMA((2,))],
    )
    return pl.pallas_call(
        _combine_kernel,
        out_shape=jax.ShapeDtypeStruct((n_tiles * TM, d), F32),
        grid_spec=grid_spec,
        compiler_params=_cparams("arbitrary"),
        name="moe_combine",
    )(slots3, slots3, ys, wts, x, mods, lng, lnb)


def _moe(h2, mi, mf, cnt, x, cond_of, mods, lng, lnb, layer, wg, wu, wd):
    t = x.shape[0]
    n_tiles = t // TM
    counts = cnt[:N_CLASSES, 0].astype(jnp.int32)
    padded = ((counts + TE - 1) // TE) * TE
    ends = jnp.cumsum(padded)
    offs = ends - padded
    cls = mi[0]
    cids = jnp.arange(N_CLASSES, dtype=jnp.int32)[:, None]
    slots = jnp.sum(jnp.where(cls[None] == cids, offs[:, None], 0), axis=0) + mi[1]
    slots3 = slots.reshape(n_tiles, 1, TM)
    n_cls_tiles = t // TE + N_CLASSES
    total_tiles = ends[-1] // TE
    tile_ids = jnp.arange(n_cls_tiles, dtype=jnp.int32)
    tile_valid = (tile_ids < total_tiles).astype(jnp.int32)
    tile_last = jnp.minimum(tile_ids, total_tiles - 1)
    tile_class = jnp.sum((tile_last[:, None] * TE >= ends[None, :]).astype(jnp.int32), axis=1)
    tile_class = jnp.minimum(tile_class, N_CLASSES - 1)
    pair_lo = jnp.asarray([p[0] for p in PAIRS], jnp.int32)
    pair_hi = jnp.asarray([p[1] for p in PAIRS], jnp.int32)
    pair_sel = (tile_class % len(PAIRS))[:, None] == jnp.arange(len(PAIRS), dtype=jnp.int32)[None, :]
    group0 = (tile_class // len(PAIRS)) * EXPERTS_PER_GROUP
    tile_ea = (group0 + jnp.sum(jnp.where(pair_sel, pair_lo[None, :], 0), axis=1)).astype(jnp.int32)
    tile_eb = (group0 + jnp.sum(jnp.where(pair_sel, pair_hi[None, :], 0), axis=1)).astype(jnp.int32)
    zoff = jnp.concatenate([jnp.maximum(ends // TE - 1, 0), total_tiles[None], padded]).astype(jnp.int32)
    xs = _dispatch(h2, slots, zoff, n_cls_tiles * TE)
    ys = _experts(xs, tile_ea + layer * N_EXPERTS, tile_eb + layer * N_EXPERTS, tile_valid, wg, wu, wd)
    wts = jnp.transpose(mf[0:2])
    return _combine(ys, slots3, wts, x, mods, lng, lnb, cond_of, n_tiles)


def _chunk_cumsum(x, tri):
    hi = x.astype(BF16)
    r1 = x - hi.astype(F32)
    mid = r1.astype(BF16)
    lo = (r1 - mid.astype(F32)).astype(BF16)
    return _dot(tri, hi) + _dot(tri, mid) + _dot(tri, lo)


def _gla_kernel(*refs, nb):
    f_in, b_in, (of_ref, ob_ref, sf_ref, sb_ref) = refs[:6], refs[6:12], refs[12:]

    @pl.when(pl.program_id(0) == 0)
    def _():
        sf_ref[...] = jnp.zeros_like(sf_ref)
        sb_ref[...] = jnp.zeros_like(sb_ref)

    _gla_dir(*f_in, of_ref, sf_ref, False, nb)
    _gla_dir(*b_in, ob_ref, sb_ref, True, nb)


def _gla_dir(q_ref, k_ref, v_ref, lr_ref, gw_ref, gb_ref, o_ref, s_ref, reverse, nb):
    C = GLA_CHUNK
    nchunk = TM // C
    ii = lax.broadcasted_iota(jnp.int32, (C, C), 0)
    jj = lax.broadcasted_iota(jnp.int32, (C, C), 1)
    keep = (jj >= ii) if reverse else (jj <= ii)
    tri = jnp.where(keep, 1.0, 0.0).astype(BF16)
    order = tuple(reversed(range(nchunk))) if reverse else tuple(range(nchunk))
    end_row = 0 if reverse else C - 1
    gw = gw_ref[...]
    gb = gb_ref[...]
    for b in range(nb):
        pre = _dot(lr_ref[b], gw) + gb
        la_all = (jnp.minimum(pre, 0.0) - jnp.log(1.0 + jnp.exp(-jnp.abs(pre)))) * (1.0 / GLA_TAU)
        for c in order:
            rows = slice(c * C, (c + 1) * C)
            bc = _chunk_cumsum(la_all[rows, :], tri)
            bend = bc[end_row:end_row + 1, :]
            eb = jnp.exp(bc)
            enb = jnp.exp(-bc)
            ekb = jnp.exp(bend - bc)
            ebend = jnp.exp(bend)
            for h in range(GLA_HEADS):
                kc = slice(h * GLA_DK, (h + 1) * GLA_DK)
                vc = slice(h * GLA_DV, (h + 1) * GLA_DV)
                q = q_ref[b, rows, kc].astype(F32) * (GLA_DK ** -0.5)
                k = k_ref[b, rows, kc].astype(F32)
                v = v_ref[b, rows, vc]
                qd = (q * eb[:, kc]).astype(BF16)
                kd = (k * enb[:, kc]).astype(BF16)
                st = s_ref[b, h]
                scores = jnp.where(keep, _dot_nt(qd, kd), 0.0)
                o = _dot(scores.astype(BF16), v) + _dot_nt(qd, st.astype(BF16))
                s_ref[b, h] = st * ebend[:, kc] + _dot_tn(v, (k * ekb[:, kc]).astype(BF16))
                o_ref[b, rows, vc] = o.astype(BF16)


def _gla(z3, gw_f, gb_f, gw_b, gb_b):
    nb, s, _ = z3.shape
    nt = s // TM
    kw = GLA_HEADS * GLA_DK
    vw = GLA_HEADS * GLA_DV
    lr_blk = (2 * kw + 2 * vw) // LANES

    def dir_specs(jmap):
        return [pl.BlockSpec((nb, TM, kw), lambda j: (0, jmap(j), 0)),
                pl.BlockSpec((nb, TM, kw), lambda j: (0, jmap(j), 1)),
                pl.BlockSpec((nb, TM, vw), lambda j: (0, jmap(j), 1)),
                pl.BlockSpec((nb, TM, LANES), lambda j: (0, jmap(j), lr_blk)),
                pl.BlockSpec((LANES, kw), lambda j: (0, 0)),
                pl.BlockSpec((1, kw), lambda j: (0, 0))]

    fwd, bwd = _scan_tile_orders(nt)
    state = pltpu.VMEM((nb, GLA_HEADS, GLA_DV, GLA_DK), F32)
    out = jax.ShapeDtypeStruct((nb, s, vw), BF16)
    return pl.pallas_call(
        functools.partial(_gla_kernel, nb=nb),
        out_shape=(out, out),
        grid=(nt,),
        in_specs=dir_specs(fwd) + dir_specs(bwd),
        out_specs=(pl.BlockSpec((nb, TM, vw), lambda j: (0, fwd(j), 0)),
                   pl.BlockSpec((nb, TM, vw), lambda j: (0, bwd(j), 0))),
        scratch_shapes=[state, state],
        compiler_params=_cparams("arbitrary"),
        name="gla",
    )(z3, z3, z3, z3, gw_f, gb_f, z3, z3, z3, z3, gw_b, gb_b)


def kernel(x, c, ctx, c_ctx, mod_w, mod_b, ln_g, ln_b, ab_w_in, ret_log_decay_f, ret_log_decay_b, hy_conv_w, hy_conv_b, hy_w1, hy_b1, hy_w2, hy_b2, hy_w3, hy_b3, hy_w4, hy_freq, hy_skip, ab_w_out, gla_w_in, gla_gate_w_f, gla_gate_b_f, gla_gate_w_b, gla_gate_b_b, gla_w_out, router_w, router_bias, exp_w_gate, exp_w_up, exp_w_down):
    nb, length, d = x.shape
    ctx_len = ctx.shape[1]
    assert nb == 2 and ctx_len == TM and length % TM == 0 and length % (FFT_N2 * 2) == 0
    s = length + ctx_len
    tpb = s // TM
    r = nb * s
    lat_tiles = tpb - 1

    cond = jnp.zeros((8, d), F32).at[0].set(c[0]).at[1].set(c_ctx).at[2].set(c[1]).at[3].set(c_ctx)
    mods = _ada(cond, mod_w, mod_b).reshape(DEPTH, 8, 6, d)

    def cond_of(i):
        return 2 * (i // tpb) + jnp.where(i % tpb == tpb - 1, 1, 0)

    x_lat = x.reshape(nb * length, d)
    x_ctx = ctx.reshape(nb * ctx_len, d)

    ret_w = RET_HEADS * RET_HD
    z0, *hy_lat, hy_ctx = _inproj0(x_lat, x_ctx, mods[0], ab_w_in[0].astype(BF16), tpb, r, 4 * ret_w)
    z0_3 = z0.reshape(nb, s, -1)
    cosf, sinf = _rope_tables(length, ctx_len)
    o_f, o_b = _retention(z0_3, ret_log_decay_f[0], ret_log_decay_b[0], cosf, sinf)

    cw3 = jnp.concatenate([hy_conv_w[0].T, hy_conv_b[0][:, None]], axis=1).reshape(3, HY_WIDTH, 4)
    na = length // FFT_N2
    tb = _dft_tables(na)
    filt_w = (hy_w1[0], hy_b1[0], hy_w2[0], hy_b2[0], hy_w3[0], hy_b3[0], hy_w4[0], hy_freq[0])
    skip = hy_skip[0]

    filt, ss = _hyena_filters_raw(length, *filt_w)
    filt5 = filt.reshape(2, 2, HY_WIDTH, na, FFT_N2)
    ss5 = ss.reshape(2, 2, HY_WIDTH, 1, 1)
    hy_lat = [a.reshape(nb, length, HY_WIDTH) for a in hy_lat]
    zcur = _to_freq_layout(hy_lat[2], na)
    for n in range(2):
        hspec = _spectrum(filt5, ss5, n, tb, na)
        gate = _to_freq_layout(hy_lat[n], na)
        zcur = _fftconv(zcur, gate, hspec, skip[n].reshape(-1, 1, 1), cw3[2].reshape(-1, 1, 4),
                        cw3[n].reshape(-1, 1, 4), tb, na, conv_u=(n == 0))
    y_hy_lat = _from_freq_layout(zcur)

    filt_c, ss_c = _hyena_filters_raw(ctx_len, *filt_w)
    filt_c = filt_c.reshape(2, 2, HY_WIDTH, ctx_len)
    ss_c = ss_c.reshape(2, 2, HY_WIDTH, 1)
    u_ctx = jnp.transpose(hy_ctx.reshape(nb, ctx_len, 3, HY_WIDTH), (2, 3, 0, 1))
    y_ctx_t = _ctxconv(u_ctx.reshape(3, HY_WIDTH, nb * ctx_len), cw3, filt_c[:, 0], filt_c[:, 1], ss_c[:, 0],
                       ss_c[:, 1], skip.reshape(2, HY_WIDTH, 1), _ctx_dft_tables(ctx_len))
    y_hy_ctx = jnp.transpose(y_ctx_t.reshape(HY_WIDTH, nb, ctx_len), (1, 2, 0))

    gate_blk = 3

    def specs0(h, parts):
        tile = lambda i: parts * i + h
        trow = lambda i: (tile(i), 0)
        return [pl.BlockSpec((TM, ret_w), trow), pl.BlockSpec((TM, ret_w), trow),
                pl.BlockSpec((TM, ret_w), lambda i: (tile(i), gate_blk)),
                *_lat_or_ctx_specs(HY_WIDTH, tpb, tile), *_lat_or_ctx_specs(d, tpb, tile),
                pl.BlockSpec((1, 6, d), lambda i: (cond_of(tile(i)), 0, 0))]

    x_mid, h2, mi, mf, cnt = _outproj_call(
        True, tpb, r // TM, d, specs0,
        (o_f.reshape(r, ret_w), o_b.reshape(r, ret_w), z0, y_hy_lat.reshape(nb * length, HY_WIDTH),
         y_hy_ctx.reshape(nb * ctx_len, HY_WIDTH), x_lat, x_ctx, mods[0]),
        ab_w_out[0].astype(BF16), ln_g[0, 0].reshape(1, d), ln_b[0, 0].reshape(1, d), router_w, router_bias)

    stack = lambda w: w.astype(BF16).reshape((DEPTH * N_EXPERTS,) + w.shape[2:])
    exp_w = (stack(exp_w_gate), stack(exp_w_up), stack(exp_w_down))
    x1 = _moe(h2, mi, mf, cnt, x_mid, cond_of, mods[0], ln_g[0, 1].reshape(1, d), ln_b[0, 1].reshape(1, d),
              0, *exp_w)

    kw = GLA_HEADS * GLA_DK
    vw = GLA_HEADS * GLA_DV
    n_in = gla_w_in.shape[2]
    n_pad = 2 * kw + 2 * vw + LANES
    w_in1 = jnp.pad(gla_w_in[0], ((0, 0), (0, n_pad - n_in))).astype(BF16)
    z1 = _inproj(x1, mods[1], w_in1, tpb)
    z1_3 = z1.reshape(nb, s, n_pad)
    gw_f = jnp.zeros((LANES, kw), F32).at[:GLA_RANK].set(gla_gate_w_f[0]).astype(BF16)
    gw_b = jnp.zeros((LANES, kw), F32).at[GLA_RANK:2 * GLA_RANK].set(gla_gate_w_b[0]).astype(BF16)
    g_f, g_b = _gla(z1_3, gw_f, gla_gate_b_f[0].reshape(1, kw), gw_b, gla_gate_b_b[0].reshape(1, kw))

    n_lat = nb * lat_tiles

    def lat_tile(i):
        return (i // lat_tiles) * tpb + i % lat_tiles

    def lat_cond(i):
        return 2 * (i // lat_tiles)

    def specs1(h, parts):
        tile = lambda i: parts * i + h
        lrow = lambda i: (lat_tile(tile(i)), 0)
        return [pl.BlockSpec((TM, vw), lrow), pl.BlockSpec((TM, vw), lrow),
                pl.BlockSpec((TM, vw), lambda i: (lat_tile(tile(i)), 2)),
                pl.BlockSpec((TM, d), lrow),
                pl.BlockSpec((1, 6, d), lambda i: (lat_cond(tile(i)), 0, 0))]

    x_mid1, h2_1, mi1, mf1, cnt1 = _outproj_call(
        False, tpb, n_lat, d, specs1, (g_f.reshape(r, vw), g_b.reshape(r, vw), z1, x1, mods[1]),
        gla_w_out[0].astype(BF16), ln_g[1, 0].reshape(1, d), ln_b[1, 0].reshape(1, d), router_w, router_bias)

    out = _moe(h2_1, mi1, mf1, cnt1, x_mid1, lat_cond, mods[1], ln_g[1, 1].reshape(1, d), ln_b[1, 1].reshape(1, d),
               1, *exp_w)
    return out.reshape(nb, length, d)
```
